```python
import jax, jax.numpy as jnp
from jax import lax
import numpy as np

D_MODEL = 2048
BATCH = 2
SEQ = 8192
DEPTH = 1

GDN_HEADS = 8
GDN_DK = D_MODEL // 16
GDN_DV = D_MODEL // GDN_HEADS
GDN_QK = GDN_HEADS * GDN_DK
GDN_V = GDN_HEADS * GDN_DV
CONV_WIDTH = 4
RET_HEADS = 8
RET_DK = D_MODEL // 16
RET_DV = D_MODEL // RET_HEADS
RET_QK = RET_HEADS * RET_DK
RET_V = RET_HEADS * RET_DV
ROPE_BASE = 10000.0
CHUNK = 64
D_FF = ((8 * D_MODEL // 3 + 255) // 256) * 256
EPS = 1e-6
IN_SPLITS = (2 * GDN_QK + GDN_V, GDN_V, GDN_HEADS, GDN_HEADS,
             RET_QK, RET_QK, RET_V, RET_V, D_MODEL, D_MODEL)
N_IN = sum(IN_SPLITS)

kernel_name = "hybrid_gdn_retention_gated_merge"


def rmsnorm(x, w):
    xf = x.astype(jnp.float32)
    xf = xf * lax.rsqrt(jnp.mean(xf * xf, axis=-1, keepdims=True) + EPS)
    return xf.astype(x.dtype) * w


def group_norm_heads(o, w):
    mu = jnp.mean(o, axis=-1, keepdims=True)
    var = jnp.mean(jnp.square(o - mu), axis=-1, keepdims=True)
    return (o - mu) * lax.rsqrt(var + EPS) * w.reshape(o.shape[2], o.shape[3]).astype(jnp.float32)


def l2norm(x):
    return x * lax.rsqrt(jnp.sum(x * x, axis=-1, keepdims=True) + EPS)


def split_cols(p):
    offs = [int(o) for o in np.cumsum(IN_SPLITS)[:-1]]
    return jnp.split(p, offs, axis=-1)


def causal_conv(x, w):
    width, s = w.shape[0], x.shape[1]
    xp = jnp.pad(x, ((0, 0), (width - 1, 0), (0, 0)))
    out = xp[:, 0:s] * w[0]
    for j in range(1, width):
        out = out + xp[:, j:j + s] * w[j]
    return out


def rotary(x):
    d, s = x.shape[-1], x.shape[1]
    inv = ROPE_BASE ** (-jnp.arange(0, d, 2, dtype=jnp.float32) / d)
    ang = jnp.arange(s, dtype=jnp.float32)[:, None] * inv[None, :]
    cos = jnp.cos(ang)[None, :, None, :]
    sin = jnp.sin(ang)[None, :, None, :]
    x1, x2 = x[..., : d // 2], x[..., d // 2:]
    return jnp.concatenate([x1 * cos - x2 * sin, x2 * cos + x1 * sin], axis=-1)


def to_chunks(t):
    b, s, h, d = t.shape
    return t.reshape(b, s // CHUNK, CHUNK, h, d).transpose(0, 3, 1, 2, 4)


def to_chunks_scalar(t):
    b, s, h = t.shape
    return t.reshape(b, s // CHUNK, CHUNK, h).transpose(0, 3, 1, 2)


def from_scan(o):
    n, b, h, c, d = o.shape
    return o.transpose(1, 0, 3, 2, 4).reshape(b, n * c, h, d)


def gated_delta_rule(q, k, v, beta, g):
    c = q.shape[-2]
    incl = jnp.tril(jnp.ones((c, c), dtype=bool))
    strict = jnp.tril(jnp.ones((c, c), dtype=bool), -1)
    g = jnp.cumsum(g, axis=-1)
    decay = jnp.exp(jnp.where(incl, g[..., :, None] - g[..., None, :], -jnp.inf))
    kb = k * beta[..., None]
    a = jnp.where(strict, jnp.einsum('bhnid,bhnjd->bhnij', kb, k) * decay, 0.0)
    t = a + jnp.eye(c, dtype=a.dtype)
    u = lax.linalg.triangular_solve(t, v * beta[..., None], left_side=True, lower=True, unit_diagonal=True)
    w = lax.linalg.triangular_solve(t, kb * jnp.exp(g)[..., None], left_side=True, lower=True, unit_diagonal=True)
    attn = jnp.einsum('bhnid,bhnjd->bhnij', q, k) * decay

    def step(state, xs):
        q_c, k_c, u_c, w_c, g_c, attn_c = xs
        v_new = u_c - jnp.einsum('bhcd,bhde->bhce', w_c, state)
        o_c = (jnp.einsum('bhcd,bhde->bhce', q_c * jnp.exp(g_c)[..., None], state)
               + jnp.einsum('bhij,bhje->bhie', attn_c, v_new))
        g_last = g_c[..., -1:]
        state = (state * jnp.exp(g_last)[..., None]
                 + jnp.einsum('bhcd,bhce->bhde', k_c * jnp.exp(g_last - g_c)[..., None], v_new))
        return state, o_c

    b, h, _, _, dk = q.shape
    dv = v.shape[-1]
    s0 = jnp.zeros((b, h, dk, dv), jnp.float32)
    xs = tuple(jnp.moveaxis(z, 2, 0) for z in (q, k, u, w, g, attn))
    _, o = lax.scan(step, s0, xs)
    return from_scan(o)


def retention_chunked(q, k, v, log_gamma):
    c = q.shape[-2]
    pos = jnp.arange(c, dtype=jnp.float32)
    dist = pos[:, None] - pos[None, :]
    dmat = jnp.exp(jnp.where(dist >= 0, dist * log_gamma[:, None, None], -jnp.inf))
    scores = jnp.einsum('bhnid,bhnjd->bhnij', q, k) * dmat[:, None]
    intra = jnp.einsum('bhnij,bhnje->bhnie', scores, v)
    xi = jnp.exp((pos + 1.0) * log_gamma[:, None])[:, :, None]
    zeta = jnp.exp((c - 1.0 - pos) * log_gamma[:, None])[:, :, None]
    gamma_c = jnp.exp(c * log_gamma)[:, None, None]

    def step(state, xs):
        q_c, k_c, v_c = xs
        o_c = jnp.einsum('bhcd,bhde->bhce', q_c, state) * xi
        state = state * gamma_c + jnp.einsum('bhcd,bhce->bhde', k_c * zeta, v_c)
        return state, o_c

    b, h, _, _, dk = q.shape
    dv = v.shape[-1]
    s0 = jnp.zeros((b, h, dk, dv), jnp.float32)
    xs = tuple(jnp.moveaxis(z, 2, 0) for z in (q, k, v))
    _, inter = lax.scan(step, s0, xs)
    return from_scan(jnp.moveaxis(intra, 2, 0) + inter)


def setup_inputs(seed: int = 0) -> dict:
    key = jax.random.key(seed)
    ks = jax.random.split(key, 16)
    f32 = jnp.float32
    x = jax.random.normal(ks[0], (BATCH, SEQ, D_MODEL), f32)
    norm1_w = 1.0 + 0.02 * jax.random.normal(ks[1], (DEPTH, D_MODEL), f32)
    w_in = jax.random.normal(ks[2], (DEPTH, D_MODEL, N_IN), f32) * D_MODEL ** -0.5
    conv_w = jax.random.normal(ks[3], (DEPTH, CONV_WIDTH, 2 * GDN_QK + GDN_V), f32) * CONV_WIDTH ** -0.5
    a_log = jnp.log(jax.random.uniform(ks[4], (DEPTH, GDN_HEADS), f32, 1.0, 16.0))
    dt = jnp.exp(jax.random.uniform(ks[5], (DEPTH, GDN_HEADS), f32, float(np.log(1e-3)), float(np.log(1e-1))))
    dt_bias = dt + jnp.log(-jnp.expm1(-dt))
    gdn_norm_w = 1.0 + 0.02 * jax.random.normal(ks[6], (DEPTH, GDN_DV), f32)
    ret_norm_w = 1.0 + 0.02 * jax.random.normal(ks[7], (DEPTH, RET_V), f32)
    w_out = jax.random.normal(ks[8], (DEPTH, D_MODEL, D_MODEL), f32) * D_MODEL ** -0.5
    norm2_w = 1.0 + 0.02 * jax.random.normal(ks[9], (DEPTH, D_MODEL), f32)
    w_gate = jax.random.normal(ks[10], (DEPTH, D_MODEL, D_FF), f32) * D_MODEL ** -0.5
    w_up = jax.random.normal(ks[11], (DEPTH, D_MODEL, D_FF), f32) * D_MODEL ** -0.5
    w_down = jax.random.normal(ks[12], (DEPTH, D_FF, D_MODEL), f32) * D_FF ** -0.5
    norm_f_w = 1.0 + 0.02 * jax.random.normal(ks[13], (D_MODEL,), f32)
    return {"x": x, "norm1_w": norm1_w, "w_in": w_in, "conv_w": conv_w, "a_log": a_log,
            "dt_bias": dt_bias, "gdn_norm_w": gdn_norm_w, "ret_norm_w": ret_norm_w,
            "w_out": w_out, "norm2_w": norm2_w, "w_gate": w_gate, "w_up": w_up,
            "w_down": w_down, "norm_f_w": norm_f_w}


def reference(x, norm1_w, w_in, conv_w, a_log, dt_bias, gdn_norm_w, ret_norm_w,
              w_out, norm2_w, w_gate, w_up, w_down, norm_f_w):
    b, s, _ = x.shape
    f32 = jnp.float32
    log_gamma = jnp.log1p(-jnp.exp2(-5.0 - jnp.arange(RET_HEADS, dtype=f32)))
    h = x
    for l in range(DEPTH):
        u = rmsnorm(h, norm1_w[l])
        proj = u @ w_in[l]
        a_qkv, a_z, a_b, a_a, r_q, r_k, r_v, r_g, gate_a, gate_b = split_cols(proj)

        qkv = jax.nn.silu(causal_conv(a_qkv, conv_w[l])).astype(f32)
        q_a = l2norm(qkv[..., :GDN_QK].reshape(b, s, GDN_HEADS, GDN_DK)) * GDN_DK ** -0.5
        k_a = l2norm(qkv[..., GDN_QK:2 * GDN_QK].reshape(b, s, GDN_HEADS, GDN_DK))
        v_a = qkv[..., 2 * GDN_QK:].reshape(b, s, GDN_HEADS, GDN_DV)
        beta = jax.nn.sigmoid(a_b.astype(f32))
        g = -jnp.exp(a_log[l].astype(f32)) * jax.nn.softplus(a_a.astype(f32) + dt_bias[l].astype(f32))
        o_a = gated_delta_rule(to_chunks(q_a), to_chunks(k_a), to_chunks(v_a),
                               to_chunks_scalar(beta), to_chunks_scalar(g))
        o_a = rmsnorm(o_a, gdn_norm_w[l].astype(f32)) * jax.nn.silu(a_z.astype(f32).reshape(b, s, GDN_HEADS, GDN_DV))
        o_a = o_a.reshape(b, s, GDN_V).astype(x.dtype)

        q_b = rotary(r_q.astype(f32).reshape(b, s, RET_HEADS, RET_DK))
        k_b = rotary(r_k.astype(f32).reshape(b, s, RET_HEADS, RET_DK)) * RET_DK ** -0.5
        v_b = r_v.astype(f32).reshape(b, s, RET_HEADS, RET_DV)
        o_b = retention_chunked(to_chunks(q_b), to_chunks(k_b), to_chunks(v_b), log_gamma)
        o_b = group_norm_heads(o_b, ret_norm_w[l]).reshape(b, s, RET_V) * jax.nn.silu(r_g.astype(f32))
        o_b = o_b.astype(x.dtype)

        mixed = jax.nn.sigmoid(gate_a) * o_a + jax.nn.sigmoid(gate_b) * o_b
        h = h + mixed @ w_out[l]

        hn = rmsnorm(h, norm2_w[l])
        h = h + (jax.nn.silu(hn @ w_gate[l]) * (hn @ w_up[l])) @ w_down[l]
    return rmsnorm(h, norm_f_w)
```

```python
import functools

import jax
import jax.numpy as jnp
import numpy as np
from jax import lax
from jax.experimental import pallas as pl
from jax.experimental.pallas import tpu as pltpu

F32 = jnp.float32
BF16 = jnp.bfloat16

EPS = 1e-6
CONV_WIDTH = 4
GDN_CHUNK = 64
ROPE_BASE = 10000.0
HALO_ROWS = 16
VMEM_LIMIT_BYTES = 56 * 1024 * 1024


def _cparams(sem):
    return pltpu.CompilerParams(dimension_semantics=sem, vmem_limit_bytes=VMEM_LIMIT_BYTES)


def _sigmoid(x):
    return 1.0 / (1.0 + jnp.exp(-x))


def _dot(a, b):
    return jnp.dot(a, b, preferred_element_type=F32)


def _dot_nt(a, b):
    return lax.dot_general(a, b, (((1,), (1,)), ((), ())), preferred_element_type=F32)


def _dot_tn(a, b):
    return lax.dot_general(a, b, (((0,), (0,)), ((), ())), preferred_element_type=F32)


def _inproj_kernel(x_ref, nw_ref, w_ref, o_ref, u_ref):
    @pl.when(pl.program_id(1) == 0)
    def _():
        xf = x_ref[...]
        ms = jnp.mean(xf * xf, axis=-1, keepdims=True)
        u_ref[...] = ((xf * lax.rsqrt(ms + EPS)) * nw_ref[...]).astype(u_ref.dtype)

    o_ref[...] = _dot(u_ref[...], w_ref[...]).astype(o_ref.dtype)


def _inproj(x2, nw, w, tm, tn):
    m, d = x2.shape
    n = w.shape[1]
    return pl.pallas_call(
        _inproj_kernel,
        grid=(m // tm, n // tn),
        in_specs=[
            pl.BlockSpec((tm, d), lambda i, j: (i, 0)),
            pl.BlockSpec((1, d), lambda i, j: (0, 0)),
            pl.BlockSpec((d, tn), lambda i, j: (0, j)),
        ],
        out_specs=[
            pl.BlockSpec((tm, tn), lambda i, j: (i, j)),
            pl.BlockSpec((tm, d), lambda i, j: (i, 0)),
        ],
        out_shape=[jax.ShapeDtypeStruct((m, n), BF16), jax.ShapeDtypeStruct((m, d), BF16)],
        compiler_params=_cparams(("parallel", "arbitrary")),
        name="inproj",
    )(x2, nw, w)


def _scalars_kernel(u_ref, wb_ref, wa_ref, alog_ref, dtb_ref, beta_ref, gc_ref, gl_ref, *, chunk):
    u = u_ref[...]
    b = _dot(u, wb_ref[...])
    a = _dot(u, wa_ref[...]) + dtb_ref[...]
    softplus = jnp.maximum(a, 0.0) + jnp.log1p(jnp.exp(-jnp.abs(a)))
    g = -jnp.exp(alog_ref[...]) * softplus
    tm = g.shape[0]
    row = lax.broadcasted_iota(jnp.int32, (tm, tm), 0)
    col = lax.broadcasted_iota(jnp.int32, (tm, tm), 1)
    same = (row // chunk) == (col // chunk)
    incl = jnp.where(same & (row >= col), 1.0, 0.0).astype(F32)
    full = jnp.where(same, 1.0, 0.0).astype(F32)
    beta_ref[...] = _sigmoid(b)
    gc_ref[...] = jnp.dot(incl, g, preferred_element_type=F32, precision=lax.Precision.HIGHEST)
    gl_ref[...] = jnp.dot(full, g, preferred_element_type=F32, precision=lax.Precision.HIGHEST)


def _scalars(u, wb, wa, alog, dtb, tm, chunk):
    m, d = u.shape
    h = wb.shape[1]
    out = jax.ShapeDtypeStruct((m, h), F32)
    return pl.pallas_call(
        functools.partial(_scalars_kernel, chunk=chunk),
        grid=(m // tm,),
        in_specs=[
            pl.BlockSpec((tm, d), lambda i: (i, 0)),
            pl.BlockSpec((d, h), lambda i: (0, 0)),
            pl.BlockSpec((d, h), lambda i: (0, 0)),
            pl.BlockSpec((1, h), lambda i: (0, 0)),
            pl.BlockSpec((1, h), lambda i: (0, 0)),
        ],
        out_specs=[pl.BlockSpec((tm, h), lambda i: (i, 0))] * 3,
        out_shape=[out, out, out],
        compiler_params=_cparams(("parallel",)),
        name="gate_scalars",
    )(u, wb, wa, alog, dtb)


def _gdn_prep_kernel(q_ref, qh_ref, k_ref, kh_ref, v_ref, vh_ref, cwq_ref, cwk_ref, cwv_ref,
                     beta_ref, gcc_ref, glc_ref, gcr_ref,
                     wq_ref, kpt_ref, u_ref, attn_ref, egl_ref,
                     extk_scr, extv_scr, *, dk, dv, chunk, nchunks):
    t = pl.program_id(2)
    rows = chunk * nchunks

    def conv_silu(x_ref, h_ref, cw_ref, ext_scr):
        halo = jnp.where(t == 0, 0.0, h_ref[...].astype(F32))
        ext_scr[0:HALO_ROWS, :] = halo
        ext_scr[HALO_ROWS:HALO_ROWS + rows, :] = x_ref[...].astype(F32)
        acc = ext_scr[HALO_ROWS:HALO_ROWS + rows, :] * cw_ref[CONV_WIDTH - 1:CONV_WIDTH, :]
        for j in range(1, CONV_WIDTH):
            acc = acc + (ext_scr[HALO_ROWS - j:HALO_ROWS - j + rows, :]
                         * cw_ref[CONV_WIDTH - 1 - j:CONV_WIDTH - j, :])
        return acc * _sigmoid(acc)

    def l2norm(y):
        return y * lax.rsqrt(jnp.sum(y * y, axis=-1, keepdims=True) + EPS)

    q = l2norm(conv_silu(q_ref, qh_ref, cwq_ref, extk_scr)) * (dk ** -0.5)
    k = l2norm(conv_silu(k_ref, kh_ref, cwk_ref, extk_scr))
    v = conv_silu(v_ref, vh_ref, cwv_ref, extv_scr)

    ri = lax.broadcasted_iota(jnp.int32, (chunk, chunk), 0)
    ci = lax.broadcasted_iota(jnp.int32, (chunk, chunk), 1)
    incl = ri >= ci
    strict = ri > ci
    eye = jnp.where(ri == ci, 1.0, 0.0).astype(F32)
    ident_k = jnp.where(lax.broadcasted_iota(jnp.int32, (dk, dk), 0)
                        == lax.broadcasted_iota(jnp.int32, (dk, dk), 1), 1.0, 0.0).astype(BF16)

    for c in range(nchunks):
        sl = slice(c * chunk, (c + 1) * chunk)
        qc, kc, vc = q[sl], k[sl], v[sl]
        bc = beta_ref[0, 0, sl, :]
        gc = gcc_ref[0, 0, sl, :]
        gl = glc_ref[0, 0, sl, :]
        grow = gcr_ref[0, 0, c]
        decay = jnp.exp(jnp.where(incl, gc - grow, -1e30))
        eg = jnp.exp(gc)
        kb = kc * bc
        kcb = kc.astype(BF16)
        a = jnp.where(strict, _dot_nt(kb.astype(BF16), kcb) * decay, 0.0)
        attn = _dot_nt(qc.astype(BF16), kcb) * decay
        tinv = eye - a
        apow = a
        span = 2
        while span < chunk:
            ab = apow.astype(BF16)
            apow = _dot(ab, ab)
            tinv = tinv + _dot(tinv.astype(BF16), apow.astype(BF16))
            span *= 2
        tb = tinv.astype(BF16)
        w = _dot(tb, (kb * eg).astype(BF16))
        u = _dot(tb, (vc * bc).astype(BF16))
        kp = (kc * jnp.exp(gl - gc)).astype(BF16)
        wq_ref[0, 0, c, 0:chunk, :] = w.astype(wq_ref.dtype)
        wq_ref[0, 0, c, chunk:2 * chunk, :] = (qc * eg).astype(wq_ref.dtype)
        kpt_ref[0, 0, c] = _dot_nt(ident_k, kp).astype(kpt_ref.dtype)
        u_ref[0, 0, c] = u
        attn_ref[0, 0, c] = attn.astype(attn_ref.dtype)
        egl_ref[0, 0, c] = jnp.broadcast_to(jnp.exp(gl[0:1, :]), (1, dv))


def _gdn_prep(proj, conv_w, beta_c, gc_c, gl_c, gc_r, dims, rows):
    b, s, h, dk, dv = dims
    chunk = GDN_CHUNK
    nchunks = rows // chunk
    n = s // chunk
    qk = h * dk
    tiles = s // rows
    kcol = qk // dk
    vcol = 2 * qk // dv

    def rowblk(bi, ti):
        return bi * tiles + ti

    def halo(bi, ti):
        return jnp.maximum((bi * s + ti * rows) // HALO_ROWS - 1, 0)

    col_spec = pl.BlockSpec((1, 1, rows, 1), lambda bi, hi, ti: (bi, hi, ti, 0))
    in_specs = [
        pl.BlockSpec((rows, dk), lambda bi, hi, ti: (rowblk(bi, ti), hi)),
        pl.BlockSpec((HALO_ROWS, dk), lambda bi, hi, ti: (halo(bi, ti), hi)),
        pl.BlockSpec((rows, dk), lambda bi, hi, ti: (rowblk(bi, ti), kcol + hi)),
        pl.BlockSpec((HALO_ROWS, dk), lambda bi, hi, ti: (halo(bi, ti), kcol + hi)),
        pl.BlockSpec((rows, dv), lambda bi, hi, ti: (rowblk(bi, ti), vcol + hi)),
        pl.BlockSpec((HALO_ROWS, dv), lambda bi, hi, ti: (halo(bi, ti), vcol + hi)),
        pl.BlockSpec((CONV_WIDTH, dk), lambda bi, hi, ti: (0, hi)),
        pl.BlockSpec((CONV_WIDTH, dk), lambda bi, hi, ti: (0, kcol + hi)),
        pl.BlockSpec((CONV_WIDTH, dv), lambda bi, hi, ti: (0, vcol + hi)),
        col_spec, col_spec, col_spec,
        pl.BlockSpec((1, 1, nchunks, 1, chunk), lambda bi, hi, ti: (bi, hi, ti, 0, 0)),
    ]

    def out5(r, c):
        return pl.BlockSpec((1, 1, nchunks, r, c), lambda bi, hi, ti: (bi, hi, ti, 0, 0))

    out_specs = [out5(2 * chunk, dk), out5(dk, chunk), out5(chunk, dv), out5(chunk, chunk), out5(1, dv)]
    out_shape = [
        jax.ShapeDtypeStruct((b, h, n, 2 * chunk, dk), BF16),
        jax.ShapeDtypeStruct((b, h, n, dk, chunk), BF16),
        jax.ShapeDtypeStruct((b, h, n, chunk, dv), F32),
        jax.ShapeDtypeStruct((b, h, n, chunk, chunk), BF16),
        jax.ShapeDtypeStruct((b, h, n, 1, dv), F32),
    ]
    return pl.pallas_call(
        functools.partial(_gdn_prep_kernel, dk=dk, dv=dv, chunk=chunk, nchunks=nchunks),
        grid=(b, h, tiles),
        in_specs=in_specs,
        out_specs=out_specs,
        out_shape=out_shape,
        scratch_shapes=[pltpu.VMEM((HALO_ROWS + rows, dk), F32), pltpu.VMEM((HALO_ROWS + rows, dv), F32)],
        compiler_params=_cparams(("parallel", "parallel", "parallel")),
        name="gdn_prep",
    )(proj, proj, proj, proj, proj, proj, conv_w, conv_w, conv_w, beta_c, gc_c, gl_c, gc_r)


def _gdn_scan_kernel(wq_ref, kpt_ref, u_ref, attn_ref, egl_ref, z_ref, ga_ref, nw_ref, o_ref, s_scr,
                     *, dv, chunk, nchunks, heads):
    @pl.when(pl.program_id(2) == 0)
    def _():
        s_scr[...] = jnp.zeros_like(s_scr)

    nw = nw_ref[...]
    for c in range(nchunks):
        rs = slice(c * chunk, (c + 1) * chunk)
        for hh in range(heads):
            cs = slice(hh * dv, (hh + 1) * dv)
            state = s_scr[hh]
            r = _dot(wq_ref[0, hh, c], state.astype(BF16))
            v_new = (u_ref[0, hh, c] - r[0:chunk]).astype(BF16)
            o = r[chunk:2 * chunk] + _dot(attn_ref[0, hh, c], v_new)
            s_scr[hh] = state * egl_ref[0, hh, c] + _dot(kpt_ref[0, hh, c], v_new)
            on = o * lax.rsqrt(jnp.mean(o * o, axis=-1, keepdims=True) + EPS) * nw
            z = z_ref[rs, cs].astype(F32)
            ga = ga_ref[rs, cs].astype(F32)
            o_ref[rs, cs] = (on * (z * _sigmoid(z)) * _sigmoid(ga)).astype(o_ref.dtype)


def _gdn_scan(wq, kpt, u, attn, egl, proj, gdn_nw, dims, nchunks, heads, zcol, gacol):
    b, s, h, dk, dv = dims
    chunk = GDN_CHUNK
    n = s // chunk
    rows = nchunks * chunk
    steps = n // nchunks
    width = heads * dv

    def in5(r, c):
        return pl.BlockSpec((1, heads, nchunks, r, c), lambda bi, gi, i: (bi, gi, i, 0, 0))

    in_specs = [
        in5(2 * chunk, dk), in5(dk, chunk), in5(chunk, dv), in5(chunk, chunk), in5(1, dv),
        pl.BlockSpec((rows, width), lambda bi, gi, i: (bi * steps + i, zcol // width + gi)),
        pl.BlockSpec((rows, width), lambda bi, gi, i: (bi * steps + i, gacol // width + gi)),
        pl.BlockSpec((1, dv), lambda bi, gi, i: (0, 0)),
    ]
    return pl.pallas_call(
        functools.partial(_gdn_scan_kernel, dv=dv, chunk=chunk, nchunks=nchunks, heads=heads),
        grid=(b, h // heads, steps),
        in_specs=in_specs,
        out_specs=pl.BlockSpec((rows, width), lambda bi, gi, i: (bi * steps + i, gi)),
        out_shape=jax.ShapeDtypeStruct((b * s, h * dv), BF16),
        scratch_shapes=[pltpu.VMEM((heads, dk, dv), F32)],
        compiler_params=_cparams(("parallel", "parallel", "arbitrary")),
        name="gdn_scan",
    )(wq, kpt, u, attn, egl, proj, proj, gdn_nw)


def _retention_kernel(q_ref, k_ref, v_ref, rg_ref, gb_ref, ma_ref, cos_ref, sin_ref, dmat_ref,
                      xi_ref, zeta_ref, gpow_ref, nw_ref, o_ref, s_scr, *, dk):
    @pl.when(pl.program_id(2) == 0)
    def _():
        s_scr[...] = jnp.zeros_like(s_scr)

    cos = cos_ref[...]
    sin = sin_ref[...]

    def rotary(x):
        return x * cos + pltpu.roll(x, dk // 2, 1) * sin

    q = rotary(q_ref[...].astype(F32))
    k = rotary(k_ref[...].astype(F32)) * (dk ** -0.5)
    v = v_ref[...]
    qb = q.astype(BF16)
    state = s_scr[...]
    scores = _dot_nt(qb, k.astype(BF16)) * dmat_ref[0]
    o = _dot(scores.astype(BF16), v) + _dot(qb, state.astype(BF16)) * xi_ref[0]
    s_scr[...] = state * gpow_ref[0] + _dot_tn((k * zeta_ref[0]).astype(BF16), v)
    mu = jnp.mean(o, axis=-1, keepdims=True)
    dev = o - mu
    var = jnp.mean(dev * dev, axis=-1, keepdims=True)
    ob = dev * lax.rsqrt(var + EPS) * nw_ref[...]
    rg = rg_ref[...].astype(F32)
    ob = ob * (rg * _sigmoid(rg))
    mixed = _sigmoid(gb_ref[...].astype(F32)) * ob + ma_ref[...].astype(F32)
    o_ref[...] = mixed.astype(o_ref.dtype)


def _retention(proj, ma, ret_nw, tables, dims, rows, cols):
    b, s, h, dk, dv = dims
    cosf, sins, dmat, xi, zeta, gpow = tables
    rqcol, rkcol, rvcol, rgcol, gbcol = cols
    steps = s // rows

    def rb(bi, i):
        return bi * steps + i

    in_specs = [
        pl.BlockSpec((rows, dk), lambda bi, hi, i: (rb(bi, i), rqcol // dk + hi)),
        pl.BlockSpec((rows, dk), lambda bi, hi, i: (rb(bi, i), rkcol // dk + hi)),
        pl.BlockSpec((rows, dv), lambda bi, hi, i: (rb(bi, i), rvcol // dv + hi)),
        pl.BlockSpec((rows, dv), lambda bi, hi, i: (rb(bi, i), rgcol // dv + hi)),
        pl.BlockSpec((rows, dv), lambda bi, hi, i: (rb(bi, i), gbcol // dv + hi)),
        pl.BlockSpec((rows, dv), lambda bi, hi, i: (rb(bi, i), hi)),
        pl.BlockSpec((rows, dk), lambda bi, hi, i: (i, 0)),
        pl.BlockSpec((rows, dk), lambda bi, hi, i: (i, 0)),
        pl.BlockSpec((1, rows, rows), lambda bi, hi, i: (hi, 0, 0)),
        pl.BlockSpec((1, rows, 1), lambda bi, hi, i: (hi, 0, 0)),
        pl.BlockSpec((1, rows, 1), lambda bi, hi, i: (hi, 0, 0)),
        pl.BlockSpec((1, 1, dv), lambda bi, hi, i: (hi, 0, 0)),
        pl.BlockSpec((1, dv), lambda bi, hi, i: (0, hi)),
    ]
    return pl.pallas_call(
        functools.partial(_retention_kernel, dk=dk),
        grid=(b, h, steps),
        in_specs=in_specs,
        out_specs=pl.BlockSpec((rows, dv), lambda bi, hi, i: (rb(bi, i), hi)),
        out_shape=jax.ShapeDtypeStruct((b * s, h * dv), BF16),
        scratch_shapes=[pltpu.VMEM((dk, dv), F32)],
        compiler_params=_cparams(("parallel", "parallel", "arbitrary")),
        name="retention",
    )(proj, proj, proj, proj, proj, ma, cosf, sins, dmat, xi, zeta, gpow, ret_nw)


def _retention_tables(s, h, dk, dv, rows):
    inv = ROPE_BASE ** (-jnp.arange(0, dk, 2, dtype=F32) / dk)
    ang = jnp.arange(s, dtype=F32)[:, None] * inv[None, :]
    cos, sin = jnp.cos(ang), jnp.sin(ang)
    cosf = jnp.concatenate([cos, cos], axis=-1)
    sins = jnp.concatenate([-sin, sin], axis=-1)
    log_gamma = jnp.log1p(-jnp.exp2(-5.0 - jnp.arange(h, dtype=F32)))
    pos = jnp.arange(rows, dtype=F32)
    dist = pos[:, None] - pos[None, :]
    dmat = jnp.exp(jnp.where(dist >= 0, dist * log_gamma[:, None, None], -jnp.inf))
    xi = jnp.exp((pos + 1.0) * log_gamma[:, None])[:, :, None]
    zeta = jnp.exp((rows - 1.0 - pos) * log_gamma[:, None])[:, :, None]
    gpow = jnp.broadcast_to(jnp.exp(rows * log_gamma)[:, None, None], (h, 1, dv))
    return cosf, sins, dmat, xi, zeta, gpow


def _outproj_kernel(m_ref, w_ref, x_ref, nw_ref, h_ref, hn_ref):
    hres = x_ref[...] + _dot(m_ref[...], w_ref[...])
    h_ref[...] = hres
    ms = jnp.mean(hres * hres, axis=-1, keepdims=True)
    hn_ref[...] = ((hres * lax.rsqrt(ms + EPS)) * nw_ref[...]).astype(hn_ref.dtype)


def _outproj(mixed, w, x2, nw, tm):
    m, d = x2.shape
    kdim = mixed.shape[1]
    return pl.pallas_call(
        _outproj_kernel,
        grid=(m // tm,),
        in_specs=[
            pl.BlockSpec((tm, kdim), lambda i: (i, 0)),
            pl.BlockSpec((kdim, d), lambda i: (0, 0)),
            pl.BlockSpec((tm, d), lambda i: (i, 0)),
            pl.BlockSpec((1, d), lambda i: (0, 0)),
        ],
        out_specs=[pl.BlockSpec((tm, d), lambda i: (i, 0))] * 2,
        out_shape=[jax.ShapeDtypeStruct((m, d), F32), jax.ShapeDtypeStruct((m, d), BF16)],
        compiler_params=_cparams(("parallel",)),
        name="outproj",
    )(mixed, w, x2, nw)


def _ffn_kernel(hn_ref, h_ref, wg_ref, wu_ref, wd_ref, nw_ref, o_ref):
    f = pl.program_id(1)

    @pl.when(f == 0)
    def _():
        o_ref[...] = h_ref[...]

    hn = hn_ref[...]
    g = _dot(hn, wg_ref[...])
    up = _dot(hn, wu_ref[...])
    act = (g * _sigmoid(g) * up).astype(BF16)
    o_ref[...] += _dot(act, wd_ref[...])

    @pl.when(f == pl.num_programs(1) - 1)
    def _():
        hres = o_ref[...]
        ms = jnp.mean(hres * hres, axis=-1, keepdims=True)
        o_ref[...] = (hres * lax.rsqrt(ms + EPS)) * nw_ref[...]


def _ffn(hn, hres, wg, wu, wd, nw, tm, tf):
    m, d = hres.shape
    ff = wg.shape[1]
    return pl.pallas_call(
        _ffn_kernel,
        grid=(m // tm, ff // tf),
        in_specs=[
            pl.BlockSpec((tm, d), lambda i, j: (i, 0)),
            pl.BlockSpec((tm, d), lambda i, j: (i, 0)),
            pl.BlockSpec((d, tf), lambda i, j: (0, j)),
            pl.BlockSpec((d, tf), lambda i, j: (0, j)),
            pl.BlockSpec((tf, d), lambda i, j: (j, 0)),
            pl.BlockSpec((1, d), lambda i, j: (0, 0)),
        ],
        out_specs=pl.BlockSpec((tm, d), lambda i, j: (i, 0)),
        out_shape=jax.ShapeDtypeStruct((m, d), F32),
        compiler_params=_cparams(("parallel", "arbitrary")),
        name="ffn",
    )(hn, hres, wg, wu, wd, nw)


def _pick(total, want):
    t = min(total, want)
    while total % t:
        t //= 2
    return t


def _layer(h2, dims, norm1_w, w_in, conv_w, a_log, dt_bias, gdn_norm_w, ret_norm_w,
           w_out, norm2_w, w_gate, w_up, w_down, out_norm_w):
    b, s, h, dk, dv = dims
    m, d = h2.shape
    qk, vd = h * dk, h * dv
    chunk = GDN_CHUNK
    n = s // chunk

    sc0 = 2 * qk + 2 * vd
    w_main = jnp.concatenate([w_in[:, :sc0], w_in[:, sc0 + 2 * h:]], axis=1).astype(BF16)
    w_b = w_in[:, sc0:sc0 + h].astype(BF16)
    w_a = w_in[:, sc0 + h:sc0 + 2 * h].astype(BF16)
    zcol = 2 * qk + vd
    rqcol = zcol + vd
    rkcol = rqcol + qk
    rvcol = rkcol + qk
    rgcol = rvcol + vd
    gacol = rgcol + vd
    gbcol = gacol + vd

    proj, u = _inproj(h2, norm1_w.reshape(1, d), w_main, _pick(m, 1024), _pick(w_main.shape[1], 1024))

    beta, gcum, gtot = _scalars(u, w_b, w_a, a_log.reshape(1, h), dt_bias.reshape(1, h),
                                _pick(m, 512), chunk)

    def per_head_col(t):
        return t.reshape(b, s, h).transpose(0, 2, 1)[..., None]

    gc_r = gcum.reshape(b, n, chunk, h).transpose(0, 3, 1, 2)[:, :, :, None, :]
    prep_rows = _pick(s, 256)
    wq, kpt, uu, attn, egl = _gdn_prep(proj, conv_w, per_head_col(beta), per_head_col(gcum),
                                        per_head_col(gtot), gc_r, dims, prep_rows)

    scan_heads = _pick(h, 4)
    ma = _gdn_scan(wq, kpt, uu, attn, egl, proj, gdn_norm_w.reshape(1, dv), dims,
                   _pick(n, 4), scan_heads, zcol, gacol)

    ret_rows = _pick(s, 256)
    tables = _retention_tables(s, h, dk, dv, ret_rows)
    mixed = _retention(proj, ma, ret_norm_w.reshape(1, vd), tables, dims, ret_rows,
                       (rqcol, rkcol, rvcol, rgcol, gbcol))

    hres, hn = _outproj(mixed, w_out.astype(BF16), h2, norm2_w.reshape(1, d), _pick(m, 256))

    return _ffn(hn, hres, w_gate.astype(BF16), w_up.astype(BF16), w_down.astype(BF16),
                out_norm_w.reshape(1, d), _pick(m, 512), _pick(w_gate.shape[1], 512))


def kernel(x, norm1_w, w_in, conv_w, a_log, dt_bias, gdn_norm_w, ret_norm_w, w_out, norm2_w,
           w_gate, w_up, w_down, norm_f_w):
    b, s, d = x.shape
    depth, h = a_log.shape
    assert depth == 1, "the final norm is fused into the single layer's FFN kernel"
    dk = d // 16
    dv = d // h
    dims = (b, s, h, dk, dv)
    out = _layer(x.reshape(b * s, d), dims, norm1_w[0], w_in[0], conv_w[0], a_log[0], dt_bias[0],
                 gdn_norm_w[0], ret_norm_w[0], w_out[0], norm2_w[0], w_gate[0], w_up[0], w_down[0],
                 norm_f_w)
    return out.reshape(b, s, d)
```

```python
import functools

import jax
import jax.numpy as jnp
import numpy as np
from jax import lax
from jax.experimental import pallas as pl
from jax.experimental.pallas import tpu as pltpu

F32 = jnp.float32
BF16 = jnp.bfloat16

EPS = 1e-6
CONV_WIDTH = 4
GDN_CHUNK = 64
ROPE_BASE = 10000.0
HALO_ROWS = 16
VMEM_LIMIT_BYTES = 56 * 1024 * 1024


def _cparams(sem):
    return pltpu.CompilerParams(dimension_semantics=sem, vmem_limit_bytes=VMEM_LIMIT_BYTES)


def _sigmoid(x):
    return 1.0 / (1.0 + jnp.exp(-x))


def _dot(a, b):
    return jnp.dot(a, b, preferred_element_type=F32)


def _dot_nt(a, b):
    return lax.dot_general(a, b, (((1,), (1,)), ((), ())), preferred_element_type=F32)


def _dot_tn(a, b):
    return lax.dot_general(a, b, (((0,), (0,)), ((), ())), preferred_element_type=F32)


def _inproj_kernel(x_ref, nw_ref, w_ref, o_ref, u_ref):
    @pl.when(pl.program_id(1) == 0)
    def _():
        xf = x_ref[...]
        ms = jnp.mean(xf * xf, axis=-1, keepdims=True)
        u_ref[...] = ((xf * lax.rsqrt(ms + EPS)) * nw_ref[...]).astype(u_ref.dtype)

    o_ref[...] = _dot(u_ref[...], w_ref[...]).astype(o_ref.dtype)


def _inproj(x2, nw, w, tm, tn):
    m, d = x2.shape
    n = w.shape[1]
    return pl.pallas_call(
        _inproj_kernel,
        grid=(m // tm, n // tn),
        in_specs=[
            pl.BlockSpec((tm, d), lambda i, j: (i, 0)),
            pl.BlockSpec((1, d), lambda i, j: (0, 0)),
            pl.BlockSpec((d, tn), lambda i, j: (0, j)),
        ],
        out_specs=[
            pl.BlockSpec((tm, tn), lambda i, j: (i, j)),
            pl.BlockSpec((tm, d), lambda i, j: (i, 0)),
        ],
        out_shape=[jax.ShapeDtypeStruct((m, n), BF16), jax.ShapeDtypeStruct((m, d), BF16)],
        compiler_params=_cparams(("parallel", "arbitrary")),
        name="inproj",
    )(x2, nw, w)


def _scalars_kernel(u_ref, wb_ref, wa_ref, alog_ref, dtb_ref, beta_ref, gc_ref, gl_ref, *, chunk):
    u = u_ref[...]
    b = _dot(u, wb_ref[...])
    a = _dot(u, wa_ref[...]) + dtb_ref[...]
    softplus = jnp.maximum(a, 0.0) + jnp.log1p(jnp.exp(-jnp.abs(a)))
    g = -jnp.exp(alog_ref[...]) * softplus
    tm = g.shape[0]
    row = lax.broadcasted_iota(jnp.int32, (tm, tm), 0)
    col = lax.broadcasted_iota(jnp.int32, (tm, tm), 1)
    same = (row // chunk) == (col // chunk)
    incl = jnp.where(same & (row >= col), 1.0, 0.0).astype(F32)
    full = jnp.where(same, 1.0, 0.0).astype(F32)
    beta_ref[...] = _sigmoid(b)
    gc_ref[...] = jnp.dot(incl, g, preferred_element_type=F32, precision=lax.Precision.HIGHEST)
    gl_ref[...] = jnp.dot(full, g, preferred_element_type=F32, precision=lax.Precision.HIGHEST)


def _scalars(u, wb, wa, alog, dtb, tm, chunk):
    m, d = u.shape
    h = wb.shape[1]
    out = jax.ShapeDtypeStruct((m, h), F32)
    return pl.pallas_call(
        functools.partial(_scalars_kernel, chunk=chunk),
        grid=(m // tm,),
        in_specs=[
            pl.BlockSpec((tm, d), lambda i: (i, 0)),
            pl.BlockSpec((d, h), lambda i: (0, 0)),
            pl.BlockSpec((d, h), lambda i: (0, 0)),
            pl.BlockSpec((1, h), lambda i: (0, 0)),
            pl.BlockSpec((1, h), lambda i: (0, 0)),
        ],
        out_specs=[pl.BlockSpec((tm, h), lambda i: (i, 0))] * 3,
        out_shape=[out, out, out],
        compiler_params=_cparams(("parallel",)),
        name="gate_scalars",
    )(u, wb, wa, alog, dtb)


def _gdn_prep_kernel(q_ref, qh_ref, k_ref, kh_ref, v_ref, vh_ref, cwq_ref, cwk_ref, cwv_ref,
                     beta_ref, gcc_ref, glc_ref, gcr_ref,
                     wq_ref, kpt_ref, u_ref, attn_ref, egl_ref,
                     extk_scr, extv_scr, *, dk, dv, chunk, nchunks):
    t = pl.program_id(1)
    rows = chunk * nchunks
    nheads = beta_ref.shape[1]
    hsel = lax.broadcasted_iota(jnp.int32, (1, nheads), 1) == pl.program_id(2)

    def head_col(ref):
        return jnp.sum(jnp.where(hsel, ref[...], 0.0), axis=1, keepdims=True)

    def conv_silu(x_ref, h_ref, cw_ref, ext_scr):
        halo = jnp.where(t == 0, 0.0, h_ref[...].astype(F32))
        ext_scr[0:HALO_ROWS, :] = halo
        ext_scr[HALO_ROWS:HALO_ROWS + rows, :] = x_ref[...].astype(F32)
        acc = ext_scr[HALO_ROWS:HALO_ROWS + rows, :] * cw_ref[CONV_WIDTH - 1:CONV_WIDTH, :]
        for j in range(1, CONV_WIDTH):
            acc = acc + (ext_scr[HALO_ROWS - j:HALO_ROWS - j + rows, :]
                         * cw_ref[CONV_WIDTH - 1 - j:CONV_WIDTH - j, :])
        return acc * _sigmoid(acc)

    def l2norm(y):
        return y * lax.rsqrt(jnp.sum(y * y, axis=-1, keepdims=True) + EPS)

    q = l2norm(conv_silu(q_ref, qh_ref, cwq_ref, extk_scr)) * (dk ** -0.5)
    k = l2norm(conv_silu(k_ref, kh_ref, cwk_ref, extk_scr))
    v = conv_silu(v_ref, vh_ref, cwv_ref, extv_scr)

    ri = lax.broadcasted_iota(jnp.int32, (chunk, chunk), 0)
    ci = lax.broadcasted_iota(jnp.int32, (chunk, chunk), 1)
    incl = ri >= ci
    strict = ri > ci
    eye = jnp.where(ri == ci, 1.0, 0.0).astype(F32)
    ident_k = jnp.where(lax.broadcasted_iota(jnp.int32, (dk, dk), 0)
                        == lax.broadcasted_iota(jnp.int32, (dk, dk), 1), 1.0, 0.0).astype(BF16)

    cr = range(nchunks)
    sls = [slice(c * chunk, (c + 1) * chunk) for c in cr]
    qs, ks, vs = [q[sl] for sl in sls], [k[sl] for sl in sls], [v[sl] for sl in sls]
    beta, gcum, gtot = head_col(beta_ref), head_col(gcc_ref), head_col(glc_ref)
    bs = [beta[sl] for sl in sls]
    gcs = [gcum[sl] for sl in sls]
    gls = [gtot[sl] for sl in sls]
    decays = [jnp.exp(jnp.where(incl, gcs[c] - gcr_ref[0, 0, c], -1e30)) for c in cr]
    egs = [jnp.exp(gc) for gc in gcs]
    kbs = [ks[c] * bs[c] for c in cr]
    kq = [_dot_nt(jnp.concatenate([kbs[c], qs[c]], axis=0).astype(BF16), ks[c].astype(BF16)) for c in cr]
    amats = [jnp.where(strict, kq[c][0:chunk] * decays[c], 0.0) for c in cr]
    for c in cr:
        attn_ref[0, 0, c] = (kq[c][chunk:2 * chunk] * decays[c]).astype(attn_ref.dtype)
    tinvs = [eye - a for a in amats]
    apows = amats
    span = 2
    while span < chunk:
        apows = [_dot(ab, ab) for ab in [a.astype(BF16) for a in apows]]
        tinvs = [tinvs[c] + _dot(tinvs[c].astype(BF16), apows[c].astype(BF16)) for c in cr]
        span *= 2
    rhs = [jnp.concatenate([kbs[c] * egs[c], vs[c] * bs[c]], axis=1).astype(BF16) for c in cr]
    wu = [_dot(tinvs[c].astype(BF16), rhs[c]) for c in cr]
    kpt = [_dot_nt(ident_k, (ks[c] * jnp.exp(gls[c] - gcs[c])).astype(BF16)) for c in cr]
    for c in cr:
        wq_ref[0, 0, c, 0:chunk, :] = wu[c][:, 0:dk].astype(wq_ref.dtype)
        wq_ref[0, 0, c, chunk:2 * chunk, :] = (qs[c] * egs[c]).astype(wq_ref.dtype)
        kpt_ref[0, 0, c] = kpt[c].astype(kpt_ref.dtype)
        u_ref[0, 0, c] = wu[c][:, dk:dk + dv]
        egl_ref[0, 0, c] = jnp.broadcast_to(jnp.exp(gls[c][0:1, :]), (1, dv))


def _gdn_prep(proj, conv_w, beta_c, gc_c, gl_c, gc_r, dims, rows):
    b, s, h, dk, dv = dims
    chunk = GDN_CHUNK
    nchunks = rows // chunk
    n = s // chunk
    qk = h * dk
    tiles = s // rows
    kcol = qk // dk
    vcol = 2 * qk // dv

    def rowblk(bi, ti):
        return bi * tiles + ti

    def halo(bi, ti):
        return jnp.maximum((bi * s + ti * rows) // HALO_ROWS - 1, 0)

    col_spec = pl.BlockSpec((rows, h), lambda bi, ti, hi: (rowblk(bi, ti), 0))
    in_specs = [
        pl.BlockSpec((rows, dk), lambda bi, ti, hi: (rowblk(bi, ti), hi)),
        pl.BlockSpec((HALO_ROWS, dk), lambda bi, ti, hi: (halo(bi, ti), hi)),
        pl.BlockSpec((rows, dk), lambda bi, ti, hi: (rowblk(bi, ti), kcol + hi)),
        pl.BlockSpec((HALO_ROWS, dk), lambda bi, ti, hi: (halo(bi, ti), kcol + hi)),
        pl.BlockSpec((rows, dv), lambda bi, ti, hi: (rowblk(bi, ti), vcol + hi)),
        pl.BlockSpec((HALO_ROWS, dv), lambda bi, ti, hi: (halo(bi, ti), vcol + hi)),
        pl.BlockSpec((CONV_WIDTH, dk), lambda bi, ti, hi: (0, hi)),
        pl.BlockSpec((CONV_WIDTH, dk), lambda bi, ti, hi: (0, kcol + hi)),
        pl.BlockSpec((CONV_WIDTH, dv), lambda bi, ti, hi: (0, vcol + hi)),
        col_spec, col_spec, col_spec,
        pl.BlockSpec((1, 1, nchunks, 1, chunk), lambda bi, ti, hi: (bi, hi, ti, 0, 0)),
    ]

    def out5(r, c):
        return pl.BlockSpec((1, 1, nchunks, r, c), lambda bi, ti, hi: (bi, hi, ti, 0, 0))

    out_specs = [out5(2 * chunk, dk), out5(dk, chunk), out5(chunk, dv), out5(chunk, chunk), out5(1, dv)]
    out_shape = [
        jax.ShapeDtypeStruct((b, h, n, 2 * chunk, dk), BF16),
        jax.ShapeDtypeStruct((b, h, n, dk, chunk), BF16),
        jax.ShapeDtypeStruct((b, h, n, chunk, dv), F32),
        jax.ShapeDtypeStruct((b, h, n, chunk, chunk), BF16),
        jax.ShapeDtypeStruct((b, h, n, 1, dv), F32),
    ]
    return pl.pallas_call(
        functools.partial(_gdn_prep_kernel, dk=dk, dv=dv, chunk=chunk, nchunks=nchunks),
        grid=(b, tiles, h),
        in_specs=in_specs,
        out_specs=out_specs,
        out_shape=out_shape,
        scratch_shapes=[pltpu.VMEM((HALO_ROWS + rows, dk), F32), pltpu.VMEM((HALO_ROWS + rows, dv), F32)],
        compiler_params=_cparams(("parallel", "parallel", "parallel")),
        name="gdn_prep",
    )(proj, proj, proj, proj, proj, proj, conv_w, conv_w, conv_w, beta_c, gc_c, gl_c, gc_r)


def _gdn_scan_kernel(wq_ref, kpt_ref, u_ref, attn_ref, egl_ref, z_ref, ga_ref, nw_ref, o_ref, s_scr,
                     *, dv, chunk, nchunks, heads):
    @pl.when(pl.program_id(2) == 0)
    def _():
        s_scr[...] = jnp.zeros_like(s_scr)

    nw = nw_ref[...]
    hr = range(heads)
    for c in range(nchunks):
        rs = slice(c * chunk, (c + 1) * chunk)
        states = [s_scr[hh] for hh in hr]
        r = [_dot(wq_ref[0, hh, c], states[hh].astype(BF16)) for hh in hr]
        v_new = [(u_ref[0, hh, c] - r[hh][0:chunk]).astype(BF16) for hh in hr]
        upd = [_dot(kpt_ref[0, hh, c], v_new[hh]) for hh in hr]
        for hh in hr:
            s_scr[hh] = states[hh] * egl_ref[0, hh, c] + upd[hh]
        o = [r[hh][chunk:2 * chunk] + _dot(attn_ref[0, hh, c], v_new[hh]) for hh in hr]
        for hh in hr:
            cs = slice(hh * dv, (hh + 1) * dv)
            on = o[hh] * lax.rsqrt(jnp.mean(o[hh] * o[hh], axis=-1, keepdims=True) + EPS) * nw
            z = z_ref[rs, cs].astype(F32)
            ga = ga_ref[rs, cs].astype(F32)
            o_ref[rs, cs] = (on * (z * _sigmoid(z)) * _sigmoid(ga)).astype(o_ref.dtype)


def _gdn_scan(wq, kpt, u, attn, egl, proj, gdn_nw, dims, nchunks, heads, zcol, gacol):
    b, s, h, dk, dv = dims
    chunk = GDN_CHUNK
    n = s // chunk
    rows = nchunks * chunk
    steps = n // nchunks
    width = heads * dv

    def in5(r, c):
        return pl.BlockSpec((1, heads, nchunks, r, c), lambda bi, gi, i: (bi, gi, i, 0, 0))

    in_specs = [
        in5(2 * chunk, dk), in5(dk, chunk), in5(chunk, dv), in5(chunk, chunk), in5(1, dv),
        pl.BlockSpec((rows, width), lambda bi, gi, i: (bi * steps + i, zcol // width + gi)),
        pl.BlockSpec((rows, width), lambda bi, gi, i: (bi * steps + i, gacol // width + gi)),
        pl.BlockSpec((1, dv), lambda bi, gi, i: (0, 0)),
    ]
    return pl.pallas_call(
        functools.partial(_gdn_scan_kernel, dv=dv, chunk=chunk, nchunks=nchunks, heads=heads),
        grid=(b, h // heads, steps),
        in_specs=in_specs,
        out_specs=pl.BlockSpec((rows, width), lambda bi, gi, i: (bi * steps + i, gi)),
        out_shape=jax.ShapeDtypeStruct((b * s, h * dv), BF16),
        scratch_shapes=[pltpu.VMEM((heads, dk, dv), F32)],
        compiler_params=_cparams(("parallel", "parallel", "arbitrary")),
        name="gdn_scan",
    )(wq, kpt, u, attn, egl, proj, proj, gdn_nw)


def _retention_kernel(q_ref, k_ref, v_ref, rg_ref, gb_ref, ma_ref, cos_ref, sin_ref, dmat_ref,
                      xi_ref, zeta_ref, gpow_ref, nw_ref, o_ref, s_scr, *, dk, dv, heads):
    @pl.when(pl.program_id(2) == 0)
    def _():
        s_scr[...] = jnp.zeros_like(s_scr)

    cos = cos_ref[...]
    sin = sin_ref[...]

    def rotary(x):
        return x * cos + pltpu.roll(x, dk // 2, 1) * sin

    hr = range(heads)
    ksl = [slice(hh * dk, (hh + 1) * dk) for hh in hr]
    vsl = [slice(hh * dv, (hh + 1) * dv) for hh in hr]
    q = [rotary(q_ref[:, ksl[hh]].astype(F32)).astype(BF16) for hh in hr]
    k = [rotary(k_ref[:, ksl[hh]].astype(F32)) * (dk ** -0.5) for hh in hr]
    kz = [(k[hh] * zeta_ref[hh]).astype(BF16) for hh in hr]
    states = [s_scr[hh] for hh in hr]
    scores = [(_dot_nt(q[hh], k[hh].astype(BF16)) * dmat_ref[hh]).astype(BF16) for hh in hr]
    inter = [_dot(q[hh], states[hh].astype(BF16)) * xi_ref[hh] for hh in hr]
    upd = [_dot_tn(kz[hh], v_ref[:, vsl[hh]]) for hh in hr]
    for hh in hr:
        s_scr[hh] = states[hh] * gpow_ref[hh] + upd[hh]
    o = [_dot(scores[hh], v_ref[:, vsl[hh]]) + inter[hh] for hh in hr]
    for hh in hr:
        mu = jnp.mean(o[hh], axis=-1, keepdims=True)
        dev = o[hh] - mu
        var = jnp.mean(dev * dev, axis=-1, keepdims=True)
        ob = dev * lax.rsqrt(var + EPS) * nw_ref[:, vsl[hh]]
        rg = rg_ref[:, vsl[hh]].astype(F32)
        ob = ob * (rg * _sigmoid(rg))
        mixed = _sigmoid(gb_ref[:, vsl[hh]].astype(F32)) * ob + ma_ref[:, vsl[hh]].astype(F32)
        o_ref[:, vsl[hh]] = mixed.astype(o_ref.dtype)


def _retention(proj, ma, ret_nw, tables, dims, rows, heads, cols):
    b, s, h, dk, dv = dims
    cosf, sins, dmat, xi, zeta, gpow = tables
    rqcol, rkcol, rvcol, rgcol, gbcol = cols
    steps = s // rows
    kw, vw = heads * dk, heads * dv

    def rb(bi, i):
        return bi * steps + i

    in_specs = [
        pl.BlockSpec((rows, kw), lambda bi, gi, i: (rb(bi, i), rqcol // kw + gi)),
        pl.BlockSpec((rows, kw), lambda bi, gi, i: (rb(bi, i), rkcol // kw + gi)),
        pl.BlockSpec((rows, vw), lambda bi, gi, i: (rb(bi, i), rvcol // vw + gi)),
        pl.BlockSpec((rows, vw), lambda bi, gi, i: (rb(bi, i), rgcol // vw + gi)),
        pl.BlockSpec((rows, vw), lambda bi, gi, i: (rb(bi, i), gbcol // vw + gi)),
        pl.BlockSpec((rows, vw), lambda bi, gi, i: (rb(bi, i), gi)),
        pl.BlockSpec((rows, dk), lambda bi, gi, i: (i, 0)),
        pl.BlockSpec((rows, dk), lambda bi, gi, i: (i, 0)),
        pl.BlockSpec((heads, rows, rows), lambda bi, gi, i: (gi, 0, 0)),
        pl.BlockSpec((heads, rows, 1), lambda bi, gi, i: (gi, 0, 0)),
        pl.BlockSpec((heads, rows, 1), lambda bi, gi, i: (gi, 0, 0)),
        pl.BlockSpec((heads, 1, dv), lambda bi, gi, i: (gi, 0, 0)),
        pl.BlockSpec((1, vw), lambda bi, gi, i: (0, gi)),
    ]
    return pl.pallas_call(
        functools.partial(_retention_kernel, dk=dk, dv=dv, heads=heads),
        grid=(b, h // heads, steps),
        in_specs=in_specs,
        out_specs=pl.BlockSpec((rows, vw), lambda bi, gi, i: (rb(bi, i), gi)),
        out_shape=jax.ShapeDtypeStruct((b * s, h * dv), BF16),
        scratch_shapes=[pltpu.VMEM((heads, dk, dv), F32)],
        compiler_params=_cparams(("parallel", "parallel", "arbitrary")),
        name="retention",
    )(proj, proj, proj, proj, proj, ma, cosf, sins, dmat, xi, zeta, gpow, ret_nw)


def _retention_tables(s, h, dk, dv, rows):
    inv = ROPE_BASE ** (-jnp.arange(0, dk, 2, dtype=F32) / dk)
    ang = jnp.arange(s, dtype=F32)[:, None] * inv[None, :]
    cos, sin = jnp.cos(ang), jnp.sin(ang)
    cosf = jnp.concatenate([cos, cos], axis=-1)
    sins = jnp.concatenate([-sin, sin], axis=-1)
    log_gamma = jnp.log1p(-jnp.exp2(-5.0 - jnp.arange(h, dtype=F32)))
    pos = jnp.arange(rows, dtype=F32)
    dist = pos[:, None] - pos[None, :]
    dmat = jnp.exp(jnp.where(dist >= 0, dist * log_gamma[:, None, None], -jnp.inf))
    xi = jnp.exp((pos + 1.0) * log_gamma[:, None])[:, :, None]
    zeta = jnp.exp((rows - 1.0 - pos) * log_gamma[:, None])[:, :, None]
    gpow = jnp.broadcast_to(jnp.exp(rows * log_gamma)[:, None, None], (h, 1, dv))
    return cosf, sins, dmat, xi, zeta, gpow


def _outproj_kernel(m_ref, w_ref, x_ref, nw_ref, h_ref, hn_ref):
    hres = x_ref[...] + _dot(m_ref[...], w_ref[...])
    h_ref[...] = hres
    ms = jnp.mean(hres * hres, axis=-1, keepdims=True)
    hn_ref[...] = ((hres * lax.rsqrt(ms + EPS)) * nw_ref[...]).astype(hn_ref.dtype)


def _outproj(mixed, w, x2, nw, tm):
    m, d = x2.shape
    kdim = mixed.shape[1]
    return pl.pallas_call(
        _outproj_kernel,
        grid=(m // tm,),
        in_specs=[
            pl.BlockSpec((tm, kdim), lambda i: (i, 0)),
            pl.BlockSpec((kdim, d), lambda i: (0, 0)),
            pl.BlockSpec((tm, d), lambda i: (i, 0)),
            pl.BlockSpec((1, d), lambda i: (0, 0)),
        ],
        out_specs=[pl.BlockSpec((tm, d), lambda i: (i, 0))] * 2,
        out_shape=[jax.ShapeDtypeStruct((m, d), F32), jax.ShapeDtypeStruct((m, d), BF16)],
        compiler_params=_cparams(("parallel",)),
        name="outproj",
    )(mixed, w, x2, nw)


def _ffn_kernel(hn_ref, h_ref, wg_ref, wu_ref, wd_ref, nw_ref, o_ref):
    f = pl.program_id(1)

    @pl.when(f == 0)
    def _():
        o_ref[...] = h_ref[...]

    hn = hn_ref[...]
    g = _dot(hn, wg_ref[...])
    up = _dot(hn, wu_ref[...])
    act = (g * _sigmoid(g) * up).astype(BF16)
    o_ref[...] += _dot(act, wd_ref[...])

    @pl.when(f == pl.num_programs(1) - 1)
    def _():
        hres = o_ref[...]
        ms = jnp.mean(hres * hres, axis=-1, keepdims=True)
        o_ref[...] = (hres * lax.rsqrt(ms + EPS)) * nw_ref[...]


def _ffn(hn, hres, wg, wu, wd, nw, tm, tf):
    m, d = hres.shape
    ff = wg.shape[1]
    return pl.pallas_call(
        _ffn_kernel,
        grid=(m // tm, ff // tf),
        in_specs=[
            pl.BlockSpec((tm, d), lambda i, j: (i, 0)),
            pl.BlockSpec((tm, d), lambda i, j: (i, 0)),
            pl.BlockSpec((d, tf), lambda i, j: (0, j)),
            pl.BlockSpec((d, tf), lambda i, j: (0, j)),
            pl.BlockSpec((tf, d), lambda i, j: (j, 0)),
            pl.BlockSpec((1, d), lambda i, j: (0, 0)),
        ],
        out_specs=pl.BlockSpec((tm, d), lambda i, j: (i, 0)),
        out_shape=jax.ShapeDtypeStruct((m, d), F32),
        compiler_params=_cparams(("parallel", "arbitrary")),
        name="ffn",
    )(hn, hres, wg, wu, wd, nw)


def _pick(total, want):
    t = min(total, want)
    while total % t:
        t //= 2
    return t


def _layer(h2, dims, norm1_w, w_in, conv_w, a_log, dt_bias, gdn_norm_w, ret_norm_w,
           w_out, norm2_w, w_gate, w_up, w_down, out_norm_w):
    b, s, h, dk, dv = dims
    m, d = h2.shape
    qk, vd = h * dk, h * dv
    chunk = GDN_CHUNK
    n = s // chunk

    sc0 = 2 * qk + 2 * vd
    w_main = jnp.concatenate([w_in[:, :sc0], w_in[:, sc0 + 2 * h:]], axis=1).astype(BF16)
    w_b = w_in[:, sc0:sc0 + h].astype(BF16)
    w_a = w_in[:, sc0 + h:sc0 + 2 * h].astype(BF16)
    zcol = 2 * qk + vd
    rqcol = zcol + vd
    rkcol = rqcol + qk
    rvcol = rkcol + qk
    rgcol = rvcol + vd
    gacol = rgcol + vd
    gbcol = gacol + vd

    proj, u = _inproj(h2, norm1_w.reshape(1, d), w_main, _pick(m, 1024), _pick(w_main.shape[1], 1024))

    beta, gcum, gtot = _scalars(u, w_b, w_a, a_log.reshape(1, h), dt_bias.reshape(1, h),
                                _pick(m, 512), chunk)

    gc_r = gcum.reshape(b, n, chunk, h).transpose(0, 3, 1, 2)[:, :, :, None, :]
    prep_rows = _pick(s, 512)
    wq, kpt, uu, attn, egl = _gdn_prep(proj, conv_w, beta, gcum, gtot, gc_r, dims, prep_rows)

    scan_heads = _pick(h, 8)
    ma = _gdn_scan(wq, kpt, uu, attn, egl, proj, gdn_norm_w.reshape(1, dv), dims,
                   _pick(n, 2), scan_heads, zcol, gacol)

    ret_rows = _pick(s, 256)
    tables = _retention_tables(s, h, dk, dv, ret_rows)
    mixed = _retention(proj, ma, ret_norm_w.reshape(1, vd), tables, dims, ret_rows, _pick(h, 4),
                       (rqcol, rkcol, rvcol, rgcol, gbcol))

    hres, hn = _outproj(mixed, w_out.astype(BF16), h2, norm2_w.reshape(1, d), _pick(m, 256))

    return _ffn(hn, hres, w_gate.astype(BF16), w_up.astype(BF16), w_down.astype(BF16),
                out_norm_w.reshape(1, d), _pick(m, 512), _pick(w_gate.shape[1], 512))


def kernel(x, norm1_w, w_in, conv_w, a_log, dt_bias, gdn_norm_w, ret_norm_w, w_out, norm2_w,
           w_gate, w_up, w_down, norm_f_w):
    b, s, d = x.shape
    depth, h = a_log.shape
    assert depth == 1, "the final norm is fused into the single layer's FFN kernel"
    dk = d // 16
    dv = d // h
    dims = (b, s, h, dk, dv)
    out = _layer(x.reshape(b * s, d), dims, norm1_w[0], w_in[0], conv_w[0], a_log[0], dt_bias[0],
                 gdn_norm_w[0], ret_norm_w[0], w_out[0], norm2_w[0], w_gate[0], w_up[0], w_down[0],
                 norm_f_w)
    return out.reshape(b, s, d)
```

```python
import functools

import jax
import jax.numpy as jnp
import numpy as np
from jax import lax
from jax.experimental import pallas as pl
from jax.experimental.pallas import tpu as pltpu

F32 = jnp.float32
BF16 = jnp.bfloat16

EPS = 1e-6
CONV_WIDTH = 4
GDN_CHUNK = 64
ROPE_BASE = 10000.0
HALO_ROWS = 16
VMEM_LIMIT_BYTES = 56 * 1024 * 1024


def _cparams(sem):
    return pltpu.CompilerParams(dimension_semantics=sem, vmem_limit_bytes=VMEM_LIMIT_BYTES)


def _sigmoid(x):
    return 1.0 / (1.0 + jnp.exp(-x))


def _dot(a, b):
    return jnp.dot(a, b, preferred_element_type=F32)


def _dot_nt(a, b):
    return lax.dot_general(a, b, (((1,), (1,)), ((), ())), preferred_element_type=F32)


def _dot_tn(a, b):
    return lax.dot_general(a, b, (((0,), (0,)), ((), ())), preferred_element_type=F32)


def _drop_cols_kernel(cur_ref, nxt_ref, o_ref, *, first_shifted, shift):
    j = pl.program_id(0)
    tn = o_ref.shape[1]

    @pl.when(j < first_shifted)
    def _():
        o_ref[...] = cur_ref[...].astype(o_ref.dtype)

    @pl.when(j >= first_shifted)
    def _():
        cat = jnp.concatenate([cur_ref[...], nxt_ref[...]], axis=1)
        o_ref[...] = cat[:, shift:shift + tn].astype(o_ref.dtype)


def _drop_cols(w, start, count, tn):
    d, n_in = w.shape
    n = n_in - count
    lane = 128
    assert start % tn == 0 and n % tn == 0 and count < lane
    return pl.pallas_call(
        functools.partial(_drop_cols_kernel, first_shifted=start // tn, shift=count),
        grid=(n // tn,),
        in_specs=[
            pl.BlockSpec((d, tn), lambda j: (0, j)),
            pl.BlockSpec((d, lane), lambda j: (0, (j + 1) * (tn // lane))),
        ],
        out_specs=pl.BlockSpec((d, tn), lambda j: (0, j)),
        out_shape=jax.ShapeDtypeStruct((d, n), BF16),
        compiler_params=_cparams(("parallel",)),
        name="drop_cols",
    )(w, w)


def _inproj_kernel(x_ref, nw_ref, w_ref, o_ref, u_ref):
    @pl.when(pl.program_id(1) == 0)
    def _():
        xf = x_ref[...]
        ms = jnp.mean(xf * xf, axis=-1, keepdims=True)
        u_ref[...] = ((xf * lax.rsqrt(ms + EPS)) * nw_ref[...]).astype(u_ref.dtype)

    o_ref[...] = _dot(u_ref[...], w_ref[...]).astype(o_ref.dtype)


def _inproj(x2, nw, w, tm, tn):
    m, d = x2.shape
    n = w.shape[1]
    return pl.pallas_call(
        _inproj_kernel,
        grid=(m // tm, n // tn),
        in_specs=[
            pl.BlockSpec((tm, d), lambda i, j: (i, 0)),
            pl.BlockSpec((1, d), lambda i, j: (0, 0)),
            pl.BlockSpec((d, tn), lambda i, j: (0, j)),
        ],
        out_specs=[
            pl.BlockSpec((tm, tn), lambda i, j: (i, j)),
            pl.BlockSpec((tm, d), lambda i, j: (i, 0)),
        ],
        out_shape=[jax.ShapeDtypeStruct((m, n), BF16), jax.ShapeDtypeStruct((m, d), BF16)],
        compiler_params=_cparams(("parallel", "arbitrary")),
        name="inproj",
    )(x2, nw, w)


def _scalars_kernel(u_ref, wba_ref, alog_ref, dtb_ref, beta_ref, gc_ref, *, chunk, heads):
    ba = _dot(u_ref[...], wba_ref[...])
    a = ba[:, heads:2 * heads] + dtb_ref[...]
    softplus = jnp.maximum(a, 0.0) + jnp.log1p(jnp.exp(-jnp.abs(a)))
    g = -jnp.exp(alog_ref[...]) * softplus
    beta_ref[...] = _sigmoid(ba[:, 0:heads])
    tm = g.shape[0]
    row = lax.broadcasted_iota(jnp.int32, (tm, tm), 0)
    col = lax.broadcasted_iota(jnp.int32, (tm, tm), 1)
    incl = jnp.where(((row // chunk) == (col // chunk)) & (row >= col), 1.0, 0.0).astype(BF16)
    g1 = g.astype(BF16)
    r1 = g - g1.astype(F32)
    g2 = r1.astype(BF16)
    g3 = (r1 - g2.astype(F32)).astype(BF16)
    gc_ref[...] = _dot(incl, g1) + _dot(incl, g2) + _dot(incl, g3)


def _scalars(u, wba, alog, dtb, tm, chunk):
    m, d = u.shape
    h = alog.shape[1]
    out = jax.ShapeDtypeStruct((m, h), F32)
    return pl.pallas_call(
        functools.partial(_scalars_kernel, chunk=chunk, heads=h),
        grid=(m // tm,),
        in_specs=[
            pl.BlockSpec((tm, d), lambda i: (i, 0)),
            pl.BlockSpec((d, 2 * h), lambda i: (0, 0)),
            pl.BlockSpec((1, h), lambda i: (0, 0)),
            pl.BlockSpec((1, h), lambda i: (0, 0)),
        ],
        out_specs=[pl.BlockSpec((tm, h), lambda i: (i, 0))] * 2,
        out_shape=[out, out],
        compiler_params=_cparams(("parallel",)),
        name="gate_scalars",
    )(u, wba, alog, dtb)


def _gdn_prep_kernel(q_ref, qh_ref, k_ref, kh_ref, v_ref, vh_ref, cwq_ref, cwk_ref, cwv_ref,
                     beta_ref, gcc_ref, gcr_ref,
                     wq_ref, kpt_ref, u_ref, attn_ref, egl_ref,
                     extk_scr, extv_scr, *, dk, dv, chunk, nchunks, heads):
    t = pl.program_id(1)
    rows = chunk * nchunks
    lane_head = lax.broadcasted_iota(jnp.int32, (1, beta_ref.shape[1]), 1)

    def head_col(ref, hh):
        hsel = lane_head == pl.program_id(2) * heads + hh
        return jnp.sum(jnp.where(hsel, ref[...], 0.0), axis=1, keepdims=True)

    def conv_silu(x_ref, h_ref, cw_ref, ext_scr):
        halo = jnp.where(t == 0, 0.0, h_ref[...].astype(F32))
        ext_scr[0:HALO_ROWS, :] = halo
        ext_scr[HALO_ROWS:HALO_ROWS + rows, :] = x_ref[...].astype(F32)
        acc = ext_scr[HALO_ROWS:HALO_ROWS + rows, :] * cw_ref[CONV_WIDTH - 1:CONV_WIDTH, :]
        for j in range(1, CONV_WIDTH):
            acc = acc + (ext_scr[HALO_ROWS - j:HALO_ROWS - j + rows, :]
                         * cw_ref[CONV_WIDTH - 1 - j:CONV_WIDTH - j, :])
        return acc * _sigmoid(acc)

    def l2norm(y):
        return y * lax.rsqrt(jnp.sum(y * y, axis=-1, keepdims=True) + EPS)

    q_all = conv_silu(q_ref, qh_ref, cwq_ref, extk_scr)
    k_all = conv_silu(k_ref, kh_ref, cwk_ref, extk_scr)
    v_all = conv_silu(v_ref, vh_ref, cwv_ref, extv_scr)

    ri = lax.broadcasted_iota(jnp.int32, (chunk, chunk), 0)
    ci = lax.broadcasted_iota(jnp.int32, (chunk, chunk), 1)
    incl = ri >= ci
    strict = ri > ci
    eye = jnp.where(ri == ci, 1.0, 0.0).astype(F32)
    ident_k = jnp.where(lax.broadcasted_iota(jnp.int32, (dk, dk), 0)
                        == lax.broadcasted_iota(jnp.int32, (dk, dk), 1), 1.0, 0.0).astype(BF16)

    pairs = [(hh, c) for hh in range(heads) for c in range(nchunks)]
    cr = range(len(pairs))
    qs, ks, vs, bs, gcs = [], [], [], [], []
    for hh in range(heads):
        q = l2norm(q_all[:, hh * dk:(hh + 1) * dk]) * (dk ** -0.5)
        k = l2norm(k_all[:, hh * dk:(hh + 1) * dk])
        v = v_all[:, hh * dv:(hh + 1) * dv]
        beta, gcum = head_col(beta_ref, hh), head_col(gcc_ref, hh)
        for c in range(nchunks):
            sl = slice(c * chunk, (c + 1) * chunk)
            qs.append(q[sl])
            ks.append(k[sl])
            vs.append(v[sl])
            bs.append(beta[sl])
            gcs.append(gcum[sl])
    gls = [gc[chunk - 1:chunk, :] for gc in gcs]
    decays = [jnp.exp(jnp.where(incl, gcs[i] - gcr_ref[0, hh, c], -1e30)) for i, (hh, c) in enumerate(pairs)]
    egs = [jnp.exp(gc) for gc in gcs]
    kbs = [ks[c] * bs[c] for c in cr]
    kq = [_dot_nt(jnp.concatenate([kbs[c], qs[c]], axis=0).astype(BF16), ks[c].astype(BF16)) for c in cr]
    amats = [jnp.where(strict, kq[c][0:chunk] * decays[c], 0.0) for c in cr]
    for i, (hh, c) in enumerate(pairs):
        attn_ref[0, hh, c] = (kq[i][chunk:2 * chunk] * decays[i]).astype(attn_ref.dtype)
    tinvs = [eye - a for a in amats]
    apows = amats
    span = 2
    while span < chunk:
        apows = [_dot(ab, ab) for ab in [a.astype(BF16) for a in apows]]
        tinvs = [tinvs[c] + _dot(tinvs[c].astype(BF16), apows[c].astype(BF16)) for c in cr]
        span *= 2
    rhs = [jnp.concatenate([kbs[c] * egs[c], vs[c] * bs[c]], axis=1).astype(BF16) for c in cr]
    wu = [_dot(tinvs[c].astype(BF16), rhs[c]) for c in cr]
    kpt = [_dot_nt(ident_k, (ks[c] * jnp.exp(gls[c] - gcs[c])).astype(BF16)) for c in cr]
    for i, (hh, c) in enumerate(pairs):
        wq_ref[0, hh, c, 0:chunk, :] = wu[i][:, 0:dk].astype(wq_ref.dtype)
        wq_ref[0, hh, c, chunk:2 * chunk, :] = (qs[i] * egs[i]).astype(wq_ref.dtype)
        kpt_ref[0, hh, c] = kpt[i].astype(kpt_ref.dtype)
        u_ref[0, hh, c] = wu[i][:, dk:dk + dv]
        egl_ref[0, hh, c] = jnp.broadcast_to(jnp.exp(gls[i]), (1, dv))


def _gdn_prep(proj, conv_w, beta_c, gc_c, gc_r, dims, rows, heads):
    b, s, h, dk, dv = dims
    chunk = GDN_CHUNK
    nchunks = rows // chunk
    n = s // chunk
    qk = h * dk
    tiles = s // rows
    kw, vw = heads * dk, heads * dv
    kcol = qk // kw
    vcol = 2 * qk // vw

    def rowblk(bi, ti):
        return bi * tiles + ti

    def halo(bi, ti):
        return jnp.maximum((bi * s + ti * rows) // HALO_ROWS - 1, 0)

    col_spec = pl.BlockSpec((rows, h), lambda bi, ti, hi: (rowblk(bi, ti), 0))
    in_specs = [
        pl.BlockSpec((rows, kw), lambda bi, ti, hi: (rowblk(bi, ti), hi)),
        pl.BlockSpec((HALO_ROWS, kw), lambda bi, ti, hi: (halo(bi, ti), hi)),
        pl.BlockSpec((rows, kw), lambda bi, ti, hi: (rowblk(bi, ti), kcol + hi)),
        pl.BlockSpec((HALO_ROWS, kw), lambda bi, ti, hi: (halo(bi, ti), kcol + hi)),
        pl.BlockSpec((rows, vw), lambda bi, ti, hi: (rowblk(bi, ti), vcol + hi)),
        pl.BlockSpec((HALO_ROWS, vw), lambda bi, ti, hi: (halo(bi, ti), vcol + hi)),
        pl.BlockSpec((CONV_WIDTH, kw), lambda bi, ti, hi: (0, hi)),
        pl.BlockSpec((CONV_WIDTH, kw), lambda bi, ti, hi: (0, kcol + hi)),
        pl.BlockSpec((CONV_WIDTH, vw), lambda bi, ti, hi: (0, vcol + hi)),
        col_spec, col_spec,
        pl.BlockSpec((1, heads, nchunks, 1, chunk), lambda bi, ti, hi: (bi, hi, ti, 0, 0)),
    ]

    def out5(r, c):
        return pl.BlockSpec((1, heads, nchunks, r, c), lambda bi, ti, hi: (bi, hi, ti, 0, 0))

    out_specs = [out5(2 * chunk, dk), out5(dk, chunk), out5(chunk, dv), out5(chunk, chunk), out5(1, dv)]
    out_shape = [
        jax.ShapeDtypeStruct((b, h, n, 2 * chunk, dk), BF16),
        jax.ShapeDtypeStruct((b, h, n, dk, chunk), BF16),
        jax.ShapeDtypeStruct((b, h, n, chunk, dv), F32),
        jax.ShapeDtypeStruct((b, h, n, chunk, chunk), BF16),
        jax.ShapeDtypeStruct((b, h, n, 1, dv), F32),
    ]
    return pl.pallas_call(
        functools.partial(_gdn_prep_kernel, dk=dk, dv=dv, chunk=chunk, nchunks=nchunks, heads=heads),
        grid=(b, tiles, h // heads),
        in_specs=in_specs,
        out_specs=out_specs,
        out_shape=out_shape,
        scratch_shapes=[pltpu.VMEM((HALO_ROWS + rows, kw), F32), pltpu.VMEM((HALO_ROWS + rows, vw), F32)],
        compiler_params=_cparams(("parallel", "parallel", "parallel")),
        name="gdn_prep",
    )(proj, proj, proj, proj, proj, proj, conv_w, conv_w, conv_w, beta_c, gc_c, gc_r)


def _gdn_scan_kernel(wq_ref, kpt_ref, u_ref, attn_ref, egl_ref, z_ref, ga_ref, nw_ref, o_ref, s_scr,
                     *, dv, chunk, nchunks, heads):
    @pl.when(pl.program_id(2) == 0)
    def _():
        s_scr[...] = jnp.zeros_like(s_scr)

    nw = nw_ref[...]
    hr = range(heads)
    for c in range(nchunks):
        rs = slice(c * chunk, (c + 1) * chunk)
        states = [s_scr[hh] for hh in hr]
        r = [_dot(wq_ref[0, hh, c], states[hh].astype(BF16)) for hh in hr]
        v_new = [(u_ref[0, hh, c] - r[hh][0:chunk]).astype(BF16) for hh in hr]
        upd = [_dot(kpt_ref[0, hh, c], v_new[hh]) for hh in hr]
        for hh in hr:
            s_scr[hh] = states[hh] * egl_ref[0, hh, c] + upd[hh]
        o = [r[hh][chunk:2 * chunk] + _dot(attn_ref[0, hh, c], v_new[hh]) for hh in hr]
        for hh in hr:
            cs = slice(hh * dv, (hh + 1) * dv)
            on = o[hh] * lax.rsqrt(jnp.mean(o[hh] * o[hh], axis=-1, keepdims=True) + EPS) * nw
            z = z_ref[rs, cs].astype(F32)
            ga = ga_ref[rs, cs].astype(F32)
            o_ref[rs, cs] = (on * (z * _sigmoid(z)) * _sigmoid(ga)).astype(o_ref.dtype)


def _gdn_scan(wq, kpt, u, attn, egl, proj, gdn_nw, dims, nchunks, heads, zcol, gacol):
    b, s, h, dk, dv = dims
    chunk = GDN_CHUNK
    n = s // chunk
    rows = nchunks * chunk
    steps = n // nchunks
    width = heads * dv

    def in5(r, c):
        return pl.BlockSpec((1, heads, nchunks, r, c), lambda bi, gi, i: (bi, gi, i, 0, 0))

    in_specs = [
        in5(2 * chunk, dk), in5(dk, chunk), in5(chunk, dv), in5(chunk, chunk), in5(1, dv),
        pl.BlockSpec((rows, width), lambda bi, gi, i: (bi * steps + i, zcol // width + gi)),
        pl.BlockSpec((rows, width), lambda bi, gi, i: (bi * steps + i, gacol // width + gi)),
        pl.BlockSpec((1, dv), lambda bi, gi, i: (0, 0)),
    ]
    return pl.pallas_call(
        functools.partial(_gdn_scan_kernel, dv=dv, chunk=chunk, nchunks=nchunks, heads=heads),
        grid=(b, h // heads, steps),
        in_specs=in_specs,
        out_specs=pl.BlockSpec((rows, width), lambda bi, gi, i: (bi * steps + i, gi)),
        out_shape=jax.ShapeDtypeStruct((b * s, h * dv), BF16),
        scratch_shapes=[pltpu.VMEM((heads, dk, dv), F32)],
        compiler_params=_cparams(("parallel", "parallel", "arbitrary")),
        name="gdn_scan",
    )(wq, kpt, u, attn, egl, proj, proj, gdn_nw)


def _retention_kernel(q_ref, k_ref, v_ref, rg_ref, gb_ref, ma_ref, cos_ref, sin_ref, dmat_ref,
                      xi_ref, zeta_ref, gpow_ref, nw_ref, o_ref, s_scr, *, dk, dv, heads):
    @pl.when(pl.program_id(2) == 0)
    def _():
        s_scr[...] = jnp.zeros_like(s_scr)

    cos = cos_ref[...]
    sin = sin_ref[...]

    def rotary(x):
        return x * cos + pltpu.roll(x, dk // 2, 1) * sin

    hr = range(heads)
    ksl = [slice(hh * dk, (hh + 1) * dk) for hh in hr]
    vsl = [slice(hh * dv, (hh + 1) * dv) for hh in hr]
    q = [rotary(q_ref[:, ksl[hh]].astype(F32)).astype(BF16) for hh in hr]
    k = [rotary(k_ref[:, ksl[hh]].astype(F32)) * (dk ** -0.5) for hh in hr]
    kz = [(k[hh] * zeta_ref[hh]).astype(BF16) for hh in hr]
    states = [s_scr[hh] for hh in hr]
    scores = [(_dot_nt(q[hh], k[hh].astype(BF16)) * dmat_ref[hh]).astype(BF16) for hh in hr]
    inter = [_dot(q[hh], states[hh].astype(BF16)) * xi_ref[hh] for hh in hr]
    upd = [_dot_tn(kz[hh], v_ref[:, vsl[hh]]) for hh in hr]
    for hh in hr:
        s_scr[hh] = states[hh] * gpow_ref[hh] + upd[hh]
    o = [_dot(scores[hh], v_ref[:, vsl[hh]]) + inter[hh] for hh in hr]
    for hh in hr:
        mu = jnp.mean(o[hh], axis=-1, keepdims=True)
        dev = o[hh] - mu
        var = jnp.mean(dev * dev, axis=-1, keepdims=True)
        ob = dev * lax.rsqrt(var + EPS) * nw_ref[:, vsl[hh]]
        rg = rg_ref[:, vsl[hh]].astype(F32)
        ob = ob * (rg * _sigmoid(rg))
        mixed = _sigmoid(gb_ref[:, vsl[hh]].astype(F32)) * ob + ma_ref[:, vsl[hh]].astype(F32)
        o_ref[:, vsl[hh]] = mixed.astype(o_ref.dtype)


def _retention(proj, ma, ret_nw, tables, dims, rows, heads, cols):
    b, s, h, dk, dv = dims
    cosf, sins, dmat, xi, zeta, gpow = tables
    rqcol, rkcol, rvcol, rgcol, gbcol = cols
    steps = s // rows
    kw, vw = heads * dk, heads * dv

    def rb(bi, i):
        return bi * steps + i

    in_specs = [
        pl.BlockSpec((rows, kw), lambda bi, gi, i: (rb(bi, i), rqcol // kw + gi)),
        pl.BlockSpec((rows, kw), lambda bi, gi, i: (rb(bi, i), rkcol // kw + gi)),
        pl.BlockSpec((rows, vw), lambda bi, gi, i: (rb(bi, i), rvcol // vw + gi)),
        pl.BlockSpec((rows, vw), lambda bi, gi, i: (rb(bi, i), rgcol // vw + gi)),
        pl.BlockSpec((rows, vw), lambda bi, gi, i: (rb(bi, i), gbcol // vw + gi)),
        pl.BlockSpec((rows, vw), lambda bi, gi, i: (rb(bi, i), gi)),
        pl.BlockSpec((rows, dk), lambda bi, gi, i: (i, 0)),
        pl.BlockSpec((rows, dk), lambda bi, gi, i: (i, 0)),
        pl.BlockSpec((heads, rows, rows), lambda bi, gi, i: (gi, 0, 0)),
        pl.BlockSpec((heads, rows, 1), lambda bi, gi, i: (gi, 0, 0)),
        pl.BlockSpec((heads, rows, 1), lambda bi, gi, i: (gi, 0, 0)),
        pl.BlockSpec((heads, 1, dv), lambda bi, gi, i: (gi, 0, 0)),
        pl.BlockSpec((1, vw), lambda bi, gi, i: (0, gi)),
    ]
    return pl.pallas_call(
        functools.partial(_retention_kernel, dk=dk, dv=dv, heads=heads),
        grid=(b, h // heads, steps),
        in_specs=in_specs,
        out_specs=pl.BlockSpec((rows, vw), lambda bi, gi, i: (rb(bi, i), gi)),
        out_shape=jax.ShapeDtypeStruct((b * s, h * dv), BF16),
        scratch_shapes=[pltpu.VMEM((heads, dk, dv), F32)],
        compiler_params=_cparams(("parallel", "parallel", "arbitrary")),
        name="retention",
    )(proj, proj, proj, proj, proj, ma, cosf, sins, dmat, xi, zeta, gpow, ret_nw)


def _retention_tables(s, h, dk, dv, rows):
    inv = ROPE_BASE ** (-jnp.arange(0, dk, 2, dtype=F32) / dk)
    ang = jnp.arange(s, dtype=F32)[:, None] * inv[None, :]
    cos, sin = jnp.cos(ang), jnp.sin(ang)
    cosf = jnp.concatenate([cos, cos], axis=-1)
    sins = jnp.concatenate([-sin, sin], axis=-1)
    log_gamma = jnp.log1p(-jnp.exp2(-5.0 - jnp.arange(h, dtype=F32)))
    pos = jnp.arange(rows, dtype=F32)
    dist = pos[:, None] - pos[None, :]
    dmat = jnp.exp(jnp.where(dist >= 0, dist * log_gamma[:, None, None], -jnp.inf))
    xi = jnp.exp((pos + 1.0) * log_gamma[:, None])[:, :, None]
    zeta = jnp.exp((rows - 1.0 - pos) * log_gamma[:, None])[:, :, None]
    gpow = jnp.broadcast_to(jnp.exp(rows * log_gamma)[:, None, None], (h, 1, dv))
    return cosf, sins, dmat, xi, zeta, gpow


def _outproj_kernel(m_ref, w_ref, x_ref, nw_ref, h_ref, hn_ref):
    hres = x_ref[...] + _dot(m_ref[...], w_ref[...])
    h_ref[...] = hres
    ms = jnp.mean(hres * hres, axis=-1, keepdims=True)
    hn_ref[...] = ((hres * lax.rsqrt(ms + EPS)) * nw_ref[...]).astype(hn_ref.dtype)


def _outproj(mixed, w, x2, nw, tm):
    m, d = x2.shape
    kdim = mixed.shape[1]
    return pl.pallas_call(
        _outproj_kernel,
        grid=(m // tm,),
        in_specs=[
            pl.BlockSpec((tm, kdim), lambda i: (i, 0)),
            pl.BlockSpec((kdim, d), lambda i: (0, 0)),
            pl.BlockSpec((tm, d), lambda i: (i, 0)),
            pl.BlockSpec((1, d), lambda i: (0, 0)),
        ],
        out_specs=[pl.BlockSpec((tm, d), lambda i: (i, 0))] * 2,
        out_shape=[jax.ShapeDtypeStruct((m, d), F32), jax.ShapeDtypeStruct((m, d), BF16)],
        compiler_params=_cparams(("parallel",)),
        name="outproj",
    )(mixed, w, x2, nw)


def _ffn_kernel(hn_ref, h_ref, wg_ref, wu_ref, wd_ref, nw_ref, o_ref):
    f = pl.program_id(1)

    @pl.when(f == 0)
    def _():
        o_ref[...] = h_ref[...]

    hn = hn_ref[...]
    g = _dot(hn, wg_ref[...])
    up = _dot(hn, wu_ref[...])
    act = (g * _sigmoid(g) * up).astype(BF16)
    o_ref[...] += _dot(act, wd_ref[...])

    @pl.when(f == pl.num_programs(1) - 1)
    def _():
        hres = o_ref[...]
        ms = jnp.mean(hres * hres, axis=-1, keepdims=True)
        o_ref[...] = (hres * lax.rsqrt(ms + EPS)) * nw_ref[...]


def _ffn(hn, hres, wg, wu, wd, nw, tm, tf):
    m, d = hres.shape
    ff = wg.shape[1]
    return pl.pallas_call(
        _ffn_kernel,
        grid=(m // tm, ff // tf),
        in_specs=[
            pl.BlockSpec((tm, d), lambda i, j: (i, 0)),
            pl.BlockSpec((tm, d), lambda i, j: (i, 0)),
            pl.BlockSpec((d, tf), lambda i, j: (0, j)),
            pl.BlockSpec((d, tf), lambda i, j: (0, j)),
            pl.BlockSpec((tf, d), lambda i, j: (j, 0)),
            pl.BlockSpec((1, d), lambda i, j: (0, 0)),
        ],
        out_specs=pl.BlockSpec((tm, d), lambda i, j: (i, 0)),
        out_shape=jax.ShapeDtypeStruct((m, d), F32),
        compiler_params=_cparams(("parallel", "arbitrary")),
        name="ffn",
    )(hn, hres, wg, wu, wd, nw)


def _pick(total, want):
    t = min(total, want)
    while total % t:
        t //= 2
    return t


def _layer(h2, dims, norm1_w, w_in, conv_w, a_log, dt_bias, gdn_norm_w, ret_norm_w,
           w_out, norm2_w, w_gate, w_up, w_down, out_norm_w):
    b, s, h, dk, dv = dims
    m, d = h2.shape
    qk, vd = h * dk, h * dv
    chunk = GDN_CHUNK
    n = s // chunk

    sc0 = 2 * qk + 2 * vd
    w_main = _drop_cols(w_in, sc0, 2 * h, _pick(sc0, 1024))
    w_ba = w_in[:, sc0:sc0 + 2 * h].astype(BF16)
    zcol = 2 * qk + vd
    rqcol = zcol + vd
    rkcol = rqcol + qk
    rvcol = rkcol + qk
    rgcol = rvcol + vd
    gacol = rgcol + vd
    gbcol = gacol + vd

    proj, u = _inproj(h2, norm1_w.reshape(1, d), w_main, _pick(m, 1024), _pick(w_main.shape[1], 1024))

    beta, gcum = _scalars(u, w_ba, a_log.reshape(1, h), dt_bias.reshape(1, h), _pick(m, 512), chunk)

    gc_r = gcum.reshape(b, n, chunk, h).transpose(0, 3, 1, 2)[:, :, :, None, :]
    wq, kpt, uu, attn, egl = _gdn_prep(proj, conv_w, beta, gcum, gc_r, dims, _pick(s, 512), _pick(h, 2))

    scan_heads = _pick(h, 8)
    ma = _gdn_scan(wq, kpt, uu, attn, egl, proj, gdn_norm_w.reshape(1, dv), dims,
                   _pick(n, 2), scan_heads, zcol, gacol)

    ret_rows = _pick(s, 256)
    tables = _retention_tables(s, h, dk, dv, ret_rows)
    mixed = _retention(proj, ma, ret_norm_w.reshape(1, vd), tables, dims, ret_rows, _pick(h, 8),
                       (rqcol, rkcol, rvcol, rgcol, gbcol))

    hres, hn = _outproj(mixed, w_out.astype(BF16), h2, norm2_w.reshape(1, d), _pick(m, 256))

    return _ffn(hn, hres, w_gate.astype(BF16), w_up.astype(BF16), w_down.astype(BF16),
                out_norm_w.reshape(1, d), _pick(m, 512), _pick(w_gate.shape[1], 512))


def kernel(x, norm1_w, w_in, conv_w, a_log, dt_bias, gdn_norm_w, ret_norm_w, w_out, norm2_w,
           w_gate, w_up, w_down, norm_f_w):
    b, s, d = x.shape
    depth, h = a_log.shape
    assert depth == 1, "the final norm is fused into the single layer's FFN kernel"
    dk = d // 16
    dv = d // h
    dims = (b, s, h, dk, dv)
    out = _layer(x.reshape(b * s, d), dims, norm1_w[0], w_in[0], conv_w[0], a_log[0], dt_bias[0],
                 gdn_norm_w[0], ret_norm_w[0], w_out[0], norm2_w[0], w_gate[0], w_up[0], w_down[0],
                 norm_f_w)
    return out.reshape(b, s, d)
```

```python
import functools

import jax
import jax.numpy as jnp
import numpy as np
from jax import lax
from jax.experimental import pallas as pl
from jax.experimental.pallas import tpu as pltpu

F32 = jnp.float32
BF16 = jnp.bfloat16

EPS = 1e-6
CONV_WIDTH = 4
GDN_CHUNK = 64
ROPE_BASE = 10000.0
HALO_ROWS = 16
VMEM_LIMIT_BYTES = 56 * 1024 * 1024


def _cparams(sem):
    return pltpu.CompilerParams(dimension_semantics=sem, vmem_limit_bytes=VMEM_LIMIT_BYTES)


def _sigmoid(x):
    return 1.0 / (1.0 + jnp.exp(-x))


def _dot(a, b):
    return jnp.dot(a, b, preferred_element_type=F32)


def _dot_nt(a, b):
    return lax.dot_general(a, b, (((1,), (1,)), ((), ())), preferred_element_type=F32)


def _dot_tn(a, b):
    return lax.dot_general(a, b, (((0,), (0,)), ((), ())), preferred_element_type=F32)


def _drop_rows_kernel(cur_ref, nxt_ref, o_ref, *, first_shifted, count):
    j = pl.program_id(0)
    tr = o_ref.shape[0]

    @pl.when(j < first_shifted)
    def _():
        o_ref[...] = cur_ref[...].astype(o_ref.dtype)

    @pl.when(j >= first_shifted)
    def _():
        o_ref[0:tr - count, :] = cur_ref[count:tr, :].astype(o_ref.dtype)
        o_ref[tr - count:tr, :] = nxt_ref[...].astype(o_ref.dtype)


def _drop_rows(wt, start, count, tr):
    n_in, d = wt.shape
    n = n_in - count
    assert start % tr == 0 and n % tr == 0 and tr % count == 0 and count % HALO_ROWS == 0
    return pl.pallas_call(
        functools.partial(_drop_rows_kernel, first_shifted=start // tr, count=count),
        grid=(n // tr,),
        in_specs=[
            pl.BlockSpec((tr, d), lambda j: (j, 0)),
            pl.BlockSpec((count, d), lambda j: ((j + 1) * (tr // count), 0)),
        ],
        out_specs=pl.BlockSpec((tr, d), lambda j: (j, 0)),
        out_shape=jax.ShapeDtypeStruct((n, d), BF16),
        compiler_params=_cparams(("parallel",)),
        name="drop_rows",
    )(wt, wt)


def _inproj_kernel(x_ref, nw_ref, w_ref, o_ref, u_ref):
    @pl.when(pl.program_id(1) == 0)
    def _():
        xf = x_ref[...]
        ms = jnp.mean(xf * xf, axis=-1, keepdims=True)
        u_ref[...] = ((xf * lax.rsqrt(ms + EPS)) * nw_ref[...]).astype(u_ref.dtype)

    o_ref[...] = _dot_nt(u_ref[...], w_ref[...]).astype(o_ref.dtype)


def _inproj(x2, nw, w, tm, tn, n, skip_at, skip):
    m, d = x2.shape
    return pl.pallas_call(
        _inproj_kernel,
        grid=(m // tm, n // tn),
        in_specs=[
            pl.BlockSpec((tm, d), lambda i, j: (i, 0)),
            pl.BlockSpec((1, d), lambda i, j: (0, 0)),
            pl.BlockSpec((tn, d), lambda i, j: (jnp.where(j < skip_at, j, j + skip), 0)),
        ],
        out_specs=[
            pl.BlockSpec((tm, tn), lambda i, j: (i, j)),
            pl.BlockSpec((tm, d), lambda i, j: (i, 0)),
        ],
        out_shape=[jax.ShapeDtypeStruct((m, n), BF16), jax.ShapeDtypeStruct((m, d), BF16)],
        compiler_params=_cparams(("parallel", "arbitrary")),
        name="inproj",
    )(x2, nw, w)


def _gates_kernel(u_ref, ws_ref, wg_ref, o_ref):
    u = u_ref[...]
    a = _dot_nt(u, ws_ref[...])
    g = _dot_nt(u, wg_ref[...])
    o_ref[...] = (a / ((1.0 + jnp.exp(-a)) * (1.0 + jnp.exp(-g)))).astype(o_ref.dtype)


def _gates(u, w, tm, tg, vd, pairs):
    m, d = u.shape
    per = vd // tg

    def blk(j, which):
        idx = pairs[0][which] // tg + j
        for p in range(1, len(pairs)):
            idx = jnp.where(j >= p * per, pairs[p][which] // tg + j - p * per, idx)
        return idx

    return pl.pallas_call(
        _gates_kernel,
        grid=(m // tm, len(pairs) * per),
        in_specs=[
            pl.BlockSpec((tm, d), lambda i, j: (i, 0)),
            pl.BlockSpec((tg, d), lambda i, j: (blk(j, 0), 0)),
            pl.BlockSpec((tg, d), lambda i, j: (blk(j, 1), 0)),
        ],
        out_specs=pl.BlockSpec((tm, tg), lambda i, j: (i, j)),
        out_shape=jax.ShapeDtypeStruct((m, len(pairs) * vd), BF16),
        compiler_params=_cparams(("parallel", "arbitrary")),
        name="gates",
    )(u, w, w)


def _scalars_kernel(u_ref, wba_ref, alog_ref, dtb_ref, beta_ref, gc_ref, *, chunk, heads):
    ba = _dot_nt(u_ref[...], wba_ref[...])
    a = ba[:, heads:2 * heads] + dtb_ref[...]
    softplus = jnp.maximum(a, 0.0) + jnp.log1p(jnp.exp(-jnp.abs(a)))
    g = -jnp.exp(alog_ref[...]) * softplus
    beta_ref[...] = _sigmoid(ba[:, 0:heads])
    tm = g.shape[0]
    row = lax.broadcasted_iota(jnp.int32, (tm, tm), 0)
    col = lax.broadcasted_iota(jnp.int32, (tm, tm), 1)
    incl = jnp.where(((row // chunk) == (col // chunk)) & (row >= col), 1.0, 0.0).astype(BF16)
    g1 = g.astype(BF16)
    r1 = g - g1.astype(F32)
    g2 = r1.astype(BF16)
    g3 = (r1 - g2.astype(F32)).astype(BF16)
    gc_ref[...] = _dot(incl, g1) + _dot(incl, g2) + _dot(incl, g3)


def _scalars(u, wba, alog, dtb, tm, chunk):
    m, d = u.shape
    h = alog.shape[1]
    out = jax.ShapeDtypeStruct((m, h), F32)
    return pl.pallas_call(
        functools.partial(_scalars_kernel, chunk=chunk, heads=h),
        grid=(m // tm,),
        in_specs=[
            pl.BlockSpec((tm, d), lambda i: (i, 0)),
            pl.BlockSpec((2 * h, d), lambda i: (0, 0)),
            pl.BlockSpec((1, h), lambda i: (0, 0)),
            pl.BlockSpec((1, h), lambda i: (0, 0)),
        ],
        out_specs=[pl.BlockSpec((tm, h), lambda i: (i, 0))] * 2,
        out_shape=[out, out],
        compiler_params=_cparams(("parallel",)),
        name="gate_scalars",
    )(u, wba, alog, dtb)


def _gdn_prep_kernel(q_ref, qh_ref, k_ref, kh_ref, v_ref, vh_ref, cwq_ref, cwk_ref, cwv_ref,
                     beta_ref, gcc_ref, gcr_ref,
                     wq_ref, kpt_ref, u_ref, attn_ref, egl_ref,
                     extk_scr, extv_scr, *, dk, dv, chunk, nchunks, heads):
    t = pl.program_id(1)
    rows = chunk * nchunks
    lane_head = lax.broadcasted_iota(jnp.int32, (1, beta_ref.shape[1]), 1)

    def head_col(ref, hh):
        hsel = lane_head == pl.program_id(2) * heads + hh
        return jnp.sum(jnp.where(hsel, ref[...], 0.0), axis=1, keepdims=True)

    def conv_silu(x_ref, h_ref, cw_ref, ext_scr):
        halo = jnp.where(t == 0, 0.0, h_ref[...].astype(F32))
        ext_scr[0:HALO_ROWS, :] = halo
        ext_scr[HALO_ROWS:HALO_ROWS + rows, :] = x_ref[...].astype(F32)
        acc = ext_scr[HALO_ROWS:HALO_ROWS + rows, :] * cw_ref[CONV_WIDTH - 1:CONV_WIDTH, :]
        for j in range(1, CONV_WIDTH):
            acc = acc + (ext_scr[HALO_ROWS - j:HALO_ROWS - j + rows, :]
                         * cw_ref[CONV_WIDTH - 1 - j:CONV_WIDTH - j, :])
        return acc * _sigmoid(acc)

    def l2norm(y):
        return y * lax.rsqrt(jnp.sum(y * y, axis=-1, keepdims=True) + EPS)

    q_all = conv_silu(q_ref, qh_ref, cwq_ref, extk_scr)
    k_all = conv_silu(k_ref, kh_ref, cwk_ref, extk_scr)
    v_all = conv_silu(v_ref, vh_ref, cwv_ref, extv_scr)

    ri = lax.broadcasted_iota(jnp.int32, (chunk, chunk), 0)
    ci = lax.broadcasted_iota(jnp.int32, (chunk, chunk), 1)
    incl = ri >= ci
    strict = ri > ci
    eye = jnp.where(ri == ci, 1.0, 0.0).astype(F32)
    ident_k = jnp.where(lax.broadcasted_iota(jnp.int32, (dk, dk), 0)
                        == lax.broadcasted_iota(jnp.int32, (dk, dk), 1), 1.0, 0.0).astype(BF16)

    pairs = [(hh, c) for hh in range(heads) for c in range(nchunks)]
    cr = range(len(pairs))
    qs, ks, vs, bs, gcs = [], [], [], [], []
    for hh in range(heads):
        q = l2norm(q_all[:, hh * dk:(hh + 1) * dk]) * (dk ** -0.5)
        k = l2norm(k_all[:, hh * dk:(hh + 1) * dk])
        v = v_all[:, hh * dv:(hh + 1) * dv]
        beta, gcum = head_col(beta_ref, hh), head_col(gcc_ref, hh)
        for c in range(nchunks):
            sl = slice(c * chunk, (c + 1) * chunk)
            qs.append(q[sl])
            ks.append(k[sl])
            vs.append(v[sl])
            bs.append(beta[sl])
            gcs.append(gcum[sl])
    gls = [gc[chunk - 1:chunk, :] for gc in gcs]
    decays = [jnp.exp(jnp.where(incl, gcs[i] - gcr_ref[0, hh, c], -1e30)) for i, (hh, c) in enumerate(pairs)]
    egs = [jnp.exp(gc) for gc in gcs]
    kbs = [ks[c] * bs[c] for c in cr]
    kq = [_dot_nt(jnp.concatenate([kbs[c], qs[c]], axis=0).astype(BF16), ks[c].astype(BF16)) for c in cr]
    amats = [jnp.where(strict, kq[c][0:chunk] * decays[c], 0.0) for c in cr]
    for i, (hh, c) in enumerate(pairs):
        attn_ref[0, hh, c] = (kq[i][chunk:2 * chunk] * decays[i]).astype(attn_ref.dtype)
    tinvs = [eye - a for a in amats]
    apows = amats
    span = 2
    while span < chunk:
        apows = [_dot(ab, ab) for ab in [a.astype(BF16) for a in apows]]
        tinvs = [tinvs[c] + _dot(tinvs[c].astype(BF16), apows[c].astype(BF16)) for c in cr]
        span *= 2
    rhs = [jnp.concatenate([kbs[c] * egs[c], vs[c] * bs[c]], axis=1).astype(BF16) for c in cr]
    wu = [_dot(tinvs[c].astype(BF16), rhs[c]) for c in cr]
    kpt = [_dot_nt(ident_k, (ks[c] * jnp.exp(gls[c] - gcs[c])).astype(BF16)) for c in cr]
    for i, (hh, c) in enumerate(pairs):
        wq_ref[0, hh, c, 0:chunk, :] = wu[i][:, 0:dk].astype(wq_ref.dtype)
        wq_ref[0, hh, c, chunk:2 * chunk, :] = (qs[i] * egs[i]).astype(wq_ref.dtype)
        kpt_ref[0, hh, c] = kpt[i].astype(kpt_ref.dtype)
        u_ref[0, hh, c] = wu[i][:, dk:dk + dv]
        egl_ref[0, hh, c] = jnp.broadcast_to(jnp.exp(gls[i]), (1, dv))


def _gdn_prep(proj, conv_w, beta_c, gc_c, gc_r, dims, rows, heads):
    b, s, h, dk, dv = dims
    chunk = GDN_CHUNK
    nchunks = rows // chunk
    n = s // chunk
    qk = h * dk
    tiles = s // rows
    kw, vw = heads * dk, heads * dv
    kcol = qk // kw
    vcol = 2 * qk // vw

    def rowblk(bi, ti):
        return bi * tiles + ti

    def halo(bi, ti):
        return jnp.maximum((bi * s + ti * rows) // HALO_ROWS - 1, 0)

    col_spec = pl.BlockSpec((rows, h), lambda bi, ti, hi: (rowblk(bi, ti), 0))
    in_specs = [
        pl.BlockSpec((rows, kw), lambda bi, ti, hi: (rowblk(bi, ti), hi)),
        pl.BlockSpec((HALO_ROWS, kw), lambda bi, ti, hi: (halo(bi, ti), hi)),
        pl.BlockSpec((rows, kw), lambda bi, ti, hi: (rowblk(bi, ti), kcol + hi)),
        pl.BlockSpec((HALO_ROWS, kw), lambda bi, ti, hi: (halo(bi, ti), kcol + hi)),
        pl.BlockSpec((rows, vw), lambda bi, ti, hi: (rowblk(bi, ti), vcol + hi)),
        pl.BlockSpec((HALO_ROWS, vw), lambda bi, ti, hi: (halo(bi, ti), vcol + hi)),
        pl.BlockSpec((CONV_WIDTH, kw), lambda bi, ti, hi: (0, hi)),
        pl.BlockSpec((CONV_WIDTH, kw), lambda bi, ti, hi: (0, kcol + hi)),
        pl.BlockSpec((CONV_WIDTH, vw), lambda bi, ti, hi: (0, vcol + hi)),
        col_spec, col_spec,
        pl.BlockSpec((1, heads, nchunks, 1, chunk), lambda bi, ti, hi: (bi, hi, ti, 0, 0)),
    ]

    def out5(r, c):
        return pl.BlockSpec((1, heads, nchunks, r, c), lambda bi, ti, hi: (bi, hi, ti, 0, 0))

    out_specs = [out5(2 * chunk, dk), out5(dk, chunk), out5(chunk, dv), out5(chunk, chunk), out5(1, dv)]
    out_shape = [
        jax.ShapeDtypeStruct((b, h, n, 2 * chunk, dk), BF16),
        jax.ShapeDtypeStruct((b, h, n, dk, chunk), BF16),
        jax.ShapeDtypeStruct((b, h, n, chunk, dv), F32),
        jax.ShapeDtypeStruct((b, h, n, chunk, chunk), BF16),
        jax.ShapeDtypeStruct((b, h, n, 1, dv), F32),
    ]
    return pl.pallas_call(
        functools.partial(_gdn_prep_kernel, dk=dk, dv=dv, chunk=chunk, nchunks=nchunks, heads=heads),
        grid=(b, tiles, h // heads),
        in_specs=in_specs,
        out_specs=out_specs,
        out_shape=out_shape,
        scratch_shapes=[pltpu.VMEM((HALO_ROWS + rows, kw), F32), pltpu.VMEM((HALO_ROWS + rows, vw), F32)],
        compiler_params=_cparams(("parallel", "parallel", "parallel")),
        name="gdn_prep",
    )(proj, proj, proj, proj, proj, proj, conv_w, conv_w, conv_w, beta_c, gc_c, gc_r)


def _gdn_scan_kernel(wq_ref, kpt_ref, u_ref, attn_ref, egl_ref, f_ref, nw_ref, o_ref, s_scr,
                     *, dv, chunk, nchunks, heads):
    @pl.when(pl.program_id(2) == 0)
    def _():
        s_scr[...] = jnp.zeros_like(s_scr)

    nw = nw_ref[...]
    hr = range(heads)
    for c in range(nchunks):
        rs = slice(c * chunk, (c + 1) * chunk)
        states = [s_scr[hh] for hh in hr]
        r = [_dot(wq_ref[0, hh, c], states[hh].astype(BF16)) for hh in hr]
        v_new = [(u_ref[0, hh, c] - r[hh][0:chunk]).astype(BF16) for hh in hr]
        upd = [_dot(kpt_ref[0, hh, c], v_new[hh]) for hh in hr]
        for hh in hr:
            s_scr[hh] = states[hh] * egl_ref[0, hh, c] + upd[hh]
        o = [r[hh][chunk:2 * chunk] + _dot(attn_ref[0, hh, c], v_new[hh]) for hh in hr]
        for hh in hr:
            cs = slice(hh * dv, (hh + 1) * dv)
            on = o[hh] * lax.rsqrt(jnp.mean(o[hh] * o[hh], axis=-1, keepdims=True) + EPS) * nw
            o_ref[rs, cs] = (on * f_ref[rs, cs].astype(F32)).astype(o_ref.dtype)


def _gdn_scan(wq, kpt, u, attn, egl, gates, gdn_nw, dims, nchunks, heads, fcol):
    b, s, h, dk, dv = dims
    chunk = GDN_CHUNK
    n = s // chunk
    rows = nchunks * chunk
    steps = n // nchunks
    width = heads * dv

    def in5(r, c):
        return pl.BlockSpec((1, heads, nchunks, r, c), lambda bi, gi, i: (bi, gi, i, 0, 0))

    in_specs = [
        in5(2 * chunk, dk), in5(dk, chunk), in5(chunk, dv), in5(chunk, chunk), in5(1, dv),
        pl.BlockSpec((rows, width), lambda bi, gi, i: (bi * steps + i, fcol // width + gi)),
        pl.BlockSpec((1, dv), lambda bi, gi, i: (0, 0)),
    ]
    return pl.pallas_call(
        functools.partial(_gdn_scan_kernel, dv=dv, chunk=chunk, nchunks=nchunks, heads=heads),
        grid=(b, h // heads, steps),
        in_specs=in_specs,
        out_specs=pl.BlockSpec((rows, width), lambda bi, gi, i: (bi * steps + i, gi)),
        out_shape=jax.ShapeDtypeStruct((b * s, h * dv), BF16),
        scratch_shapes=[pltpu.VMEM((heads, dk, dv), F32)],
        compiler_params=_cparams(("parallel", "parallel", "arbitrary")),
        name="gdn_scan",
    )(wq, kpt, u, attn, egl, gates, gdn_nw)


def _retention_kernel(q_ref, k_ref, v_ref, f_ref, ma_ref, cos_ref, sin_ref, dmat_ref,
                      xi_ref, zeta_ref, gpow_ref, nw_ref, o_ref, s_scr, *, dk, dv, heads):
    @pl.when(pl.program_id(2) == 0)
    def _():
        s_scr[...] = jnp.zeros_like(s_scr)

    cos = cos_ref[...]
    sin = sin_ref[...]

    def rotary(x):
        return x * cos + pltpu.roll(x, dk // 2, 1) * sin

    hr = range(heads)
    ksl = [slice(hh * dk, (hh + 1) * dk) for hh in hr]
    vsl = [slice(hh * dv, (hh + 1) * dv) for hh in hr]
    q = [rotary(q_ref[:, ksl[hh]].astype(F32)).astype(BF16) for hh in hr]
    k = [rotary(k_ref[:, ksl[hh]].astype(F32)) * (dk ** -0.5) for hh in hr]
    kz = [(k[hh] * zeta_ref[hh]).astype(BF16) for hh in hr]
    states = [s_scr[hh] for hh in hr]
    scores = [(_dot_nt(q[hh], k[hh].astype(BF16)) * dmat_ref[hh]).astype(BF16) for hh in hr]
    inter = [_dot(q[hh], states[hh].astype(BF16)) * xi_ref[hh] for hh in hr]
    upd = [_dot_tn(kz[hh], v_ref[:, vsl[hh]]) for hh in hr]
    for hh in hr:
        s_scr[hh] = states[hh] * gpow_ref[hh] + upd[hh]
    o = [_dot(scores[hh], v_ref[:, vsl[hh]]) + inter[hh] for hh in hr]
    for hh in hr:
        mu = jnp.mean(o[hh], axis=-1, keepdims=True)
        dev = o[hh] - mu
        var = jnp.mean(dev * dev, axis=-1, keepdims=True)
        ob = dev * lax.rsqrt(var + EPS) * nw_ref[:, vsl[hh]]
        mixed = ob * f_ref[:, vsl[hh]].astype(F32) + ma_ref[:, vsl[hh]].astype(F32)
        o_ref[:, vsl[hh]] = mixed.astype(o_ref.dtype)


def _retention(proj, gates, ma, ret_nw, tables, dims, rows, heads, cols):
    b, s, h, dk, dv = dims
    cosf, sins, dmat, xi, zeta, gpow = tables
    rqcol, rkcol, rvcol, fcol = cols
    steps = s // rows
    kw, vw = heads * dk, heads * dv

    def rb(bi, i):
        return bi * steps + i

    in_specs = [
        pl.BlockSpec((rows, kw), lambda bi, gi, i: (rb(bi, i), rqcol // kw + gi)),
        pl.BlockSpec((rows, kw), lambda bi, gi, i: (rb(bi, i), rkcol // kw + gi)),
        pl.BlockSpec((rows, vw), lambda bi, gi, i: (rb(bi, i), rvcol // vw + gi)),
        pl.BlockSpec((rows, vw), lambda bi, gi, i: (rb(bi, i), fcol // vw + gi)),
        pl.BlockSpec((rows, vw), lambda bi, gi, i: (rb(bi, i), gi)),
        pl.BlockSpec((rows, dk), lambda bi, gi, i: (i, 0)),
        pl.BlockSpec((rows, dk), lambda bi, gi, i: (i, 0)),
        pl.BlockSpec((heads, rows, rows), lambda bi, gi, i: (gi, 0, 0)),
        pl.BlockSpec((heads, rows, 1), lambda bi, gi, i: (gi, 0, 0)),
        pl.BlockSpec((heads, rows, 1), lambda bi, gi, i: (gi, 0, 0)),
        pl.BlockSpec((heads, 1, dv), lambda bi, gi, i: (gi, 0, 0)),
        pl.BlockSpec((1, vw), lambda bi, gi, i: (0, gi)),
    ]
    return pl.pallas_call(
        functools.partial(_retention_kernel, dk=dk, dv=dv, heads=heads),
        grid=(b, h // heads, steps),
        in_specs=in_specs,
        out_specs=pl.BlockSpec((rows, vw), lambda bi, gi, i: (rb(bi, i), gi)),
        out_shape=jax.ShapeDtypeStruct((b * s, h * dv), BF16),
        scratch_shapes=[pltpu.VMEM((heads, dk, dv), F32)],
        compiler_params=_cparams(("parallel", "parallel", "arbitrary")),
        name="retention",
    )(proj, proj, proj, gates, ma, cosf, sins, dmat, xi, zeta, gpow, ret_nw)


def _retention_tables(s, h, dk, dv, rows):
    inv = ROPE_BASE ** (-jnp.arange(0, dk, 2, dtype=F32) / dk)
    ang = jnp.arange(s, dtype=F32)[:, None] * inv[None, :]
    cos, sin = jnp.cos(ang), jnp.sin(ang)
    cosf = jnp.concatenate([cos, cos], axis=-1)
    sins = jnp.concatenate([-sin, sin], axis=-1)
    log_gamma = jnp.log1p(-jnp.exp2(-5.0 - jnp.arange(h, dtype=F32)))
    pos = jnp.arange(rows, dtype=F32)
    dist = pos[:, None] - pos[None, :]
    dmat = jnp.exp(jnp.where(dist >= 0, dist * log_gamma[:, None, None], -jnp.inf))
    xi = jnp.exp((pos + 1.0) * log_gamma[:, None])[:, :, None]
    zeta = jnp.exp((rows - 1.0 - pos) * log_gamma[:, None])[:, :, None]
    gpow = jnp.broadcast_to(jnp.exp(rows * log_gamma)[:, None, None], (h, 1, dv))
    return cosf, sins, dmat, xi, zeta, gpow


def _outproj_kernel(m_ref, w_ref, x_ref, nw_ref, h_ref, hn_ref):
    hres = x_ref[...] + _dot(m_ref[...], w_ref[...])
    h_ref[...] = hres
    ms = jnp.mean(hres * hres, axis=-1, keepdims=True)
    hn_ref[...] = ((hres * lax.rsqrt(ms + EPS)) * nw_ref[...]).astype(hn_ref.dtype)


def _outproj(mixed, w, x2, nw, tm):
    m, d = x2.shape
    kdim = mixed.shape[1]
    return pl.pallas_call(
        _outproj_kernel,
        grid=(m // tm,),
        in_specs=[
            pl.BlockSpec((tm, kdim), lambda i: (i, 0)),
            pl.BlockSpec((kdim, d), lambda i: (0, 0)),
            pl.BlockSpec((tm, d), lambda i: (i, 0)),
            pl.BlockSpec((1, d), lambda i: (0, 0)),
        ],
        out_specs=[pl.BlockSpec((tm, d), lambda i: (i, 0))] * 2,
        out_shape=[jax.ShapeDtypeStruct((m, d), F32), jax.ShapeDtypeStruct((m, d), BF16)],
        compiler_params=_cparams(("parallel",)),
        name="outproj",
    )(mixed, w, x2, nw)


def _ffn_kernel(hn_ref, h_ref, wg_ref, wu_ref, wd_ref, nw_ref, o_ref):
    f = pl.program_id(1)

    @pl.when(f == 0)
    def _():
        o_ref[...] = h_ref[...]

    hn = hn_ref[...]
    g = _dot(hn, wg_ref[...])
    up = _dot(hn, wu_ref[...])
    act = (g * _sigmoid(g) * up).astype(BF16)
    o_ref[...] += _dot(act, wd_ref[...])

    @pl.when(f == pl.num_programs(1) - 1)
    def _():
        hres = o_ref[...]
        ms = jnp.mean(hres * hres, axis=-1, keepdims=True)
        o_ref[...] = (hres * lax.rsqrt(ms + EPS)) * nw_ref[...]


def _ffn(hn, hres, wg, wu, wd, nw, tm, tf):
    m, d = hres.shape
    ff = wg.shape[1]
    return pl.pallas_call(
        _ffn_kernel,
        grid=(m // tm, ff // tf),
        in_specs=[
            pl.BlockSpec((tm, d), lambda i, j: (i, 0)),
            pl.BlockSpec((tm, d), lambda i, j: (i, 0)),
            pl.BlockSpec((d, tf), lambda i, j: (0, j)),
            pl.BlockSpec((d, tf), lambda i, j: (0, j)),
            pl.BlockSpec((tf, d), lambda i, j: (j, 0)),
            pl.BlockSpec((1, d), lambda i, j: (0, 0)),
        ],
        out_specs=pl.BlockSpec((tm, d), lambda i, j: (i, 0)),
        out_shape=jax.ShapeDtypeStruct((m, d), F32),
        compiler_params=_cparams(("parallel", "arbitrary")),
        name="ffn",
    )(hn, hres, wg, wu, wd, nw)


def _pick(total, want):
    t = min(total, want)
    while total % t:
        t //= 2
    return t


def _layer(h2, dims, norm1_w, w_in, conv_w, a_log, dt_bias, gdn_norm_w, ret_norm_w,
           w_out, norm2_w, w_gate, w_up, w_down, out_norm_w):
    b, s, h, dk, dv = dims
    m, d = h2.shape
    qk, vd = h * dk, h * dv
    chunk = GDN_CHUNK
    n = s // chunk

    sc0 = 2 * qk + 2 * vd
    w_in_t = w_in.T
    w_main = _drop_rows(w_in_t, sc0, 2 * h, _pick(sc0, 1024))
    w_ba = w_in_t[sc0:sc0 + 2 * h].astype(BF16)
    mixw = 2 * qk + vd
    zrow = mixw
    rgrow = zrow + vd + mixw
    garow = rgrow + vd
    gbrow = garow + vd

    tn = _pick(np.gcd(mixw, vd), 1024)
    proj, u = _inproj(h2, norm1_w.reshape(1, d), w_main, _pick(m, 1024), tn,
                      2 * mixw, mixw // tn, vd // tn)
    gates = _gates(u, w_main, _pick(m, 1024), _pick(vd, 512), vd, ((zrow, garow), (rgrow, gbrow)))

    beta, gcum = _scalars(u, w_ba, a_log.reshape(1, h), dt_bias.reshape(1, h), _pick(m, 512), chunk)

    gc_r = gcum.reshape(b, n, chunk, h).transpose(0, 3, 1, 2)[:, :, :, None, :]
    wq, kpt, uu, attn, egl = _gdn_prep(proj, conv_w, beta, gcum, gc_r, dims, _pick(s, 512), _pick(h, 2))

    scan_heads = _pick(h, 8)
    ma = _gdn_scan(wq, kpt, uu, attn, egl, gates, gdn_norm_w.reshape(1, dv), dims,
                   _pick(n, 2), scan_heads, 0)

    ret_rows = _pick(s, 256)
    tables = _retention_tables(s, h, dk, dv, ret_rows)
    mixed = _retention(proj, gates, ma, ret_norm_w.reshape(1, vd), tables, dims, ret_rows, _pick(h, 8),
                       (mixw, mixw + qk, mixw + 2 * qk, vd))

    hres, hn = _outproj(mixed, w_out.astype(BF16), h2, norm2_w.reshape(1, d), _pick(m, 256))

    return _ffn(hn, hres, w_gate.astype(BF16), w_up.astype(BF16), w_down.astype(BF16),
                out_norm_w.reshape(1, d), _pick(m, 512), _pick(w_gate.shape[1], 512))


def kernel(x, norm1_w, w_in, conv_w, a_log, dt_bias, gdn_norm_w, ret_norm_w, w_out, norm2_w,
           w_gate, w_up, w_down, norm_f_w):
    b, s, d = x.shape
    depth, h = a_log.shape
    assert depth == 1, "the final norm is fused into the single layer's FFN kernel"
    dk = d // 16
    dv = d // h
    dims = (b, s, h, dk, dv)
    out = _layer(x.reshape(b * s, d), dims, norm1_w[0], w_in[0], conv_w[0], a_log[0], dt_bias[0],
                 gdn_norm_w[0], ret_norm_w[0], w_out[0], norm2_w[0], w_gate[0], w_up[0], w_down[0],
                 norm_f_w)
    return out.reshape(b, s, d)
```

```python
import functools

import jax
import jax.numpy as jnp
import numpy as np
from jax import lax
from jax.experimental import pallas as pl
from jax.experimental.pallas import tpu as pltpu

F32 = jnp.float32
BF16 = jnp.bfloat16

EPS = 1e-6
CONV_WIDTH = 4
GDN_CHUNK = 64
ROPE_BASE = 10000.0
HALO_ROWS = 16
VMEM_LIMIT_BYTES = 56 * 1024 * 1024


def _cparams(sem):
    return pltpu.CompilerParams(dimension_semantics=sem, vmem_limit_bytes=VMEM_LIMIT_BYTES)


def _sigmoid(x):
    return 1.0 / (1.0 + jnp.exp(-x))


def _dot(a, b):
    return jnp.dot(a, b, preferred_element_type=F32)


def _dot_nt(a, b):
    return lax.dot_general(a, b, (((1,), (1,)), ((), ())), preferred_element_type=F32)


def _dot_tn(a, b):
    return lax.dot_general(a, b, (((0,), (0,)), ((), ())), preferred_element_type=F32)


def _drop_rows_kernel(cur_ref, nxt_ref, o_ref, *, first_shifted, count):
    j = pl.program_id(0)
    tr = o_ref.shape[0]

    @pl.when(j < first_shifted)
    def _():
        o_ref[...] = cur_ref[...].astype(o_ref.dtype)

    @pl.when(j >= first_shifted)
    def _():
        o_ref[0:tr - count, :] = cur_ref[count:tr, :].astype(o_ref.dtype)
        o_ref[tr - count:tr, :] = nxt_ref[...].astype(o_ref.dtype)


def _drop_rows(wt, start, count, tr):
    n_in, d = wt.shape
    n = n_in - count
    assert start % tr == 0 and n % tr == 0 and tr % count == 0 and count % HALO_ROWS == 0
    return pl.pallas_call(
        functools.partial(_drop_rows_kernel, first_shifted=start // tr, count=count),
        grid=(n // tr,),
        in_specs=[
            pl.BlockSpec((tr, d), lambda j: (j, 0)),
            pl.BlockSpec((count, d), lambda j: ((j + 1) * (tr // count), 0)),
        ],
        out_specs=pl.BlockSpec((tr, d), lambda j: (j, 0)),
        out_shape=jax.ShapeDtypeStruct((n, d), BF16),
        compiler_params=_cparams(("parallel",)),
        name="drop_rows",
    )(wt, wt)


def _inproj_kernel(x_ref, nw_ref, w_ref, o_ref, u_ref):
    @pl.when(pl.program_id(1) == 0)
    def _():
        xf = x_ref[...]
        ms = jnp.mean(xf * xf, axis=-1, keepdims=True)
        u_ref[...] = ((xf * lax.rsqrt(ms + EPS)) * nw_ref[...]).astype(u_ref.dtype)

    o_ref[...] = _dot_nt(u_ref[...], w_ref[...]).astype(o_ref.dtype)


def _inproj(x2, nw, w, tm, tn, n, skip_at, skip):
    m, d = x2.shape
    return pl.pallas_call(
        _inproj_kernel,
        grid=(m // tm, n // tn),
        in_specs=[
            pl.BlockSpec((tm, d), lambda i, j: (i, 0)),
            pl.BlockSpec((1, d), lambda i, j: (0, 0)),
            pl.BlockSpec((tn, d), lambda i, j: (jnp.where(j < skip_at, j, j + skip), 0)),
        ],
        out_specs=[
            pl.BlockSpec((tm, tn), lambda i, j: (i, j)),
            pl.BlockSpec((tm, d), lambda i, j: (i, 0)),
        ],
        out_shape=[jax.ShapeDtypeStruct((m, n), BF16), jax.ShapeDtypeStruct((m, d), BF16)],
        compiler_params=_cparams(("parallel", "arbitrary")),
        name="inproj",
    )(x2, nw, w)


def _gates_kernel(u_ref, ws_ref, wg_ref, o_ref):
    u = u_ref[...]
    a = _dot_nt(u, ws_ref[...])
    g = _dot_nt(u, wg_ref[...])
    o_ref[...] = (a / ((1.0 + jnp.exp(-a)) * (1.0 + jnp.exp(-g)))).astype(o_ref.dtype)


def _gates(u, w, tm, tg, vd, pairs):
    m, d = u.shape
    per = vd // tg

    def blk(j, which):
        idx = pairs[0][which] // tg + j
        for p in range(1, len(pairs)):
            idx = jnp.where(j >= p * per, pairs[p][which] // tg + j - p * per, idx)
        return idx

    return pl.pallas_call(
        _gates_kernel,
        grid=(m // tm, len(pairs) * per),
        in_specs=[
            pl.BlockSpec((tm, d), lambda i, j: (i, 0)),
            pl.BlockSpec((tg, d), lambda i, j: (blk(j, 0), 0)),
            pl.BlockSpec((tg, d), lambda i, j: (blk(j, 1), 0)),
        ],
        out_specs=pl.BlockSpec((tm, tg), lambda i, j: (i, j)),
        out_shape=jax.ShapeDtypeStruct((m, len(pairs) * vd), BF16),
        compiler_params=_cparams(("parallel", "arbitrary")),
        name="gates",
    )(u, w, w)


def _scalars_kernel(u_ref, wba_ref, alog_ref, dtb_ref, beta_ref, gc_ref, *, chunk, heads, sub):
    parts = [slice(p * sub, (p + 1) * sub) for p in range(u_ref.shape[0] // sub)]
    row = lax.broadcasted_iota(jnp.int32, (sub, sub), 0)
    col = lax.broadcasted_iota(jnp.int32, (sub, sub), 1)
    incl = jnp.where(((row // chunk) == (col // chunk)) & (row >= col), 1.0, 0.0).astype(BF16)
    bas = [_dot_nt(u_ref[sl, :], wba_ref[...]) for sl in parts]
    gs = []
    for sl, ba in zip(parts, bas):
        beta_ref[sl, :] = _sigmoid(ba[:, 0:heads])
        a = ba[:, heads:2 * heads] + dtb_ref[...]
        softplus = jnp.maximum(a, 0.0) + jnp.log1p(jnp.exp(-jnp.abs(a)))
        gs.append(-jnp.exp(alog_ref[...]) * softplus)
    g1 = [g.astype(BF16) for g in gs]
    r1 = [g - p1.astype(F32) for g, p1 in zip(gs, g1)]
    g2 = [r.astype(BF16) for r in r1]
    g3 = [(r - p2.astype(F32)).astype(BF16) for r, p2 in zip(r1, g2)]
    gcs = [_dot(incl, p1) + _dot(incl, p2) + _dot(incl, p3) for p1, p2, p3 in zip(g1, g2, g3)]
    for sl, gc in zip(parts, gcs):
        gc_ref[sl, :] = gc


def _scalars(u, wba, alog, dtb, tm, sub, chunk):
    m, d = u.shape
    h = alog.shape[1]
    assert tm % sub == 0 and sub % chunk == 0
    out = jax.ShapeDtypeStruct((m, h), F32)
    return pl.pallas_call(
        functools.partial(_scalars_kernel, chunk=chunk, heads=h, sub=sub),
        grid=(m // tm,),
        in_specs=[
            pl.BlockSpec((tm, d), lambda i: (i, 0)),
            pl.BlockSpec((2 * h, d), lambda i: (0, 0)),
            pl.BlockSpec((1, h), lambda i: (0, 0)),
            pl.BlockSpec((1, h), lambda i: (0, 0)),
        ],
        out_specs=[pl.BlockSpec((tm, h), lambda i: (i, 0))] * 2,
        out_shape=[out, out],
        compiler_params=_cparams(("parallel",)),
        name="gate_scalars",
    )(u, wba, alog, dtb)


def _gdn_prep_kernel(q_ref, qh_ref, k_ref, kh_ref, v_ref, vh_ref, cwq_ref, cwk_ref, cwv_ref,
                     beta_ref, gcc_ref, gcr_ref,
                     wq_ref, kpt_ref, u_ref, attn_ref, egl_ref,
                     extk_scr, extv_scr, *, dk, dv, chunk, nchunks, heads):
    t = pl.program_id(1)
    rows = chunk * nchunks
    lane_head = lax.broadcasted_iota(jnp.int32, (1, beta_ref.shape[1]), 1)

    def head_col(ref, hh):
        hsel = lane_head == pl.program_id(2) * heads + hh
        return jnp.sum(jnp.where(hsel, ref[...], 0.0), axis=1, keepdims=True)

    def conv_silu(x_ref, h_ref, cw_ref, ext_scr):
        halo = jnp.where(t == 0, 0.0, h_ref[...].astype(F32))
        ext_scr[0:HALO_ROWS, :] = halo
        ext_scr[HALO_ROWS:HALO_ROWS + rows, :] = x_ref[...].astype(F32)
        acc = ext_scr[HALO_ROWS:HALO_ROWS + rows, :] * cw_ref[CONV_WIDTH - 1:CONV_WIDTH, :]
        for j in range(1, CONV_WIDTH):
            acc = acc + (ext_scr[HALO_ROWS - j:HALO_ROWS - j + rows, :]
                         * cw_ref[CONV_WIDTH - 1 - j:CONV_WIDTH - j, :])
        return acc * _sigmoid(acc)

    def l2norm(y):
        return y * lax.rsqrt(jnp.sum(y * y, axis=-1, keepdims=True) + EPS)

    q_all = conv_silu(q_ref, qh_ref, cwq_ref, extk_scr)
    k_all = conv_silu(k_ref, kh_ref, cwk_ref, extk_scr)
    v_all = conv_silu(v_ref, vh_ref, cwv_ref, extv_scr)

    ri = lax.broadcasted_iota(jnp.int32, (chunk, chunk), 0)
    ci = lax.broadcasted_iota(jnp.int32, (chunk, chunk), 1)
    incl = ri >= ci
    strict = ri > ci
    eye = jnp.where(ri == ci, 1.0, 0.0).astype(F32)
    ident_k = jnp.where(lax.broadcasted_iota(jnp.int32, (dk, dk), 0)
                        == lax.broadcasted_iota(jnp.int32, (dk, dk), 1), 1.0, 0.0).astype(BF16)

    pairs = [(hh, c) for hh in range(heads) for c in range(nchunks)]
    cr = range(len(pairs))
    qs, ks, vs, bs, gcs = [], [], [], [], []
    for hh in range(heads):
        q = l2norm(q_all[:, hh * dk:(hh + 1) * dk]) * (dk ** -0.5)
        k = l2norm(k_all[:, hh * dk:(hh + 1) * dk])
        v = v_all[:, hh * dv:(hh + 1) * dv]
        beta, gcum = head_col(beta_ref, hh), head_col(gcc_ref, hh)
        for c in range(nchunks):
            sl = slice(c * chunk, (c + 1) * chunk)
            qs.append(q[sl])
            ks.append(k[sl])
            vs.append(v[sl])
            bs.append(beta[sl])
            gcs.append(gcum[sl])
    gls = [gc[chunk - 1:chunk, :] for gc in gcs]
    decays = [jnp.exp(jnp.where(incl, gcs[i] - gcr_ref[0, hh, c], -1e30)) for i, (hh, c) in enumerate(pairs)]
    egs = [jnp.exp(gc) for gc in gcs]
    kbs = [ks[c] * bs[c] for c in cr]
    kq = [_dot_nt(jnp.concatenate([kbs[c], qs[c]], axis=0).astype(BF16), ks[c].astype(BF16)) for c in cr]
    amats = [jnp.where(strict, kq[c][0:chunk] * decays[c], 0.0) for c in cr]
    for i, (hh, c) in enumerate(pairs):
        attn_ref[0, hh, c] = (kq[i][chunk:2 * chunk] * decays[i]).astype(attn_ref.dtype)
    tinvs = [eye - a for a in amats]
    apows = amats
    span = 2
    while span < chunk:
        apows = [_dot(ab, ab) for ab in [a.astype(BF16) for a in apows]]
        tinvs = [tinvs[c] + _dot(tinvs[c].astype(BF16), apows[c].astype(BF16)) for c in cr]
        span *= 2
    rhs = [jnp.concatenate([kbs[c] * egs[c], vs[c] * bs[c]], axis=1).astype(BF16) for c in cr]
    wu = [_dot(tinvs[c].astype(BF16), rhs[c]) for c in cr]
    kpt = [_dot_nt(ident_k, (ks[c] * jnp.exp(gls[c] - gcs[c])).astype(BF16)) for c in cr]
    for i, (hh, c) in enumerate(pairs):
        wq_ref[0, hh, c, 0:chunk, :] = wu[i][:, 0:dk].astype(wq_ref.dtype)
        wq_ref[0, hh, c, chunk:2 * chunk, :] = (qs[i] * egs[i]).astype(wq_ref.dtype)
        kpt_ref[0, hh, c] = kpt[i].astype(kpt_ref.dtype)
        u_ref[0, hh, c] = wu[i][:, dk:dk + dv]
        egl_ref[0, hh, c] = jnp.broadcast_to(jnp.exp(gls[i]), (1, dv))


def _gdn_prep(proj, conv_w, beta_c, gc_c, gc_r, dims, rows, heads):
    b, s, h, dk, dv = dims
    chunk = GDN_CHUNK
    nchunks = rows // chunk
    n = s // chunk
    qk = h * dk
    tiles = s // rows
    kw, vw = heads * dk, heads * dv
    kcol = qk // kw
    vcol = 2 * qk // vw

    def rowblk(bi, ti):
        return bi * tiles + ti

    def halo(bi, ti):
        return jnp.maximum((bi * s + ti * rows) // HALO_ROWS - 1, 0)

    col_spec = pl.BlockSpec((rows, h), lambda bi, ti, hi: (rowblk(bi, ti), 0))
    in_specs = [
        pl.BlockSpec((rows, kw), lambda bi, ti, hi: (rowblk(bi, ti), hi)),
        pl.BlockSpec((HALO_ROWS, kw), lambda bi, ti, hi: (halo(bi, ti), hi)),
        pl.BlockSpec((rows, kw), lambda bi, ti, hi: (rowblk(bi, ti), kcol + hi)),
        pl.BlockSpec((HALO_ROWS, kw), lambda bi, ti, hi: (halo(bi, ti), kcol + hi)),
        pl.BlockSpec((rows, vw), lambda bi, ti, hi: (rowblk(bi, ti), vcol + hi)),
        pl.BlockSpec((HALO_ROWS, vw), lambda bi, ti, hi: (halo(bi, ti), vcol + hi)),
        pl.BlockSpec((CONV_WIDTH, kw), lambda bi, ti, hi: (0, hi)),
        pl.BlockSpec((CONV_WIDTH, kw), lambda bi, ti, hi: (0, kcol + hi)),
        pl.BlockSpec((CONV_WIDTH, vw), lambda bi, ti, hi: (0, vcol + hi)),
        col_spec, col_spec,
        pl.BlockSpec((1, heads, nchunks, 1, chunk), lambda bi, ti, hi: (bi, hi, ti, 0, 0)),
    ]

    def out5(r, c):
        return pl.BlockSpec((1, heads, nchunks, r, c), lambda bi, ti, hi: (bi, hi, ti, 0, 0))

    out_specs = [out5(2 * chunk, dk), out5(dk, chunk), out5(chunk, dv), out5(chunk, chunk), out5(1, dv)]
    out_shape = [
        jax.ShapeDtypeStruct((b, h, n, 2 * chunk, dk), BF16),
        jax.ShapeDtypeStruct((b, h, n, dk, chunk), BF16),
        jax.ShapeDtypeStruct((b, h, n, chunk, dv), F32),
        jax.ShapeDtypeStruct((b, h, n, chunk, chunk), BF16),
        jax.ShapeDtypeStruct((b, h, n, 1, dv), F32),
    ]
    return pl.pallas_call(
        functools.partial(_gdn_prep_kernel, dk=dk, dv=dv, chunk=chunk, nchunks=nchunks, heads=heads),
        grid=(b, tiles, h // heads),
        in_specs=in_specs,
        out_specs=out_specs,
        out_shape=out_shape,
        scratch_shapes=[pltpu.VMEM((HALO_ROWS + rows, kw), F32), pltpu.VMEM((HALO_ROWS + rows, vw), F32)],
        compiler_params=_cparams(("parallel", "parallel", "parallel")),
        name="gdn_prep",
    )(proj, proj, proj, proj, proj, proj, conv_w, conv_w, conv_w, beta_c, gc_c, gc_r)


def _gdn_scan_kernel(wq_ref, kpt_ref, u_ref, attn_ref, egl_ref, f_ref, nw_ref, o_ref, s_scr,
                     *, dv, chunk, nchunks, unroll, heads):
    @pl.when(pl.program_id(2) == 0)
    def _():
        s_scr[...] = jnp.zeros_like(s_scr)

    nw = nw_ref[...]
    hr = range(heads)

    def chunk_group(it, carry):
        for cc in range(unroll):
            c = it * unroll + cc
            rs = pl.ds(pl.multiple_of(c * chunk, chunk), chunk)
            states = [s_scr[hh] for hh in hr]
            r = [_dot(wq_ref[0, hh, c], states[hh].astype(BF16)) for hh in hr]
            v_new = [(u_ref[0, hh, c] - r[hh][0:chunk]).astype(BF16) for hh in hr]
            upd = [_dot(kpt_ref[0, hh, c], v_new[hh]) for hh in hr]
            for hh in hr:
                s_scr[hh] = states[hh] * egl_ref[0, hh, c] + upd[hh]
            o = [r[hh][chunk:2 * chunk] + _dot(attn_ref[0, hh, c], v_new[hh]) for hh in hr]
            for hh in hr:
                cs = slice(hh * dv, (hh + 1) * dv)
                on = o[hh] * lax.rsqrt(jnp.mean(o[hh] * o[hh], axis=-1, keepdims=True) + EPS) * nw
                o_ref[rs, cs] = (on * f_ref[rs, cs].astype(F32)).astype(o_ref.dtype)
        return carry

    lax.fori_loop(0, nchunks // unroll, chunk_group, 0)


def _gdn_scan(wq, kpt, u, attn, egl, gates, gdn_nw, dims, nchunks, unroll, heads, fcol):
    b, s, h, dk, dv = dims
    chunk = GDN_CHUNK
    n = s // chunk
    rows = nchunks * chunk
    steps = n // nchunks
    width = heads * dv
    assert nchunks % unroll == 0

    def in5(r, c):
        return pl.BlockSpec((1, heads, nchunks, r, c), lambda bi, gi, i: (bi, gi, i, 0, 0))

    in_specs = [
        in5(2 * chunk, dk), in5(dk, chunk), in5(chunk, dv), in5(chunk, chunk), in5(1, dv),
        pl.BlockSpec((rows, width), lambda bi, gi, i: (bi * steps + i, fcol // width + gi)),
        pl.BlockSpec((1, dv), lambda bi, gi, i: (0, 0)),
    ]
    return pl.pallas_call(
        functools.partial(_gdn_scan_kernel, dv=dv, chunk=chunk, nchunks=nchunks, unroll=unroll, heads=heads),
        grid=(b, h // heads, steps),
        in_specs=in_specs,
        out_specs=pl.BlockSpec((rows, width), lambda bi, gi, i: (bi * steps + i, gi)),
        out_shape=jax.ShapeDtypeStruct((b * s, h * dv), BF16),
        scratch_shapes=[pltpu.VMEM((heads, dk, dv), F32)],
        compiler_params=_cparams(("parallel", "parallel", "arbitrary")),
        name="gdn_scan",
    )(wq, kpt, u, attn, egl, gates, gdn_nw)


def _retention_kernel(q_ref, k_ref, v_ref, f_ref, ma_ref, cos_ref, sin_ref, dmat_ref,
                      xi_ref, zeta_ref, gpow_ref, nw_ref, o_ref, s_scr, *, dk, dv, heads):
    @pl.when(pl.program_id(2) == 0)
    def _():
        s_scr[...] = jnp.zeros_like(s_scr)

    cos = cos_ref[...]
    sin = sin_ref[...]

    def rotary(x):
        return x * cos + pltpu.roll(x, dk // 2, 1) * sin

    hr = range(heads)
    ksl = [slice(hh * dk, (hh + 1) * dk) for hh in hr]
    vsl = [slice(hh * dv, (hh + 1) * dv) for hh in hr]
    q = [rotary(q_ref[:, ksl[hh]].astype(F32)).astype(BF16) for hh in hr]
    k = [rotary(k_ref[:, ksl[hh]].astype(F32)) * (dk ** -0.5) for hh in hr]
    kz = [(k[hh] * zeta_ref[hh]).astype(BF16) for hh in hr]
    states = [s_scr[hh] for hh in hr]
    scores = [(_dot_nt(q[hh], k[hh].astype(BF16)) * dmat_ref[hh]).astype(BF16) for hh in hr]
    inter = [_dot(q[hh], states[hh].astype(BF16)) * xi_ref[hh] for hh in hr]
    upd = [_dot_tn(kz[hh], v_ref[:, vsl[hh]]) for hh in hr]
    for hh in hr:
        s_scr[hh] = states[hh] * gpow_ref[hh] + upd[hh]
    o = [_dot(scores[hh], v_ref[:, vsl[hh]]) + inter[hh] for hh in hr]
    for hh in hr:
        mu = jnp.mean(o[hh], axis=-1, keepdims=True)
        dev = o[hh] - mu
        var = jnp.mean(dev * dev, axis=-1, keepdims=True)
        ob = dev * lax.rsqrt(var + EPS) * nw_ref[:, vsl[hh]]
        mixed = ob * f_ref[:, vsl[hh]].astype(F32) + ma_ref[:, vsl[hh]].astype(F32)
        o_ref[:, vsl[hh]] = mixed.astype(o_ref.dtype)


def _retention(proj, gates, ma, ret_nw, tables, dims, rows, heads, cols):
    b, s, h, dk, dv = dims
    cosf, sins, dmat, xi, zeta, gpow = tables
    rqcol, rkcol, rvcol, fcol = cols
    steps = s // rows
    kw, vw = heads * dk, heads * dv

    def rb(bi, i):
        return bi * steps + i

    in_specs = [
        pl.BlockSpec((rows, kw), lambda bi, gi, i: (rb(bi, i), rqcol // kw + gi)),
        pl.BlockSpec((rows, kw), lambda bi, gi, i: (rb(bi, i), rkcol // kw + gi)),
        pl.BlockSpec((rows, vw), lambda bi, gi, i: (rb(bi, i), rvcol // vw + gi)),
        pl.BlockSpec((rows, vw), lambda bi, gi, i: (rb(bi, i), fcol // vw + gi)),
        pl.BlockSpec((rows, vw), lambda bi, gi, i: (rb(bi, i), gi)),
        pl.BlockSpec((rows, dk), lambda bi, gi, i: (i, 0)),
        pl.BlockSpec((rows, dk), lambda bi, gi, i: (i, 0)),
        pl.BlockSpec((heads, rows, rows), lambda bi, gi, i: (gi, 0, 0)),
        pl.BlockSpec((heads, rows, 1), lambda bi, gi, i: (gi, 0, 0)),
        pl.BlockSpec((heads, rows, 1), lambda bi, gi, i: (gi, 0, 0)),
        pl.BlockSpec((heads, 1, dv), lambda bi, gi, i: (gi, 0, 0)),
        pl.BlockSpec((1, vw), lambda bi, gi, i: (0, gi)),
    ]
    return pl.pallas_call(
        functools.partial(_retention_kernel, dk=dk, dv=dv, heads=heads),
        grid=(b, h // heads, steps),
        in_specs=in_specs,
        out_specs=pl.BlockSpec((rows, vw), lambda bi, gi, i: (rb(bi, i), gi)),
        out_shape=jax.ShapeDtypeStruct((b * s, h * dv), BF16),
        scratch_shapes=[pltpu.VMEM((heads, dk, dv), F32)],
        compiler_params=_cparams(("parallel", "parallel", "arbitrary")),
        name="retention",
    )(proj, proj, proj, gates, ma, cosf, sins, dmat, xi, zeta, gpow, ret_nw)


def _retention_tables(s, h, dk, dv, rows):
    inv = ROPE_BASE ** (-jnp.arange(0, dk, 2, dtype=F32) / dk)
    ang = jnp.arange(s, dtype=F32)[:, None] * inv[None, :]
    cos, sin = jnp.cos(ang), jnp.sin(ang)
    cosf = jnp.concatenate([cos, cos], axis=-1)
    sins = jnp.concatenate([-sin, sin], axis=-1)
    log_gamma = jnp.log1p(-jnp.exp2(-5.0 - jnp.arange(h, dtype=F32)))
    pos = jnp.arange(rows, dtype=F32)
    dist = pos[:, None] - pos[None, :]
    dmat = jnp.exp(jnp.where(dist >= 0, dist * log_gamma[:, None, None], -jnp.inf))
    xi = jnp.exp((pos + 1.0) * log_gamma[:, None])[:, :, None]
    zeta = jnp.exp((rows - 1.0 - pos) * log_gamma[:, None])[:, :, None]
    gpow = jnp.broadcast_to(jnp.exp(rows * log_gamma)[:, None, None], (h, 1, dv))
    return cosf, sins, dmat, xi, zeta, gpow


def _outproj_kernel(m_ref, w_ref, x_ref, nw_ref, h_ref, hn_ref):
    hres = x_ref[...] + _dot(m_ref[...], w_ref[...])
    h_ref[...] = hres
    ms = jnp.mean(hres * hres, axis=-1, keepdims=True)
    hn_ref[...] = ((hres * lax.rsqrt(ms + EPS)) * nw_ref[...]).astype(hn_ref.dtype)


def _outproj(mixed, w, x2, nw, tm):
    m, d = x2.shape
    kdim = mixed.shape[1]
    return pl.pallas_call(
        _outproj_kernel,
        grid=(m // tm,),
        in_specs=[
            pl.BlockSpec((tm, kdim), lambda i: (i, 0)),
            pl.BlockSpec((kdim, d), lambda i: (0, 0)),
            pl.BlockSpec((tm, d), lambda i: (i, 0)),
            pl.BlockSpec((1, d), lambda i: (0, 0)),
        ],
        out_specs=[pl.BlockSpec((tm, d), lambda i: (i, 0))] * 2,
        out_shape=[jax.ShapeDtypeStruct((m, d), F32), jax.ShapeDtypeStruct((m, d), BF16)],
        compiler_params=_cparams(("parallel",)),
        name="outproj",
    )(mixed, w, x2, nw)


def _ffn_kernel(hn_ref, h_ref, wg_ref, wu_ref, wd_ref, nw_ref, o_ref):
    f = pl.program_id(1)

    @pl.when(f == 0)
    def _():
        o_ref[...] = h_ref[...]

    hn = hn_ref[...]
    g = _dot(hn, wg_ref[...])
    up = _dot(hn, wu_ref[...])
    act = (g * _sigmoid(g) * up).astype(BF16)
    o_ref[...] += _dot(act, wd_ref[...])

    @pl.when(f == pl.num_programs(1) - 1)
    def _():
        hres = o_ref[...]
        ms = jnp.mean(hres * hres, axis=-1, keepdims=True)
        o_ref[...] = (hres * lax.rsqrt(ms + EPS)) * nw_ref[...]


def _ffn(hn, hres, wg, wu, wd, nw, tm, tf):
    m, d = hres.shape
    ff = wg.shape[1]
    return pl.pallas_call(
        _ffn_kernel,
        grid=(m // tm, ff // tf),
        in_specs=[
            pl.BlockSpec((tm, d), lambda i, j: (i, 0)),
            pl.BlockSpec((tm, d), lambda i, j: (i, 0)),
            pl.BlockSpec((d, tf), lambda i, j: (0, j)),
            pl.BlockSpec((d, tf), lambda i, j: (0, j)),
            pl.BlockSpec((tf, d), lambda i, j: (j, 0)),
            pl.BlockSpec((1, d), lambda i, j: (0, 0)),
        ],
        out_specs=pl.BlockSpec((tm, d), lambda i, j: (i, 0)),
        out_shape=jax.ShapeDtypeStruct((m, d), F32),
        compiler_params=_cparams(("parallel", "arbitrary")),
        name="ffn",
    )(hn, hres, wg, wu, wd, nw)


def _pick(total, want):
    t = min(total, want)
    while total % t:
        t //= 2
    return t


def _layer(h2, dims, norm1_w, w_in, conv_w, a_log, dt_bias, gdn_norm_w, ret_norm_w,
           w_out, norm2_w, w_gate, w_up, w_down, out_norm_w):
    b, s, h, dk, dv = dims
    m, d = h2.shape
    qk, vd = h * dk, h * dv
    chunk = GDN_CHUNK
    n = s // chunk

    sc0 = 2 * qk + 2 * vd
    w_in_t = w_in.T
    w_main = _drop_rows(w_in_t, sc0, 2 * h, _pick(sc0, 1024))
    w_ba = w_in_t[sc0:sc0 + 2 * h].astype(BF16)
    mixw = 2 * qk + vd
    zrow = mixw
    rgrow = zrow + vd + mixw
    garow = rgrow + vd
    gbrow = garow + vd

    tn = _pick(np.gcd(mixw, vd), 1024)
    proj, u = _inproj(h2, norm1_w.reshape(1, d), w_main, _pick(m, 1024), tn,
                      2 * mixw, mixw // tn, vd // tn)
    gates = _gates(u, w_main, _pick(m, 1024), _pick(vd, 512), vd, ((zrow, garow), (rgrow, gbrow)))

    beta, gcum = _scalars(u, w_ba, a_log.reshape(1, h), dt_bias.reshape(1, h),
                          _pick(m, 1024), _pick(m, 256), chunk)

    gc_r = gcum.reshape(b, n, chunk, h).transpose(0, 3, 1, 2)[:, :, :, None, :]
    wq, kpt, uu, attn, egl = _gdn_prep(proj, conv_w, beta, gcum, gc_r, dims, _pick(s, 512), _pick(h, 2))

    ma = _gdn_scan(wq, kpt, uu, attn, egl, gates, gdn_norm_w.reshape(1, dv), dims,
                   _pick(n, 8), _pick(n, 2), _pick(h, 8), 0)

    ret_rows = _pick(s, 256)
    tables = _retention_tables(s, h, dk, dv, ret_rows)
    mixed = _retention(proj, gates, ma, ret_norm_w.reshape(1, vd), tables, dims, ret_rows, _pick(h, 8),
                       (mixw, mixw + qk, mixw + 2 * qk, vd))

    hres, hn = _outproj(mixed, w_out.astype(BF16), h2, norm2_w.reshape(1, d), _pick(m, 512))

    return _ffn(hn, hres, w_gate.astype(BF16), w_up.astype(BF16), w_down.astype(BF16),
                out_norm_w.reshape(1, d), _pick(m, 512), _pick(w_gate.shape[1], 512))


def kernel(x, norm1_w, w_in, conv_w, a_log, dt_bias, gdn_norm_w, ret_norm_w, w_out, norm2_w,
           w_gate, w_up, w_down, norm_f_w):
    b, s, d = x.shape
    depth, h = a_log.shape
    assert depth == 1, "the final norm is fused into the single layer's FFN kernel"
    dk = d // 16
    dv = d // h
    dims = (b, s, h, dk, dv)
    out = _layer(x.reshape(b * s, d), dims, norm1_w[0], w_in[0], conv_w[0], a_log[0], dt_bias[0],
                 gdn_norm_w[0], ret_norm_w[0], w_out[0], norm2_w[0], w_gate[0], w_up[0], w_down[0],
                 norm_f_w)
    return out.reshape(b, s, d)
```

```python
import functools

import jax
import jax.numpy as jnp
import numpy as np
from jax import lax
from jax.experimental import pallas as pl
from jax.experimental.pallas import tpu as pltpu

F32 = jnp.float32
BF16 = jnp.bfloat16

EPS = 1e-6
CONV_WIDTH = 4
GDN_CHUNK = 64
ROPE_BASE = 10000.0
HALO_ROWS = 16
VMEM_LIMIT_BYTES = 56 * 1024 * 1024


def _cparams(sem):
    return pltpu.CompilerParams(dimension_semantics=sem, vmem_limit_bytes=VMEM_LIMIT_BYTES)


def _sigmoid(x):
    return 1.0 / (1.0 + jnp.exp(-x))


def _dot(a, b):
    return jnp.dot(a, b, preferred_element_type=F32)


def _dot_nt(a, b):
    return lax.dot_general(a, b, (((1,), (1,)), ((), ())), preferred_element_type=F32)


def _dot_tn(a, b):
    return lax.dot_general(a, b, (((0,), (0,)), ((), ())), preferred_element_type=F32)


def _drop_rows_kernel(cur_ref, nxt_ref, o_ref, *, first_shifted, count):
    j = pl.program_id(0)
    tr = o_ref.shape[0]

    @pl.when(j < first_shifted)
    def _():
        o_ref[...] = cur_ref[...].astype(o_ref.dtype)

    @pl.when(j >= first_shifted)
    def _():
        o_ref[0:tr - count, :] = cur_ref[count:tr, :].astype(o_ref.dtype)
        o_ref[tr - count:tr, :] = nxt_ref[...].astype(o_ref.dtype)


def _drop_rows(wt, start, count, tr):
    n_in, d = wt.shape
    n = n_in - count
    assert start % tr == 0 and n % tr == 0 and tr % count == 0 and count % HALO_ROWS == 0
    return pl.pallas_call(
        functools.partial(_drop_rows_kernel, first_shifted=start // tr, count=count),
        grid=(n // tr,),
        in_specs=[
            pl.BlockSpec((tr, d), lambda j: (j, 0)),
            pl.BlockSpec((count, d), lambda j: ((j + 1) * (tr // count), 0)),
        ],
        out_specs=pl.BlockSpec((tr, d), lambda j: (j, 0)),
        out_shape=jax.ShapeDtypeStruct((n, d), BF16),
        compiler_params=_cparams(("parallel",)),
        name="drop_rows",
    )(wt, wt)


def _inproj_kernel(x_ref, nw_ref, w_ref, o_ref, u_ref):
    @pl.when(pl.program_id(1) == 0)
    def _():
        xf = x_ref[...]
        ms = jnp.mean(xf * xf, axis=-1, keepdims=True)
        u_ref[...] = ((xf * lax.rsqrt(ms + EPS)) * nw_ref[...]).astype(u_ref.dtype)

    o_ref[...] = _dot_nt(u_ref[...], w_ref[...]).astype(o_ref.dtype)


def _inproj(x2, nw, w, tm, tn, n, skip_at, skip):
    m, d = x2.shape
    return pl.pallas_call(
        _inproj_kernel,
        grid=(m // tm, n // tn),
        in_specs=[
            pl.BlockSpec((tm, d), lambda i, j: (i, 0)),
            pl.BlockSpec((1, d), lambda i, j: (0, 0)),
            pl.BlockSpec((tn, d), lambda i, j: (jnp.where(j < skip_at, j, j + skip), 0)),
        ],
        out_specs=[
            pl.BlockSpec((tm, tn), lambda i, j: (i, j)),
            pl.BlockSpec((tm, d), lambda i, j: (i, 0)),
        ],
        out_shape=[jax.ShapeDtypeStruct((m, n), BF16), jax.ShapeDtypeStruct((m, d), BF16)],
        compiler_params=_cparams(("parallel", "arbitrary")),
        name="inproj",
    )(x2, nw, w)


def _gates_kernel(u_ref, ws_ref, wg_ref, o_ref):
    u = u_ref[...]
    a = _dot_nt(u, ws_ref[...])
    g = _dot_nt(u, wg_ref[...])
    o_ref[...] = (a / ((1.0 + jnp.exp(-a)) * (1.0 + jnp.exp(-g)))).astype(o_ref.dtype)


def _gates(u, w, tm, tg, vd, pairs):
    m, d = u.shape
    per = vd // tg

    def blk(j, which):
        idx = pairs[0][which] // tg + j
        for p in range(1, len(pairs)):
            idx = jnp.where(j >= p * per, pairs[p][which] // tg + j - p * per, idx)
        return idx

    return pl.pallas_call(
        _gates_kernel,
        grid=(m // tm, len(pairs) * per),
        in_specs=[
            pl.BlockSpec((tm, d), lambda i, j: (i, 0)),
            pl.BlockSpec((tg, d), lambda i, j: (blk(j, 0), 0)),
            pl.BlockSpec((tg, d), lambda i, j: (blk(j, 1), 0)),
        ],
        out_specs=pl.BlockSpec((tm, tg), lambda i, j: (i, j)),
        out_shape=jax.ShapeDtypeStruct((m, len(pairs) * vd), BF16),
        compiler_params=_cparams(("parallel", "arbitrary")),
        name="gates",
    )(u, w, w)


def _scalars_kernel(u_ref, wba_ref, alog_ref, dtb_ref, beta_ref, gc_ref, *, chunk, heads, sub):
    parts = [slice(p * sub, (p + 1) * sub) for p in range(u_ref.shape[0] // sub)]
    row = lax.broadcasted_iota(jnp.int32, (sub, sub), 0)
    col = lax.broadcasted_iota(jnp.int32, (sub, sub), 1)
    incl = jnp.where(((row // chunk) == (col // chunk)) & (row >= col), 1.0, 0.0).astype(BF16)
    bas = [_dot_nt(u_ref[sl, :], wba_ref[...]) for sl in parts]
    gs = []
    for sl, ba in zip(parts, bas):
        beta_ref[sl, :] = _sigmoid(ba[:, 0:heads])
        a = ba[:, heads:2 * heads] + dtb_ref[...]
        softplus = jnp.maximum(a, 0.0) + jnp.log1p(jnp.exp(-jnp.abs(a)))
        gs.append(-jnp.exp(alog_ref[...]) * softplus)
    g1 = [g.astype(BF16) for g in gs]
    r1 = [g - p1.astype(F32) for g, p1 in zip(gs, g1)]
    g2 = [r.astype(BF16) for r in r1]
    g3 = [(r - p2.astype(F32)).astype(BF16) for r, p2 in zip(r1, g2)]
    gcs = [_dot(incl, p1) + _dot(incl, p2) + _dot(incl, p3) for p1, p2, p3 in zip(g1, g2, g3)]
    for sl, gc in zip(parts, gcs):
        gc_ref[sl, :] = gc


def _scalars(u, wba, alog, dtb, tm, sub, chunk):
    m, d = u.shape
    h = alog.shape[1]
    assert tm % sub == 0 and sub % chunk == 0
    out = jax.ShapeDtypeStruct((m, h), F32)
    return pl.pallas_call(
        functools.partial(_scalars_kernel, chunk=chunk, heads=h, sub=sub),
        grid=(m // tm,),
        in_specs=[
            pl.BlockSpec((tm, d), lambda i: (i, 0)),
            pl.BlockSpec((2 * h, d), lambda i: (0, 0)),
            pl.BlockSpec((1, h), lambda i: (0, 0)),
            pl.BlockSpec((1, h), lambda i: (0, 0)),
        ],
        out_specs=[pl.BlockSpec((tm, h), lambda i: (i, 0))] * 2,
        out_shape=[out, out],
        compiler_params=_cparams(("parallel",)),
        name="gate_scalars",
    )(u, wba, alog, dtb)


def _gdn_prep_kernel(q_ref, qh_ref, k_ref, kh_ref, v_ref, vh_ref, cwq_ref, cwk_ref, cwv_ref,
                     beta_ref, gcc_ref, gcr_ref,
                     wq_ref, kpt_ref, u_ref, attn_ref, egl_ref,
                     extk_scr, extv_scr, *, dk, dv, chunk, nchunks, heads):
    t = pl.program_id(1)
    rows = chunk * nchunks
    lane_head = lax.broadcasted_iota(jnp.int32, (1, beta_ref.shape[1]), 1)

    def head_col(ref, hh):
        hsel = lane_head == pl.program_id(2) * heads + hh
        return jnp.sum(jnp.where(hsel, ref[...], 0.0), axis=1, keepdims=True)

    def conv_silu(x_ref, h_ref, cw_ref, ext_scr):
        halo = jnp.where(t == 0, 0.0, h_ref[...].astype(F32))
        ext_scr[0:HALO_ROWS, :] = halo
        ext_scr[HALO_ROWS:HALO_ROWS + rows, :] = x_ref[...].astype(F32)
        acc = ext_scr[HALO_ROWS:HALO_ROWS + rows, :] * cw_ref[CONV_WIDTH - 1:CONV_WIDTH, :]
        for j in range(1, CONV_WIDTH):
            acc = acc + (ext_scr[HALO_ROWS - j:HALO_ROWS - j + rows, :]
                         * cw_ref[CONV_WIDTH - 1 - j:CONV_WIDTH - j, :])
        return acc * _sigmoid(acc)

    def l2norm(y):
        return y * lax.rsqrt(jnp.sum(y * y, axis=-1, keepdims=True) + EPS)

    q_all = conv_silu(q_ref, qh_ref, cwq_ref, extk_scr)
    k_all = conv_silu(k_ref, kh_ref, cwk_ref, extk_scr)
    v_all = conv_silu(v_ref, vh_ref, cwv_ref, extv_scr)

    ri = lax.broadcasted_iota(jnp.int32, (chunk, 2 * chunk), 0)
    li = lax.broadcasted_iota(jnp.int32, (chunk, 2 * chunk), 1)
    first = li < chunk
    ci = jnp.where(first, li, li - chunk)
    incl = ri >= ci
    strict = ri > ci
    eye = jnp.where(ri == ci, 1.0, 0.0).astype(F32)
    ident_k = jnp.where(lax.broadcasted_iota(jnp.int32, (dk, dk), 0)
                        == lax.broadcasted_iota(jnp.int32, (dk, dk), 1), 1.0, 0.0).astype(BF16)

    def blockdiag(x):
        return jnp.concatenate([jnp.where(first, x, 0.0), jnp.where(first, 0.0, x)], axis=0).astype(BF16)

    def pair_rows(xa, xb):
        z = jnp.zeros_like(xa)
        return jnp.concatenate([jnp.concatenate([xa, z], axis=1), jnp.concatenate([z, xb], axis=1)], axis=0)

    npairs = nchunks // 2
    units = [(hh, p) for hh in range(heads) for p in range(npairs)]
    ur = range(len(units))
    qs, ks, vs, bs, gcs = [], [], [], [], []
    for hh in range(heads):
        q = l2norm(q_all[:, hh * dk:(hh + 1) * dk]) * (dk ** -0.5)
        k = l2norm(k_all[:, hh * dk:(hh + 1) * dk])
        v = v_all[:, hh * dv:(hh + 1) * dv]
        beta, gcum = head_col(beta_ref, hh), head_col(gcc_ref, hh)
        for c in range(nchunks):
            sl = slice(c * chunk, (c + 1) * chunk)
            qs.append(q[sl])
            ks.append(k[sl])
            vs.append(v[sl])
            bs.append(beta[sl])
            gcs.append(gcum[sl])
    nc = range(len(qs))
    gls = [gc[chunk - 1:chunk, :] for gc in gcs]
    egs = [jnp.exp(gc) for gc in gcs]
    kbs = [ks[c] * bs[c] for c in nc]
    decays = [jnp.exp(jnp.where(incl, jnp.where(first, gcs[2 * u], gcs[2 * u + 1]) - gcr_ref[0, hh, p], -1e30))
              for u, (hh, p) in enumerate(units)]
    lhs = [jnp.concatenate([jnp.concatenate([kbs[2 * u], kbs[2 * u + 1]], axis=1),
                            jnp.concatenate([qs[2 * u], qs[2 * u + 1]], axis=1)], axis=0).astype(BF16) for u in ur]
    kq = [_dot_nt(lhs[u], pair_rows(ks[2 * u], ks[2 * u + 1]).astype(BF16)) for u in ur]
    amats = [jnp.where(strict, kq[u][0:chunk] * decays[u], 0.0) for u in ur]
    for u, (hh, p) in enumerate(units):
        attn_ref[0, hh, p] = (kq[u][chunk:2 * chunk] * decays[u]).astype(attn_ref.dtype)
    tinvs = [eye - a for a in amats]
    apows = amats
    span = 2
    while span < chunk:
        apows = [_dot(apows[u].astype(BF16), blockdiag(apows[u])) for u in ur]
        tinvs = [tinvs[u] + _dot(tinvs[u].astype(BF16), blockdiag(apows[u])) for u in ur]
        span *= 2
    rhs = [jnp.concatenate([jnp.concatenate([kbs[c] * egs[c], vs[c] * bs[c]], axis=1)
                            for c in (2 * u, 2 * u + 1)], axis=0).astype(BF16) for u in ur]
    wu = [_dot(blockdiag(tinvs[u]), rhs[u]) for u in ur]
    kps = [jnp.concatenate([ks[c] * jnp.exp(gls[c] - gcs[c]) for c in (2 * u, 2 * u + 1)], axis=0).astype(BF16)
           for u in ur]
    kpt = [_dot_nt(ident_k, kps[u]) for u in ur]
    for u, (hh, p) in enumerate(units):
        kpt_ref[0, hh, p] = kpt[u].astype(kpt_ref.dtype)
        for half in range(2):
            c = 2 * u + half
            rs = slice(half * chunk, (half + 1) * chunk)
            wq_ref[0, hh, 2 * p + half, 0:chunk, :] = wu[u][rs, 0:dk].astype(wq_ref.dtype)
            wq_ref[0, hh, 2 * p + half, chunk:2 * chunk, :] = (qs[c] * egs[c]).astype(wq_ref.dtype)
            u_ref[0, hh, 2 * p + half] = wu[u][rs, dk:dk + dv]
            egl_ref[0, hh, 2 * p + half] = jnp.broadcast_to(jnp.exp(gls[c]), (1, dv))


def _gdn_prep(proj, conv_w, beta_c, gc_c, gc_r, dims, rows, heads):
    b, s, h, dk, dv = dims
    chunk = GDN_CHUNK
    nchunks = rows // chunk
    n = s // chunk
    qk = h * dk
    tiles = s // rows
    kw, vw = heads * dk, heads * dv
    kcol = qk // kw
    vcol = 2 * qk // vw

    def rowblk(bi, ti):
        return bi * tiles + ti

    def halo(bi, ti):
        return jnp.maximum((bi * s + ti * rows) // HALO_ROWS - 1, 0)

    col_spec = pl.BlockSpec((rows, h), lambda bi, ti, hi: (rowblk(bi, ti), 0))
    in_specs = [
        pl.BlockSpec((rows, kw), lambda bi, ti, hi: (rowblk(bi, ti), hi)),
        pl.BlockSpec((HALO_ROWS, kw), lambda bi, ti, hi: (halo(bi, ti), hi)),
        pl.BlockSpec((rows, kw), lambda bi, ti, hi: (rowblk(bi, ti), kcol + hi)),
        pl.BlockSpec((HALO_ROWS, kw), lambda bi, ti, hi: (halo(bi, ti), kcol + hi)),
        pl.BlockSpec((rows, vw), lambda bi, ti, hi: (rowblk(bi, ti), vcol + hi)),
        pl.BlockSpec((HALO_ROWS, vw), lambda bi, ti, hi: (halo(bi, ti), vcol + hi)),
        pl.BlockSpec((CONV_WIDTH, kw), lambda bi, ti, hi: (0, hi)),
        pl.BlockSpec((CONV_WIDTH, kw), lambda bi, ti, hi: (0, kcol + hi)),
        pl.BlockSpec((CONV_WIDTH, vw), lambda bi, ti, hi: (0, vcol + hi)),
        col_spec, col_spec,
        pl.BlockSpec((1, heads, nchunks // 2, 1, 2 * chunk), lambda bi, ti, hi: (bi, hi, ti, 0, 0)),
    ]

    def out5(r, c):
        return pl.BlockSpec((1, heads, nchunks, r, c), lambda bi, ti, hi: (bi, hi, ti, 0, 0))

    def pair5(r, c):
        return pl.BlockSpec((1, heads, nchunks // 2, r, c), lambda bi, ti, hi: (bi, hi, ti, 0, 0))

    out_specs = [out5(2 * chunk, dk), pair5(dk, 2 * chunk), out5(chunk, dv), pair5(chunk, 2 * chunk), out5(1, dv)]
    out_shape = [
        jax.ShapeDtypeStruct((b, h, n, 2 * chunk, dk), BF16),
        jax.ShapeDtypeStruct((b, h, n // 2, dk, 2 * chunk), BF16),
        jax.ShapeDtypeStruct((b, h, n, chunk, dv), F32),
        jax.ShapeDtypeStruct((b, h, n // 2, chunk, 2 * chunk), BF16),
        jax.ShapeDtypeStruct((b, h, n, 1, dv), F32),
    ]
    return pl.pallas_call(
        functools.partial(_gdn_prep_kernel, dk=dk, dv=dv, chunk=chunk, nchunks=nchunks, heads=heads),
        grid=(b, tiles, h // heads),
        in_specs=in_specs,
        out_specs=out_specs,
        out_shape=out_shape,
        scratch_shapes=[pltpu.VMEM((HALO_ROWS + rows, kw), F32), pltpu.VMEM((HALO_ROWS + rows, vw), F32)],
        compiler_params=_cparams(("parallel", "parallel", "parallel")),
        name="gdn_prep",
    )(proj, proj, proj, proj, proj, proj, conv_w, conv_w, conv_w, beta_c, gc_c, gc_r)


def _pair_pad(x, half):
    z = jnp.zeros_like(x)
    return jnp.concatenate([x, z] if half == 0 else [z, x], axis=0)


def _gdn_scan_kernel(wq_ref, kpt_ref, u_ref, attn_ref, egl_ref, f_ref, nw_ref, o_ref, s_scr,
                     *, dv, chunk, nchunks, unroll, heads):
    @pl.when(pl.program_id(2) == 0)
    def _():
        s_scr[...] = jnp.zeros_like(s_scr)

    nw = nw_ref[...]
    hr = range(heads)

    def chunk_group(it, carry):
        for cc in range(unroll):
            c = it * unroll + cc
            rs = pl.ds(pl.multiple_of(c * chunk, chunk), chunk)
            states = [s_scr[hh] for hh in hr]
            r = [_dot(wq_ref[0, hh, c], states[hh].astype(BF16)) for hh in hr]
            v_new = [_pair_pad((u_ref[0, hh, c] - r[hh][0:chunk]).astype(BF16), cc) for hh in hr]
            upd = [_dot(kpt_ref[0, hh, it], v_new[hh]) for hh in hr]
            for hh in hr:
                s_scr[hh] = states[hh] * egl_ref[0, hh, c] + upd[hh]
            o = [r[hh][chunk:2 * chunk] + _dot(attn_ref[0, hh, it], v_new[hh]) for hh in hr]
            for hh in hr:
                cs = slice(hh * dv, (hh + 1) * dv)
                on = o[hh] * lax.rsqrt(jnp.mean(o[hh] * o[hh], axis=-1, keepdims=True) + EPS) * nw
                o_ref[rs, cs] = (on * f_ref[rs, cs].astype(F32)).astype(o_ref.dtype)
        return carry

    lax.fori_loop(0, nchunks // unroll, chunk_group, 0)


def _gdn_scan(wq, kpt, u, attn, egl, gates, gdn_nw, dims, nchunks, unroll, heads, fcol):
    b, s, h, dk, dv = dims
    chunk = GDN_CHUNK
    n = s // chunk
    rows = nchunks * chunk
    steps = n // nchunks
    width = heads * dv
    assert nchunks % unroll == 0 and unroll == 2

    def in5(r, c):
        return pl.BlockSpec((1, heads, nchunks, r, c), lambda bi, gi, i: (bi, gi, i, 0, 0))

    def pair5(r, c):
        return pl.BlockSpec((1, heads, nchunks // 2, r, c), lambda bi, gi, i: (bi, gi, i, 0, 0))

    in_specs = [
        in5(2 * chunk, dk), pair5(dk, 2 * chunk), in5(chunk, dv), pair5(chunk, 2 * chunk), in5(1, dv),
        pl.BlockSpec((rows, width), lambda bi, gi, i: (bi * steps + i, fcol // width + gi)),
        pl.BlockSpec((1, dv), lambda bi, gi, i: (0, 0)),
    ]
    return pl.pallas_call(
        functools.partial(_gdn_scan_kernel, dv=dv, chunk=chunk, nchunks=nchunks, unroll=unroll, heads=heads),
        grid=(b, h // heads, steps),
        in_specs=in_specs,
        out_specs=pl.BlockSpec((rows, width), lambda bi, gi, i: (bi * steps + i, gi)),
        out_shape=jax.ShapeDtypeStruct((b * s, h * dv), BF16),
        scratch_shapes=[pltpu.VMEM((heads, dk, dv), F32)],
        compiler_params=_cparams(("parallel", "parallel", "arbitrary")),
        name="gdn_scan",
    )(wq, kpt, u, attn, egl, gates, gdn_nw)


def _retention_kernel(q_ref, k_ref, v_ref, f_ref, ma_ref, cos_ref, sin_ref, dmat_ref,
                      xi_ref, zeta_ref, gpow_ref, nw_ref, o_ref, s_scr, *, dk, dv, heads):
    @pl.when(pl.program_id(2) == 0)
    def _():
        s_scr[...] = jnp.zeros_like(s_scr)

    cos = cos_ref[...]
    sin = sin_ref[...]

    def rotary(x):
        return x * cos + pltpu.roll(x, dk // 2, 1) * sin

    hr = range(heads)
    ksl = [slice(hh * dk, (hh + 1) * dk) for hh in hr]
    vsl = [slice(hh * dv, (hh + 1) * dv) for hh in hr]
    qf = [rotary(q_ref[:, ksl[hh]].astype(F32)) for hh in hr]
    k = [rotary(k_ref[:, ksl[hh]].astype(F32)) for hh in hr]
    kz = [(k[hh] * zeta_ref[hh]).astype(BF16) for hh in hr]
    states = [s_scr[hh] for hh in hr]
    scores = [(_dot_nt(qf[hh].astype(BF16), k[hh].astype(BF16)) * dmat_ref[hh]).astype(BF16) for hh in hr]
    lhs = [jnp.concatenate([scores[hh], (qf[hh] * xi_ref[hh]).astype(BF16)], axis=1) for hh in hr]
    rhs = [jnp.concatenate([v_ref[:, vsl[hh]], states[hh].astype(BF16)], axis=0) for hh in hr]
    upd = [_dot_tn(kz[hh], v_ref[:, vsl[hh]]) for hh in hr]
    for hh in hr:
        s_scr[hh] = states[hh] * gpow_ref[hh] + upd[hh]
    o = [_dot(lhs[hh], rhs[hh]) for hh in hr]
    for hh in hr:
        mu = jnp.mean(o[hh], axis=-1, keepdims=True)
        dev = o[hh] - mu
        var = jnp.mean(dev * dev, axis=-1, keepdims=True)
        ob = dev * lax.rsqrt(var + EPS) * nw_ref[:, vsl[hh]]
        mixed = ob * f_ref[:, vsl[hh]].astype(F32) + ma_ref[:, vsl[hh]].astype(F32)
        o_ref[:, vsl[hh]] = mixed.astype(o_ref.dtype)


def _retention(proj, gates, ma, ret_nw, tables, dims, rows, heads, cols):
    b, s, h, dk, dv = dims
    cosf, sins, dmat, xi, zeta, gpow = tables
    rqcol, rkcol, rvcol, fcol = cols
    steps = s // rows
    kw, vw = heads * dk, heads * dv

    def rb(bi, i):
        return bi * steps + i

    in_specs = [
        pl.BlockSpec((rows, kw), lambda bi, gi, i: (rb(bi, i), rqcol // kw + gi)),
        pl.BlockSpec((rows, kw), lambda bi, gi, i: (rb(bi, i), rkcol // kw + gi)),
        pl.BlockSpec((rows, vw), lambda bi, gi, i: (rb(bi, i), rvcol // vw + gi)),
        pl.BlockSpec((rows, vw), lambda bi, gi, i: (rb(bi, i), fcol // vw + gi)),
        pl.BlockSpec((rows, vw), lambda bi, gi, i: (rb(bi, i), gi)),
        pl.BlockSpec((rows, dk), lambda bi, gi, i: (i, 0)),
        pl.BlockSpec((rows, dk), lambda bi, gi, i: (i, 0)),
        pl.BlockSpec((heads, rows, rows), lambda bi, gi, i: (gi, 0, 0)),
        pl.BlockSpec((heads, rows, 1), lambda bi, gi, i: (gi, 0, 0)),
        pl.BlockSpec((heads, rows, 1), lambda bi, gi, i: (gi, 0, 0)),
        pl.BlockSpec((heads, 1, dv), lambda bi, gi, i: (gi, 0, 0)),
        pl.BlockSpec((1, vw), lambda bi, gi, i: (0, gi)),
    ]
    return pl.pallas_call(
        functools.partial(_retention_kernel, dk=dk, dv=dv, heads=heads),
        grid=(b, h // heads, steps),
        in_specs=in_specs,
        out_specs=pl.BlockSpec((rows, vw), lambda bi, gi, i: (rb(bi, i), gi)),
        out_shape=jax.ShapeDtypeStruct((b * s, h * dv), BF16),
        scratch_shapes=[pltpu.VMEM((heads, dk, dv), F32)],
        compiler_params=_cparams(("parallel", "parallel", "arbitrary")),
        name="retention",
    )(proj, proj, proj, gates, ma, cosf, sins, dmat, xi, zeta, gpow, ret_nw)


def _retention_tables(s, h, dk, dv, rows):
    inv = ROPE_BASE ** (-jnp.arange(0, dk, 2, dtype=F32) / dk)
    ang = jnp.arange(s, dtype=F32)[:, None] * inv[None, :]
    cos, sin = jnp.cos(ang), jnp.sin(ang)
    cosf = jnp.concatenate([cos, cos], axis=-1)
    sins = jnp.concatenate([-sin, sin], axis=-1)
    log_gamma = jnp.log1p(-jnp.exp2(-5.0 - jnp.arange(h, dtype=F32)))
    pos = jnp.arange(rows, dtype=F32)
    dist = pos[:, None] - pos[None, :]
    kscale = dk ** -0.5
    dmat = jnp.exp(jnp.where(dist >= 0, dist * log_gamma[:, None, None], -jnp.inf)) * kscale
    xi = jnp.exp((pos + 1.0) * log_gamma[:, None])[:, :, None]
    zeta = jnp.exp((rows - 1.0 - pos) * log_gamma[:, None])[:, :, None] * kscale
    gpow = jnp.broadcast_to(jnp.exp(rows * log_gamma)[:, None, None], (h, 1, dv))
    return cosf, sins, dmat, xi, zeta, gpow


def _outproj_kernel(m_ref, w_ref, x_ref, nw_ref, h_ref, hn_ref):
    hres = x_ref[...] + _dot(m_ref[...], w_ref[...])
    h_ref[...] = hres
    ms = jnp.mean(hres * hres, axis=-1, keepdims=True)
    hn_ref[...] = ((hres * lax.rsqrt(ms + EPS)) * nw_ref[...]).astype(hn_ref.dtype)


def _outproj(mixed, w, x2, nw, tm):
    m, d = x2.shape
    kdim = mixed.shape[1]
    return pl.pallas_call(
        _outproj_kernel,
        grid=(m // tm,),
        in_specs=[
            pl.BlockSpec((tm, kdim), lambda i: (i, 0)),
            pl.BlockSpec((kdim, d), lambda i: (0, 0)),
            pl.BlockSpec((tm, d), lambda i: (i, 0)),
            pl.BlockSpec((1, d), lambda i: (0, 0)),
        ],
        out_specs=[pl.BlockSpec((tm, d), lambda i: (i, 0))] * 2,
        out_shape=[jax.ShapeDtypeStruct((m, d), F32), jax.ShapeDtypeStruct((m, d), BF16)],
        compiler_params=_cparams(("parallel",)),
        name="outproj",
    )(mixed, w, x2, nw)


def _ffn_kernel(hn_ref, h_ref, wg_ref, wu_ref, wd_ref, nw_ref, o_ref):
    f = pl.program_id(1)

    @pl.when(f == 0)
    def _():
        o_ref[...] = h_ref[...]

    hn = hn_ref[...]
    g = _dot(hn, wg_ref[...])
    up = _dot(hn, wu_ref[...])
    act = (g * _sigmoid(g) * up).astype(BF16)
    o_ref[...] += _dot(act, wd_ref[...])

    @pl.when(f == pl.num_programs(1) - 1)
    def _():
        hres = o_ref[...]
        ms = jnp.mean(hres * hres, axis=-1, keepdims=True)
        o_ref[...] = (hres * lax.rsqrt(ms + EPS)) * nw_ref[...]


def _ffn(hn, hres, wg, wu, wd, nw, tm, tf):
    m, d = hres.shape
    ff = wg.shape[1]
    return pl.pallas_call(
        _ffn_kernel,
        grid=(m // tm, ff // tf),
        in_specs=[
            pl.BlockSpec((tm, d), lambda i, j: (i, 0)),
            pl.BlockSpec((tm, d), lambda i, j: (i, 0)),
            pl.BlockSpec((d, tf), lambda i, j: (0, j)),
            pl.BlockSpec((d, tf), lambda i, j: (0, j)),
            pl.BlockSpec((tf, d), lambda i, j: (j, 0)),
            pl.BlockSpec((1, d), lambda i, j: (0, 0)),
        ],
        out_specs=pl.BlockSpec((tm, d), lambda i, j: (i, 0)),
        out_shape=jax.ShapeDtypeStruct((m, d), F32),
        compiler_params=_cparams(("parallel", "arbitrary")),
        name="ffn",
    )(hn, hres, wg, wu, wd, nw)


def _pick(total, want):
    t = min(total, want)
    while total % t:
        t //= 2
    return t


def _layer(h2, dims, norm1_w, w_in, conv_w, a_log, dt_bias, gdn_norm_w, ret_norm_w,
           w_out, norm2_w, w_gate, w_up, w_down, out_norm_w):
    b, s, h, dk, dv = dims
    m, d = h2.shape
    qk, vd = h * dk, h * dv
    chunk = GDN_CHUNK
    n = s // chunk

    sc0 = 2 * qk + 2 * vd
    w_in_t = w_in.T
    w_main = _drop_rows(w_in_t, sc0, 2 * h, _pick(sc0, 1024))
    w_ba = w_in_t[sc0:sc0 + 2 * h].astype(BF16)
    mixw = 2 * qk + vd
    zrow = mixw
    rgrow = zrow + vd + mixw
    garow = rgrow + vd
    gbrow = garow + vd

    tn = _pick(np.gcd(mixw, vd), 1024)
    proj, u = _inproj(h2, norm1_w.reshape(1, d), w_main, _pick(m, 1024), tn,
                      2 * mixw, mixw // tn, vd // tn)
    gates = _gates(u, w_main, _pick(m, 1024), _pick(vd, 512), vd, ((zrow, garow), (rgrow, gbrow)))

    beta, gcum = _scalars(u, w_ba, a_log.reshape(1, h), dt_bias.reshape(1, h),
                          _pick(m, 1024), _pick(m, 256), chunk)

    gc_r = gcum.reshape(b, n // 2, 2 * chunk, h).transpose(0, 3, 1, 2)[:, :, :, None, :]
    wq, kpt, uu, attn, egl = _gdn_prep(proj, conv_w, beta, gcum, gc_r, dims, _pick(s, 512), _pick(h, 8))

    ma = _gdn_scan(wq, kpt, uu, attn, egl, gates, gdn_norm_w.reshape(1, dv), dims,
                   _pick(n, 8), _pick(n, 2), _pick(h, 8), 0)

    ret_rows = _pick(s, 256)
    tables = _retention_tables(s, h, dk, dv, ret_rows)
    mixed = _retention(proj, gates, ma, ret_norm_w.reshape(1, vd), tables, dims, ret_rows, _pick(h, 8),
                       (mixw, mixw + qk, mixw + 2 * qk, vd))

    hres, hn = _outproj(mixed, w_out.astype(BF16), h2, norm2_w.reshape(1, d), _pick(m, 512))

    return _ffn(hn, hres, w_gate.astype(BF16), w_up.astype(BF16), w_down.astype(BF16),
                out_norm_w.reshape(1, d), _pick(m, 512), _pick(w_gate.shape[1], 512))


def kernel(x, norm1_w, w_in, conv_w, a_log, dt_bias, gdn_norm_w, ret_norm_w, w_out, norm2_w,
           w_gate, w_up, w_down, norm_f_w):
    b, s, d = x.shape
    depth, h = a_log.shape
    assert depth == 1, "the final norm is fused into the single layer's FFN kernel"
    dk = d // 16
    dv = d // h
    dims = (b, s, h, dk, dv)
    out = _layer(x.reshape(b * s, d), dims, norm1_w[0], w_in[0], conv_w[0], a_log[0], dt_bias[0],
                 gdn_norm_w[0], ret_norm_w[0], w_out[0], norm2_w[0], w_gate[0], w_up[0], w_down[0],
                 norm_f_w)
    return out.reshape(b, s, d)
```

```python
import functools

import jax
import jax.numpy as jnp
import numpy as np
from jax import lax
from jax.experimental import pallas as pl
from jax.experimental.pallas import tpu as pltpu

F32 = jnp.float32
BF16 = jnp.bfloat16

EPS = 1e-6
CONV_WIDTH = 4
GDN_CHUNK = 64
ROPE_BASE = 10000.0
HALO_ROWS = 16
VMEM_LIMIT_BYTES = 56 * 1024 * 1024


def _cparams(sem):
    return pltpu.CompilerParams(dimension_semantics=sem, vmem_limit_bytes=VMEM_LIMIT_BYTES)


def _sigmoid(x):
    return 1.0 / (1.0 + jnp.exp(-x))


def _silu(x):
    h = 0.5 * x
    return h + h * jnp.tanh(h)


def _dot(a, b):
    return jnp.dot(a, b, preferred_element_type=F32)


def _dot_nt(a, b):
    return lax.dot_general(a, b, (((1,), (1,)), ((), ())), preferred_element_type=F32)


def _dot_tn(a, b):
    return lax.dot_general(a, b, (((0,), (0,)), ((), ())), preferred_element_type=F32)


def _drop_rows_kernel(cur_ref, nxt_ref, o_ref, *, first_shifted, count):
    j = pl.program_id(0)
    tr = o_ref.shape[0]

    @pl.when(j < first_shifted)
    def _():
        o_ref[...] = cur_ref[...].astype(o_ref.dtype)

    @pl.when(j >= first_shifted)
    def _():
        o_ref[0:tr - count, :] = cur_ref[count:tr, :].astype(o_ref.dtype)
        o_ref[tr - count:tr, :] = nxt_ref[...].astype(o_ref.dtype)


def _drop_rows(wt, start, count, tr):
    n_in, d = wt.shape
    n = n_in - count
    assert start % tr == 0 and n % tr == 0 and tr % count == 0 and count % HALO_ROWS == 0
    return pl.pallas_call(
        functools.partial(_drop_rows_kernel, first_shifted=start // tr, count=count),
        grid=(n // tr,),
        in_specs=[
            pl.BlockSpec((tr, d), lambda j: (j, 0)),
            pl.BlockSpec((count, d), lambda j: ((j + 1) * (tr // count), 0)),
        ],
        out_specs=pl.BlockSpec((tr, d), lambda j: (j, 0)),
        out_shape=jax.ShapeDtypeStruct((n, d), BF16),
        compiler_params=_cparams(("parallel",)),
        name="drop_rows",
    )(wt, wt)


def _inproj_kernel(x_ref, nw_ref, w_ref, o_ref, u_ref):
    @pl.when(pl.program_id(1) == 0)
    def _():
        xf = x_ref[...]
        ms = jnp.mean(xf * xf, axis=-1, keepdims=True)
        u_ref[...] = ((xf * lax.rsqrt(ms + EPS)) * nw_ref[...]).astype(u_ref.dtype)

    o_ref[...] = _dot_nt(u_ref[...], w_ref[...]).astype(o_ref.dtype)


def _inproj(x2, nw, w, tm, tn, n, skip_at, skip):
    m, d = x2.shape
    return pl.pallas_call(
        _inproj_kernel,
        grid=(m // tm, n // tn),
        in_specs=[
            pl.BlockSpec((tm, d), lambda i, j: (i, 0)),
            pl.BlockSpec((1, d), lambda i, j: (0, 0)),
            pl.BlockSpec((tn, d), lambda i, j: (jnp.where(j < skip_at, j, j + skip), 0)),
        ],
        out_specs=[
            pl.BlockSpec((tm, tn), lambda i, j: (i, j)),
            pl.BlockSpec((tm, d), lambda i, j: (i, 0)),
        ],
        out_shape=[jax.ShapeDtypeStruct((m, n), BF16), jax.ShapeDtypeStruct((m, d), BF16)],
        compiler_params=_cparams(("parallel", "arbitrary")),
        name="inproj",
    )(x2, nw, w)


def _gates_kernel(u_ref, ws_ref, wg_ref, o_ref):
    u = u_ref[...]
    a = _dot_nt(u, ws_ref[...])
    g = _dot_nt(u, wg_ref[...])
    o_ref[...] = (_silu(a) * (0.5 + 0.5 * jnp.tanh(0.5 * g))).astype(o_ref.dtype)


def _gates(u, w, tm, tg, vd, pairs):
    m, d = u.shape
    per = vd // tg

    def blk(j, which):
        idx = pairs[0][which] // tg + j
        for p in range(1, len(pairs)):
            idx = jnp.where(j >= p * per, pairs[p][which] // tg + j - p * per, idx)
        return idx

    return pl.pallas_call(
        _gates_kernel,
        grid=(m // tm, len(pairs) * per),
        in_specs=[
            pl.BlockSpec((tm, d), lambda i, j: (i, 0)),
            pl.BlockSpec((tg, d), lambda i, j: (blk(j, 0), 0)),
            pl.BlockSpec((tg, d), lambda i, j: (blk(j, 1), 0)),
        ],
        out_specs=pl.BlockSpec((tm, tg), lambda i, j: (i, j)),
        out_shape=jax.ShapeDtypeStruct((m, len(pairs) * vd), BF16),
        compiler_params=_cparams(("parallel", "arbitrary")),
        name="gates",
    )(u, w, w)


def _scalars_kernel(u_ref, wba_ref, alog_ref, dtb_ref, beta_ref, gc_ref, *, chunk, heads, sub):
    parts = [slice(p * sub, (p + 1) * sub) for p in range(u_ref.shape[0] // sub)]
    row = lax.broadcasted_iota(jnp.int32, (sub, sub), 0)
    col = lax.broadcasted_iota(jnp.int32, (sub, sub), 1)
    incl = jnp.where(((row // chunk) == (col // chunk)) & (row >= col), 1.0, 0.0).astype(BF16)
    bas = [_dot_nt(u_ref[sl, :], wba_ref[...]) for sl in parts]
    gs = []
    for sl, ba in zip(parts, bas):
        beta_ref[sl, :] = _sigmoid(ba[:, 0:heads])
        a = ba[:, heads:2 * heads] + dtb_ref[...]
        softplus = jnp.maximum(a, 0.0) + jnp.log1p(jnp.exp(-jnp.abs(a)))
        gs.append(-jnp.exp(alog_ref[...]) * softplus)
    g1 = [g.astype(BF16) for g in gs]
    r1 = [g - p1.astype(F32) for g, p1 in zip(gs, g1)]
    g2 = [r.astype(BF16) for r in r1]
    g3 = [(r - p2.astype(F32)).astype(BF16) for r, p2 in zip(r1, g2)]
    gcs = [_dot(incl, p1) + _dot(incl, p2) + _dot(incl, p3) for p1, p2, p3 in zip(g1, g2, g3)]
    for sl, gc in zip(parts, gcs):
        gc_ref[sl, :] = gc


def _scalars(u, wba, alog, dtb, tm, sub, chunk):
    m, d = u.shape
    h = alog.shape[1]
    assert tm % sub == 0 and sub % chunk == 0
    out = jax.ShapeDtypeStruct((m, h), F32)
    return pl.pallas_call(
        functools.partial(_scalars_kernel, chunk=chunk, heads=h, sub=sub),
        grid=(m // tm,),
        in_specs=[
            pl.BlockSpec((tm, d), lambda i: (i, 0)),
            pl.BlockSpec((2 * h, d), lambda i: (0, 0)),
            pl.BlockSpec((1, h), lambda i: (0, 0)),
            pl.BlockSpec((1, h), lambda i: (0, 0)),
        ],
        out_specs=[pl.BlockSpec((tm, h), lambda i: (i, 0))] * 2,
        out_shape=[out, out],
        compiler_params=_cparams(("parallel",)),
        name="gate_scalars",
    )(u, wba, alog, dtb)


def _gdn_prep_kernel(q_ref, qh_ref, k_ref, kh_ref, v_ref, vh_ref, cwq_ref, cwk_ref, cwv_ref,
                     beta_ref, gcc_ref, gcr_ref,
                     wq_ref, kpt_ref, u_ref, attn_ref, egl_ref,
                     extk_scr, extv_scr, *, dk, dv, chunk, nchunks, heads):
    t = pl.program_id(1)
    rows = chunk * nchunks
    lane_head = lax.broadcasted_iota(jnp.int32, (1, beta_ref.shape[1]), 1)

    def head_col(ref, hh):
        hsel = lane_head == pl.program_id(2) * heads + hh
        return jnp.sum(jnp.where(hsel, ref[...], 0.0), axis=1, keepdims=True)

    blk = 128
    si = lax.broadcasted_iota(jnp.int32, ((CONV_WIDTH - 1) * blk, 2 * blk), 0)
    sc = lax.broadcasted_iota(jnp.int32, ((CONV_WIDTH - 1) * blk, 2 * blk), 1)
    shift_mat = jnp.where(sc == blk + (si % blk) - (si // blk + 1), 1.0, 0.0).astype(BF16)

    def conv_silu(x_ref, h_ref, cw_ref, ext_scr):
        del ext_scr
        halo = jnp.where(t == 0, jnp.zeros_like(h_ref[...]), h_ref[...])
        x = x_ref[...]
        ext = jnp.concatenate([jnp.zeros((blk - HALO_ROWS, x.shape[1]), x.dtype), halo, x], axis=0)
        out = []
        for r in range(rows // blk):
            sh = _dot(shift_mat, ext[r * blk:(r + 2) * blk])
            acc = x[r * blk:(r + 1) * blk].astype(F32) * cw_ref[CONV_WIDTH - 1:CONV_WIDTH, :]
            for j in range(1, CONV_WIDTH):
                acc = acc + sh[(j - 1) * blk:j * blk] * cw_ref[CONV_WIDTH - 1 - j:CONV_WIDTH - j, :]
            out.append(_silu(acc))
        return jnp.concatenate(out, axis=0)

    def l2norm(y):
        return y * lax.rsqrt(jnp.sum(y * y, axis=-1, keepdims=True) + EPS)

    q_all = conv_silu(q_ref, qh_ref, cwq_ref, extk_scr)
    k_all = conv_silu(k_ref, kh_ref, cwk_ref, extk_scr)
    v_all = conv_silu(v_ref, vh_ref, cwv_ref, extv_scr)

    ri = lax.broadcasted_iota(jnp.int32, (chunk, 2 * chunk), 0)
    li = lax.broadcasted_iota(jnp.int32, (chunk, 2 * chunk), 1)
    first = li < chunk
    ci = jnp.where(first, li, li - chunk)
    incl = ri >= ci
    strict = ri > ci
    eye = jnp.where(ri == ci, 1.0, 0.0).astype(F32)
    ident_k = jnp.where(lax.broadcasted_iota(jnp.int32, (dk, dk), 0)
                        == lax.broadcasted_iota(jnp.int32, (dk, dk), 1), 1.0, 0.0).astype(BF16)

    def blockdiag(x):
        return jnp.concatenate([jnp.where(first, x, 0.0), jnp.where(first, 0.0, x)], axis=0).astype(BF16)

    def pair_rows(xa, xb):
        z = jnp.zeros_like(xa)
        return jnp.concatenate([jnp.concatenate([xa, z], axis=1), jnp.concatenate([z, xb], axis=1)], axis=0)

    npairs = nchunks // 2
    units = [(hh, p) for hh in range(heads) for p in range(npairs)]
    ur = range(len(units))
    qs, ks, vs, bs, gcs = [], [], [], [], []
    for hh in range(heads):
        q = l2norm(q_all[:, hh * dk:(hh + 1) * dk]) * (dk ** -0.5)
        k = l2norm(k_all[:, hh * dk:(hh + 1) * dk])
        v = v_all[:, hh * dv:(hh + 1) * dv]
        beta, gcum = head_col(beta_ref, hh), head_col(gcc_ref, hh)
        for c in range(nchunks):
            sl = slice(c * chunk, (c + 1) * chunk)
            qs.append(q[sl])
            ks.append(k[sl])
            vs.append(v[sl])
            bs.append(beta[sl])
            gcs.append(gcum[sl])
    nc = range(len(qs))
    gls = [gc[chunk - 1:chunk, :] for gc in gcs]
    egs = [jnp.exp(gc) for gc in gcs]
    kbs = [ks[c] * bs[c] for c in nc]
    decays = [jnp.exp(jnp.where(incl, jnp.where(first, gcs[2 * u], gcs[2 * u + 1]) - gcr_ref[0, hh, p], -1e30))
              for u, (hh, p) in enumerate(units)]
    lhs = [jnp.concatenate([jnp.concatenate([kbs[2 * u], kbs[2 * u + 1]], axis=1),
                            jnp.concatenate([qs[2 * u], qs[2 * u + 1]], axis=1)], axis=0).astype(BF16) for u in ur]
    kq = [_dot_nt(lhs[u], pair_rows(ks[2 * u], ks[2 * u + 1]).astype(BF16)) for u in ur]
    amats = [jnp.where(strict, kq[u][0:chunk] * decays[u], 0.0) for u in ur]
    for u, (hh, p) in enumerate(units):
        attn_ref[0, hh, p] = (kq[u][chunk:2 * chunk] * decays[u]).astype(attn_ref.dtype)
    tinvs = [eye - a for a in amats]
    apows = amats
    span = 2
    while span < chunk:
        apows = [_dot(apows[u].astype(BF16), blockdiag(apows[u])) for u in ur]
        tinvs = [tinvs[u] + _dot(tinvs[u].astype(BF16), blockdiag(apows[u])) for u in ur]
        span *= 2
    rhs = [jnp.concatenate([jnp.concatenate([kbs[c] * egs[c], vs[c] * bs[c]], axis=1)
                            for c in (2 * u, 2 * u + 1)], axis=0).astype(BF16) for u in ur]
    wu = [_dot(blockdiag(tinvs[u]), rhs[u]) for u in ur]
    kps = [jnp.concatenate([ks[c] * jnp.exp(gls[c] - gcs[c]) for c in (2 * u, 2 * u + 1)], axis=0).astype(BF16)
           for u in ur]
    kpt = [_dot_nt(ident_k, kps[u]) for u in ur]
    for u, (hh, p) in enumerate(units):
        kpt_ref[0, hh, p] = kpt[u].astype(kpt_ref.dtype)
        for half in range(2):
            c = 2 * u + half
            rs = slice(half * chunk, (half + 1) * chunk)
            wq_ref[0, hh, 2 * p + half, 0:chunk, :] = wu[u][rs, 0:dk].astype(wq_ref.dtype)
            wq_ref[0, hh, 2 * p + half, chunk:2 * chunk, :] = (qs[c] * egs[c]).astype(wq_ref.dtype)
            u_ref[0, hh, 2 * p + half] = wu[u][rs, dk:dk + dv]
            egl_ref[0, hh, 2 * p + half] = jnp.broadcast_to(jnp.exp(gls[c]), (1, dv))


def _gdn_prep(proj, conv_w, beta_c, gc_c, gc_r, dims, rows, heads):
    b, s, h, dk, dv = dims
    chunk = GDN_CHUNK
    nchunks = rows // chunk
    n = s // chunk
    qk = h * dk
    tiles = s // rows
    kw, vw = heads * dk, heads * dv
    kcol = qk // kw
    vcol = 2 * qk // vw

    def rowblk(bi, ti):
        return bi * tiles + ti

    def halo(bi, ti):
        return jnp.maximum((bi * s + ti * rows) // HALO_ROWS - 1, 0)

    col_spec = pl.BlockSpec((rows, h), lambda bi, ti, hi: (rowblk(bi, ti), 0))
    in_specs = [
        pl.BlockSpec((rows, kw), lambda bi, ti, hi: (rowblk(bi, ti), hi)),
        pl.BlockSpec((HALO_ROWS, kw), lambda bi, ti, hi: (halo(bi, ti), hi)),
        pl.BlockSpec((rows, kw), lambda bi, ti, hi: (rowblk(bi, ti), kcol + hi)),
        pl.BlockSpec((HALO_ROWS, kw), lambda bi, ti, hi: (halo(bi, ti), kcol + hi)),
        pl.BlockSpec((rows, vw), lambda bi, ti, hi: (rowblk(bi, ti), vcol + hi)),
        pl.BlockSpec((HALO_ROWS, vw), lambda bi, ti, hi: (halo(bi, ti), vcol + hi)),
        pl.BlockSpec((CONV_WIDTH, kw), lambda bi, ti, hi: (0, hi)),
        pl.BlockSpec((CONV_WIDTH, kw), lambda bi, ti, hi: (0, kcol + hi)),
        pl.BlockSpec((CONV_WIDTH, vw), lambda bi, ti, hi: (0, vcol + hi)),
        col_spec, col_spec,
        pl.BlockSpec((1, heads, nchunks // 2, 1, 2 * chunk), lambda bi, ti, hi: (bi, hi, ti, 0, 0)),
    ]

    def out5(r, c):
        return pl.BlockSpec((1, heads, nchunks, r, c), lambda bi, ti, hi: (bi, hi, ti, 0, 0))

    def pair5(r, c):
        return pl.BlockSpec((1, heads, nchunks // 2, r, c), lambda bi, ti, hi: (bi, hi, ti, 0, 0))

    out_specs = [out5(2 * chunk, dk), pair5(dk, 2 * chunk), out5(chunk, dv), pair5(chunk, 2 * chunk), out5(1, dv)]
    out_shape = [
        jax.ShapeDtypeStruct((b, h, n, 2 * chunk, dk), BF16),
        jax.ShapeDtypeStruct((b, h, n // 2, dk, 2 * chunk), BF16),
        jax.ShapeDtypeStruct((b, h, n, chunk, dv), F32),
        jax.ShapeDtypeStruct((b, h, n // 2, chunk, 2 * chunk), BF16),
        jax.ShapeDtypeStruct((b, h, n, 1, dv), F32),
    ]
    return pl.pallas_call(
        functools.partial(_gdn_prep_kernel, dk=dk, dv=dv, chunk=chunk, nchunks=nchunks, heads=heads),
        grid=(b, tiles, h // heads),
        in_specs=in_specs,
        out_specs=out_specs,
        out_shape=out_shape,
        scratch_shapes=[pltpu.VMEM((HALO_ROWS + rows, kw), F32), pltpu.VMEM((HALO_ROWS + rows, vw), F32)],
        compiler_params=_cparams(("parallel", "parallel", "parallel")),
        name="gdn_prep",
    )(proj, proj, proj, proj, proj, proj, conv_w, conv_w, conv_w, beta_c, gc_c, gc_r)


def _pair_pad(x, half):
    z = jnp.zeros_like(x)
    return jnp.concatenate([x, z] if half == 0 else [z, x], axis=0)


def _gdn_scan_kernel(wq_ref, kpt_ref, u_ref, attn_ref, egl_ref, f_ref, nw_ref, o_ref, s_scr,
                     *, dv, chunk, nchunks, unroll, heads):
    @pl.when(pl.program_id(2) == 0)
    def _():
        s_scr[...] = jnp.zeros_like(s_scr)

    nw = nw_ref[...]
    hr = range(heads)

    def chunk_group(it, carry):
        for cc in range(unroll):
            c = it * unroll + cc
            rs = pl.ds(pl.multiple_of(c * chunk, chunk), chunk)
            states = [s_scr[hh] for hh in hr]
            r = [_dot(wq_ref[0, hh, c], states[hh].astype(BF16)) for hh in hr]
            v_new = [_pair_pad((u_ref[0, hh, c] - r[hh][0:chunk]).astype(BF16), cc) for hh in hr]
            upd = [_dot(kpt_ref[0, hh, it], v_new[hh]) for hh in hr]
            for hh in hr:
                s_scr[hh] = states[hh] * egl_ref[0, hh, c] + upd[hh]
            o = [r[hh][chunk:2 * chunk] + _dot(attn_ref[0, hh, it], v_new[hh]) for hh in hr]
            for hh in hr:
                cs = slice(hh * dv, (hh + 1) * dv)
                on = o[hh] * lax.rsqrt(jnp.mean(o[hh] * o[hh], axis=-1, keepdims=True) + EPS) * nw
                o_ref[rs, cs] = (on * f_ref[rs, cs].astype(F32)).astype(o_ref.dtype)
        return carry

    lax.fori_loop(0, nchunks // unroll, chunk_group, 0)


def _gdn_scan(wq, kpt, u, attn, egl, gates, gdn_nw, dims, nchunks, unroll, heads, fcol):
    b, s, h, dk, dv = dims
    chunk = GDN_CHUNK
    n = s // chunk
    rows = nchunks * chunk
    steps = n // nchunks
    width = heads * dv
    assert nchunks % unroll == 0 and unroll == 2

    def in5(r, c):
        return pl.BlockSpec((1, heads, nchunks, r, c), lambda bi, gi, i: (bi, gi, i, 0, 0))

    def pair5(r, c):
        return pl.BlockSpec((1, heads, nchunks // 2, r, c), lambda bi, gi, i: (bi, gi, i, 0, 0))

    in_specs = [
        in5(2 * chunk, dk), pair5(dk, 2 * chunk), in5(chunk, dv), pair5(chunk, 2 * chunk), in5(1, dv),
        pl.BlockSpec((rows, width), lambda bi, gi, i: (bi * steps + i, fcol // width + gi)),
        pl.BlockSpec((1, dv), lambda bi, gi, i: (0, 0)),
    ]
    return pl.pallas_call(
        functools.partial(_gdn_scan_kernel, dv=dv, chunk=chunk, nchunks=nchunks, unroll=unroll, heads=heads),
        grid=(b, h // heads, steps),
        in_specs=in_specs,
        out_specs=pl.BlockSpec((rows, width), lambda bi, gi, i: (bi * steps + i, gi)),
        out_shape=jax.ShapeDtypeStruct((b * s, h * dv), BF16),
        scratch_shapes=[pltpu.VMEM((heads, dk, dv), F32)],
        compiler_params=_cparams(("parallel", "parallel", "arbitrary")),
        name="gdn_scan",
    )(wq, kpt, u, attn, egl, gates, gdn_nw)


def _retention_kernel(q_ref, k_ref, v_ref, f_ref, ma_ref, cos_ref, sin_ref, dmat_ref,
                      xi_ref, zeta_ref, gpow_ref, nw_ref, o_ref, s_scr, *, dk, dv, heads):
    @pl.when(pl.program_id(2) == 0)
    def _():
        s_scr[...] = jnp.zeros_like(s_scr)

    cos = cos_ref[...]
    sin = sin_ref[...]

    def rotary(x):
        return x * cos + pltpu.roll(x, dk // 2, 1) * sin

    hr = range(heads)
    ksl = [slice(hh * dk, (hh + 1) * dk) for hh in hr]
    vsl = [slice(hh * dv, (hh + 1) * dv) for hh in hr]
    qf = [rotary(q_ref[:, ksl[hh]].astype(F32)) for hh in hr]
    k = [rotary(k_ref[:, ksl[hh]].astype(F32)) for hh in hr]
    kz = [(k[hh] * zeta_ref[hh]).astype(BF16) for hh in hr]
    states = [s_scr[hh] for hh in hr]
    scores = [(_dot_nt(qf[hh].astype(BF16), k[hh].astype(BF16)) * dmat_ref[hh]).astype(BF16) for hh in hr]
    lhs = [jnp.concatenate([scores[hh], (qf[hh] * xi_ref[hh]).astype(BF16)], axis=1) for hh in hr]
    rhs = [jnp.concatenate([v_ref[:, vsl[hh]], states[hh].astype(BF16)], axis=0) for hh in hr]
    upd = [_dot_tn(kz[hh], v_ref[:, vsl[hh]]) for hh in hr]
    for hh in hr:
        s_scr[hh] = states[hh] * gpow_ref[hh] + upd[hh]
    o = [_dot(lhs[hh], rhs[hh]) for hh in hr]
    for hh in hr:
        mu = jnp.mean(o[hh], axis=-1, keepdims=True)
        dev = o[hh] - mu
        var = jnp.mean(dev * dev, axis=-1, keepdims=True)
        ob = dev * lax.rsqrt(var + EPS) * nw_ref[:, vsl[hh]]
        mixed = ob * f_ref[:, vsl[hh]].astype(F32) + ma_ref[:, vsl[hh]].astype(F32)
        o_ref[:, vsl[hh]] = mixed.astype(o_ref.dtype)


def _retention(proj, gates, ma, ret_nw, tables, dims, rows, heads, cols):
    b, s, h, dk, dv = dims
    cosf, sins, dmat, xi, zeta, gpow = tables
    rqcol, rkcol, rvcol, fcol = cols
    steps = s // rows
    kw, vw = heads * dk, heads * dv

    def rb(bi, i):
        return bi * steps + i

    in_specs = [
        pl.BlockSpec((rows, kw), lambda bi, gi, i: (rb(bi, i), rqcol // kw + gi)),
        pl.BlockSpec((rows, kw), lambda bi, gi, i: (rb(bi, i), rkcol // kw + gi)),
        pl.BlockSpec((rows, vw), lambda bi, gi, i: (rb(bi, i), rvcol // vw + gi)),
        pl.BlockSpec((rows, vw), lambda bi, gi, i: (rb(bi, i), fcol // vw + gi)),
        pl.BlockSpec((rows, vw), lambda bi, gi, i: (rb(bi, i), gi)),
        pl.BlockSpec((rows, dk), lambda bi, gi, i: (i, 0)),
        pl.BlockSpec((rows, dk), lambda bi, gi, i: (i, 0)),
        pl.BlockSpec((heads, rows, rows), lambda bi, gi, i: (gi, 0, 0)),
        pl.BlockSpec((heads, rows, 1), lambda bi, gi, i: (gi, 0, 0)),
        pl.BlockSpec((heads, rows, 1), lambda bi, gi, i: (gi, 0, 0)),
        pl.BlockSpec((heads, 1, dv), lambda bi, gi, i: (gi, 0, 0)),
        pl.BlockSpec((1, vw), lambda bi, gi, i: (0, gi)),
    ]
    return pl.pallas_call(
        functools.partial(_retention_kernel, dk=dk, dv=dv, heads=heads),
        grid=(b, h // heads, steps),
        in_specs=in_specs,
        out_specs=pl.BlockSpec((rows, vw), lambda bi, gi, i: (rb(bi, i), gi)),
        out_shape=jax.ShapeDtypeStruct((b * s, h * dv), BF16),
        scratch_shapes=[pltpu.VMEM((heads, dk, dv), F32)],
        compiler_params=_cparams(("parallel", "parallel", "arbitrary")),
        name="retention",
    )(proj, proj, proj, gates, ma, cosf, sins, dmat, xi, zeta, gpow, ret_nw)


def _retention_tables(s, h, dk, dv, rows):
    inv = ROPE_BASE ** (-jnp.arange(0, dk, 2, dtype=F32) / dk)
    ang = jnp.arange(s, dtype=F32)[:, None] * inv[None, :]
    cos, sin = jnp.cos(ang), jnp.sin(ang)
    cosf = jnp.concatenate([cos, cos], axis=-1)
    sins = jnp.concatenate([-sin, sin], axis=-1)
    log_gamma = jnp.log1p(-jnp.exp2(-5.0 - jnp.arange(h, dtype=F32)))
    pos = jnp.arange(rows, dtype=F32)
    dist = pos[:, None] - pos[None, :]
    kscale = dk ** -0.5
    dmat = jnp.exp(jnp.where(dist >= 0, dist * log_gamma[:, None, None], -jnp.inf)) * kscale
    xi = jnp.exp((pos + 1.0) * log_gamma[:, None])[:, :, None]
    zeta = jnp.exp((rows - 1.0 - pos) * log_gamma[:, None])[:, :, None] * kscale
    gpow = jnp.broadcast_to(jnp.exp(rows * log_gamma)[:, None, None], (h, 1, dv))
    return cosf, sins, dmat, xi, zeta, gpow


def _outproj_kernel(m_ref, w_ref, x_ref, nw_ref, h_ref, hn_ref):
    hres = x_ref[...] + _dot(m_ref[...], w_ref[...])
    h_ref[...] = hres
    ms = jnp.mean(hres * hres, axis=-1, keepdims=True)
    hn_ref[...] = ((hres * lax.rsqrt(ms + EPS)) * nw_ref[...]).astype(hn_ref.dtype)


def _outproj(mixed, w, x2, nw, tm):
    m, d = x2.shape
    kdim = mixed.shape[1]
    return pl.pallas_call(
        _outproj_kernel,
        grid=(m // tm,),
        in_specs=[
            pl.BlockSpec((tm, kdim), lambda i: (i, 0)),
            pl.BlockSpec((kdim, d), lambda i: (0, 0)),
            pl.BlockSpec((tm, d), lambda i: (i, 0)),
            pl.BlockSpec((1, d), lambda i: (0, 0)),
        ],
        out_specs=[pl.BlockSpec((tm, d), lambda i: (i, 0))] * 2,
        out_shape=[jax.ShapeDtypeStruct((m, d), F32), jax.ShapeDtypeStruct((m, d), BF16)],
        compiler_params=_cparams(("parallel",)),
        name="outproj",
    )(mixed, w, x2, nw)


def _ffn_kernel(hn_ref, h_ref, wg_ref, wu_ref, wd_ref, nw_ref, o_ref):
    f = pl.program_id(1)

    @pl.when(f == 0)
    def _():
        o_ref[...] = h_ref[...]

    hn = hn_ref[...]
    g = _dot(hn, wg_ref[...])
    up = _dot(hn, wu_ref[...])
    act = (_silu(g) * up).astype(BF16)
    o_ref[...] += _dot(act, wd_ref[...])

    @pl.when(f == pl.num_programs(1) - 1)
    def _():
        hres = o_ref[...]
        ms = jnp.mean(hres * hres, axis=-1, keepdims=True)
        o_ref[...] = (hres * lax.rsqrt(ms + EPS)) * nw_ref[...]


def _ffn(hn, hres, wg, wu, wd, nw, tm, tf):
    m, d = hres.shape
    ff = wg.shape[1]
    return pl.pallas_call(
        _ffn_kernel,
        grid=(m // tm, ff // tf),
        in_specs=[
            pl.BlockSpec((tm, d), lambda i, j: (i, 0)),
            pl.BlockSpec((tm, d), lambda i, j: (i, 0)),
            pl.BlockSpec((d, tf), lambda i, j: (0, j)),
            pl.BlockSpec((d, tf), lambda i, j: (0, j)),
            pl.BlockSpec((tf, d), lambda i, j: (j, 0)),
            pl.BlockSpec((1, d), lambda i, j: (0, 0)),
        ],
        out_specs=pl.BlockSpec((tm, d), lambda i, j: (i, 0)),
        out_shape=jax.ShapeDtypeStruct((m, d), F32),
        compiler_params=_cparams(("parallel", "arbitrary")),
        name="ffn",
    )(hn, hres, wg, wu, wd, nw)


def _pick(total, want):
    t = min(total, want)
    while total % t:
        t //= 2
    return t


def _layer(h2, dims, norm1_w, w_in, conv_w, a_log, dt_bias, gdn_norm_w, ret_norm_w,
           w_out, norm2_w, w_gate, w_up, w_down, out_norm_w):
    b, s, h, dk, dv = dims
    m, d = h2.shape
    qk, vd = h * dk, h * dv
    chunk = GDN_CHUNK
    n = s // chunk

    sc0 = 2 * qk + 2 * vd
    w_in_t = w_in.T
    w_main = _drop_rows(w_in_t, sc0, 2 * h, _pick(sc0, 1024))
    w_ba = w_in_t[sc0:sc0 + 2 * h].astype(BF16)
    mixw = 2 * qk + vd
    zrow = mixw
    rgrow = zrow + vd + mixw
    garow = rgrow + vd
    gbrow = garow + vd

    tn = _pick(np.gcd(mixw, vd), 1024)
    proj, u = _inproj(h2, norm1_w.reshape(1, d), w_main, _pick(m, 1024), tn,
                      2 * mixw, mixw // tn, vd // tn)
    gates = _gates(u, w_main, _pick(m, 1024), _pick(vd, 512), vd, ((zrow, garow), (rgrow, gbrow)))

    beta, gcum = _scalars(u, w_ba, a_log.reshape(1, h), dt_bias.reshape(1, h),
                          _pick(m, 1024), _pick(m, 256), chunk)

    gc_r = gcum.reshape(b, n // 2, 2 * chunk, h).transpose(0, 3, 1, 2)[:, :, :, None, :]
    wq, kpt, uu, attn, egl = _gdn_prep(proj, conv_w, beta, gcum, gc_r, dims, _pick(s, 512), _pick(h, 8))

    ma = _gdn_scan(wq, kpt, uu, attn, egl, gates, gdn_norm_w.reshape(1, dv), dims,
                   _pick(n, 8), _pick(n, 2), _pick(h, 8), 0)

    ret_rows = _pick(s, 256)
    tables = _retention_tables(s, h, dk, dv, ret_rows)
    mixed = _retention(proj, gates, ma, ret_norm_w.reshape(1, vd), tables, dims, ret_rows, _pick(h, 8),
                       (mixw, mixw + qk, mixw + 2 * qk, vd))

    hres, hn = _outproj(mixed, w_out.astype(BF16), h2, norm2_w.reshape(1, d), _pick(m, 512))

    return _ffn(hn, hres, w_gate.astype(BF16), w_up.astype(BF16), w_down.astype(BF16),
                out_norm_w.reshape(1, d), _pick(m, 512), _pick(w_gate.shape[1], 512))


def kernel(x, norm1_w, w_in, conv_w, a_log, dt_bias, gdn_norm_w, ret_norm_w, w_out, norm2_w,
           w_gate, w_up, w_down, norm_f_w):
    b, s, d = x.shape
    depth, h = a_log.shape
    assert depth == 1, "the final norm is fused into the single layer's FFN kernel"
    dk = d // 16
    dv = d // h
    dims = (b, s, h, dk, dv)
    out = _layer(x.reshape(b * s, d), dims, norm1_w[0], w_in[0], conv_w[0], a_log[0], dt_bias[0],
                 gdn_norm_w[0], ret_norm_w[0], w_out[0], norm2_w[0], w_gate[0], w_up[0], w_down[0],
                 norm_f_w)
    return out.reshape(b, s, d)
```

```python
import functools

import jax
import jax.numpy as jnp
import numpy as np
from jax import lax
from jax.experimental import pallas as pl
from jax.experimental.pallas import tpu as pltpu

F32 = jnp.float32
BF16 = jnp.bfloat16

EPS = 1e-6
CONV_WIDTH = 4
GDN_CHUNK = 64
ROPE_BASE = 10000.0
HALO_ROWS = 16
VMEM_LIMIT_BYTES = 56 * 1024 * 1024


def _cparams(sem):
    return pltpu.CompilerParams(dimension_semantics=sem, vmem_limit_bytes=VMEM_LIMIT_BYTES)


def _sigmoid(x):
    return 1.0 / (1.0 + jnp.exp(-x))


def _silu(x):
    h = 0.5 * x
    return h + h * jnp.tanh(h)


def _dot(a, b):
    return jnp.dot(a, b, preferred_element_type=F32)


def _dot_nt(a, b):
    return lax.dot_general(a, b, (((1,), (1,)), ((), ())), preferred_element_type=F32)


def _dot_tn(a, b):
    return lax.dot_general(a, b, (((0,), (0,)), ((), ())), preferred_element_type=F32)


def _drop_rows_kernel(cur_ref, nxt_ref, o_ref, *, first_shifted, count):
    j = pl.program_id(0)
    tr = o_ref.shape[0]

    @pl.when(j < first_shifted)
    def _():
        o_ref[...] = cur_ref[...].astype(o_ref.dtype)

    @pl.when(j >= first_shifted)
    def _():
        o_ref[0:tr - count, :] = cur_ref[count:tr, :].astype(o_ref.dtype)
        o_ref[tr - count:tr, :] = nxt_ref[...].astype(o_ref.dtype)


def _drop_rows(wt, start, count, tr):
    n_in, d = wt.shape
    n = n_in - count
    assert start % tr == 0 and n % tr == 0 and tr % count == 0 and count % HALO_ROWS == 0
    return pl.pallas_call(
        functools.partial(_drop_rows_kernel, first_shifted=start // tr, count=count),
        grid=(n // tr,),
        in_specs=[
            pl.BlockSpec((tr, d), lambda j: (j, 0)),
            pl.BlockSpec((count, d), lambda j: ((j + 1) * (tr // count), 0)),
        ],
        out_specs=pl.BlockSpec((tr, d), lambda j: (j, 0)),
        out_shape=jax.ShapeDtypeStruct((n, d), BF16),
        compiler_params=_cparams(("parallel",)),
        name="drop_rows",
    )(wt, wt)


def _inproj_kernel(x_ref, nw_ref, w_ref, o_ref, u_ref):
    @pl.when(pl.program_id(1) == 0)
    def _():
        xf = x_ref[...]
        ms = jnp.mean(xf * xf, axis=-1, keepdims=True)
        u_ref[...] = ((xf * lax.rsqrt(ms + EPS)) * nw_ref[...]).astype(u_ref.dtype)

    o_ref[...] = _dot_nt(u_ref[...], w_ref[...]).astype(o_ref.dtype)


def _inproj(x2, nw, w, tm, tn, n, skip_at, skip):
    m, d = x2.shape
    return pl.pallas_call(
        _inproj_kernel,
        grid=(m // tm, n // tn),
        in_specs=[
            pl.BlockSpec((tm, d), lambda i, j: (i, 0)),
            pl.BlockSpec((1, d), lambda i, j: (0, 0)),
            pl.BlockSpec((tn, d), lambda i, j: (jnp.where(j < skip_at, j, j + skip), 0)),
        ],
        out_specs=[
            pl.BlockSpec((tm, tn), lambda i, j: (i, j)),
            pl.BlockSpec((tm, d), lambda i, j: (i, 0)),
        ],
        out_shape=[jax.ShapeDtypeStruct((m, n), BF16), jax.ShapeDtypeStruct((m, d), BF16)],
        compiler_params=_cparams(("parallel", "arbitrary")),
        name="inproj",
    )(x2, nw, w)


def _gates_kernel(u_ref, ws_ref, wg_ref, o_ref):
    u = u_ref[...]
    a = _dot_nt(u, ws_ref[...])
    g = _dot_nt(u, wg_ref[...])
    o_ref[...] = (_silu(a) * (0.5 + 0.5 * jnp.tanh(0.5 * g))).astype(o_ref.dtype)


def _gates(u, w, tm, tg, vd, pairs):
    m, d = u.shape
    per = vd // tg

    def blk(j, which):
        idx = pairs[0][which] // tg + j
        for p in range(1, len(pairs)):
            idx = jnp.where(j >= p * per, pairs[p][which] // tg + j - p * per, idx)
        return idx

    return pl.pallas_call(
        _gates_kernel,
        grid=(m // tm, len(pairs) * per),
        in_specs=[
            pl.BlockSpec((tm, d), lambda i, j: (i, 0)),
            pl.BlockSpec((tg, d), lambda i, j: (blk(j, 0), 0)),
            pl.BlockSpec((tg, d), lambda i, j: (blk(j, 1), 0)),
        ],
        out_specs=pl.BlockSpec((tm, tg), lambda i, j: (i, j)),
        out_shape=jax.ShapeDtypeStruct((m, len(pairs) * vd), BF16),
        compiler_params=_cparams(("parallel", "arbitrary")),
        name="gates",
    )(u, w, w)


def _scalars_kernel(u_ref, wba_ref, alog_ref, dtb_ref, beta_ref, gc_ref, *, chunk, heads, sub):
    parts = [slice(p * sub, (p + 1) * sub) for p in range(u_ref.shape[0] // sub)]
    row = lax.broadcasted_iota(jnp.int32, (sub, sub), 0)
    col = lax.broadcasted_iota(jnp.int32, (sub, sub), 1)
    incl = jnp.where(((row // chunk) == (col // chunk)) & (row >= col), 1.0, 0.0).astype(BF16)
    bas = [_dot_nt(u_ref[sl, :], wba_ref[...]) for sl in parts]
    gs = []
    for sl, ba in zip(parts, bas):
        beta_ref[sl, :] = _sigmoid(ba[:, 0:heads])
        a = ba[:, heads:2 * heads] + dtb_ref[...]
        softplus = jnp.maximum(a, 0.0) + jnp.log1p(jnp.exp(-jnp.abs(a)))
        gs.append(-jnp.exp(alog_ref[...]) * softplus)
    g1 = [g.astype(BF16) for g in gs]
    r1 = [g - p1.astype(F32) for g, p1 in zip(gs, g1)]
    g2 = [r.astype(BF16) for r in r1]
    g3 = [(r - p2.astype(F32)).astype(BF16) for r, p2 in zip(r1, g2)]
    gcs = [_dot(incl, p1) + _dot(incl, p2) + _dot(incl, p3) for p1, p2, p3 in zip(g1, g2, g3)]
    for sl, gc in zip(parts, gcs):
        gc_ref[sl, :] = gc


def _scalars(u, wba, alog, dtb, tm, sub, chunk):
    m, d = u.shape
    h = alog.shape[1]
    assert tm % sub == 0 and sub % chunk == 0
    out = jax.ShapeDtypeStruct((m, h), F32)
    return pl.pallas_call(
        functools.partial(_scalars_kernel, chunk=chunk, heads=h, sub=sub),
        grid=(m // tm,),
        in_specs=[
            pl.BlockSpec((tm, d), lambda i: (i, 0)),
            pl.BlockSpec((2 * h, d), lambda i: (0, 0)),
            pl.BlockSpec((1, h), lambda i: (0, 0)),
            pl.BlockSpec((1, h), lambda i: (0, 0)),
        ],
        out_specs=[pl.BlockSpec((tm, h), lambda i: (i, 0))] * 2,
        out_shape=[out, out],
        compiler_params=_cparams(("parallel",)),
        name="gate_scalars",
    )(u, wba, alog, dtb)


def _gdn_prep_kernel(q_ref, qh_ref, k_ref, kh_ref, v_ref, vh_ref, cwq_ref, cwk_ref, cwv_ref,
                     beta_ref, gcc_ref, gcr_ref,
                     wq_ref, kpt_ref, u_ref, attn_ref, egl_ref,
                     extk_scr, extv_scr, *, dk, dv, chunk, nchunks, heads):
    t = pl.program_id(1)
    rows = chunk * nchunks
    lane_head = lax.broadcasted_iota(jnp.int32, (1, beta_ref.shape[1]), 1)

    def head_col(ref, hh):
        hsel = lane_head == pl.program_id(2) * heads + hh
        return jnp.sum(jnp.where(hsel, ref[...], 0.0), axis=1, keepdims=True)

    blk = 128
    si = lax.broadcasted_iota(jnp.int32, ((CONV_WIDTH - 1) * blk, 2 * blk), 0)
    sc = lax.broadcasted_iota(jnp.int32, ((CONV_WIDTH - 1) * blk, 2 * blk), 1)
    shift_mat = jnp.where(sc == blk + (si % blk) - (si // blk + 1), 1.0, 0.0).astype(BF16)

    def conv_silu(x_ref, h_ref, cw_ref, ext_scr):
        del ext_scr
        halo = jnp.where(t == 0, jnp.zeros_like(h_ref[...]), h_ref[...])
        x = x_ref[...]
        ext = jnp.concatenate([jnp.zeros((blk - HALO_ROWS, x.shape[1]), x.dtype), halo, x], axis=0)
        out = []
        for r in range(rows // blk):
            sh = _dot(shift_mat, ext[r * blk:(r + 2) * blk])
            acc = x[r * blk:(r + 1) * blk].astype(F32) * cw_ref[CONV_WIDTH - 1:CONV_WIDTH, :]
            for j in range(1, CONV_WIDTH):
                acc = acc + sh[(j - 1) * blk:j * blk] * cw_ref[CONV_WIDTH - 1 - j:CONV_WIDTH - j, :]
            out.append(_silu(acc))
        return jnp.concatenate(out, axis=0)

    def l2norm(y):
        return y * lax.rsqrt(jnp.sum(y * y, axis=-1, keepdims=True) + EPS)

    q_all = conv_silu(q_ref, qh_ref, cwq_ref, extk_scr)
    k_all = conv_silu(k_ref, kh_ref, cwk_ref, extk_scr)
    v_all = conv_silu(v_ref, vh_ref, cwv_ref, extv_scr)

    ri = lax.broadcasted_iota(jnp.int32, (chunk, 2 * chunk), 0)
    li = lax.broadcasted_iota(jnp.int32, (chunk, 2 * chunk), 1)
    first = li < chunk
    ci = jnp.where(first, li, li - chunk)
    incl = ri >= ci
    strict = ri > ci
    eye = jnp.where(ri == ci, 1.0, 0.0).astype(F32)
    ident_k = jnp.where(lax.broadcasted_iota(jnp.int32, (dk, dk), 0)
                        == lax.broadcasted_iota(jnp.int32, (dk, dk), 1), 1.0, 0.0).astype(BF16)

    def blockdiag(x):
        return jnp.concatenate([jnp.where(first, x, 0.0), jnp.where(first, 0.0, x)], axis=0).astype(BF16)

    def pair_rows(xa, xb):
        z = jnp.zeros_like(xa)
        return jnp.concatenate([jnp.concatenate([xa, z], axis=1), jnp.concatenate([z, xb], axis=1)], axis=0)

    npairs = nchunks // 2
    units = [(hh, p) for hh in range(heads) for p in range(npairs)]
    ur = range(len(units))
    qs, ks, vs, bs, gcs = [], [], [], [], []
    for hh in range(heads):
        q = l2norm(q_all[:, hh * dk:(hh + 1) * dk]) * (dk ** -0.5)
        k = l2norm(k_all[:, hh * dk:(hh + 1) * dk])
        v = v_all[:, hh * dv:(hh + 1) * dv]
        beta, gcum = head_col(beta_ref, hh), head_col(gcc_ref, hh)
        for c in range(nchunks):
            sl = slice(c * chunk, (c + 1) * chunk)
            qs.append(q[sl])
            ks.append(k[sl])
            vs.append(v[sl])
            bs.append(beta[sl])
            gcs.append(gcum[sl])
    nc = range(len(qs))
    gls = [gc[chunk - 1:chunk, :] for gc in gcs]
    egs = [jnp.exp(gc) for gc in gcs]
    kbs = [ks[c] * bs[c] for c in nc]
    decays = [jnp.exp(jnp.where(incl, jnp.where(first, gcs[2 * u], gcs[2 * u + 1]) - gcr_ref[0, hh, p], -1e30))
              for u, (hh, p) in enumerate(units)]
    lhs = [jnp.concatenate([jnp.concatenate([kbs[2 * u], kbs[2 * u + 1]], axis=1),
                            jnp.concatenate([qs[2 * u], qs[2 * u + 1]], axis=1)], axis=0).astype(BF16) for u in ur]
    kq = [_dot_nt(lhs[u], pair_rows(ks[2 * u], ks[2 * u + 1]).astype(BF16)) for u in ur]
    amats = [jnp.where(strict, kq[u][0:chunk] * decays[u], 0.0) for u in ur]
    for u, (hh, p) in enumerate(units):
        attn_ref[0, hh, p] = (kq[u][chunk:2 * chunk] * decays[u]).astype(attn_ref.dtype)
    tinvs = [eye - a for a in amats]
    apows = amats
    span = 2
    while span < chunk:
        apows = [_dot(apows[u].astype(BF16), blockdiag(apows[u])) for u in ur]
        tinvs = [tinvs[u] + _dot(tinvs[u].astype(BF16), blockdiag(apows[u])) for u in ur]
        span *= 2
    rhs = [jnp.concatenate([jnp.concatenate([kbs[c] * egs[c], vs[c] * bs[c]], axis=1)
                            for c in (2 * u, 2 * u + 1)], axis=0).astype(BF16) for u in ur]
    wu = [_dot(blockdiag(tinvs[u]), rhs[u]) for u in ur]
    kps = [jnp.concatenate([ks[c] * jnp.exp(gls[c] - gcs[c]) for c in (2 * u, 2 * u + 1)], axis=0).astype(BF16)
           for u in ur]
    kpt = [_dot_nt(ident_k, kps[u]) for u in ur]
    for u, (hh, p) in enumerate(units):
        kpt_ref[0, hh, p] = kpt[u].astype(kpt_ref.dtype)
        for half in range(2):
            c = 2 * u + half
            rs = slice(half * chunk, (half + 1) * chunk)
            wq_ref[0, hh, 2 * p + half, 0:chunk, :] = wu[u][rs, 0:dk].astype(wq_ref.dtype)
            wq_ref[0, hh, 2 * p + half, chunk:2 * chunk, :] = (qs[c] * egs[c]).astype(wq_ref.dtype)
            u_ref[0, hh, 2 * p + half] = wu[u][rs, dk:dk + dv].astype(u_ref.dtype)
            egl_ref[0, hh, 2 * p + half] = jnp.broadcast_to(jnp.exp(gls[c]), (1, dv))


def _gdn_prep(proj, conv_w, beta_c, gc_c, gc_r, dims, rows, heads):
    b, s, h, dk, dv = dims
    chunk = GDN_CHUNK
    nchunks = rows // chunk
    n = s // chunk
    qk = h * dk
    tiles = s // rows
    kw, vw = heads * dk, heads * dv
    kcol = qk // kw
    vcol = 2 * qk // vw

    def rowblk(bi, ti):
        return bi * tiles + ti

    def halo(bi, ti):
        return jnp.maximum((bi * s + ti * rows) // HALO_ROWS - 1, 0)

    col_spec = pl.BlockSpec((rows, h), lambda bi, ti, hi: (rowblk(bi, ti), 0))
    in_specs = [
        pl.BlockSpec((rows, kw), lambda bi, ti, hi: (rowblk(bi, ti), hi)),
        pl.BlockSpec((HALO_ROWS, kw), lambda bi, ti, hi: (halo(bi, ti), hi)),
        pl.BlockSpec((rows, kw), lambda bi, ti, hi: (rowblk(bi, ti), kcol + hi)),
        pl.BlockSpec((HALO_ROWS, kw), lambda bi, ti, hi: (halo(bi, ti), kcol + hi)),
        pl.BlockSpec((rows, vw), lambda bi, ti, hi: (rowblk(bi, ti), vcol + hi)),
        pl.BlockSpec((HALO_ROWS, vw), lambda bi, ti, hi: (halo(bi, ti), vcol + hi)),
        pl.BlockSpec((CONV_WIDTH, kw), lambda bi, ti, hi: (0, hi)),
        pl.BlockSpec((CONV_WIDTH, kw), lambda bi, ti, hi: (0, kcol + hi)),
        pl.BlockSpec((CONV_WIDTH, vw), lambda bi, ti, hi: (0, vcol + hi)),
        col_spec, col_spec,
        pl.BlockSpec((1, heads, nchunks // 2, 1, 2 * chunk), lambda bi, ti, hi: (bi, hi, ti, 0, 0)),
    ]

    def out5(r, c):
        return pl.BlockSpec((1, heads, nchunks, r, c), lambda bi, ti, hi: (bi, hi, ti, 0, 0))

    def pair5(r, c):
        return pl.BlockSpec((1, heads, nchunks // 2, r, c), lambda bi, ti, hi: (bi, hi, ti, 0, 0))

    out_specs = [out5(2 * chunk, dk), pair5(dk, 2 * chunk), out5(chunk, dv), pair5(chunk, 2 * chunk), out5(1, dv)]
    out_shape = [
        jax.ShapeDtypeStruct((b, h, n, 2 * chunk, dk), BF16),
        jax.ShapeDtypeStruct((b, h, n // 2, dk, 2 * chunk), BF16),
        jax.ShapeDtypeStruct((b, h, n, chunk, dv), BF16),
        jax.ShapeDtypeStruct((b, h, n // 2, chunk, 2 * chunk), BF16),
        jax.ShapeDtypeStruct((b, h, n, 1, dv), F32),
    ]
    return pl.pallas_call(
        functools.partial(_gdn_prep_kernel, dk=dk, dv=dv, chunk=chunk, nchunks=nchunks, heads=heads),
        grid=(b, tiles, h // heads),
        in_specs=in_specs,
        out_specs=out_specs,
        out_shape=out_shape,
        scratch_shapes=[pltpu.VMEM((HALO_ROWS + rows, kw), F32), pltpu.VMEM((HALO_ROWS + rows, vw), F32)],
        compiler_params=_cparams(("parallel", "parallel", "parallel")),
        name="gdn_prep",
    )(proj, proj, proj, proj, proj, proj, conv_w, conv_w, conv_w, beta_c, gc_c, gc_r)


def _pair_pad(x, half):
    z = jnp.zeros_like(x)
    return jnp.concatenate([x, z] if half == 0 else [z, x], axis=0)


def _gdn_scan_kernel(wq_ref, kpt_ref, u_ref, attn_ref, egl_ref, f_ref, nw_ref, o_ref, s_scr,
                     *, dv, chunk, nchunks, unroll, heads):
    @pl.when(pl.program_id(2) == 0)
    def _():
        s_scr[...] = jnp.zeros_like(s_scr)

    nw = nw_ref[...]
    hr = range(heads)

    def chunk_group(it, carry):
        for cc in range(unroll):
            c = it * unroll + cc
            rs = pl.ds(pl.multiple_of(c * chunk, chunk), chunk)
            states = [s_scr[hh] for hh in hr]
            r = [_dot(wq_ref[0, hh, c], states[hh].astype(BF16)) for hh in hr]
            v_new = [_pair_pad((u_ref[0, hh, c] - r[hh][0:chunk]).astype(BF16), cc) for hh in hr]
            upd = [_dot(kpt_ref[0, hh, it], v_new[hh]) for hh in hr]
            for hh in hr:
                s_scr[hh] = states[hh] * egl_ref[0, hh, c] + upd[hh]
            o = [r[hh][chunk:2 * chunk] + _dot(attn_ref[0, hh, it], v_new[hh]) for hh in hr]
            for hh in hr:
                cs = slice(hh * dv, (hh + 1) * dv)
                on = o[hh] * lax.rsqrt(jnp.mean(o[hh] * o[hh], axis=-1, keepdims=True) + EPS) * nw
                o_ref[rs, cs] = (on * f_ref[rs, cs].astype(F32)).astype(o_ref.dtype)
        return carry

    lax.fori_loop(0, nchunks // unroll, chunk_group, 0)


def _gdn_scan(wq, kpt, u, attn, egl, gates, gdn_nw, dims, nchunks, unroll, heads, fcol):
    b, s, h, dk, dv = dims
    chunk = GDN_CHUNK
    n = s // chunk
    rows = nchunks * chunk
    steps = n // nchunks
    width = heads * dv
    assert nchunks % unroll == 0 and unroll == 2

    def in5(r, c):
        return pl.BlockSpec((1, heads, nchunks, r, c), lambda bi, gi, i: (bi, gi, i, 0, 0))

    def pair5(r, c):
        return pl.BlockSpec((1, heads, nchunks // 2, r, c), lambda bi, gi, i: (bi, gi, i, 0, 0))

    in_specs = [
        in5(2 * chunk, dk), pair5(dk, 2 * chunk), in5(chunk, dv), pair5(chunk, 2 * chunk), in5(1, dv),
        pl.BlockSpec((rows, width), lambda bi, gi, i: (bi * steps + i, fcol // width + gi)),
        pl.BlockSpec((1, dv), lambda bi, gi, i: (0, 0)),
    ]
    return pl.pallas_call(
        functools.partial(_gdn_scan_kernel, dv=dv, chunk=chunk, nchunks=nchunks, unroll=unroll, heads=heads),
        grid=(b, h // heads, steps),
        in_specs=in_specs,
        out_specs=pl.BlockSpec((rows, width), lambda bi, gi, i: (bi * steps + i, gi)),
        out_shape=jax.ShapeDtypeStruct((b * s, h * dv), BF16),
        scratch_shapes=[pltpu.VMEM((heads, dk, dv), F32)],
        compiler_params=_cparams(("parallel", "parallel", "arbitrary")),
        name="gdn_scan",
    )(wq, kpt, u, attn, egl, gates, gdn_nw)


def _retention_kernel(q_ref, k_ref, v_ref, f_ref, ma_ref, cos_ref, sin_ref, dmat_ref,
                      xi_ref, zeta_ref, gpow_ref, nw_ref, o_ref, s_scr, *, dk, dv, heads, sub):
    @pl.when(pl.program_id(2) == 0)
    def _():
        s_scr[...] = jnp.zeros_like(s_scr)

    hr = range(heads)
    ksl = [slice(hh * dk, (hh + 1) * dk) for hh in hr]
    vsl = [slice(hh * dv, (hh + 1) * dv) for hh in hr]

    def sub_chunk(it, carry):
        rs = pl.ds(pl.multiple_of(it * sub, sub), sub)
        cos = cos_ref[rs, :]
        sin = sin_ref[rs, :]

        def rotary(x):
            return x * cos + pltpu.roll(x, dk // 2, 1) * sin

        qf = [rotary(q_ref[rs, ksl[hh]].astype(F32)) for hh in hr]
        k = [rotary(k_ref[rs, ksl[hh]].astype(F32)) for hh in hr]
        kz = [(k[hh] * zeta_ref[hh]).astype(BF16) for hh in hr]
        states = [s_scr[hh] for hh in hr]
        scores = [(_dot_nt(qf[hh].astype(BF16), k[hh].astype(BF16)) * dmat_ref[hh]).astype(BF16) for hh in hr]
        lhs = [jnp.concatenate([scores[hh], (qf[hh] * xi_ref[hh]).astype(BF16)], axis=1) for hh in hr]
        rhs = [jnp.concatenate([v_ref[rs, vsl[hh]], states[hh].astype(BF16)], axis=0) for hh in hr]
        upd = [_dot_tn(kz[hh], v_ref[rs, vsl[hh]]) for hh in hr]
        for hh in hr:
            s_scr[hh] = states[hh] * gpow_ref[hh] + upd[hh]
        o = [_dot(lhs[hh], rhs[hh]) for hh in hr]
        for hh in hr:
            mu = jnp.mean(o[hh], axis=-1, keepdims=True)
            dev = o[hh] - mu
            var = jnp.mean(dev * dev, axis=-1, keepdims=True)
            ob = dev * lax.rsqrt(var + EPS) * nw_ref[:, vsl[hh]]
            mixed = ob * f_ref[rs, vsl[hh]].astype(F32) + ma_ref[rs, vsl[hh]].astype(F32)
            o_ref[rs, vsl[hh]] = mixed.astype(o_ref.dtype)
        return carry

    lax.fori_loop(0, q_ref.shape[0] // sub, sub_chunk, 0)


def _retention(proj, gates, ma, ret_nw, tables, dims, rows, sub, heads, cols):
    b, s, h, dk, dv = dims
    cosf, sins, dmat, xi, zeta, gpow = tables
    rqcol, rkcol, rvcol, fcol = cols
    steps = s // rows
    kw, vw = heads * dk, heads * dv
    assert rows % sub == 0

    def rb(bi, i):
        return bi * steps + i

    in_specs = [
        pl.BlockSpec((rows, kw), lambda bi, gi, i: (rb(bi, i), rqcol // kw + gi)),
        pl.BlockSpec((rows, kw), lambda bi, gi, i: (rb(bi, i), rkcol // kw + gi)),
        pl.BlockSpec((rows, vw), lambda bi, gi, i: (rb(bi, i), rvcol // vw + gi)),
        pl.BlockSpec((rows, vw), lambda bi, gi, i: (rb(bi, i), fcol // vw + gi)),
        pl.BlockSpec((rows, vw), lambda bi, gi, i: (rb(bi, i), gi)),
        pl.BlockSpec((rows, dk), lambda bi, gi, i: (i, 0)),
        pl.BlockSpec((rows, dk), lambda bi, gi, i: (i, 0)),
        pl.BlockSpec((heads, sub, sub), lambda bi, gi, i: (gi, 0, 0)),
        pl.BlockSpec((heads, sub, 1), lambda bi, gi, i: (gi, 0, 0)),
        pl.BlockSpec((heads, sub, 1), lambda bi, gi, i: (gi, 0, 0)),
        pl.BlockSpec((heads, 1, dv), lambda bi, gi, i: (gi, 0, 0)),
        pl.BlockSpec((1, vw), lambda bi, gi, i: (0, gi)),
    ]
    return pl.pallas_call(
        functools.partial(_retention_kernel, dk=dk, dv=dv, heads=heads, sub=sub),
        grid=(b, h // heads, steps),
        in_specs=in_specs,
        out_specs=pl.BlockSpec((rows, vw), lambda bi, gi, i: (rb(bi, i), gi)),
        out_shape=jax.ShapeDtypeStruct((b * s, h * dv), BF16),
        scratch_shapes=[pltpu.VMEM((heads, dk, dv), F32)],
        compiler_params=_cparams(("parallel", "parallel", "arbitrary")),
        name="retention",
    )(proj, proj, proj, gates, ma, cosf, sins, dmat, xi, zeta, gpow, ret_nw)


def _retention_tables(s, h, dk, dv, rows):
    inv = ROPE_BASE ** (-jnp.arange(0, dk, 2, dtype=F32) / dk)
    lo = jnp.arange(rows, dtype=F32)[:, None] * inv[None, :]
    hi = (jnp.arange(s // rows, dtype=F32) * rows)[:, None] * inv[None, :]
    cl, sl, ch, sh = jnp.cos(lo), jnp.sin(lo), jnp.cos(hi)[:, None], jnp.sin(hi)[:, None]
    cos = (ch * cl - sh * sl).reshape(s, dk // 2)
    sin = (sh * cl + ch * sl).reshape(s, dk // 2)
    cosf = jnp.concatenate([cos, cos], axis=-1)
    sins = jnp.concatenate([-sin, sin], axis=-1)
    log_gamma = jnp.log1p(-jnp.exp2(-5.0 - jnp.arange(h, dtype=F32)))
    pos = jnp.arange(rows, dtype=F32)
    dist = pos[:, None] - pos[None, :]
    kscale = dk ** -0.5
    dmat = jnp.exp(jnp.where(dist >= 0, dist * log_gamma[:, None, None], -jnp.inf)) * kscale
    xi = jnp.exp((pos + 1.0) * log_gamma[:, None])[:, :, None]
    zeta = jnp.exp((rows - 1.0 - pos) * log_gamma[:, None])[:, :, None] * kscale
    gpow = jnp.broadcast_to(jnp.exp(rows * log_gamma)[:, None, None], (h, 1, dv))
    return cosf, sins, dmat, xi, zeta, gpow


def _outproj_kernel(m_ref, w_ref, x_ref, nw_ref, h_ref, hn_ref):
    hres = x_ref[...] + _dot(m_ref[...], w_ref[...])
    h_ref[...] = hres
    ms = jnp.mean(hres * hres, axis=-1, keepdims=True)
    hn_ref[...] = ((hres * lax.rsqrt(ms + EPS)) * nw_ref[...]).astype(hn_ref.dtype)


def _outproj(mixed, w, x2, nw, tm):
    m, d = x2.shape
    kdim = mixed.shape[1]
    return pl.pallas_call(
        _outproj_kernel,
        grid=(m // tm,),
        in_specs=[
            pl.BlockSpec((tm, kdim), lambda i: (i, 0)),
            pl.BlockSpec((kdim, d), lambda i: (0, 0)),
            pl.BlockSpec((tm, d), lambda i: (i, 0)),
            pl.BlockSpec((1, d), lambda i: (0, 0)),
        ],
        out_specs=[pl.BlockSpec((tm, d), lambda i: (i, 0))] * 2,
        out_shape=[jax.ShapeDtypeStruct((m, d), F32), jax.ShapeDtypeStruct((m, d), BF16)],
        compiler_params=_cparams(("parallel",)),
        name="outproj",
    )(mixed, w, x2, nw)


def _ffn_kernel(hn_ref, h_ref, wg_ref, wu_ref, wd_ref, nw_ref, o_ref):
    f = pl.program_id(1)

    @pl.when(f == 0)
    def _():
        o_ref[...] = h_ref[...]

    hn = hn_ref[...]
    g = _dot(hn, wg_ref[...])
    up = _dot(hn, wu_ref[...])
    act = (_silu(g) * up).astype(BF16)
    o_ref[...] += _dot(act, wd_ref[...])

    @pl.when(f == pl.num_programs(1) - 1)
    def _():
        hres = o_ref[...]
        ms = jnp.mean(hres * hres, axis=-1, keepdims=True)
        o_ref[...] = (hres * lax.rsqrt(ms + EPS)) * nw_ref[...]


def _ffn(hn, hres, wg, wu, wd, nw, tm, tf):
    m, d = hres.shape
    ff = wg.shape[1]
    return pl.pallas_call(
        _ffn_kernel,
        grid=(m // tm, ff // tf),
        in_specs=[
            pl.BlockSpec((tm, d), lambda i, j: (i, 0)),
            pl.BlockSpec((tm, d), lambda i, j: (i, 0)),
            pl.BlockSpec((d, tf), lambda i, j: (0, j)),
            pl.BlockSpec((d, tf), lambda i, j: (0, j)),
            pl.BlockSpec((tf, d), lambda i, j: (j, 0)),
            pl.BlockSpec((1, d), lambda i, j: (0, 0)),
        ],
        out_specs=pl.BlockSpec((tm, d), lambda i, j: (i, 0)),
        out_shape=jax.ShapeDtypeStruct((m, d), F32),
        compiler_params=_cparams(("parallel", "arbitrary")),
        name="ffn",
    )(hn, hres, wg, wu, wd, nw)


def _pick(total, want):
    t = min(total, want)
    while total % t:
        t //= 2
    return t


def _layer(h2, dims, norm1_w, w_in, conv_w, a_log, dt_bias, gdn_norm_w, ret_norm_w,
           w_out, norm2_w, w_gate, w_up, w_down, out_norm_w):
    b, s, h, dk, dv = dims
    m, d = h2.shape
    qk, vd = h * dk, h * dv
    chunk = GDN_CHUNK
    n = s // chunk

    sc0 = 2 * qk + 2 * vd
    w_in_t = w_in.T
    w_main = _drop_rows(w_in_t, sc0, 2 * h, _pick(sc0, 1024))
    w_ba = w_in_t[sc0:sc0 + 2 * h].astype(BF16)
    mixw = 2 * qk + vd
    zrow = mixw
    rgrow = zrow + vd + mixw
    garow = rgrow + vd
    gbrow = garow + vd

    tn = _pick(int(np.gcd(mixw, vd)), 2048)
    proj, u = _inproj(h2, norm1_w.reshape(1, d), w_main, _pick(m, 1024), tn,
                      2 * mixw, mixw // tn, vd // tn)
    gates = _gates(u, w_main, _pick(m, 1024), _pick(vd, 1024), vd, ((zrow, garow), (rgrow, gbrow)))

    beta, gcum = _scalars(u, w_ba, a_log.reshape(1, h), dt_bias.reshape(1, h),
                          _pick(m, 1024), _pick(m, 256), chunk)

    gc_r = gcum.reshape(b, n // 2, 2 * chunk, h).transpose(0, 3, 1, 2)[:, :, :, None, :]
    wq, kpt, uu, attn, egl = _gdn_prep(proj, conv_w, beta, gcum, gc_r, dims, _pick(s, 512), _pick(h, 8))

    ma = _gdn_scan(wq, kpt, uu, attn, egl, gates, gdn_norm_w.reshape(1, dv), dims,
                   _pick(n, 8), _pick(n, 2), _pick(h, 8), 0)

    ret_sub = _pick(s, 256)
    tables = _retention_tables(s, h, dk, dv, ret_sub)
    mixed = _retention(proj, gates, ma, ret_norm_w.reshape(1, vd), tables, dims, _pick(s, 512), ret_sub,
                       _pick(h, 8), (mixw, mixw + qk, mixw + 2 * qk, vd))

    hres, hn = _outproj(mixed, w_out.astype(BF16), h2, norm2_w.reshape(1, d), _pick(m, 512))

    return _ffn(hn, hres, w_gate.astype(BF16), w_up.astype(BF16), w_down.astype(BF16),
                out_norm_w.reshape(1, d), _pick(m, 512), _pick(w_gate.shape[1], 512))


def kernel(x, norm1_w, w_in, conv_w, a_log, dt_bias, gdn_norm_w, ret_norm_w, w_out, norm2_w,
           w_gate, w_up, w_down, norm_f_w):
    b, s, d = x.shape
    depth, h = a_log.shape
    assert depth == 1, "the final norm is fused into the single layer's FFN kernel"
    dk = d // 16
    dv = d // h
    dims = (b, s, h, dk, dv)
    out = _layer(x.reshape(b * s, d), dims, norm1_w[0], w_in[0], conv_w[0], a_log[0], dt_bias[0],
                 gdn_norm_w[0], ret_norm_w[0], w_out[0], norm2_w[0], w_gate[0], w_up[0], w_down[0],
                 norm_f_w)
    return out.reshape(b, s, d)
```

```python
import functools

import jax
import jax.numpy as jnp
import numpy as np
from jax import lax
from jax.experimental import pallas as pl
from jax.experimental.pallas import tpu as pltpu

F32 = jnp.float32
BF16 = jnp.bfloat16

EPS = 1e-6
CONV_WIDTH = 4
GDN_CHUNK = 64
ROPE_BASE = 10000.0
BF16_TILE_ROWS = 16
HALO_ROWS = BF16_TILE_ROWS
VMEM_LIMIT_BYTES = 56 * 1024 * 1024


def _cparams(sem):
    return pltpu.CompilerParams(dimension_semantics=sem, vmem_limit_bytes=VMEM_LIMIT_BYTES)


def _sigmoid(x):
    return 1.0 / (1.0 + jnp.exp(-x))


def _silu(x):
    h = 0.5 * x
    return h + h * jnp.tanh(h)


def _dot(a, b):
    return jnp.dot(a, b, preferred_element_type=F32)


def _dot_nt(a, b):
    return lax.dot_general(a, b, (((1,), (1,)), ((), ())), preferred_element_type=F32)


def _dot_tn(a, b):
    return lax.dot_general(a, b, (((0,), (0,)), ((), ())), preferred_element_type=F32)


def _drop_rows_kernel(cur_ref, nxt_ref, o_ref, *, first_shifted, count):
    j = pl.program_id(0)
    tr = o_ref.shape[0]

    @pl.when(j < first_shifted)
    def _():
        o_ref[...] = cur_ref[...].astype(o_ref.dtype)

    @pl.when(j >= first_shifted)
    def _():
        o_ref[0:tr - count, :] = cur_ref[count:tr, :].astype(o_ref.dtype)
        o_ref[tr - count:tr, :] = nxt_ref[...].astype(o_ref.dtype)


def _drop_rows(wt, start, count, tr):
    n_in, d = wt.shape
    n = n_in - count
    assert start % tr == 0 and n % tr == 0 and tr % count == 0 and count % HALO_ROWS == 0
    return pl.pallas_call(
        functools.partial(_drop_rows_kernel, first_shifted=start // tr, count=count),
        grid=(n // tr,),
        in_specs=[
            pl.BlockSpec((tr, d), lambda j: (j, 0)),
            pl.BlockSpec((count, d), lambda j: ((j + 1) * (tr // count), 0)),
        ],
        out_specs=pl.BlockSpec((tr, d), lambda j: (j, 0)),
        out_shape=jax.ShapeDtypeStruct((n, d), BF16),
        compiler_params=_cparams(("parallel",)),
        name="drop_rows",
    )(wt, wt)


def _inproj_kernel(x_ref, nw_ref, w_ref, o_ref, u_ref):
    @pl.when(pl.program_id(1) == 0)
    def _():
        xf = x_ref[...]
        ms = jnp.mean(xf * xf, axis=-1, keepdims=True)
        u_ref[...] = ((xf * lax.rsqrt(ms + EPS)) * nw_ref[...]).astype(u_ref.dtype)

    o_ref[...] = _dot_nt(u_ref[...], w_ref[...]).astype(o_ref.dtype)


def _inproj(x2, nw, w, tm, tn, n, skip_at, skip):
    m, d = x2.shape
    return pl.pallas_call(
        _inproj_kernel,
        grid=(m // tm, n // tn),
        in_specs=[
            pl.BlockSpec((tm, d), lambda i, j: (i, 0)),
            pl.BlockSpec((1, d), lambda i, j: (0, 0)),
            pl.BlockSpec((tn, d), lambda i, j: (jnp.where(j < skip_at, j, j + skip), 0)),
        ],
        out_specs=[
            pl.BlockSpec((tm, tn), lambda i, j: (i, j)),
            pl.BlockSpec((tm, d), lambda i, j: (i, 0)),
        ],
        out_shape=[jax.ShapeDtypeStruct((m, n), BF16), jax.ShapeDtypeStruct((m, d), BF16)],
        compiler_params=_cparams(("parallel", "arbitrary")),
        name="inproj",
    )(x2, nw, w)


def _gates_kernel(u_ref, ws_ref, wg_ref, *rest):
    ncast = (len(rest) - 1) // 2
    cast_in, o_ref, cast_out = rest[:ncast], rest[ncast], rest[ncast + 1:]
    u = u_ref[...]
    a = _dot_nt(u, ws_ref[...])
    g = _dot_nt(u, wg_ref[...])
    o_ref[...] = (_silu(a) * (0.5 + 0.5 * jnp.tanh(0.5 * g))).astype(o_ref.dtype)
    for src, dst in zip(cast_in, cast_out):
        dst[...] = src[...].astype(dst.dtype)


def _cast_slabs(rows, nsteps):
    nb = nsteps
    while nb > 1 and (rows % nb or (rows // nb) % BF16_TILE_ROWS):
        nb //= 2
    return nb


def _gates(u, w, tm, tg, vd, pairs, to_cast):
    m, d = u.shape
    per = vd // tg
    nj = len(pairs) * per
    nsteps = (m // tm) * nj

    def blk(j, which):
        idx = pairs[0][which] // tg + j
        for p in range(1, len(pairs)):
            idx = jnp.where(j >= p * per, pairs[p][which] // tg + j - p * per, idx)
        return idx

    cast_specs = []
    for mat in to_cast:
        nb = _cast_slabs(mat.shape[0], nsteps)
        rep = nsteps // nb
        cast_specs.append(pl.BlockSpec((mat.shape[0] // nb, mat.shape[1]),
                                       lambda i, j, rep=rep: ((i * nj + j) // rep, 0)))

    outs = pl.pallas_call(
        _gates_kernel,
        grid=(m // tm, nj),
        in_specs=[
            pl.BlockSpec((tm, d), lambda i, j: (i, 0)),
            pl.BlockSpec((tg, d), lambda i, j: (blk(j, 0), 0)),
            pl.BlockSpec((tg, d), lambda i, j: (blk(j, 1), 0)),
        ] + cast_specs,
        out_specs=[pl.BlockSpec((tm, tg), lambda i, j: (i, j))] + cast_specs,
        out_shape=[jax.ShapeDtypeStruct((m, len(pairs) * vd), BF16)]
        + [jax.ShapeDtypeStruct(mat.shape, BF16) for mat in to_cast],
        compiler_params=_cparams(("arbitrary", "arbitrary")),
        name="gates",
    )(u, w, w, *to_cast)
    return outs[0], outs[1:]


def _scalars_kernel(u_ref, wba_ref, alog_ref, dtb_ref, beta_ref, gc_ref, *, chunk, heads, sub):
    parts = [slice(p * sub, (p + 1) * sub) for p in range(u_ref.shape[0] // sub)]
    row = lax.broadcasted_iota(jnp.int32, (sub, sub), 0)
    col = lax.broadcasted_iota(jnp.int32, (sub, sub), 1)
    incl = jnp.where(((row // chunk) == (col // chunk)) & (row >= col), 1.0, 0.0).astype(BF16)
    bas = [_dot_nt(u_ref[sl, :], wba_ref[...]) for sl in parts]
    gs = []
    for sl, ba in zip(parts, bas):
        beta_ref[sl, :] = _sigmoid(ba[:, 0:heads])
        a = ba[:, heads:2 * heads] + dtb_ref[...]
        softplus = jnp.maximum(a, 0.0) + jnp.log1p(jnp.exp(-jnp.abs(a)))
        gs.append(-jnp.exp(alog_ref[...]) * softplus)
    g1 = [g.astype(BF16) for g in gs]
    r1 = [g - p1.astype(F32) for g, p1 in zip(gs, g1)]
    g2 = [r.astype(BF16) for r in r1]
    g3 = [(r - p2.astype(F32)).astype(BF16) for r, p2 in zip(r1, g2)]
    gcs = [_dot(incl, p1) + _dot(incl, p2) + _dot(incl, p3) for p1, p2, p3 in zip(g1, g2, g3)]
    for sl, gc in zip(parts, gcs):
        gc_ref[sl, :] = gc


def _scalars(u, wba, alog, dtb, tm, sub, chunk):
    m, d = u.shape
    h = alog.shape[1]
    assert tm % sub == 0 and sub % chunk == 0
    out = jax.ShapeDtypeStruct((m, h), F32)
    return pl.pallas_call(
        functools.partial(_scalars_kernel, chunk=chunk, heads=h, sub=sub),
        grid=(m // tm,),
        in_specs=[
            pl.BlockSpec((tm, d), lambda i: (i, 0)),
            pl.BlockSpec((2 * h, d), lambda i: (0, 0)),
            pl.BlockSpec((1, h), lambda i: (0, 0)),
            pl.BlockSpec((1, h), lambda i: (0, 0)),
        ],
        out_specs=[pl.BlockSpec((tm, h), lambda i: (i, 0))] * 2,
        out_shape=[out, out],
        compiler_params=_cparams(("parallel",)),
        name="gate_scalars",
    )(u, wba, alog, dtb)


def _gdn_prep_kernel(q_ref, qh_ref, k_ref, kh_ref, v_ref, vh_ref, cwq_ref, cwk_ref, cwv_ref,
                     beta_ref, gcc_ref, gcr_ref,
                     wq_ref, kpt_ref, u_ref, attn_ref, egl_ref,
                     extk_scr, extv_scr, *, dk, dv, chunk, nchunks, heads):
    t = pl.program_id(1)
    rows = chunk * nchunks
    lane_head = lax.broadcasted_iota(jnp.int32, (1, beta_ref.shape[1]), 1)

    def head_col(ref, hh):
        hsel = lane_head == pl.program_id(2) * heads + hh
        return jnp.sum(jnp.where(hsel, ref[...], 0.0), axis=1, keepdims=True)

    blk = 128
    si = lax.broadcasted_iota(jnp.int32, ((CONV_WIDTH - 1) * blk, 2 * blk), 0)
    sc = lax.broadcasted_iota(jnp.int32, ((CONV_WIDTH - 1) * blk, 2 * blk), 1)
    shift_mat = jnp.where(sc == blk + (si % blk) - (si // blk + 1), 1.0, 0.0).astype(BF16)

    def conv_silu(x_ref, h_ref, cw_ref, ext_scr):
        del ext_scr
        halo = jnp.where(t == 0, jnp.zeros_like(h_ref[...]), h_ref[...])
        x = x_ref[...]
        ext = jnp.concatenate([jnp.zeros((blk - HALO_ROWS, x.shape[1]), x.dtype), halo, x], axis=0)
        out = []
        for r in range(rows // blk):
            sh = _dot(shift_mat, ext[r * blk:(r + 2) * blk])
            acc = x[r * blk:(r + 1) * blk].astype(F32) * cw_ref[CONV_WIDTH - 1:CONV_WIDTH, :]
            for j in range(1, CONV_WIDTH):
                acc = acc + sh[(j - 1) * blk:j * blk] * cw_ref[CONV_WIDTH - 1 - j:CONV_WIDTH - j, :]
            out.append(_silu(acc))
        return jnp.concatenate(out, axis=0)

    def l2norm(y):
        return y * lax.rsqrt(jnp.sum(y * y, axis=-1, keepdims=True) + EPS)

    q_all = conv_silu(q_ref, qh_ref, cwq_ref, extk_scr)
    k_all = conv_silu(k_ref, kh_ref, cwk_ref, extk_scr)
    v_all = conv_silu(v_ref, vh_ref, cwv_ref, extv_scr)

    ri = lax.broadcasted_iota(jnp.int32, (chunk, 2 * chunk), 0)
    li = lax.broadcasted_iota(jnp.int32, (chunk, 2 * chunk), 1)
    first = li < chunk
    ci = jnp.where(first, li, li - chunk)
    incl = ri >= ci
    strict = ri > ci
    eye = jnp.where(ri == ci, 1.0, 0.0).astype(F32)
    ident_k = jnp.where(lax.broadcasted_iota(jnp.int32, (dk, dk), 0)
                        == lax.broadcasted_iota(jnp.int32, (dk, dk), 1), 1.0, 0.0).astype(BF16)

    def blockdiag(x):
        return jnp.concatenate([jnp.where(first, x, 0.0), jnp.where(first, 0.0, x)], axis=0).astype(BF16)

    def pair_rows(xa, xb):
        z = jnp.zeros_like(xa)
        return jnp.concatenate([jnp.concatenate([xa, z], axis=1), jnp.concatenate([z, xb], axis=1)], axis=0)

    npairs = nchunks // 2
    units = [(hh, p) for hh in range(heads) for p in range(npairs)]
    ur = range(len(units))
    qs, ks, vs, bs, gcs = [], [], [], [], []
    for hh in range(heads):
        q = l2norm(q_all[:, hh * dk:(hh + 1) * dk]) * (dk ** -0.5)
        k = l2norm(k_all[:, hh * dk:(hh + 1) * dk])
        v = v_all[:, hh * dv:(hh + 1) * dv]
        beta, gcum = head_col(beta_ref, hh), head_col(gcc_ref, hh)
        for c in range(nchunks):
            sl = slice(c * chunk, (c + 1) * chunk)
            qs.append(q[sl])
            ks.append(k[sl])
            vs.append(v[sl])
            bs.append(beta[sl])
            gcs.append(gcum[sl])
    nc = range(len(qs))
    gls = [gc[chunk - 1:chunk, :] for gc in gcs]
    egs = [jnp.exp(gc) for gc in gcs]
    kbs = [ks[c] * bs[c] for c in nc]
    decays = [jnp.exp(jnp.where(incl, jnp.where(first, gcs[2 * u], gcs[2 * u + 1]) - gcr_ref[0, hh, p], -1e30))
              for u, (hh, p) in enumerate(units)]
    lhs = [jnp.concatenate([jnp.concatenate([kbs[2 * u], kbs[2 * u + 1]], axis=1),
                            jnp.concatenate([qs[2 * u], qs[2 * u + 1]], axis=1)], axis=0).astype(BF16) for u in ur]
    kq = [_dot_nt(lhs[u], pair_rows(ks[2 * u], ks[2 * u + 1]).astype(BF16)) for u in ur]
    amats = [jnp.where(strict, kq[u][0:chunk] * decays[u], 0.0) for u in ur]
    for u, (hh, p) in enumerate(units):
        attn_ref[0, hh, p] = (kq[u][chunk:2 * chunk] * decays[u]).astype(attn_ref.dtype)
    tinvs = [eye - a for a in amats]
    apows = amats
    span = 2
    while span < chunk:
        apows = [_dot(apows[u].astype(BF16), blockdiag(apows[u])) for u in ur]
        tinvs = [tinvs[u] + _dot(tinvs[u].astype(BF16), blockdiag(apows[u])) for u in ur]
        span *= 2
    rhs = [jnp.concatenate([jnp.concatenate([kbs[c] * egs[c], vs[c] * bs[c]], axis=1)
                            for c in (2 * u, 2 * u + 1)], axis=0).astype(BF16) for u in ur]
    wu = [_dot(blockdiag(tinvs[u]), rhs[u]) for u in ur]
    kps = [jnp.concatenate([ks[c] * jnp.exp(gls[c] - gcs[c]) for c in (2 * u, 2 * u + 1)], axis=0).astype(BF16)
           for u in ur]
    kpt = [_dot_nt(ident_k, kps[u]) for u in ur]
    for u, (hh, p) in enumerate(units):
        kpt_ref[0, hh, p] = kpt[u].astype(kpt_ref.dtype)
        for half in range(2):
            c = 2 * u + half
            rs = slice(half * chunk, (half + 1) * chunk)
            wq_ref[0, hh, 2 * p + half, 0:chunk, :] = wu[u][rs, 0:dk].astype(wq_ref.dtype)
            wq_ref[0, hh, 2 * p + half, chunk:2 * chunk, :] = (qs[c] * egs[c]).astype(wq_ref.dtype)
            u_ref[0, hh, 2 * p + half] = wu[u][rs, dk:dk + dv].astype(u_ref.dtype)
            egl_ref[0, hh, 2 * p + half] = jnp.broadcast_to(jnp.exp(gls[c]), (1, dv))


def _gdn_prep(proj, conv_w, beta_c, gc_c, gc_r, dims, rows, heads):
    b, s, h, dk, dv = dims
    chunk = GDN_CHUNK
    nchunks = rows // chunk
    n = s // chunk
    qk = h * dk
    tiles = s // rows
    kw, vw = heads * dk, heads * dv
    kcol = qk // kw
    vcol = 2 * qk // vw

    def rowblk(bi, ti):
        return bi * tiles + ti

    def halo(bi, ti):
        return jnp.maximum((bi * s + ti * rows) // HALO_ROWS - 1, 0)

    col_spec = pl.BlockSpec((rows, h), lambda bi, ti, hi: (rowblk(bi, ti), 0))
    in_specs = [
        pl.BlockSpec((rows, kw), lambda bi, ti, hi: (rowblk(bi, ti), hi)),
        pl.BlockSpec((HALO_ROWS, kw), lambda bi, ti, hi: (halo(bi, ti), hi)),
        pl.BlockSpec((rows, kw), lambda bi, ti, hi: (rowblk(bi, ti), kcol + hi)),
        pl.BlockSpec((HALO_ROWS, kw), lambda bi, ti, hi: (halo(bi, ti), kcol + hi)),
        pl.BlockSpec((rows, vw), lambda bi, ti, hi: (rowblk(bi, ti), vcol + hi)),
        pl.BlockSpec((HALO_ROWS, vw), lambda bi, ti, hi: (halo(bi, ti), vcol + hi)),
        pl.BlockSpec((CONV_WIDTH, kw), lambda bi, ti, hi: (0, hi)),
        pl.BlockSpec((CONV_WIDTH, kw), lambda bi, ti, hi: (0, kcol + hi)),
        pl.BlockSpec((CONV_WIDTH, vw), lambda bi, ti, hi: (0, vcol + hi)),
        col_spec, col_spec,
        pl.BlockSpec((1, heads, nchunks // 2, 1, 2 * chunk), lambda bi, ti, hi: (bi, hi, ti, 0, 0)),
    ]

    def out5(r, c):
        return pl.BlockSpec((1, heads, nchunks, r, c), lambda bi, ti, hi: (bi, hi, ti, 0, 0))

    def pair5(r, c):
        return pl.BlockSpec((1, heads, nchunks // 2, r, c), lambda bi, ti, hi: (bi, hi, ti, 0, 0))

    out_specs = [out5(2 * chunk, dk), pair5(dk, 2 * chunk), out5(chunk, dv), pair5(chunk, 2 * chunk), out5(1, dv)]
    out_shape = [
        jax.ShapeDtypeStruct((b, h, n, 2 * chunk, dk), BF16),
        jax.ShapeDtypeStruct((b, h, n // 2, dk, 2 * chunk), BF16),
        jax.ShapeDtypeStruct((b, h, n, chunk, dv), BF16),
        jax.ShapeDtypeStruct((b, h, n // 2, chunk, 2 * chunk), BF16),
        jax.ShapeDtypeStruct((b, h, n, 1, dv), F32),
    ]
    return pl.pallas_call(
        functools.partial(_gdn_prep_kernel, dk=dk, dv=dv, chunk=chunk, nchunks=nchunks, heads=heads),
        grid=(b, tiles, h // heads),
        in_specs=in_specs,
        out_specs=out_specs,
        out_shape=out_shape,
        scratch_shapes=[pltpu.VMEM((HALO_ROWS + rows, kw), F32), pltpu.VMEM((HALO_ROWS + rows, vw), F32)],
        compiler_params=_cparams(("parallel", "parallel", "parallel")),
        name="gdn_prep",
    )(proj, proj, proj, proj, proj, proj, conv_w, conv_w, conv_w, beta_c, gc_c, gc_r)


def _pair_pad(x, half):
    z = jnp.zeros_like(x)
    return jnp.concatenate([x, z] if half == 0 else [z, x], axis=0)


def _gdn_scan_kernel(wq_ref, kpt_ref, u_ref, attn_ref, egl_ref, f_ref, nw_ref, o_ref, s_scr,
                     *, dv, chunk, nchunks, unroll, heads, batch):
    @pl.when(pl.program_id(1) == 0)
    def _():
        s_scr[...] = jnp.zeros_like(s_scr)

    nw = nw_ref[...]
    seqs = [(bb, hh) for bb in range(batch) for hh in range(heads)]
    sr = range(len(seqs))

    def chunk_group(it, carry):
        for cc in range(unroll):
            c = it * unroll + cc
            rs = pl.ds(pl.multiple_of(c * chunk, chunk), chunk)
            states = [s_scr[i] for i in sr]
            r = [_dot(wq_ref[bb, hh, c], states[i].astype(BF16)) for i, (bb, hh) in enumerate(seqs)]
            v_new = [_pair_pad((u_ref[bb, hh, c] - r[i][0:chunk]).astype(BF16), cc)
                     for i, (bb, hh) in enumerate(seqs)]
            upd = [_dot(kpt_ref[bb, hh, it], v_new[i]) for i, (bb, hh) in enumerate(seqs)]
            for i, (bb, hh) in enumerate(seqs):
                s_scr[i] = states[i] * egl_ref[bb, hh, c] + upd[i]
            o = [r[i][chunk:2 * chunk] + _dot(attn_ref[bb, hh, it], v_new[i]) for i, (bb, hh) in enumerate(seqs)]
            for i, (bb, hh) in enumerate(seqs):
                cs = slice(hh * dv, (hh + 1) * dv)
                on = o[i] * lax.rsqrt(jnp.mean(o[i] * o[i], axis=-1, keepdims=True) + EPS) * nw
                o_ref[bb, rs, cs] = (on * f_ref[bb, rs, cs].astype(F32)).astype(o_ref.dtype)
        return carry

    lax.fori_loop(0, nchunks // unroll, chunk_group, 0)


def _gdn_scan(wq, kpt, u, attn, egl, gates, gdn_nw, dims, nchunks, unroll, heads, fcol):
    b, s, h, dk, dv = dims
    chunk = GDN_CHUNK
    n = s // chunk
    rows = nchunks * chunk
    steps = n // nchunks
    width = heads * dv
    assert nchunks % unroll == 0 and unroll == 2

    def in5(r, c):
        return pl.BlockSpec((b, heads, nchunks, r, c), lambda gi, i: (0, gi, i, 0, 0))

    def pair5(r, c):
        return pl.BlockSpec((b, heads, nchunks // 2, r, c), lambda gi, i: (0, gi, i, 0, 0))

    in_specs = [
        in5(2 * chunk, dk), pair5(dk, 2 * chunk), in5(chunk, dv), pair5(chunk, 2 * chunk), in5(1, dv),
        pl.BlockSpec((b, rows, width), lambda gi, i: (0, i, fcol // width + gi)),
        pl.BlockSpec((1, dv), lambda gi, i: (0, 0)),
    ]
    return pl.pallas_call(
        functools.partial(_gdn_scan_kernel, dv=dv, chunk=chunk, nchunks=nchunks, unroll=unroll, heads=heads,
                          batch=b),
        grid=(h // heads, steps),
        in_specs=in_specs,
        out_specs=pl.BlockSpec((b, rows, width), lambda gi, i: (0, i, gi)),
        out_shape=jax.ShapeDtypeStruct((b, s, h * dv), BF16),
        scratch_shapes=[pltpu.VMEM((b * heads, dk, dv), F32)],
        compiler_params=_cparams(("parallel", "arbitrary")),
        name="gdn_scan",
    )(wq, kpt, u, attn, egl, gates, gdn_nw)


def _retention_kernel(q_ref, k_ref, v_ref, f_ref, ma_ref, cos_ref, sin_ref, dmat_ref,
                      xi_ref, zeta_ref, gpow_ref, nw_ref, o_ref, s_scr, *, dk, dv, heads, sub):
    @pl.when(pl.program_id(2) == 0)
    def _():
        s_scr[...] = jnp.zeros_like(s_scr)

    hr = range(heads)
    ksl = [slice(hh * dk, (hh + 1) * dk) for hh in hr]
    vsl = [slice(hh * dv, (hh + 1) * dv) for hh in hr]

    def sub_chunk(it, carry):
        rs = pl.ds(pl.multiple_of(it * sub, sub), sub)
        cos = cos_ref[rs, :]
        sin = sin_ref[rs, :]

        def rotary(x):
            return x * cos + pltpu.roll(x, dk // 2, 1) * sin

        qf = [rotary(q_ref[rs, ksl[hh]].astype(F32)) for hh in hr]
        k = [rotary(k_ref[rs, ksl[hh]].astype(F32)) for hh in hr]
        kz = [(k[hh] * zeta_ref[hh]).astype(BF16) for hh in hr]
        states = [s_scr[hh] for hh in hr]
        scores = [(_dot_nt(qf[hh].astype(BF16), k[hh].astype(BF16)) * dmat_ref[hh]).astype(BF16) for hh in hr]
        lhs = [jnp.concatenate([scores[hh], (qf[hh] * xi_ref[hh]).astype(BF16)], axis=1) for hh in hr]
        rhs = [jnp.concatenate([v_ref[rs, vsl[hh]], states[hh].astype(BF16)], axis=0) for hh in hr]
        upd = [_dot_tn(kz[hh], v_ref[rs, vsl[hh]]) for hh in hr]
        for hh in hr:
            s_scr[hh] = states[hh] * gpow_ref[hh] + upd[hh]
        o = [_dot(lhs[hh], rhs[hh]) for hh in hr]
        for hh in hr:
            mu = jnp.mean(o[hh], axis=-1, keepdims=True)
            dev = o[hh] - mu
            var = jnp.mean(dev * dev, axis=-1, keepdims=True)
            ob = dev * lax.rsqrt(var + EPS) * nw_ref[:, vsl[hh]]
            mixed = ob * f_ref[rs, vsl[hh]].astype(F32) + ma_ref[rs, vsl[hh]].astype(F32)
            o_ref[rs, vsl[hh]] = mixed.astype(o_ref.dtype)
        return carry

    lax.fori_loop(0, q_ref.shape[0] // sub, sub_chunk, 0)


def _retention(proj, gates, ma, ret_nw, tables, dims, rows, sub, heads, cols):
    b, s, h, dk, dv = dims
    cosf, sins, dmat, xi, zeta, gpow = tables
    rqcol, rkcol, rvcol, fcol = cols
    steps = s // rows
    kw, vw = heads * dk, heads * dv
    assert rows % sub == 0

    def rb(bi, i):
        return bi * steps + i

    in_specs = [
        pl.BlockSpec((rows, kw), lambda bi, gi, i: (rb(bi, i), rqcol // kw + gi)),
        pl.BlockSpec((rows, kw), lambda bi, gi, i: (rb(bi, i), rkcol // kw + gi)),
        pl.BlockSpec((rows, vw), lambda bi, gi, i: (rb(bi, i), rvcol // vw + gi)),
        pl.BlockSpec((rows, vw), lambda bi, gi, i: (rb(bi, i), fcol // vw + gi)),
        pl.BlockSpec((rows, vw), lambda bi, gi, i: (rb(bi, i), gi)),
        pl.BlockSpec((rows, dk), lambda bi, gi, i: (i, 0)),
        pl.BlockSpec((rows, dk), lambda bi, gi, i: (i, 0)),
        pl.BlockSpec((heads, sub, sub), lambda bi, gi, i: (gi, 0, 0)),
        pl.BlockSpec((heads, sub, 1), lambda bi, gi, i: (gi, 0, 0)),
        pl.BlockSpec((heads, sub, 1), lambda bi, gi, i: (gi, 0, 0)),
        pl.BlockSpec((heads, 1, dv), lambda bi, gi, i: (gi, 0, 0)),
        pl.BlockSpec((1, vw), lambda bi, gi, i: (0, gi)),
    ]
    return pl.pallas_call(
        functools.partial(_retention_kernel, dk=dk, dv=dv, heads=heads, sub=sub),
        grid=(b, h // heads, steps),
        in_specs=in_specs,
        out_specs=pl.BlockSpec((rows, vw), lambda bi, gi, i: (rb(bi, i), gi)),
        out_shape=jax.ShapeDtypeStruct((b * s, h * dv), BF16),
        scratch_shapes=[pltpu.VMEM((heads, dk, dv), F32)],
        compiler_params=_cparams(("parallel", "parallel", "arbitrary")),
        name="retention",
    )(proj, proj, proj, gates, ma, cosf, sins, dmat, xi, zeta, gpow, ret_nw)


def _retention_tables(s, h, dk, dv, rows):
    inv = ROPE_BASE ** (-jnp.arange(0, dk, 2, dtype=F32) / dk)
    lo = jnp.arange(rows, dtype=F32)[:, None] * inv[None, :]
    hi = (jnp.arange(s // rows, dtype=F32) * rows)[:, None] * inv[None, :]
    cl, sl, ch, sh = jnp.cos(lo), jnp.sin(lo), jnp.cos(hi)[:, None], jnp.sin(hi)[:, None]
    cos = (ch * cl - sh * sl).reshape(s, dk // 2)
    sin = (sh * cl + ch * sl).reshape(s, dk // 2)
    cosf = jnp.concatenate([cos, cos], axis=-1)
    sins = jnp.concatenate([-sin, sin], axis=-1)
    log_gamma = jnp.log1p(-jnp.exp2(-5.0 - jnp.arange(h, dtype=F32)))
    pos = jnp.arange(rows, dtype=F32)
    dist = pos[:, None] - pos[None, :]
    kscale = dk ** -0.5
    dmat = jnp.exp(jnp.where(dist >= 0, dist * log_gamma[:, None, None], -jnp.inf)) * kscale
    xi = jnp.exp((pos + 1.0) * log_gamma[:, None])[:, :, None]
    zeta = jnp.exp((rows - 1.0 - pos) * log_gamma[:, None])[:, :, None] * kscale
    gpow = jnp.broadcast_to(jnp.exp(rows * log_gamma)[:, None, None], (h, 1, dv))
    return cosf, sins, dmat, xi, zeta, gpow


def _outproj_kernel(m_ref, w_ref, x_ref, nw_ref, h_ref, hn_ref):
    hres = x_ref[...] + _dot(m_ref[...], w_ref[...])
    h_ref[...] = hres
    ms = jnp.mean(hres * hres, axis=-1, keepdims=True)
    hn_ref[...] = ((hres * lax.rsqrt(ms + EPS)) * nw_ref[...]).astype(hn_ref.dtype)


def _outproj(mixed, w, x2, nw, tm):
    m, d = x2.shape
    kdim = mixed.shape[1]
    return pl.pallas_call(
        _outproj_kernel,
        grid=(m // tm,),
        in_specs=[
            pl.BlockSpec((tm, kdim), lambda i: (i, 0)),
            pl.BlockSpec((kdim, d), lambda i: (0, 0)),
            pl.BlockSpec((tm, d), lambda i: (i, 0)),
            pl.BlockSpec((1, d), lambda i: (0, 0)),
        ],
        out_specs=[pl.BlockSpec((tm, d), lambda i: (i, 0))] * 2,
        out_shape=[jax.ShapeDtypeStruct((m, d), F32), jax.ShapeDtypeStruct((m, d), BF16)],
        compiler_params=_cparams(("parallel",)),
        name="outproj",
    )(mixed, w, x2, nw)


def _ffn_kernel(hn_ref, h_ref, wg_ref, wu_ref, wd_ref, nw_ref, o_ref):
    f = pl.program_id(1)

    @pl.when(f == 0)
    def _():
        o_ref[...] = h_ref[...]

    hn = hn_ref[...]
    g = _dot(hn, wg_ref[...])
    up = _dot(hn, wu_ref[...])
    act = (_silu(g) * up).astype(BF16)
    o_ref[...] += _dot(act, wd_ref[...])

    @pl.when(f == pl.num_programs(1) - 1)
    def _():
        hres = o_ref[...]
        ms = jnp.mean(hres * hres, axis=-1, keepdims=True)
        o_ref[...] = (hres * lax.rsqrt(ms + EPS)) * nw_ref[...]


def _ffn(hn, hres, wg, wu, wd, nw, tm, tf):
    m, d = hres.shape
    ff = wg.shape[1]
    return pl.pallas_call(
        _ffn_kernel,
        grid=(m // tm, ff // tf),
        in_specs=[
            pl.BlockSpec((tm, d), lambda i, j: (i, 0)),
            pl.BlockSpec((tm, d), lambda i, j: (i, 0)),
            pl.BlockSpec((d, tf), lambda i, j: (0, j)),
            pl.BlockSpec((d, tf), lambda i, j: (0, j)),
            pl.BlockSpec((tf, d), lambda i, j: (j, 0)),
            pl.BlockSpec((1, d), lambda i, j: (0, 0)),
        ],
        out_specs=pl.BlockSpec((tm, d), lambda i, j: (i, 0)),
        out_shape=jax.ShapeDtypeStruct((m, d), F32),
        compiler_params=_cparams(("parallel", "arbitrary")),
        name="ffn",
    )(hn, hres, wg, wu, wd, nw)


def _pick(total, want):
    t = min(total, want)
    while total % t:
        t //= 2
    return t


def _layer(h2, dims, norm1_w, w_in, conv_w, a_log, dt_bias, gdn_norm_w, ret_norm_w,
           w_out, norm2_w, w_gate, w_up, w_down, out_norm_w):
    b, s, h, dk, dv = dims
    m, d = h2.shape
    qk, vd = h * dk, h * dv
    chunk = GDN_CHUNK
    n = s // chunk

    sc0 = 2 * qk + 2 * vd
    w_in_t = w_in.T
    w_main = _drop_rows(w_in_t, sc0, 2 * h, _pick(sc0, 1024))
    w_ba = w_in_t[sc0:sc0 + 2 * h].astype(BF16)
    mixw = 2 * qk + vd
    zrow = mixw
    rgrow = zrow + vd + mixw
    garow = rgrow + vd
    gbrow = garow + vd

    tn = _pick(int(np.gcd(mixw, vd)), 2048)
    proj, u = _inproj(h2, norm1_w.reshape(1, d), w_main, _pick(m, 1024), tn,
                      2 * mixw, mixw // tn, vd // tn)
    gates, (wg_b, wu_b, wd_b, wo_b) = _gates(u, w_main, _pick(m, 1024), _pick(vd, 1024), vd,
                                             ((zrow, garow), (rgrow, gbrow)), [w_gate, w_up, w_down, w_out])

    beta, gcum = _scalars(u, w_ba, a_log.reshape(1, h), dt_bias.reshape(1, h),
                          _pick(m, 1024), _pick(m, 256), chunk)

    gc_r = gcum.reshape(b, n // 2, 2 * chunk, h).transpose(0, 3, 1, 2)[:, :, :, None, :]
    wq, kpt, uu, attn, egl = _gdn_prep(proj, conv_w, beta, gcum, gc_r, dims, _pick(s, 512), _pick(h, 8))

    ma = _gdn_scan(wq, kpt, uu, attn, egl, gates.reshape(b, s, 2 * vd), gdn_norm_w.reshape(1, dv), dims,
                   _pick(n, 8), _pick(n, 2), _pick(h, 8), 0).reshape(m, vd)

    ret_sub = _pick(s, 256)
    tables = _retention_tables(s, h, dk, dv, ret_sub)
    mixed = _retention(proj, gates, ma, ret_norm_w.reshape(1, vd), tables, dims, _pick(s, 512), ret_sub,
                       _pick(h, 8), (mixw, mixw + qk, mixw + 2 * qk, vd))

    hres, hn = _outproj(mixed, wo_b, h2, norm2_w.reshape(1, d), _pick(m, 512))

    return _ffn(hn, hres, wg_b, wu_b, wd_b, out_norm_w.reshape(1, d), _pick(m, 512), _pick(w_gate.shape[1], 512))


def kernel(x, norm1_w, w_in, conv_w, a_log, dt_bias, gdn_norm_w, ret_norm_w, w_out, norm2_w,
           w_gate, w_up, w_down, norm_f_w):
    b, s, d = x.shape
    depth, h = a_log.shape
    assert depth == 1, "the final norm is fused into the single layer's FFN kernel"
    dk = d // 16
    dv = d // h
    dims = (b, s, h, dk, dv)
    out = _layer(x.reshape(b * s, d), dims, norm1_w[0], w_in[0], conv_w[0], a_log[0], dt_bias[0],
                 gdn_norm_w[0], ret_norm_w[0], w_out[0], norm2_w[0], w_gate[0], w_up[0], w_down[0],
                 norm_f_w)
    return out.reshape(b, s, d)
```

```python
import functools

import jax
import jax.numpy as jnp
from jax import lax
from jax.experimental import pallas as pl
from jax.experimental.pallas import tpu as pltpu

F32 = jnp.float32
BF16 = jnp.bfloat16

EPS = 1e-6
CONV_WIDTH = 4
GDN_CHUNK = 64
ROPE_BASE = 10000.0
BF16_TILE_ROWS = 16
HALO_ROWS = BF16_TILE_ROWS
VMEM_LIMIT_BYTES = 56 * 1024 * 1024


def _cparams(sem):
    return pltpu.CompilerParams(dimension_semantics=sem, vmem_limit_bytes=VMEM_LIMIT_BYTES)


def _sigmoid(x):
    return 1.0 / (1.0 + jnp.exp(-x))


def _silu(x):
    h = 0.5 * x
    return h + h * jnp.tanh(h)


def _dot(a, b):
    return jnp.dot(a, b, preferred_element_type=F32)


def _dot_nt(a, b):
    return lax.dot_general(a, b, (((1,), (1,)), ((), ())), preferred_element_type=F32)


def _dot_tn(a, b):
    return lax.dot_general(a, b, (((0,), (0,)), ((), ())), preferred_element_type=F32)


def _drop_rows_kernel(cur_ref, nxt_ref, o_ref, *, first_shifted, count):
    j = pl.program_id(0)
    tr = o_ref.shape[0]

    @pl.when(j < first_shifted)
    def _():
        o_ref[...] = cur_ref[...].astype(o_ref.dtype)

    @pl.when(j >= first_shifted)
    def _():
        o_ref[0:tr - count, :] = cur_ref[count:tr, :].astype(o_ref.dtype)
        o_ref[tr - count:tr, :] = nxt_ref[...].astype(o_ref.dtype)


def _drop_rows(wt, start, count, tr):
    n_in, d = wt.shape
    n = n_in - count
    assert start % tr == 0 and n % tr == 0 and tr % count == 0 and count % HALO_ROWS == 0
    return pl.pallas_call(
        functools.partial(_drop_rows_kernel, first_shifted=start // tr, count=count),
        grid=(n // tr,),
        in_specs=[
            pl.BlockSpec((tr, d), lambda j: (j, 0)),
            pl.BlockSpec((count, d), lambda j: ((j + 1) * (tr // count), 0)),
        ],
        out_specs=pl.BlockSpec((tr, d), lambda j: (j, 0)),
        out_shape=jax.ShapeDtypeStruct((n, d), BF16),
        compiler_params=_cparams(("parallel",)),
        name="drop_rows",
    )(wt, wt)


def _inproj_kernel(x_ref, nw_ref, w_ref, o_ref, u_ref):
    @pl.when(pl.program_id(1) == 0)
    def _():
        xf = x_ref[...]
        ms = jnp.mean(xf * xf, axis=-1, keepdims=True)
        u_ref[...] = ((xf * lax.rsqrt(ms + EPS)) * nw_ref[...]).astype(u_ref.dtype)

    o_ref[...] = _dot_nt(u_ref[...], w_ref[...]).astype(o_ref.dtype)


def _nth(j, values):
    idx = values[0]
    for p in range(1, len(values)):
        idx = jnp.where(j >= p, values[p], idx)
    return idx


def _inproj(x2, nw, w, tm, tn, tiles):
    m, d = x2.shape
    return pl.pallas_call(
        _inproj_kernel,
        grid=(m // tm, len(tiles)),
        in_specs=[
            pl.BlockSpec((tm, d), lambda i, j: (i, 0)),
            pl.BlockSpec((1, d), lambda i, j: (0, 0)),
            pl.BlockSpec((tn, d), lambda i, j: (_nth(j, tiles), 0)),
        ],
        out_specs=[
            pl.BlockSpec((tm, tn), lambda i, j: (i, j)),
            pl.BlockSpec((tm, d), lambda i, j: (i, 0)),
        ],
        out_shape=[jax.ShapeDtypeStruct((m, len(tiles) * tn), BF16), jax.ShapeDtypeStruct((m, d), BF16)],
        compiler_params=_cparams(("parallel", "arbitrary")),
        name="inproj",
    )(x2, nw, w)


def _rotproj_kernel(u_ref, w_ref, cos_ref, sin_ref, o_ref, *, dk):
    res = _dot_nt(u_ref[...], w_ref[...])
    cos = cos_ref[...]
    sin = sin_ref[...]
    for hh in range(res.shape[1] // dk):
        sl = slice(hh * dk, (hh + 1) * dk)
        x = res[:, sl]
        o_ref[:, sl] = (x * cos + pltpu.roll(x, dk // 2, 1) * sin).astype(o_ref.dtype)


def _rotproj(u, w, cosf, sins, tm, tn, tile, seq, dk):
    m, d = u.shape
    assert seq % tm == 0
    per_seq = seq // tm
    return pl.pallas_call(
        functools.partial(_rotproj_kernel, dk=dk),
        grid=(m // tm,),
        in_specs=[
            pl.BlockSpec((tm, d), lambda i: (i, 0)),
            pl.BlockSpec((tn, d), lambda i: (tile, 0)),
            pl.BlockSpec((tm, dk), lambda i: (i % per_seq, 0)),
            pl.BlockSpec((tm, dk), lambda i: (i % per_seq, 0)),
        ],
        out_specs=pl.BlockSpec((tm, tn), lambda i: (i, 0)),
        out_shape=jax.ShapeDtypeStruct((m, tn), BF16),
        compiler_params=_cparams(("parallel",)),
        name="rotproj",
    )(u, w, cosf, sins)


def _gates_kernel(u_ref, ws_ref, wg_ref, *rest):
    ncast = (len(rest) - 1) // 2
    cast_in, o_ref, cast_out = rest[:ncast], rest[ncast], rest[ncast + 1:]
    u = u_ref[...]
    a = _dot_nt(u, ws_ref[...])
    g = _dot_nt(u, wg_ref[...])
    o_ref[...] = (_silu(a) * (0.5 + 0.5 * jnp.tanh(0.5 * g))).astype(o_ref.dtype)
    for src, dst in zip(cast_in, cast_out):
        dst[...] = src[...].astype(dst.dtype)


def _cast_slabs(rows, nsteps):
    nb = nsteps
    while nb > 1 and (rows % nb or (rows // nb) % BF16_TILE_ROWS):
        nb //= 2
    return nb


def _gates(u, w, tm, tg, vd, pairs, to_cast):
    m, d = u.shape
    per = vd // tg
    nj = len(pairs) * per
    nsteps = (m // tm) * nj

    def blk(j, which):
        idx = pairs[0][which] // tg + j
        for p in range(1, len(pairs)):
            idx = jnp.where(j >= p * per, pairs[p][which] // tg + j - p * per, idx)
        return idx

    cast_specs = []
    for mat in to_cast:
        nb = _cast_slabs(mat.shape[0], nsteps)
        rep = nsteps // nb
        cast_specs.append(pl.BlockSpec((mat.shape[0] // nb, mat.shape[1]),
                                       lambda i, j, rep=rep: ((i * nj + j) // rep, 0)))

    outs = pl.pallas_call(
        _gates_kernel,
        grid=(m // tm, nj),
        in_specs=[
            pl.BlockSpec((tm, d), lambda i, j: (i, 0)),
            pl.BlockSpec((tg, d), lambda i, j: (blk(j, 0), 0)),
            pl.BlockSpec((tg, d), lambda i, j: (blk(j, 1), 0)),
        ] + cast_specs,
        out_specs=[pl.BlockSpec((tm, tg), lambda i, j: (i, j))] + cast_specs,
        out_shape=[jax.ShapeDtypeStruct((m, len(pairs) * vd), BF16)]
        + [jax.ShapeDtypeStruct(mat.shape, BF16) for mat in to_cast],
        compiler_params=_cparams(("arbitrary", "arbitrary")),
        name="gates",
    )(u, w, w, *to_cast)
    return outs[0], outs[1:]


def _scalars_kernel(u_ref, wba_ref, alog_ref, dtb_ref, beta_ref, gc_ref, *, chunk, heads, sub):
    parts = [slice(p * sub, (p + 1) * sub) for p in range(u_ref.shape[0] // sub)]
    row = lax.broadcasted_iota(jnp.int32, (sub, sub), 0)
    col = lax.broadcasted_iota(jnp.int32, (sub, sub), 1)
    incl = jnp.where(((row // chunk) == (col // chunk)) & (row >= col), 1.0, 0.0).astype(BF16)
    bas = [_dot_nt(u_ref[sl, :], wba_ref[...]) for sl in parts]
    gs = []
    for sl, ba in zip(parts, bas):
        beta_ref[sl, :] = _sigmoid(ba[:, 0:heads])
        a = ba[:, heads:2 * heads] + dtb_ref[...]
        softplus = jnp.maximum(a, 0.0) + jnp.log1p(jnp.exp(-jnp.abs(a)))
        gs.append(-jnp.exp(alog_ref[...]) * softplus)
    g1 = [g.astype(BF16) for g in gs]
    r1 = [g - p1.astype(F32) for g, p1 in zip(gs, g1)]
    g2 = [r.astype(BF16) for r in r1]
    g3 = [(r - p2.astype(F32)).astype(BF16) for r, p2 in zip(r1, g2)]
    gcs = [_dot(incl, p1) + _dot(incl, p2) + _dot(incl, p3) for p1, p2, p3 in zip(g1, g2, g3)]
    for sl, gc in zip(parts, gcs):
        gc_ref[sl, :] = gc


def _scalars(u, wba, alog, dtb, tm, sub, chunk):
    m, d = u.shape
    h = alog.shape[1]
    assert tm % sub == 0 and sub % chunk == 0
    out = jax.ShapeDtypeStruct((m, h), F32)
    return pl.pallas_call(
        functools.partial(_scalars_kernel, chunk=chunk, heads=h, sub=sub),
        grid=(m // tm,),
        in_specs=[
            pl.BlockSpec((tm, d), lambda i: (i, 0)),
            pl.BlockSpec((2 * h, d), lambda i: (0, 0)),
            pl.BlockSpec((1, h), lambda i: (0, 0)),
            pl.BlockSpec((1, h), lambda i: (0, 0)),
        ],
        out_specs=[pl.BlockSpec((tm, h), lambda i: (i, 0))] * 2,
        out_shape=[out, out],
        compiler_params=_cparams(("parallel",)),
        name="gate_scalars",
    )(u, wba, alog, dtb)


def _gdn_prep_kernel(q_ref, qh_ref, k_ref, kh_ref, v_ref, vh_ref, cwq_ref, cwk_ref, cwv_ref,
                     beta_ref, gcc_ref, gcr_ref,
                     wq_ref, kpt_ref, u_ref, attn_ref, egl_ref,
                     extk_scr, extv_scr, *, dk, dv, chunk, nchunks, heads):
    t = pl.program_id(1)
    rows = chunk * nchunks
    lane_head = lax.broadcasted_iota(jnp.int32, (1, beta_ref.shape[1]), 1)

    def head_col(ref, hh):
        hsel = lane_head == pl.program_id(2) * heads + hh
        return jnp.sum(jnp.where(hsel, ref[...], 0.0), axis=1, keepdims=True)

    blk = 128
    si = lax.broadcasted_iota(jnp.int32, ((CONV_WIDTH - 1) * blk, 2 * blk), 0)
    sc = lax.broadcasted_iota(jnp.int32, ((CONV_WIDTH - 1) * blk, 2 * blk), 1)
    shift_mat = jnp.where(sc == blk + (si % blk) - (si // blk + 1), 1.0, 0.0).astype(BF16)

    def conv_silu(x_ref, h_ref, cw_ref, ext_scr):
        del ext_scr
        halo = jnp.where(t == 0, jnp.zeros_like(h_ref[...]), h_ref[...])
        x = x_ref[...]
        ext = jnp.concatenate([jnp.zeros((blk - HALO_ROWS, x.shape[1]), x.dtype), halo, x], axis=0)
        out = []
        for r in range(rows // blk):
            sh = _dot(shift_mat, ext[r * blk:(r + 2) * blk])
            acc = x[r * blk:(r + 1) * blk].astype(F32) * cw_ref[CONV_WIDTH - 1:CONV_WIDTH, :]
            for j in range(1, CONV_WIDTH):
                acc = acc + sh[(j - 1) * blk:j * blk] * cw_ref[CONV_WIDTH - 1 - j:CONV_WIDTH - j, :]
            out.append(_silu(acc))
        return jnp.concatenate(out, axis=0)

    def l2norm(y):
        return y * lax.rsqrt(jnp.sum(y * y, axis=-1, keepdims=True) + EPS)

    q_all = conv_silu(q_ref, qh_ref, cwq_ref, extk_scr)
    k_all = conv_silu(k_ref, kh_ref, cwk_ref, extk_scr)
    v_all = conv_silu(v_ref, vh_ref, cwv_ref, extv_scr)

    ri = lax.broadcasted_iota(jnp.int32, (chunk, 2 * chunk), 0)
    li = lax.broadcasted_iota(jnp.int32, (chunk, 2 * chunk), 1)
    first = li < chunk
    ci = jnp.where(first, li, li - chunk)
    incl = ri >= ci
    strict = ri > ci
    eye = jnp.where(ri == ci, 1.0, 0.0).astype(F32)
    ident_k = jnp.where(lax.broadcasted_iota(jnp.int32, (dk, dk), 0)
                        == lax.broadcasted_iota(jnp.int32, (dk, dk), 1), 1.0, 0.0).astype(BF16)

    def blockdiag(x):
        return jnp.concatenate([jnp.where(first, x, 0.0), jnp.where(first, 0.0, x)], axis=0).astype(BF16)

    def pair_rows(xa, xb):
        z = jnp.zeros_like(xa)
        return jnp.concatenate([jnp.concatenate([xa, z], axis=1), jnp.concatenate([z, xb], axis=1)], axis=0)

    npairs = nchunks // 2
    units = [(hh, p) for hh in range(heads) for p in range(npairs)]
    ur = range(len(units))
    qs, ks, vs, bs, gcs = [], [], [], [], []
    for hh in range(heads):
        q = l2norm(q_all[:, hh * dk:(hh + 1) * dk]) * (dk ** -0.5)
        k = l2norm(k_all[:, hh * dk:(hh + 1) * dk])
        v = v_all[:, hh * dv:(hh + 1) * dv]
        beta, gcum = head_col(beta_ref, hh), head_col(gcc_ref, hh)
        for c in range(nchunks):
            sl = slice(c * chunk, (c + 1) * chunk)
            qs.append(q[sl])
            ks.append(k[sl])
            vs.append(v[sl])
            bs.append(beta[sl])
            gcs.append(gcum[sl])
    nc = range(len(qs))
    gls = [gc[chunk - 1:chunk, :] for gc in gcs]
    egs = [jnp.exp(gc) for gc in gcs]
    kbs = [ks[c] * bs[c] for c in nc]
    decays = [jnp.exp(jnp.where(incl, jnp.where(first, gcs[2 * u], gcs[2 * u + 1]) - gcr_ref[0, hh, p], -1e30))
              for u, (hh, p) in enumerate(units)]
    lhs = [jnp.concatenate([jnp.concatenate([kbs[2 * u], kbs[2 * u + 1]], axis=1),
                            jnp.concatenate([qs[2 * u], qs[2 * u + 1]], axis=1)], axis=0).astype(BF16) for u in ur]
    kq = [_dot_nt(lhs[u], pair_rows(ks[2 * u], ks[2 * u + 1]).astype(BF16)) for u in ur]
    amats = [jnp.where(strict, kq[u][0:chunk] * decays[u], 0.0) for u in ur]
    for u, (hh, p) in enumerate(units):
        attn_ref[0, hh, p] = (kq[u][chunk:2 * chunk] * decays[u]).astype(attn_ref.dtype)
    tinvs = [eye - a for a in amats]
    apows = amats
    span = 2
    while span < chunk:
        apows = [_dot(apows[u].astype(BF16), blockdiag(apows[u])) for u in ur]
        tinvs = [tinvs[u] + _dot(tinvs[u].astype(BF16), blockdiag(apows[u])) for u in ur]
        span *= 2
    rhs = [jnp.concatenate([jnp.concatenate([kbs[c] * egs[c], vs[c] * bs[c]], axis=1)
                            for c in (2 * u, 2 * u + 1)], axis=0).astype(BF16) for u in ur]
    wu = [_dot(blockdiag(tinvs[u]), rhs[u]) for u in ur]
    kps = [jnp.concatenate([ks[c] * jnp.exp(gls[c] - gcs[c]) for c in (2 * u, 2 * u + 1)], axis=0).astype(BF16)
           for u in ur]
    kpt = [_dot_nt(ident_k, kps[u]) for u in ur]
    for u, (hh, p) in enumerate(units):
        kpt_ref[0, hh, p] = kpt[u].astype(kpt_ref.dtype)
        for half in range(2):
            c = 2 * u + half
            rs = slice(half * chunk, (half + 1) * chunk)
            wq_ref[0, hh, 2 * p + half, 0:chunk, :] = wu[u][rs, 0:dk].astype(wq_ref.dtype)
            wq_ref[0, hh, 2 * p + half, chunk:2 * chunk, :] = (qs[c] * egs[c]).astype(wq_ref.dtype)
            u_ref[0, hh, 2 * p + half] = wu[u][rs, dk:dk + dv].astype(u_ref.dtype)
            egl_ref[0, hh, 2 * p + half] = jnp.broadcast_to(jnp.exp(gls[c]), (1, dv))


def _gdn_prep(proj, conv_w, beta_c, gc_c, gc_r, dims, rows, heads):
    b, s, h, dk, dv = dims
    chunk = GDN_CHUNK
    nchunks = rows // chunk
    n = s // chunk
    qk = h * dk
    tiles = s // rows
    kw, vw = heads * dk, heads * dv
    kcol = qk // kw
    vcol = 2 * qk // vw

    def rowblk(bi, ti):
        return bi * tiles + ti

    def halo(bi, ti):
        return jnp.maximum((bi * s + ti * rows) // HALO_ROWS - 1, 0)

    col_spec = pl.BlockSpec((rows, h), lambda bi, ti, hi: (rowblk(bi, ti), 0))
    in_specs = [
        pl.BlockSpec((rows, kw), lambda bi, ti, hi: (rowblk(bi, ti), hi)),
        pl.BlockSpec((HALO_ROWS, kw), lambda bi, ti, hi: (halo(bi, ti), hi)),
        pl.BlockSpec((rows, kw), lambda bi, ti, hi: (rowblk(bi, ti), kcol + hi)),
        pl.BlockSpec((HALO_ROWS, kw), lambda bi, ti, hi: (halo(bi, ti), kcol + hi)),
        pl.BlockSpec((rows, vw), lambda bi, ti, hi: (rowblk(bi, ti), vcol + hi)),
        pl.BlockSpec((HALO_ROWS, vw), lambda bi, ti, hi: (halo(bi, ti), vcol + hi)),
        pl.BlockSpec((CONV_WIDTH, kw), lambda bi, ti, hi: (0, hi)),
        pl.BlockSpec((CONV_WIDTH, kw), lambda bi, ti, hi: (0, kcol + hi)),
        pl.BlockSpec((CONV_WIDTH, vw), lambda bi, ti, hi: (0, vcol + hi)),
        col_spec, col_spec,
        pl.BlockSpec((1, heads, nchunks // 2, 1, 2 * chunk), lambda bi, ti, hi: (bi, hi, ti, 0, 0)),
    ]

    def out5(r, c):
        return pl.BlockSpec((1, heads, nchunks, r, c), lambda bi, ti, hi: (bi, hi, ti, 0, 0))

    def pair5(r, c):
        return pl.BlockSpec((1, heads, nchunks // 2, r, c), lambda bi, ti, hi: (bi, hi, ti, 0, 0))

    out_specs = [out5(2 * chunk, dk), pair5(dk, 2 * chunk), out5(chunk, dv), pair5(chunk, 2 * chunk), out5(1, dv)]
    out_shape = [
        jax.ShapeDtypeStruct((b, h, n, 2 * chunk, dk), BF16),
        jax.ShapeDtypeStruct((b, h, n // 2, dk, 2 * chunk), BF16),
        jax.ShapeDtypeStruct((b, h, n, chunk, dv), BF16),
        jax.ShapeDtypeStruct((b, h, n // 2, chunk, 2 * chunk), BF16),
        jax.ShapeDtypeStruct((b, h, n, 1, dv), F32),
    ]
    return pl.pallas_call(
        functools.partial(_gdn_prep_kernel, dk=dk, dv=dv, chunk=chunk, nchunks=nchunks, heads=heads),
        grid=(b, tiles, h // heads),
        in_specs=in_specs,
        out_specs=out_specs,
        out_shape=out_shape,
        scratch_shapes=[pltpu.VMEM((HALO_ROWS + rows, kw), F32), pltpu.VMEM((HALO_ROWS + rows, vw), F32)],
        compiler_params=_cparams(("parallel", "parallel", "parallel")),
        name="gdn_prep",
    )(proj, proj, proj, proj, proj, proj, conv_w, conv_w, conv_w, beta_c, gc_c, gc_r)


def _pair_pad(x, half):
    z = jnp.zeros_like(x)
    return jnp.concatenate([x, z] if half == 0 else [z, x], axis=0)


def _gdn_scan_kernel(wq_ref, kpt_ref, u_ref, attn_ref, egl_ref, f_ref, nw_ref, o_ref, s_scr,
                     *, dv, chunk, nchunks, unroll, heads, batch):
    @pl.when(pl.program_id(1) == 0)
    def _():
        s_scr[...] = jnp.zeros_like(s_scr)

    nw = nw_ref[...]
    seqs = [(bb, hh) for bb in range(batch) for hh in range(heads)]
    sr = range(len(seqs))

    def chunk_group(it, carry):
        for cc in range(unroll):
            c = it * unroll + cc
            rs = pl.ds(pl.multiple_of(c * chunk, chunk), chunk)
            states = [s_scr[i] for i in sr]
            r = [_dot(wq_ref[bb, hh, c], states[i].astype(BF16)) for i, (bb, hh) in enumerate(seqs)]
            v_new = [_pair_pad((u_ref[bb, hh, c] - r[i][0:chunk]).astype(BF16), cc)
                     for i, (bb, hh) in enumerate(seqs)]
            upd = [_dot(kpt_ref[bb, hh, it], v_new[i]) for i, (bb, hh) in enumerate(seqs)]
            for i, (bb, hh) in enumerate(seqs):
                s_scr[i] = states[i] * egl_ref[bb, hh, c] + upd[i]
            o = [r[i][chunk:2 * chunk] + _dot(attn_ref[bb, hh, it], v_new[i]) for i, (bb, hh) in enumerate(seqs)]
            for i, (bb, hh) in enumerate(seqs):
                cs = slice(hh * dv, (hh + 1) * dv)
                on = o[i] * lax.rsqrt(jnp.mean(o[i] * o[i], axis=-1, keepdims=True) + EPS) * nw
                o_ref[bb, rs, cs] = (on * f_ref[bb, rs, cs].astype(F32)).astype(o_ref.dtype)
        return carry

    lax.fori_loop(0, nchunks // unroll, chunk_group, 0)


def _gdn_scan(wq, kpt, u, attn, egl, gates, gdn_nw, dims, nchunks, unroll, heads, fcol):
    b, s, h, dk, dv = dims
    chunk = GDN_CHUNK
    n = s // chunk
    rows = nchunks * chunk
    steps = n // nchunks
    width = heads * dv
    assert nchunks % unroll == 0 and unroll == 2

    def in5(r, c):
        return pl.BlockSpec((b, heads, nchunks, r, c), lambda gi, i: (0, gi, i, 0, 0))

    def pair5(r, c):
        return pl.BlockSpec((b, heads, nchunks // 2, r, c), lambda gi, i: (0, gi, i, 0, 0))

    in_specs = [
        in5(2 * chunk, dk), pair5(dk, 2 * chunk), in5(chunk, dv), pair5(chunk, 2 * chunk), in5(1, dv),
        pl.BlockSpec((b, rows, width), lambda gi, i: (0, i, fcol // width + gi)),
        pl.BlockSpec((1, dv), lambda gi, i: (0, 0)),
    ]
    return pl.pallas_call(
        functools.partial(_gdn_scan_kernel, dv=dv, chunk=chunk, nchunks=nchunks, unroll=unroll, heads=heads,
                          batch=b),
        grid=(h // heads, steps),
        in_specs=in_specs,
        out_specs=pl.BlockSpec((b, rows, width), lambda gi, i: (0, i, gi)),
        out_shape=jax.ShapeDtypeStruct((b, s, h * dv), BF16),
        scratch_shapes=[pltpu.VMEM((b * heads, dk, dv), F32)],
        compiler_params=_cparams(("parallel", "arbitrary")),
        name="gdn_scan",
    )(wq, kpt, u, attn, egl, gates, gdn_nw)


def _retention_kernel(q_ref, k_ref, v_ref, f_ref, ma_ref, dmat_ref,
                      xi_ref, zeta_ref, gpow_ref, nw_ref, o_ref, s_scr, *, dk, dv, heads, sub):
    @pl.when(pl.program_id(2) == 0)
    def _():
        s_scr[...] = jnp.zeros_like(s_scr)

    hr = range(heads)
    ksl = [slice(hh * dk, (hh + 1) * dk) for hh in hr]
    vsl = [slice(hh * dv, (hh + 1) * dv) for hh in hr]

    def sub_chunk(it, carry):
        rs = pl.ds(pl.multiple_of(it * sub, sub), sub)
        q = [q_ref[rs, ksl[hh]] for hh in hr]
        k = [k_ref[rs, ksl[hh]] for hh in hr]
        kz = [(k[hh].astype(F32) * zeta_ref[hh]).astype(BF16) for hh in hr]
        states = [s_scr[hh] for hh in hr]
        scores = [(_dot_nt(q[hh], k[hh]) * dmat_ref[hh]).astype(BF16) for hh in hr]
        lhs = [jnp.concatenate([scores[hh], (q[hh].astype(F32) * xi_ref[hh]).astype(BF16)], axis=1) for hh in hr]
        rhs = [jnp.concatenate([v_ref[rs, vsl[hh]], states[hh].astype(BF16)], axis=0) for hh in hr]
        upd = [_dot_tn(kz[hh], v_ref[rs, vsl[hh]]) for hh in hr]
        for hh in hr:
            s_scr[hh] = states[hh] * gpow_ref[hh] + upd[hh]
        o = [_dot(lhs[hh], rhs[hh]) for hh in hr]
        for hh in hr:
            mu = jnp.mean(o[hh], axis=-1, keepdims=True)
            dev = o[hh] - mu
            var = jnp.mean(dev * dev, axis=-1, keepdims=True)
            ob = dev * lax.rsqrt(var + EPS) * nw_ref[:, vsl[hh]]
            mixed = ob * f_ref[rs, vsl[hh]].astype(F32) + ma_ref[rs, vsl[hh]].astype(F32)
            o_ref[rs, vsl[hh]] = mixed.astype(o_ref.dtype)
        return carry

    lax.fori_loop(0, q_ref.shape[0] // sub, sub_chunk, 0)


def _retention(qk, proj, gates, ma, ret_nw, tables, dims, rows, sub, heads, cols):
    b, s, h, dk, dv = dims
    dmat, xi, zeta, gpow = tables
    rqcol, rkcol, rvcol, fcol = cols
    steps = s // rows
    kw, vw = heads * dk, heads * dv
    assert rows % sub == 0

    def rb(bi, i):
        return bi * steps + i

    in_specs = [
        pl.BlockSpec((rows, kw), lambda bi, gi, i: (rb(bi, i), rqcol // kw + gi)),
        pl.BlockSpec((rows, kw), lambda bi, gi, i: (rb(bi, i), rkcol // kw + gi)),
        pl.BlockSpec((rows, vw), lambda bi, gi, i: (rb(bi, i), rvcol // vw + gi)),
        pl.BlockSpec((rows, vw), lambda bi, gi, i: (rb(bi, i), fcol // vw + gi)),
        pl.BlockSpec((rows, vw), lambda bi, gi, i: (rb(bi, i), gi)),
        pl.BlockSpec((heads, sub, sub), lambda bi, gi, i: (gi, 0, 0)),
        pl.BlockSpec((heads, sub, 1), lambda bi, gi, i: (gi, 0, 0)),
        pl.BlockSpec((heads, sub, 1), lambda bi, gi, i: (gi, 0, 0)),
        pl.BlockSpec((heads, 1, dv), lambda bi, gi, i: (gi, 0, 0)),
        pl.BlockSpec((1, vw), lambda bi, gi, i: (0, gi)),
    ]
    return pl.pallas_call(
        functools.partial(_retention_kernel, dk=dk, dv=dv, heads=heads, sub=sub),
        grid=(b, h // heads, steps),
        in_specs=in_specs,
        out_specs=pl.BlockSpec((rows, vw), lambda bi, gi, i: (rb(bi, i), gi)),
        out_shape=jax.ShapeDtypeStruct((b * s, h * dv), BF16),
        scratch_shapes=[pltpu.VMEM((heads, dk, dv), F32)],
        compiler_params=_cparams(("parallel", "parallel", "arbitrary")),
        name="retention",
    )(qk, qk, proj, gates, ma, dmat, xi, zeta, gpow, ret_nw)


def _rotary_tables(s, dk, rows):
    inv = ROPE_BASE ** (-jnp.arange(0, dk, 2, dtype=F32) / dk)
    lo = jnp.arange(rows, dtype=F32)[:, None] * inv[None, :]
    hi = (jnp.arange(s // rows, dtype=F32) * rows)[:, None] * inv[None, :]
    cl, sl, ch, sh = jnp.cos(lo), jnp.sin(lo), jnp.cos(hi)[:, None], jnp.sin(hi)[:, None]
    cos = (ch * cl - sh * sl).reshape(s, dk // 2)
    sin = (sh * cl + ch * sl).reshape(s, dk // 2)
    return jnp.concatenate([cos, cos], axis=-1), jnp.concatenate([-sin, sin], axis=-1)


def _retention_tables(h, dk, dv, rows):
    log_gamma = jnp.log1p(-jnp.exp2(-5.0 - jnp.arange(h, dtype=F32)))
    pos = jnp.arange(rows, dtype=F32)
    dist = pos[:, None] - pos[None, :]
    kscale = dk ** -0.5
    dmat = jnp.exp(jnp.where(dist >= 0, dist * log_gamma[:, None, None], -jnp.inf)) * kscale
    xi = jnp.exp((pos + 1.0) * log_gamma[:, None])[:, :, None]
    zeta = jnp.exp((rows - 1.0 - pos) * log_gamma[:, None])[:, :, None] * kscale
    gpow = jnp.broadcast_to(jnp.exp(rows * log_gamma)[:, None, None], (h, 1, dv))
    return dmat, xi, zeta, gpow


def _outproj_kernel(m_ref, w_ref, x_ref, nw_ref, h_ref, hn_ref):
    hres = x_ref[...] + _dot(m_ref[...], w_ref[...])
    h_ref[...] = hres
    ms = jnp.mean(hres * hres, axis=-1, keepdims=True)
    hn_ref[...] = ((hres * lax.rsqrt(ms + EPS)) * nw_ref[...]).astype(hn_ref.dtype)


def _outproj(mixed, w, x2, nw, tm):
    m, d = x2.shape
    kdim = mixed.shape[1]
    return pl.pallas_call(
        _outproj_kernel,
        grid=(m // tm,),
        in_specs=[
            pl.BlockSpec((tm, kdim), lambda i: (i, 0)),
            pl.BlockSpec((kdim, d), lambda i: (0, 0)),
            pl.BlockSpec((tm, d), lambda i: (i, 0)),
            pl.BlockSpec((1, d), lambda i: (0, 0)),
        ],
        out_specs=[pl.BlockSpec((tm, d), lambda i: (i, 0))] * 2,
        out_shape=[jax.ShapeDtypeStruct((m, d), F32), jax.ShapeDtypeStruct((m, d), BF16)],
        compiler_params=_cparams(("parallel",)),
        name="outproj",
    )(mixed, w, x2, nw)


def _ffn_kernel(hn_ref, h_ref, wg_ref, wu_ref, wd_ref, nw_ref, o_ref):
    f = pl.program_id(1)

    @pl.when(f == 0)
    def _():
        o_ref[...] = h_ref[...]

    hn = hn_ref[...]
    g = _dot(hn, wg_ref[...])
    up = _dot(hn, wu_ref[...])
    act = (_silu(g) * up).astype(BF16)
    o_ref[...] += _dot(act, wd_ref[...])

    @pl.when(f == pl.num_programs(1) - 1)
    def _():
        hres = o_ref[...]
        ms = jnp.mean(hres * hres, axis=-1, keepdims=True)
        o_ref[...] = (hres * lax.rsqrt(ms + EPS)) * nw_ref[...]


def _ffn(hn, hres, wg, wu, wd, nw, tm, tf):
    m, d = hres.shape
    ff = wg.shape[1]
    return pl.pallas_call(
        _ffn_kernel,
        grid=(m // tm, ff // tf),
        in_specs=[
            pl.BlockSpec((tm, d), lambda i, j: (i, 0)),
            pl.BlockSpec((tm, d), lambda i, j: (i, 0)),
            pl.BlockSpec((d, tf), lambda i, j: (0, j)),
            pl.BlockSpec((d, tf), lambda i, j: (0, j)),
            pl.BlockSpec((tf, d), lambda i, j: (j, 0)),
            pl.BlockSpec((1, d), lambda i, j: (0, 0)),
        ],
        out_specs=pl.BlockSpec((tm, d), lambda i, j: (i, 0)),
        out_shape=jax.ShapeDtypeStruct((m, d), F32),
        compiler_params=_cparams(("parallel", "arbitrary")),
        name="ffn",
    )(hn, hres, wg, wu, wd, nw)


def _pick(total, want):
    t = min(total, want)
    while total % t:
        t //= 2
    return t


def _layer(h2, dims, norm1_w, w_in, conv_w, a_log, dt_bias, gdn_norm_w, ret_norm_w,
           w_out, norm2_w, w_gate, w_up, w_down, out_norm_w):
    b, s, h, dk, dv = dims
    m, d = h2.shape
    qk, vd = h * dk, h * dv
    chunk = GDN_CHUNK
    n = s // chunk

    sc0 = 2 * qk + 2 * vd
    w_in_t = w_in.T
    w_main = _drop_rows(w_in_t, sc0, 2 * h, _pick(sc0, 1024))
    w_ba = w_in_t[sc0:sc0 + 2 * h].astype(BF16)
    mixw = 2 * qk + vd
    zrow = mixw
    rgrow = zrow + vd + mixw
    garow = rgrow + vd
    gbrow = garow + vd

    assert 2 * qk == vd
    tn = vd
    proj, u = _inproj(h2, norm1_w.reshape(1, d), w_main, _pick(m, 1024), tn, (0, 1, 4))
    ret_sub = _pick(s, 256)
    cosf, sins = _rotary_tables(s, dk, ret_sub)
    qk_rot = _rotproj(u, w_main, cosf, sins, _pick(s, 1024), tn, 3, s, dk)
    gates, (wg_b, wu_b, wd_b, wo_b) = _gates(u, w_main, _pick(m, 1024), _pick(vd, 1024), vd,
                                             ((zrow, garow), (rgrow, gbrow)), [w_gate, w_up, w_down, w_out])

    beta, gcum = _scalars(u, w_ba, a_log.reshape(1, h), dt_bias.reshape(1, h),
                          _pick(m, 1024), _pick(m, 256), chunk)

    gc_r = gcum.reshape(b, n // 2, 2 * chunk, h).transpose(0, 3, 1, 2)[:, :, :, None, :]
    wq, kpt, uu, attn, egl = _gdn_prep(proj, conv_w, beta, gcum, gc_r, dims, _pick(s, 512), _pick(h, 8))

    ma = _gdn_scan(wq, kpt, uu, attn, egl, gates.reshape(b, s, 2 * vd), gdn_norm_w.reshape(1, dv), dims,
                   _pick(n, 8), _pick(n, 2), _pick(h, 8), 0).reshape(m, vd)

    tables = _retention_tables(h, dk, dv, ret_sub)
    mixed = _retention(qk_rot, proj, gates, ma, ret_norm_w.reshape(1, vd), tables, dims, _pick(s, 512),
                       ret_sub, _pick(h, 8), (0, qk, 2 * tn, vd))

    hres, hn = _outproj(mixed, wo_b, h2, norm2_w.reshape(1, d), _pick(m, 512))

    return _ffn(hn, hres, wg_b, wu_b, wd_b, out_norm_w.reshape(1, d), _pick(m, 512), _pick(w_gate.shape[1], 512))


def kernel(x, norm1_w, w_in, conv_w, a_log, dt_bias, gdn_norm_w, ret_norm_w, w_out, norm2_w,
           w_gate, w_up, w_down, norm_f_w):
    b, s, d = x.shape
    depth, h = a_log.shape
    assert depth == 1, "the final norm is fused into the single layer's FFN kernel"
    dk = d // 16
    dv = d // h
    dims = (b, s, h, dk, dv)
    out = _layer(x.reshape(b * s, d), dims, norm1_w[0], w_in[0], conv_w[0], a_log[0], dt_bias[0],
                 gdn_norm_w[0], ret_norm_w[0], w_out[0], norm2_w[0], w_gate[0], w_up[0], w_down[0],
                 norm_f_w)
    return out.reshape(b, s, d)
```

```python
import functools

import jax
import jax.numpy as jnp
from jax import lax
from jax.experimental import pallas as pl
from jax.experimental.pallas import tpu as pltpu

F32 = jnp.float32
BF16 = jnp.bfloat16

EPS = 1e-6
CONV_WIDTH = 4
GDN_CHUNK = 64
ROPE_BASE = 10000.0
BF16_TILE_ROWS = 16
HALO_ROWS = BF16_TILE_ROWS
VMEM_LIMIT_BYTES = 56 * 1024 * 1024


def _cparams(sem):
    return pltpu.CompilerParams(dimension_semantics=sem, vmem_limit_bytes=VMEM_LIMIT_BYTES)


def _sigmoid(x):
    return 1.0 / (1.0 + jnp.exp(-x))


def _silu(x):
    h = 0.5 * x
    return h + h * jnp.tanh(h)


def _dot(a, b):
    return jnp.dot(a, b, preferred_element_type=F32)


def _dot_nt(a, b):
    return lax.dot_general(a, b, (((1,), (1,)), ((), ())), preferred_element_type=F32)


def _dot_tn(a, b):
    return lax.dot_general(a, b, (((0,), (0,)), ((), ())), preferred_element_type=F32)


def _nth(j, values):
    idx = values[0]
    for p in range(1, len(values)):
        idx = jnp.where(j >= p, values[p], idx)
    return idx


def _row_tile_specs(srcs, tr, d, count, index):
    assert all(src % tr in (0, count) for src in srcs) and tr % count == 0 and count % BF16_TILE_ROWS == 0
    bases = tuple(src // tr for src in srcs)
    nexts = tuple((src // tr + 1) * (tr // count) for src in srcs)
    return [pl.BlockSpec((tr, d), lambda *g: (_nth(index(*g), bases), 0)),
            pl.BlockSpec((count, d), lambda *g: (_nth(index(*g), nexts), 0))]


def _cast_row_tile(cur_ref, nxt_ref, shifted, count):
    cur = cur_ref[...]
    moved = jnp.concatenate([cur[count:], nxt_ref[...]], axis=0)
    return jnp.where(shifted, moved, cur).astype(BF16)


def _cast_rows_kernel(cur_ref, nxt_ref, o_ref, *, flags, count):
    o_ref[...] = _cast_row_tile(cur_ref, nxt_ref, _nth(pl.program_id(0), flags) > 0, count)


def _cast_rows(wt, srcs, tr, count):
    d = wt.shape[1]
    flags = tuple(int(src % tr != 0) for src in srcs)
    return pl.pallas_call(
        functools.partial(_cast_rows_kernel, flags=flags, count=count),
        grid=(len(srcs),),
        in_specs=_row_tile_specs(srcs, tr, d, count, lambda j: j),
        out_specs=pl.BlockSpec((tr, d), lambda j: (j, 0)),
        out_shape=jax.ShapeDtypeStruct((len(srcs) * tr, d), BF16),
        compiler_params=_cparams(("parallel",)),
        name="cast_rows",
    )(wt, wt)


def _inproj_kernel(x_ref, nw_ref, w_ref, o_ref, u_ref):
    @pl.when(pl.program_id(1) == 0)
    def _():
        xf = x_ref[...]
        ms = jnp.mean(xf * xf, axis=-1, keepdims=True)
        u_ref[...] = ((xf * lax.rsqrt(ms + EPS)) * nw_ref[...]).astype(u_ref.dtype)

    o_ref[...] = _dot_nt(u_ref[...], w_ref[...]).astype(o_ref.dtype)


def _inproj(x2, nw, w, tm, tn, tiles):
    m, d = x2.shape
    return pl.pallas_call(
        _inproj_kernel,
        grid=(m // tm, len(tiles)),
        in_specs=[
            pl.BlockSpec((tm, d), lambda i, j: (i, 0)),
            pl.BlockSpec((1, d), lambda i, j: (0, 0)),
            pl.BlockSpec((tn, d), lambda i, j: (_nth(j, tiles), 0)),
        ],
        out_specs=[
            pl.BlockSpec((tm, tn), lambda i, j: (i, j)),
            pl.BlockSpec((tm, d), lambda i, j: (i, 0)),
        ],
        out_shape=[jax.ShapeDtypeStruct((m, len(tiles) * tn), BF16), jax.ShapeDtypeStruct((m, d), BF16)],
        compiler_params=_cparams(("parallel", "arbitrary")),
        name="inproj",
    )(x2, nw, w)


def _gate_scalars(u_ref, wba_ref, alog_ref, dtb_ref, beta_ref, gc_ref, chunk, sub):
    heads = alog_ref.shape[1]
    parts = [slice(p * sub, (p + 1) * sub) for p in range(u_ref.shape[0] // sub)]
    row = lax.broadcasted_iota(jnp.int32, (sub, sub), 0)
    col = lax.broadcasted_iota(jnp.int32, (sub, sub), 1)
    incl = jnp.where(((row // chunk) == (col // chunk)) & (row >= col), 1.0, 0.0).astype(BF16)
    bas = [_dot_nt(u_ref[sl, :], wba_ref[...]) for sl in parts]
    gs = []
    for sl, ba in zip(parts, bas):
        beta_ref[sl, :] = _sigmoid(ba[:, 0:heads])
        a = ba[:, heads:2 * heads] + dtb_ref[...]
        softplus = jnp.maximum(a, 0.0) + jnp.log1p(jnp.exp(-jnp.abs(a)))
        gs.append(-jnp.exp(alog_ref[...]) * softplus)
    g1 = [g.astype(BF16) for g in gs]
    r1 = [g - p1.astype(F32) for g, p1 in zip(gs, g1)]
    g2 = [r.astype(BF16) for r in r1]
    g3 = [(r - p2.astype(F32)).astype(BF16) for r, p2 in zip(r1, g2)]
    gcs = [_dot(incl, p1) + _dot(incl, p2) + _dot(incl, p3) for p1, p2, p3 in zip(g1, g2, g3)]
    for sl, gc in zip(parts, gcs):
        gc_ref[sl, :] = gc


def _rotproj_kernel(u_ref, w_ref, cos_ref, sin_ref, wba_ref, alog_ref, dtb_ref, *rest,
                    dk, chunk, sub, cast_flags, cast_count, per_step):
    cast_in, (o_ref, beta_ref, gc_ref, cast_ref) = rest[:2 * per_step], rest[2 * per_step:]
    cast_tr = cast_ref.shape[0] // per_step
    res = _dot_nt(u_ref[...], w_ref[...])
    cos = cos_ref[...]
    sin = sin_ref[...]
    for hh in range(res.shape[1] // dk):
        sl = slice(hh * dk, (hh + 1) * dk)
        x = res[:, sl]
        o_ref[:, sl] = (x * cos + pltpu.roll(x, dk // 2, 1) * sin).astype(o_ref.dtype)
    _gate_scalars(u_ref, wba_ref, alog_ref, dtb_ref, beta_ref, gc_ref, chunk, sub)
    for kk in range(per_step):
        shifted = _nth(pl.program_id(0) * per_step + kk, cast_flags) > 0
        cast_ref[kk * cast_tr:(kk + 1) * cast_tr, :] = _cast_row_tile(cast_in[2 * kk], cast_in[2 * kk + 1],
                                                                     shifted, cast_count)


def _rotproj(u, w, cosf, sins, wba, alog, dtb, wt, cast_srcs, cast_tr, cast_count,
             tm, tn, tile, seq, dk, chunk, sub):
    m, d = u.shape
    h = alog.shape[1]
    steps = m // tm
    assert seq % tm == 0 and tm % sub == 0 and sub % chunk == 0 and len(cast_srcs) % steps == 0
    per_seq = seq // tm
    per_step = len(cast_srcs) // steps
    scalar_out = jax.ShapeDtypeStruct((m, h), F32)
    cast_flags = tuple(int(src % cast_tr != 0) for src in cast_srcs)
    cast_in_specs = []
    for kk in range(per_step):
        cast_in_specs += _row_tile_specs(cast_srcs, cast_tr, wt.shape[1], cast_count,
                                         lambda i, kk=kk: i * per_step + kk)
    return pl.pallas_call(
        functools.partial(_rotproj_kernel, dk=dk, chunk=chunk, sub=sub, cast_flags=cast_flags,
                          cast_count=cast_count, per_step=per_step),
        grid=(steps,),
        in_specs=[
            pl.BlockSpec((tm, d), lambda i: (i, 0)),
            pl.BlockSpec((tn, d), lambda i: (tile, 0)),
            pl.BlockSpec((tm, dk), lambda i: (i % per_seq, 0)),
            pl.BlockSpec((tm, dk), lambda i: (i % per_seq, 0)),
            pl.BlockSpec((2 * h, d), lambda i: (0, 0)),
            pl.BlockSpec((1, h), lambda i: (0, 0)),
            pl.BlockSpec((1, h), lambda i: (0, 0)),
        ] + cast_in_specs,
        out_specs=[pl.BlockSpec((tm, tn), lambda i: (i, 0))] + [pl.BlockSpec((tm, h), lambda i: (i, 0))] * 2
        + [pl.BlockSpec((per_step * cast_tr, wt.shape[1]), lambda i: (i, 0))],
        out_shape=[jax.ShapeDtypeStruct((m, tn), BF16), scalar_out, scalar_out,
                   jax.ShapeDtypeStruct((len(cast_srcs) * cast_tr, wt.shape[1]), BF16)],
        compiler_params=_cparams(("arbitrary",)),
        name="rotproj",
    )(u, w, cosf, sins, wba, alog, dtb, *([wt, wt] * per_step))


def _gates_kernel(u_ref, ws_ref, wg_ref, *rest):
    ncast = (len(rest) - 1) // 2
    cast_in, o_ref, cast_out = rest[:ncast], rest[ncast], rest[ncast + 1:]
    u = u_ref[...]
    a = _dot_nt(u, ws_ref[...])
    g = _dot_nt(u, wg_ref[...])
    o_ref[...] = (_silu(a) * (0.5 + 0.5 * jnp.tanh(0.5 * g))).astype(o_ref.dtype)
    for src, dst in zip(cast_in, cast_out):
        dst[...] = src[...].astype(dst.dtype)


def _cast_slabs(rows, nsteps):
    nb = nsteps
    while nb > 1 and (rows % nb or (rows // nb) % BF16_TILE_ROWS):
        nb //= 2
    return nb


def _gates(u, w, tm, tg, vd, pairs, to_cast):
    m, d = u.shape
    per = vd // tg
    nj = len(pairs) * per
    nsteps = (m // tm) * nj

    def blk(j, which):
        idx = pairs[0][which] // tg + j
        for p in range(1, len(pairs)):
            idx = jnp.where(j >= p * per, pairs[p][which] // tg + j - p * per, idx)
        return idx

    cast_specs = []
    for mat in to_cast:
        nb = _cast_slabs(mat.shape[0], nsteps)
        rep = nsteps // nb
        cast_specs.append(pl.BlockSpec((mat.shape[0] // nb, mat.shape[1]),
                                       lambda i, j, rep=rep: ((i * nj + j) // rep, 0)))

    outs = pl.pallas_call(
        _gates_kernel,
        grid=(m // tm, nj),
        in_specs=[
            pl.BlockSpec((tm, d), lambda i, j: (i, 0)),
            pl.BlockSpec((tg, d), lambda i, j: (blk(j, 0), 0)),
            pl.BlockSpec((tg, d), lambda i, j: (blk(j, 1), 0)),
        ] + cast_specs,
        out_specs=[pl.BlockSpec((tm, tg), lambda i, j: (i, j))] + cast_specs,
        out_shape=[jax.ShapeDtypeStruct((m, len(pairs) * vd), BF16)]
        + [jax.ShapeDtypeStruct(mat.shape, BF16) for mat in to_cast],
        compiler_params=_cparams(("arbitrary", "arbitrary")),
        name="gates",
    )(u, w, w, *to_cast)
    return outs[0], outs[1:]


def _gdn_prep_kernel(q_ref, qh_ref, k_ref, kh_ref, v_ref, vh_ref, cwq_ref, cwk_ref, cwv_ref,
                     beta_ref, gcc_ref, gcr_ref,
                     wq_ref, kpt_ref, u_ref, attn_ref, egl_ref,
                     extk_scr, extv_scr, *, dk, dv, chunk, nchunks, heads):
    t = pl.program_id(1)
    rows = chunk * nchunks
    lane_head = lax.broadcasted_iota(jnp.int32, (1, beta_ref.shape[1]), 1)

    def head_col(ref, hh):
        hsel = lane_head == pl.program_id(2) * heads + hh
        return jnp.sum(jnp.where(hsel, ref[...], 0.0), axis=1, keepdims=True)

    blk = 128
    si = lax.broadcasted_iota(jnp.int32, ((CONV_WIDTH - 1) * blk, 2 * blk), 0)
    sc = lax.broadcasted_iota(jnp.int32, ((CONV_WIDTH - 1) * blk, 2 * blk), 1)
    shift_mat = jnp.where(sc == blk + (si % blk) - (si // blk + 1), 1.0, 0.0).astype(BF16)

    def conv_silu(x_ref, h_ref, cw_ref, ext_scr):
        del ext_scr
        halo = jnp.where(t == 0, jnp.zeros_like(h_ref[...]), h_ref[...])
        x = x_ref[...]
        ext = jnp.concatenate([jnp.zeros((blk - HALO_ROWS, x.shape[1]), x.dtype), halo, x], axis=0)
        out = []
        for r in range(rows // blk):
            sh = _dot(shift_mat, ext[r * blk:(r + 2) * blk])
            acc = x[r * blk:(r + 1) * blk].astype(F32) * cw_ref[CONV_WIDTH - 1:CONV_WIDTH, :]
            for j in range(1, CONV_WIDTH):
                acc = acc + sh[(j - 1) * blk:j * blk] * cw_ref[CONV_WIDTH - 1 - j:CONV_WIDTH - j, :]
            out.append(_silu(acc))
        return jnp.concatenate(out, axis=0)

    def l2norm(y):
        return y * lax.rsqrt(jnp.sum(y * y, axis=-1, keepdims=True) + EPS)

    q_all = conv_silu(q_ref, qh_ref, cwq_ref, extk_scr)
    k_all = conv_silu(k_ref, kh_ref, cwk_ref, extk_scr)
    v_all = conv_silu(v_ref, vh_ref, cwv_ref, extv_scr)

    ri = lax.broadcasted_iota(jnp.int32, (chunk, 2 * chunk), 0)
    li = lax.broadcasted_iota(jnp.int32, (chunk, 2 * chunk), 1)
    first = li < chunk
    ci = jnp.where(first, li, li - chunk)
    incl = ri >= ci
    strict = ri > ci
    eye = jnp.where(ri == ci, 1.0, 0.0).astype(F32)
    ident_k = jnp.where(lax.broadcasted_iota(jnp.int32, (dk, dk), 0)
                        == lax.broadcasted_iota(jnp.int32, (dk, dk), 1), 1.0, 0.0).astype(BF16)

    def blockdiag(x):
        return jnp.concatenate([jnp.where(first, x, 0.0), jnp.where(first, 0.0, x)], axis=0).astype(BF16)

    def pair_rows(xa, xb):
        z = jnp.zeros_like(xa)
        return jnp.concatenate([jnp.concatenate([xa, z], axis=1), jnp.concatenate([z, xb], axis=1)], axis=0)

    npairs = nchunks // 2
    units = [(hh, p) for hh in range(heads) for p in range(npairs)]
    ur = range(len(units))
    qs, ks, vs, bs, gcs = [], [], [], [], []
    for hh in range(heads):
        q = l2norm(q_all[:, hh * dk:(hh + 1) * dk]) * (dk ** -0.5)
        k = l2norm(k_all[:, hh * dk:(hh + 1) * dk])
        v = v_all[:, hh * dv:(hh + 1) * dv]
        beta, gcum = head_col(beta_ref, hh), head_col(gcc_ref, hh)
        for c in range(nchunks):
            sl = slice(c * chunk, (c + 1) * chunk)
            qs.append(q[sl])
            ks.append(k[sl])
            vs.append(v[sl])
            bs.append(beta[sl])
            gcs.append(gcum[sl])
    nc = range(len(qs))
    gls = [gc[chunk - 1:chunk, :] for gc in gcs]
    egs = [jnp.exp(gc) for gc in gcs]
    kbs = [ks[c] * bs[c] for c in nc]
    decays = [jnp.exp(jnp.where(incl, jnp.where(first, gcs[2 * u], gcs[2 * u + 1]) - gcr_ref[0, hh, p], -1e30))
              for u, (hh, p) in enumerate(units)]
    lhs = [jnp.concatenate([jnp.concatenate([kbs[2 * u], kbs[2 * u + 1]], axis=1),
                            jnp.concatenate([qs[2 * u], qs[2 * u + 1]], axis=1)], axis=0).astype(BF16) for u in ur]
    kq = [_dot_nt(lhs[u], pair_rows(ks[2 * u], ks[2 * u + 1]).astype(BF16)) for u in ur]
    amats = [jnp.where(strict, kq[u][0:chunk] * decays[u], 0.0) for u in ur]
    for u, (hh, p) in enumerate(units):
        attn_ref[0, hh, p] = (kq[u][chunk:2 * chunk] * decays[u]).astype(attn_ref.dtype)
    tinvs = [eye - a for a in amats]
    apows = amats
    span = 2
    while span < chunk:
        apows = [_dot(apows[u].astype(BF16), blockdiag(apows[u])) for u in ur]
        tinvs = [tinvs[u] + _dot(tinvs[u].astype(BF16), blockdiag(apows[u])) for u in ur]
        span *= 2
    rhs = [jnp.concatenate([jnp.concatenate([kbs[c] * egs[c], vs[c] * bs[c]], axis=1)
                            for c in (2 * u, 2 * u + 1)], axis=0).astype(BF16) for u in ur]
    wu = [_dot(blockdiag(tinvs[u]), rhs[u]) for u in ur]
    kps = [jnp.concatenate([ks[c] * jnp.exp(gls[c] - gcs[c]) for c in (2 * u, 2 * u + 1)], axis=0).astype(BF16)
           for u in ur]
    kpt = [_dot_nt(ident_k, kps[u]) for u in ur]
    for u, (hh, p) in enumerate(units):
        kpt_ref[0, hh, p] = kpt[u].astype(kpt_ref.dtype)
        for half in range(2):
            c = 2 * u + half
            rs = slice(half * chunk, (half + 1) * chunk)
            wq_ref[0, hh, 2 * p + half, 0:chunk, :] = wu[u][rs, 0:dk].astype(wq_ref.dtype)
            wq_ref[0, hh, 2 * p + half, chunk:2 * chunk, :] = (qs[c] * egs[c]).astype(wq_ref.dtype)
            u_ref[0, hh, 2 * p + half] = wu[u][rs, dk:dk + dv].astype(u_ref.dtype)
            egl_ref[0, hh, 2 * p + half] = jnp.broadcast_to(jnp.exp(gls[c]), (1, dv))


def _gdn_prep(proj, conv_w, beta_c, gc_c, gc_r, dims, rows, heads):
    b, s, h, dk, dv = dims
    chunk = GDN_CHUNK
    nchunks = rows // chunk
    n = s // chunk
    qk = h * dk
    tiles = s // rows
    kw, vw = heads * dk, heads * dv
    kcol = qk // kw
    vcol = 2 * qk // vw

    def rowblk(bi, ti):
        return bi * tiles + ti

    def halo(bi, ti):
        return jnp.maximum((bi * s + ti * rows) // HALO_ROWS - 1, 0)

    col_spec = pl.BlockSpec((rows, h), lambda bi, ti, hi: (rowblk(bi, ti), 0))
    in_specs = [
        pl.BlockSpec((rows, kw), lambda bi, ti, hi: (rowblk(bi, ti), hi)),
        pl.BlockSpec((HALO_ROWS, kw), lambda bi, ti, hi: (halo(bi, ti), hi)),
        pl.BlockSpec((rows, kw), lambda bi, ti, hi: (rowblk(bi, ti), kcol + hi)),
        pl.BlockSpec((HALO_ROWS, kw), lambda bi, ti, hi: (halo(bi, ti), kcol + hi)),
        pl.BlockSpec((rows, vw), lambda bi, ti, hi: (rowblk(bi, ti), vcol + hi)),
        pl.BlockSpec((HALO_ROWS, vw), lambda bi, ti, hi: (halo(bi, ti), vcol + hi)),
        pl.BlockSpec((CONV_WIDTH, kw), lambda bi, ti, hi: (0, hi)),
        pl.BlockSpec((CONV_WIDTH, kw), lambda bi, ti, hi: (0, kcol + hi)),
        pl.BlockSpec((CONV_WIDTH, vw), lambda bi, ti, hi: (0, vcol + hi)),
        col_spec, col_spec,
        pl.BlockSpec((1, heads, nchunks // 2, 1, 2 * chunk), lambda bi, ti, hi: (bi, hi, ti, 0, 0)),
    ]

    def out5(r, c):
        return pl.BlockSpec((1, heads, nchunks, r, c), lambda bi, ti, hi: (bi, hi, ti, 0, 0))

    def pair5(r, c):
        return pl.BlockSpec((1, heads, nchunks // 2, r, c), lambda bi, ti, hi: (bi, hi, ti, 0, 0))

    out_specs = [out5(2 * chunk, dk), pair5(dk, 2 * chunk), out5(chunk, dv), pair5(chunk, 2 * chunk), out5(1, dv)]
    out_shape = [
        jax.ShapeDtypeStruct((b, h, n, 2 * chunk, dk), BF16),
        jax.ShapeDtypeStruct((b, h, n // 2, dk, 2 * chunk), BF16),
        jax.ShapeDtypeStruct((b, h, n, chunk, dv), BF16),
        jax.ShapeDtypeStruct((b, h, n // 2, chunk, 2 * chunk), BF16),
        jax.ShapeDtypeStruct((b, h, n, 1, dv), F32),
    ]
    return pl.pallas_call(
        functools.partial(_gdn_prep_kernel, dk=dk, dv=dv, chunk=chunk, nchunks=nchunks, heads=heads),
        grid=(b, tiles, h // heads),
        in_specs=in_specs,
        out_specs=out_specs,
        out_shape=out_shape,
        scratch_shapes=[pltpu.VMEM((HALO_ROWS + rows, kw), F32), pltpu.VMEM((HALO_ROWS + rows, vw), F32)],
        compiler_params=_cparams(("parallel", "parallel", "parallel")),
        name="gdn_prep",
    )(proj, proj, proj, proj, proj, proj, conv_w, conv_w, conv_w, beta_c, gc_c, gc_r)


def _pair_pad(x, half):
    z = jnp.zeros_like(x)
    return jnp.concatenate([x, z] if half == 0 else [z, x], axis=0)


def _gdn_scan_kernel(wq_ref, kpt_ref, u_ref, attn_ref, egl_ref, f_ref, nw_ref, o_ref, s_scr,
                     *, dv, chunk, nchunks, unroll, heads, batch):
    @pl.when(pl.program_id(1) == 0)
    def _():
        s_scr[...] = jnp.zeros_like(s_scr)

    nw = nw_ref[...]
    seqs = [(bb, hh) for bb in range(batch) for hh in range(heads)]
    sr = range(len(seqs))

    def chunk_group(it, carry):
        for cc in range(unroll):
            c = it * unroll + cc
            rs = pl.ds(pl.multiple_of(c * chunk, chunk), chunk)
            states = [s_scr[i] for i in sr]
            r = [_dot(wq_ref[bb, hh, c], states[i].astype(BF16)) for i, (bb, hh) in enumerate(seqs)]
            v_new = [_pair_pad((u_ref[bb, hh, c] - r[i][0:chunk]).astype(BF16), cc)
                     for i, (bb, hh) in enumerate(seqs)]
            upd = [_dot(kpt_ref[bb, hh, it], v_new[i]) for i, (bb, hh) in enumerate(seqs)]
            for i, (bb, hh) in enumerate(seqs):
                s_scr[i] = states[i] * egl_ref[bb, hh, c] + upd[i]
            o = [r[i][chunk:2 * chunk] + _dot(attn_ref[bb, hh, it], v_new[i]) for i, (bb, hh) in enumerate(seqs)]
            for i, (bb, hh) in enumerate(seqs):
                cs = slice(hh * dv, (hh + 1) * dv)
                on = o[i] * lax.rsqrt(jnp.mean(o[i] * o[i], axis=-1, keepdims=True) + EPS) * nw
                o_ref[bb, rs, cs] = (on * f_ref[bb, rs, cs].astype(F32)).astype(o_ref.dtype)
        return carry

    lax.fori_loop(0, nchunks // unroll, chunk_group, 0)


def _gdn_scan(wq, kpt, u, attn, egl, gates, gdn_nw, dims, nchunks, unroll, heads, fcol):
    b, s, h, dk, dv = dims
    chunk = GDN_CHUNK
    n = s // chunk
    rows = nchunks * chunk
    steps = n // nchunks
    width = heads * dv
    assert nchunks % unroll == 0 and unroll == 2

    def in5(r, c):
        return pl.BlockSpec((b, heads, nchunks, r, c), lambda gi, i: (0, gi, i, 0, 0))

    def pair5(r, c):
        return pl.BlockSpec((b, heads, nchunks // 2, r, c), lambda gi, i: (0, gi, i, 0, 0))

    in_specs = [
        in5(2 * chunk, dk), pair5(dk, 2 * chunk), in5(chunk, dv), pair5(chunk, 2 * chunk), in5(1, dv),
        pl.BlockSpec((b, rows, width), lambda gi, i: (0, i, fcol // width + gi)),
        pl.BlockSpec((1, dv), lambda gi, i: (0, 0)),
    ]
    return pl.pallas_call(
        functools.partial(_gdn_scan_kernel, dv=dv, chunk=chunk, nchunks=nchunks, unroll=unroll, heads=heads,
                          batch=b),
        grid=(h // heads, steps),
        in_specs=in_specs,
        out_specs=pl.BlockSpec((b, rows, width), lambda gi, i: (0, i, gi)),
        out_shape=jax.ShapeDtypeStruct((b, s, h * dv), BF16),
        scratch_shapes=[pltpu.VMEM((b * heads, dk, dv), F32)],
        compiler_params=_cparams(("parallel", "arbitrary")),
        name="gdn_scan",
    )(wq, kpt, u, attn, egl, gates, gdn_nw)


def _retention_kernel(q_ref, k_ref, v_ref, f_ref, ma_ref, dmat_ref,
                      xi_ref, zeta_ref, gpow_ref, nw_ref, o_ref, s_scr, *, dk, dv, heads, sub):
    @pl.when(pl.program_id(2) == 0)
    def _():
        s_scr[...] = jnp.zeros_like(s_scr)

    hr = range(heads)
    ksl = [slice(hh * dk, (hh + 1) * dk) for hh in hr]
    vsl = [slice(hh * dv, (hh + 1) * dv) for hh in hr]

    def sub_chunk(it, carry):
        rs = pl.ds(pl.multiple_of(it * sub, sub), sub)
        q = [q_ref[rs, ksl[hh]] for hh in hr]
        k = [k_ref[rs, ksl[hh]] for hh in hr]
        kz = [(k[hh].astype(F32) * zeta_ref[hh]).astype(BF16) for hh in hr]
        states = [s_scr[hh] for hh in hr]
        scores = [(_dot_nt(q[hh], k[hh]) * dmat_ref[hh]).astype(BF16) for hh in hr]
        lhs = [jnp.concatenate([scores[hh], (q[hh].astype(F32) * xi_ref[hh]).astype(BF16)], axis=1) for hh in hr]
        rhs = [jnp.concatenate([v_ref[rs, vsl[hh]], states[hh].astype(BF16)], axis=0) for hh in hr]
        upd = [_dot_tn(kz[hh], v_ref[rs, vsl[hh]]) for hh in hr]
        for hh in hr:
            s_scr[hh] = states[hh] * gpow_ref[hh] + upd[hh]
        o = [_dot(lhs[hh], rhs[hh]) for hh in hr]
        for hh in hr:
            mu = jnp.mean(o[hh], axis=-1, keepdims=True)
            dev = o[hh] - mu
            var = jnp.mean(dev * dev, axis=-1, keepdims=True)
            ob = dev * lax.rsqrt(var + EPS) * nw_ref[:, vsl[hh]]
            mixed = ob * f_ref[rs, vsl[hh]].astype(F32) + ma_ref[rs, vsl[hh]].astype(F32)
            o_ref[rs, vsl[hh]] = mixed.astype(o_ref.dtype)
        return carry

    lax.fori_loop(0, q_ref.shape[0] // sub, sub_chunk, 0)


def _retention(qk, proj, gates, ma, ret_nw, tables, dims, rows, sub, heads, cols):
    b, s, h, dk, dv = dims
    dmat, xi, zeta, gpow = tables
    rqcol, rkcol, rvcol, fcol = cols
    steps = s // rows
    kw, vw = heads * dk, heads * dv
    assert rows % sub == 0

    def rb(bi, i):
        return bi * steps + i

    in_specs = [
        pl.BlockSpec((rows, kw), lambda bi, gi, i: (rb(bi, i), rqcol // kw + gi)),
        pl.BlockSpec((rows, kw), lambda bi, gi, i: (rb(bi, i), rkcol // kw + gi)),
        pl.BlockSpec((rows, vw), lambda bi, gi, i: (rb(bi, i), rvcol // vw + gi)),
        pl.BlockSpec((rows, vw), lambda bi, gi, i: (rb(bi, i), fcol // vw + gi)),
        pl.BlockSpec((rows, vw), lambda bi, gi, i: (rb(bi, i), gi)),
        pl.BlockSpec((heads, sub, sub), lambda bi, gi, i: (gi, 0, 0)),
        pl.BlockSpec((heads, sub, 1), lambda bi, gi, i: (gi, 0, 0)),
        pl.BlockSpec((heads, sub, 1), lambda bi, gi, i: (gi, 0, 0)),
        pl.BlockSpec((heads, 1, dv), lambda bi, gi, i: (gi, 0, 0)),
        pl.BlockSpec((1, vw), lambda bi, gi, i: (0, gi)),
    ]
    return pl.pallas_call(
        functools.partial(_retention_kernel, dk=dk, dv=dv, heads=heads, sub=sub),
        grid=(b, h // heads, steps),
        in_specs=in_specs,
        out_specs=pl.BlockSpec((rows, vw), lambda bi, gi, i: (rb(bi, i), gi)),
        out_shape=jax.ShapeDtypeStruct((b * s, h * dv), BF16),
        scratch_shapes=[pltpu.VMEM((heads, dk, dv), F32)],
        compiler_params=_cparams(("parallel", "parallel", "arbitrary")),
        name="retention",
    )(qk, qk, proj, gates, ma, dmat, xi, zeta, gpow, ret_nw)


def _rotary_tables(s, dk, rows):
    inv = ROPE_BASE ** (-jnp.arange(0, dk, 2, dtype=F32) / dk)
    lo = jnp.arange(rows, dtype=F32)[:, None] * inv[None, :]
    hi = (jnp.arange(s // rows, dtype=F32) * rows)[:, None] * inv[None, :]
    cl, sl, ch, sh = jnp.cos(lo), jnp.sin(lo), jnp.cos(hi)[:, None], jnp.sin(hi)[:, None]
    cos = (ch * cl - sh * sl).reshape(s, dk // 2)
    sin = (sh * cl + ch * sl).reshape(s, dk // 2)
    return jnp.concatenate([cos, cos], axis=-1), jnp.concatenate([-sin, sin], axis=-1)


def _retention_tables(h, dk, dv, rows):
    log_gamma = jnp.log1p(-jnp.exp2(-5.0 - jnp.arange(h, dtype=F32)))
    pos = jnp.arange(rows, dtype=F32)
    dist = pos[:, None] - pos[None, :]
    kscale = dk ** -0.5
    dmat = jnp.exp(jnp.where(dist >= 0, dist * log_gamma[:, None, None], -jnp.inf)) * kscale
    xi = jnp.exp((pos + 1.0) * log_gamma[:, None])[:, :, None]
    zeta = jnp.exp((rows - 1.0 - pos) * log_gamma[:, None])[:, :, None] * kscale
    gpow = jnp.broadcast_to(jnp.exp(rows * log_gamma)[:, None, None], (h, 1, dv))
    return dmat, xi, zeta, gpow


def _outproj_kernel(m_ref, w_ref, x_ref, nw_ref, h_ref, hn_ref):
    hres = x_ref[...] + _dot(m_ref[...], w_ref[...])
    h_ref[...] = hres
    ms = jnp.mean(hres * hres, axis=-1, keepdims=True)
    hn_ref[...] = ((hres * lax.rsqrt(ms + EPS)) * nw_ref[...]).astype(hn_ref.dtype)


def _outproj(mixed, w, x2, nw, tm):
    m, d = x2.shape
    kdim = mixed.shape[1]
    return pl.pallas_call(
        _outproj_kernel,
        grid=(m // tm,),
        in_specs=[
            pl.BlockSpec((tm, kdim), lambda i: (i, 0)),
            pl.BlockSpec((kdim, d), lambda i: (0, 0)),
            pl.BlockSpec((tm, d), lambda i: (i, 0)),
            pl.BlockSpec((1, d), lambda i: (0, 0)),
        ],
        out_specs=[pl.BlockSpec((tm, d), lambda i: (i, 0))] * 2,
        out_shape=[jax.ShapeDtypeStruct((m, d), F32), jax.ShapeDtypeStruct((m, d), BF16)],
        compiler_params=_cparams(("parallel",)),
        name="outproj",
    )(mixed, w, x2, nw)


def _ffn_kernel(hn_ref, h_ref, wg_ref, wu_ref, wd_ref, nw_ref, o_ref):
    f = pl.program_id(1)

    @pl.when(f == 0)
    def _():
        o_ref[...] = h_ref[...]

    hn = hn_ref[...]
    g = _dot(hn, wg_ref[...])
    up = _dot(hn, wu_ref[...])
    act = (_silu(g) * up).astype(BF16)
    o_ref[...] += _dot(act, wd_ref[...])

    @pl.when(f == pl.num_programs(1) - 1)
    def _():
        hres = o_ref[...]
        ms = jnp.mean(hres * hres, axis=-1, keepdims=True)
        o_ref[...] = (hres * lax.rsqrt(ms + EPS)) * nw_ref[...]


def _ffn(hn, hres, wg, wu, wd, nw, tm, tf):
    m, d = hres.shape
    ff = wg.shape[1]
    return pl.pallas_call(
        _ffn_kernel,
        grid=(m // tm, ff // tf),
        in_specs=[
            pl.BlockSpec((tm, d), lambda i, j: (i, 0)),
            pl.BlockSpec((tm, d), lambda i, j: (i, 0)),
            pl.BlockSpec((d, tf), lambda i, j: (0, j)),
            pl.BlockSpec((d, tf), lambda i, j: (0, j)),
            pl.BlockSpec((tf, d), lambda i, j: (j, 0)),
            pl.BlockSpec((1, d), lambda i, j: (0, 0)),
        ],
        out_specs=pl.BlockSpec((tm, d), lambda i, j: (i, 0)),
        out_shape=jax.ShapeDtypeStruct((m, d), F32),
        compiler_params=_cparams(("parallel", "arbitrary")),
        name="ffn",
    )(hn, hres, wg, wu, wd, nw)


def _pick(total, want):
    t = min(total, want)
    while total % t:
        t //= 2
    return t


def _layer(h2, dims, norm1_w, w_in, conv_w, a_log, dt_bias, gdn_norm_w, ret_norm_w,
           w_out, norm2_w, w_gate, w_up, w_down, out_norm_w):
    b, s, h, dk, dv = dims
    m, d = h2.shape
    qk, vd = h * dk, h * dv
    chunk = GDN_CHUNK
    n = s // chunk

    assert 2 * qk == vd
    tn = vd
    sc0 = 3 * vd
    sc = 2 * h
    w_in_t = w_in.T
    w_ba = w_in_t[sc0:sc0 + sc].astype(BF16)
    tr = _pick(vd, 1024)
    per = vd // tr
    w_proj = _cast_rows(w_in_t, [g * vd + k * tr for g in (0, 1) for k in range(per)]
                        + [sc0 + sc + g * vd + k * tr for g in (0, 1) for k in range(per)], tr, sc)

    proj, u = _inproj(h2, norm1_w.reshape(1, d), w_proj, _pick(m, 1024), tn, (0, 1, 3))
    ret_sub = _pick(s, 256)
    cosf, sins = _rotary_tables(s, dk, ret_sub)
    rot_tm = _pick(s, 1024)
    gtr = _pick(vd, (4 * vd) // (m // rot_tm))
    gper = vd // gtr
    gate_srcs = ([2 * vd + k * gtr for k in range(gper)]
                 + [sc0 + sc + g * vd + k * gtr for g in (2, 3, 4) for k in range(gper)])
    qk_rot, beta, gcum, w_gate_rows = _rotproj(
        u, w_proj, cosf, sins, w_ba, a_log.reshape(1, h), dt_bias.reshape(1, h), w_in_t, gate_srcs, gtr, sc,
        rot_tm, tn, 2, s, dk, chunk, _pick(s, 256))
    gates, (wg_b, wu_b, wd_b, wo_b) = _gates(u, w_gate_rows, _pick(m, 1024), _pick(vd, 1024), vd,
                                             ((0, 2 * vd), (vd, 3 * vd)), [w_gate, w_up, w_down, w_out])

    gc_r = gcum.reshape(b, n // 2, 2 * chunk, h).transpose(0, 3, 1, 2)[:, :, :, None, :]
    wq, kpt, uu, attn, egl = _gdn_prep(proj, conv_w, beta, gcum, gc_r, dims, _pick(s, 512), _pick(h, 8))

    ma = _gdn_scan(wq, kpt, uu, attn, egl, gates.reshape(b, s, 2 * vd), gdn_norm_w.reshape(1, dv), dims,
                   _pick(n, 8), _pick(n, 2), _pick(h, 8), 0).reshape(m, vd)

    tables = _retention_tables(h, dk, dv, ret_sub)
    mixed = _retention(qk_rot, proj, gates, ma, ret_norm_w.reshape(1, vd), tables, dims, _pick(s, 512),
                       ret_sub, _pick(h, 8), (0, qk, 2 * tn, vd))

    hres, hn = _outproj(mixed, wo_b, h2, norm2_w.reshape(1, d), _pick(m, 512))

    return _ffn(hn, hres, wg_b, wu_b, wd_b, out_norm_w.reshape(1, d), _pick(m, 512), _pick(w_gate.shape[1], 512))


def kernel(x, norm1_w, w_in, conv_w, a_log, dt_bias, gdn_norm_w, ret_norm_w, w_out, norm2_w,
           w_gate, w_up, w_down, norm_f_w):
    b, s, d = x.shape
    depth, h = a_log.shape
    assert depth == 1, "the final norm is fused into the single layer's FFN kernel"
    dk = d // 16
    dv = d // h
    dims = (b, s, h, dk, dv)
    out = _layer(x.reshape(b * s, d), dims, norm1_w[0], w_in[0], conv_w[0], a_log[0], dt_bias[0],
                 gdn_norm_w[0], ret_norm_w[0], w_out[0], norm2_w[0], w_gate[0], w_up[0], w_down[0],
                 norm_f_w)
    return out.reshape(b, s, d)
```

```python
import functools

import jax
import jax.numpy as jnp
from jax import lax
from jax.experimental import pallas as pl
from jax.experimental.pallas import tpu as pltpu

F32 = jnp.float32
BF16 = jnp.bfloat16

EPS = 1e-6
CONV_WIDTH = 4
GDN_CHUNK = 64
ROPE_BASE = 10000.0
BF16_TILE_ROWS = 16
HALO_ROWS = BF16_TILE_ROWS
VMEM_LIMIT_BYTES = 56 * 1024 * 1024


def _cparams(sem):
    return pltpu.CompilerParams(dimension_semantics=sem, vmem_limit_bytes=VMEM_LIMIT_BYTES)


def _sigmoid(x):
    return 1.0 / (1.0 + jnp.exp(-x))


def _silu(x):
    h = 0.5 * x
    return h + h * jnp.tanh(h)


def _dot(a, b):
    return jnp.dot(a, b, preferred_element_type=F32)


def _dot_nt(a, b):
    return lax.dot_general(a, b, (((1,), (1,)), ((), ())), preferred_element_type=F32)


def _dot_tn(a, b):
    return lax.dot_general(a, b, (((0,), (0,)), ((), ())), preferred_element_type=F32)


def _nth(j, values):
    idx = values[0]
    for p in range(1, len(values)):
        idx = jnp.where(j >= p, values[p], idx)
    return idx


def _row_tile_specs(srcs, tr, d, count, index):
    assert all(src % tr in (0, count) for src in srcs) and tr % count == 0 and count % BF16_TILE_ROWS == 0
    bases = tuple(src // tr for src in srcs)
    nexts = tuple((src // tr + 1) * (tr // count) for src in srcs)
    return [pl.BlockSpec((tr, d), lambda *g: (_nth(index(*g), bases), 0)),
            pl.BlockSpec((count, d), lambda *g: (_nth(index(*g), nexts), 0))]


def _cast_row_tile(cur_ref, nxt_ref, shifted, count):
    cur = cur_ref[...]
    moved = jnp.concatenate([cur[count:], nxt_ref[...]], axis=0)
    return jnp.where(shifted, moved, cur).astype(BF16)


def _cast_rows_kernel(cur_ref, nxt_ref, o_ref, *, flags, count):
    o_ref[...] = _cast_row_tile(cur_ref, nxt_ref, _nth(pl.program_id(0), flags) > 0, count)


def _cast_rows(wt, srcs, tr, count):
    d = wt.shape[1]
    flags = tuple(int(src % tr != 0) for src in srcs)
    return pl.pallas_call(
        functools.partial(_cast_rows_kernel, flags=flags, count=count),
        grid=(len(srcs),),
        in_specs=_row_tile_specs(srcs, tr, d, count, lambda j: j),
        out_specs=pl.BlockSpec((tr, d), lambda j: (j, 0)),
        out_shape=jax.ShapeDtypeStruct((len(srcs) * tr, d), BF16),
        compiler_params=_cparams(("parallel",)),
        name="cast_rows",
    )(wt, wt)


def _inproj_kernel(x_ref, nw_ref, w_ref, o_ref, u_ref):
    @pl.when(pl.program_id(1) == 0)
    def _():
        xf = x_ref[...]
        ms = jnp.mean(xf * xf, axis=-1, keepdims=True)
        u_ref[...] = ((xf * lax.rsqrt(ms + EPS)) * nw_ref[...]).astype(u_ref.dtype)

    o_ref[...] = _dot_nt(u_ref[...], w_ref[...]).astype(o_ref.dtype)


def _inproj(x2, nw, w, tm, tn, tiles):
    m, d = x2.shape
    return pl.pallas_call(
        _inproj_kernel,
        grid=(m // tm, len(tiles)),
        in_specs=[
            pl.BlockSpec((tm, d), lambda i, j: (i, 0)),
            pl.BlockSpec((1, d), lambda i, j: (0, 0)),
            pl.BlockSpec((tn, d), lambda i, j: (_nth(j, tiles), 0)),
        ],
        out_specs=[
            pl.BlockSpec((tm, tn), lambda i, j: (i, j)),
            pl.BlockSpec((tm, d), lambda i, j: (i, 0)),
        ],
        out_shape=[jax.ShapeDtypeStruct((m, len(tiles) * tn), BF16), jax.ShapeDtypeStruct((m, d), BF16)],
        compiler_params=_cparams(("parallel", "arbitrary")),
        name="inproj",
    )(x2, nw, w)


def _gate_scalars(u_ref, wba_ref, alog_ref, dtb_ref, beta_ref, gc_ref, chunk, sub):
    heads = alog_ref.shape[1]
    parts = [slice(p * sub, (p + 1) * sub) for p in range(u_ref.shape[0] // sub)]
    row = lax.broadcasted_iota(jnp.int32, (sub, sub), 0)
    col = lax.broadcasted_iota(jnp.int32, (sub, sub), 1)
    incl = jnp.where(((row // chunk) == (col // chunk)) & (row >= col), 1.0, 0.0).astype(BF16)
    bas = [_dot_nt(u_ref[sl, :], wba_ref[...]) for sl in parts]
    gs = []
    for sl, ba in zip(parts, bas):
        beta_ref[sl, :] = _sigmoid(ba[:, 0:heads])
        a = ba[:, heads:2 * heads] + dtb_ref[...]
        softplus = jnp.maximum(a, 0.0) + jnp.log1p(jnp.exp(-jnp.abs(a)))
        gs.append(-jnp.exp(alog_ref[...]) * softplus)
    g1 = [g.astype(BF16) for g in gs]
    r1 = [g - p1.astype(F32) for g, p1 in zip(gs, g1)]
    g2 = [r.astype(BF16) for r in r1]
    g3 = [(r - p2.astype(F32)).astype(BF16) for r, p2 in zip(r1, g2)]
    gcs = [_dot(incl, p1) + _dot(incl, p2) + _dot(incl, p3) for p1, p2, p3 in zip(g1, g2, g3)]
    for sl, gc in zip(parts, gcs):
        gc_ref[sl, :] = gc


def _rotproj_kernel(u_ref, w_ref, cos_ref, sin_ref, wba_ref, alog_ref, dtb_ref, *rest,
                    dk, chunk, sub, cast_flags, cast_count, per_step):
    cast_in, (o_ref, beta_ref, gc_ref, cast_ref) = rest[:2 * per_step], rest[2 * per_step:]
    cast_tr = cast_ref.shape[0] // per_step
    res = _dot_nt(u_ref[...], w_ref[...])
    cos = cos_ref[...]
    sin = sin_ref[...]
    for hh in range(res.shape[1] // dk):
        sl = slice(hh * dk, (hh + 1) * dk)
        x = res[:, sl]
        o_ref[:, sl] = (x * cos + pltpu.roll(x, dk // 2, 1) * sin).astype(o_ref.dtype)
    _gate_scalars(u_ref, wba_ref, alog_ref, dtb_ref, beta_ref, gc_ref, chunk, sub)
    for kk in range(per_step):
        shifted = _nth(pl.program_id(0) * per_step + kk, cast_flags) > 0
        cast_ref[kk * cast_tr:(kk + 1) * cast_tr, :] = _cast_row_tile(cast_in[2 * kk], cast_in[2 * kk + 1],
                                                                     shifted, cast_count)


def _rotproj(u, w, cosf, sins, wba, alog, dtb, wt, cast_srcs, cast_tr, cast_count,
             tm, tn, tile, seq, dk, chunk, sub):
    m, d = u.shape
    h = alog.shape[1]
    steps = m // tm
    assert seq % tm == 0 and tm % sub == 0 and sub % chunk == 0 and len(cast_srcs) % steps == 0
    per_seq = seq // tm
    per_step = len(cast_srcs) // steps
    scalar_out = jax.ShapeDtypeStruct((m, h), F32)
    cast_flags = tuple(int(src % cast_tr != 0) for src in cast_srcs)
    cast_in_specs = []
    for kk in range(per_step):
        cast_in_specs += _row_tile_specs(cast_srcs, cast_tr, wt.shape[1], cast_count,
                                         lambda i, kk=kk: i * per_step + kk)
    return pl.pallas_call(
        functools.partial(_rotproj_kernel, dk=dk, chunk=chunk, sub=sub, cast_flags=cast_flags,
                          cast_count=cast_count, per_step=per_step),
        grid=(steps,),
        in_specs=[
            pl.BlockSpec((tm, d), lambda i: (i, 0)),
            pl.BlockSpec((tn, d), lambda i: (tile, 0)),
            pl.BlockSpec((tm, dk), lambda i: (i % per_seq, 0)),
            pl.BlockSpec((tm, dk), lambda i: (i % per_seq, 0)),
            pl.BlockSpec((2 * h, d), lambda i: (0, 0)),
            pl.BlockSpec((1, h), lambda i: (0, 0)),
            pl.BlockSpec((1, h), lambda i: (0, 0)),
        ] + cast_in_specs,
        out_specs=[pl.BlockSpec((tm, tn), lambda i: (i, 0))] + [pl.BlockSpec((tm, h), lambda i: (i, 0))] * 2
        + [pl.BlockSpec((per_step * cast_tr, wt.shape[1]), lambda i: (i, 0))],
        out_shape=[jax.ShapeDtypeStruct((m, tn), BF16), scalar_out, scalar_out,
                   jax.ShapeDtypeStruct((len(cast_srcs) * cast_tr, wt.shape[1]), BF16)],
        compiler_params=_cparams(("arbitrary",)),
        name="rotproj",
    )(u, w, cosf, sins, wba, alog, dtb, *([wt, wt] * per_step))


def _gates_kernel(u_ref, ws_ref, wg_ref, *rest):
    ncast = (len(rest) - 1) // 2
    cast_in, o_ref, cast_out = rest[:ncast], rest[ncast], rest[ncast + 1:]
    u = u_ref[...]
    a = _dot_nt(u, ws_ref[...])
    g = _dot_nt(u, wg_ref[...])
    o_ref[...] = (_silu(a) * (0.5 + 0.5 * jnp.tanh(0.5 * g))).astype(o_ref.dtype)
    for src, dst in zip(cast_in, cast_out):
        dst[...] = src[...].astype(dst.dtype)


def _cast_slabs(rows, nsteps):
    nb = nsteps
    while nb > 1 and (rows % nb or (rows // nb) % BF16_TILE_ROWS):
        nb //= 2
    return nb


def _gates(u, w, tm, tg, vd, pairs, to_cast):
    m, d = u.shape
    per = vd // tg
    nj = len(pairs) * per
    nsteps = (m // tm) * nj

    def blk(j, which):
        idx = pairs[0][which] // tg + j
        for p in range(1, len(pairs)):
            idx = jnp.where(j >= p * per, pairs[p][which] // tg + j - p * per, idx)
        return idx

    cast_specs = []
    for mat in to_cast:
        nb = _cast_slabs(mat.shape[0], nsteps)
        rep = nsteps // nb
        cast_specs.append(pl.BlockSpec((mat.shape[0] // nb, mat.shape[1]),
                                       lambda i, j, rep=rep: ((i * nj + j) // rep, 0)))

    outs = pl.pallas_call(
        _gates_kernel,
        grid=(m // tm, nj),
        in_specs=[
            pl.BlockSpec((tm, d), lambda i, j: (i, 0)),
            pl.BlockSpec((tg, d), lambda i, j: (blk(j, 0), 0)),
            pl.BlockSpec((tg, d), lambda i, j: (blk(j, 1), 0)),
        ] + cast_specs,
        out_specs=[pl.BlockSpec((tm, tg), lambda i, j: (i, j))] + cast_specs,
        out_shape=[jax.ShapeDtypeStruct((m, len(pairs) * vd), BF16)]
        + [jax.ShapeDtypeStruct(mat.shape, BF16) for mat in to_cast],
        compiler_params=_cparams(("arbitrary", "arbitrary")),
        name="gates",
    )(u, w, w, *to_cast)
    return outs[0], outs[1:]


def _gdn_prep_kernel(q_ref, qh_ref, k_ref, kh_ref, v_ref, vh_ref, cwq_ref, cwk_ref, cwv_ref,
                     beta_ref, gcc_ref, gcr_ref,
                     wq_ref, kpt_ref, u_ref, attn_ref, egl_ref,
                     extk_scr, extv_scr, *, dk, dv, chunk, nchunks, heads):
    t = pl.program_id(1)
    rows = chunk * nchunks
    lane_head = lax.broadcasted_iota(jnp.int32, (1, beta_ref.shape[1]), 1)

    def head_col(ref, hh):
        hsel = lane_head == pl.program_id(2) * heads + hh
        return jnp.sum(jnp.where(hsel, ref[...], 0.0), axis=1, keepdims=True)

    blk = 128
    win = HALO_ROWS + blk
    si = lax.broadcasted_iota(jnp.int32, (blk, CONV_WIDTH * win), 0)
    sc = lax.broadcasted_iota(jnp.int32, (blk, CONV_WIDTH * win), 1)
    pick = jnp.where(sc % win == HALO_ROWS + si - (CONV_WIDTH - 1 - sc // win), 1.0, 0.0).astype(BF16)

    def conv_silu(x_ref, h_ref, cw_ref, ext_scr):
        del ext_scr
        halo = jnp.where(t == 0, jnp.zeros_like(h_ref[...]), h_ref[...])
        ext = jnp.concatenate([halo, x_ref[...]], axis=0)
        taps = [ext * cw_ref[tap:tap + 1, :].astype(BF16) for tap in range(CONV_WIDTH)]
        out = []
        for r in range(rows // blk):
            stacked = jnp.concatenate([tp[r * blk:r * blk + win] for tp in taps], axis=0)
            out.append(_silu(_dot(pick, stacked)))
        return jnp.concatenate(out, axis=0)

    def l2norm(y):
        return y * lax.rsqrt(jnp.sum(y * y, axis=-1, keepdims=True) + EPS)

    q_all = conv_silu(q_ref, qh_ref, cwq_ref, extk_scr)
    k_all = conv_silu(k_ref, kh_ref, cwk_ref, extk_scr)
    v_all = conv_silu(v_ref, vh_ref, cwv_ref, extv_scr)

    ri = lax.broadcasted_iota(jnp.int32, (chunk, 2 * chunk), 0)
    li = lax.broadcasted_iota(jnp.int32, (chunk, 2 * chunk), 1)
    first = li < chunk
    ci = jnp.where(first, li, li - chunk)
    incl = ri >= ci
    strict = ri > ci
    eye = jnp.where(ri == ci, 1.0, 0.0).astype(F32)
    ident_k = jnp.where(lax.broadcasted_iota(jnp.int32, (dk, dk), 0)
                        == lax.broadcasted_iota(jnp.int32, (dk, dk), 1), 1.0, 0.0).astype(BF16)

    def blockdiag(x):
        return jnp.concatenate([jnp.where(first, x, 0.0), jnp.where(first, 0.0, x)], axis=0).astype(BF16)

    def pair_rows(xa, xb):
        z = jnp.zeros_like(xa)
        return jnp.concatenate([jnp.concatenate([xa, z], axis=1), jnp.concatenate([z, xb], axis=1)], axis=0)

    npairs = nchunks // 2
    units = [(hh, p) for hh in range(heads) for p in range(npairs)]
    ur = range(len(units))
    qs, ks, vs, bs, gcs = [], [], [], [], []
    for hh in range(heads):
        q = l2norm(q_all[:, hh * dk:(hh + 1) * dk]) * (dk ** -0.5)
        k = l2norm(k_all[:, hh * dk:(hh + 1) * dk])
        v = v_all[:, hh * dv:(hh + 1) * dv]
        beta, gcum = head_col(beta_ref, hh), head_col(gcc_ref, hh)
        for c in range(nchunks):
            sl = slice(c * chunk, (c + 1) * chunk)
            qs.append(q[sl])
            ks.append(k[sl])
            vs.append(v[sl])
            bs.append(beta[sl])
            gcs.append(gcum[sl])
    nc = range(len(qs))
    gls = [gc[chunk - 1:chunk, :] for gc in gcs]
    egs = [jnp.exp(gc) for gc in gcs]
    kbs = [ks[c] * bs[c] for c in nc]
    decays = [jnp.exp(jnp.where(incl, jnp.where(first, gcs[2 * u], gcs[2 * u + 1]) - gcr_ref[0, hh, p], -1e30))
              for u, (hh, p) in enumerate(units)]
    lhs = [jnp.concatenate([jnp.concatenate([kbs[2 * u], kbs[2 * u + 1]], axis=1),
                            jnp.concatenate([qs[2 * u], qs[2 * u + 1]], axis=1)], axis=0).astype(BF16) for u in ur]
    kq = [_dot_nt(lhs[u], pair_rows(ks[2 * u], ks[2 * u + 1]).astype(BF16)) for u in ur]
    amats = [jnp.where(strict, kq[u][0:chunk] * decays[u], 0.0) for u in ur]
    for u, (hh, p) in enumerate(units):
        attn_ref[0, hh, p] = (kq[u][chunk:2 * chunk] * decays[u]).astype(attn_ref.dtype)
    tinvs = [eye - a for a in amats]
    apows = amats
    span = 2
    while span < chunk:
        apows = [_dot(apows[u].astype(BF16), blockdiag(apows[u])) for u in ur]
        tinvs = [tinvs[u] + _dot(tinvs[u].astype(BF16), blockdiag(apows[u])) for u in ur]
        span *= 2
    rhs = [jnp.concatenate([jnp.concatenate([kbs[c] * egs[c], vs[c] * bs[c]], axis=1)
                            for c in (2 * u, 2 * u + 1)], axis=0).astype(BF16) for u in ur]
    wu = [_dot(blockdiag(tinvs[u]), rhs[u]) for u in ur]
    kps = [jnp.concatenate([ks[c] * jnp.exp(gls[c] - gcs[c]) for c in (2 * u, 2 * u + 1)], axis=0).astype(BF16)
           for u in ur]
    kpt = [_dot_nt(ident_k, kps[u]) for u in ur]
    for u, (hh, p) in enumerate(units):
        kpt_ref[0, hh, p] = kpt[u].astype(kpt_ref.dtype)
        for half in range(2):
            c = 2 * u + half
            rs = slice(half * chunk, (half + 1) * chunk)
            wq_ref[0, hh, 2 * p + half, 0:chunk, :] = wu[u][rs, 0:dk].astype(wq_ref.dtype)
            wq_ref[0, hh, 2 * p + half, chunk:2 * chunk, :] = (qs[c] * egs[c]).astype(wq_ref.dtype)
            u_ref[0, hh, 2 * p + half] = wu[u][rs, dk:dk + dv].astype(u_ref.dtype)
            egl_ref[0, hh, 2 * p + half] = jnp.broadcast_to(jnp.exp(gls[c]), (1, dv))


def _gdn_prep(proj, conv_w, beta_c, gc_c, gc_r, dims, rows, heads):
    b, s, h, dk, dv = dims
    chunk = GDN_CHUNK
    nchunks = rows // chunk
    n = s // chunk
    qk = h * dk
    tiles = s // rows
    kw, vw = heads * dk, heads * dv
    kcol = qk // kw
    vcol = 2 * qk // vw

    def rowblk(bi, ti):
        return bi * tiles + ti

    def halo(bi, ti):
        return jnp.maximum((bi * s + ti * rows) // HALO_ROWS - 1, 0)

    col_spec = pl.BlockSpec((rows, h), lambda bi, ti, hi: (rowblk(bi, ti), 0))
    in_specs = [
        pl.BlockSpec((rows, kw), lambda bi, ti, hi: (rowblk(bi, ti), hi)),
        pl.BlockSpec((HALO_ROWS, kw), lambda bi, ti, hi: (halo(bi, ti), hi)),
        pl.BlockSpec((rows, kw), lambda bi, ti, hi: (rowblk(bi, ti), kcol + hi)),
        pl.BlockSpec((HALO_ROWS, kw), lambda bi, ti, hi: (halo(bi, ti), kcol + hi)),
        pl.BlockSpec((rows, vw), lambda bi, ti, hi: (rowblk(bi, ti), vcol + hi)),
        pl.BlockSpec((HALO_ROWS, vw), lambda bi, ti, hi: (halo(bi, ti), vcol + hi)),
        pl.BlockSpec((CONV_WIDTH, kw), lambda bi, ti, hi: (0, hi)),
        pl.BlockSpec((CONV_WIDTH, kw), lambda bi, ti, hi: (0, kcol + hi)),
        pl.BlockSpec((CONV_WIDTH, vw), lambda bi, ti, hi: (0, vcol + hi)),
        col_spec, col_spec,
        pl.BlockSpec((1, heads, nchunks // 2, 1, 2 * chunk), lambda bi, ti, hi: (bi, hi, ti, 0, 0)),
    ]

    def out5(r, c):
        return pl.BlockSpec((1, heads, nchunks, r, c), lambda bi, ti, hi: (bi, hi, ti, 0, 0))

    def pair5(r, c):
        return pl.BlockSpec((1, heads, nchunks // 2, r, c), lambda bi, ti, hi: (bi, hi, ti, 0, 0))

    out_specs = [out5(2 * chunk, dk), pair5(dk, 2 * chunk), out5(chunk, dv), pair5(chunk, 2 * chunk), out5(1, dv)]
    out_shape = [
        jax.ShapeDtypeStruct((b, h, n, 2 * chunk, dk), BF16),
        jax.ShapeDtypeStruct((b, h, n // 2, dk, 2 * chunk), BF16),
        jax.ShapeDtypeStruct((b, h, n, chunk, dv), BF16),
        jax.ShapeDtypeStruct((b, h, n // 2, chunk, 2 * chunk), BF16),
        jax.ShapeDtypeStruct((b, h, n, 1, dv), F32),
    ]
    return pl.pallas_call(
        functools.partial(_gdn_prep_kernel, dk=dk, dv=dv, chunk=chunk, nchunks=nchunks, heads=heads),
        grid=(b, tiles, h // heads),
        in_specs=in_specs,
        out_specs=out_specs,
        out_shape=out_shape,
        scratch_shapes=[pltpu.VMEM((HALO_ROWS + rows, kw), F32), pltpu.VMEM((HALO_ROWS + rows, vw), F32)],
        compiler_params=_cparams(("parallel", "parallel", "parallel")),
        name="gdn_prep",
    )(proj, proj, proj, proj, proj, proj, conv_w, conv_w, conv_w, beta_c, gc_c, gc_r)


def _pair_pad(x, half):
    z = jnp.zeros_like(x)
    return jnp.concatenate([x, z] if half == 0 else [z, x], axis=0)


def _gdn_scan_kernel(wq_ref, kpt_ref, u_ref, attn_ref, egl_ref, f_ref, nw_ref, o_ref, s_scr,
                     *, dv, chunk, nchunks, unroll, heads, batch):
    @pl.when(pl.program_id(1) == 0)
    def _():
        s_scr[...] = jnp.zeros_like(s_scr)

    nw = nw_ref[...]
    seqs = [(bb, hh) for bb in range(batch) for hh in range(heads)]
    sr = range(len(seqs))

    def chunk_group(it, carry):
        for cc in range(unroll):
            c = it * unroll + cc
            rs = pl.ds(pl.multiple_of(c * chunk, chunk), chunk)
            states = [s_scr[i] for i in sr]
            r = [_dot(wq_ref[bb, hh, c], states[i].astype(BF16)) for i, (bb, hh) in enumerate(seqs)]
            v_new = [_pair_pad((u_ref[bb, hh, c] - r[i][0:chunk]).astype(BF16), cc)
                     for i, (bb, hh) in enumerate(seqs)]
            upd = [_dot(kpt_ref[bb, hh, it], v_new[i]) for i, (bb, hh) in enumerate(seqs)]
            for i, (bb, hh) in enumerate(seqs):
                s_scr[i] = states[i] * egl_ref[bb, hh, c] + upd[i]
            o = [r[i][chunk:2 * chunk] + _dot(attn_ref[bb, hh, it], v_new[i]) for i, (bb, hh) in enumerate(seqs)]
            for i, (bb, hh) in enumerate(seqs):
                cs = slice(hh * dv, (hh + 1) * dv)
                on = o[i] * lax.rsqrt(jnp.mean(o[i] * o[i], axis=-1, keepdims=True) + EPS) * nw
                o_ref[bb, rs, cs] = (on * f_ref[bb, rs, cs].astype(F32)).astype(o_ref.dtype)
        return carry

    lax.fori_loop(0, nchunks // unroll, chunk_group, 0)


def _gdn_scan(wq, kpt, u, attn, egl, gates, gdn_nw, dims, nchunks, unroll, heads, fcol):
    b, s, h, dk, dv = dims
    chunk = GDN_CHUNK
    n = s // chunk
    rows = nchunks * chunk
    steps = n // nchunks
    width = heads * dv
    assert nchunks % unroll == 0 and unroll == 2

    def in5(r, c):
        return pl.BlockSpec((b, heads, nchunks, r, c), lambda gi, i: (0, gi, i, 0, 0))

    def pair5(r, c):
        return pl.BlockSpec((b, heads, nchunks // 2, r, c), lambda gi, i: (0, gi, i, 0, 0))

    in_specs = [
        in5(2 * chunk, dk), pair5(dk, 2 * chunk), in5(chunk, dv), pair5(chunk, 2 * chunk), in5(1, dv),
        pl.BlockSpec((b, rows, width), lambda gi, i: (0, i, fcol // width + gi)),
        pl.BlockSpec((1, dv), lambda gi, i: (0, 0)),
    ]
    return pl.pallas_call(
        functools.partial(_gdn_scan_kernel, dv=dv, chunk=chunk, nchunks=nchunks, unroll=unroll, heads=heads,
                          batch=b),
        grid=(h // heads, steps),
        in_specs=in_specs,
        out_specs=pl.BlockSpec((b, rows, width), lambda gi, i: (0, i, gi)),
        out_shape=jax.ShapeDtypeStruct((b, s, h * dv), BF16),
        scratch_shapes=[pltpu.VMEM((b * heads, dk, dv), F32)],
        compiler_params=_cparams(("parallel", "arbitrary")),
        name="gdn_scan",
    )(wq, kpt, u, attn, egl, gates, gdn_nw)


def _retention_kernel(q_ref, k_ref, v_ref, f_ref, ma_ref, dmat_ref,
                      xi_ref, zeta_ref, gpow_ref, nw_ref, o_ref, s_scr, *, dk, dv, heads, sub):
    @pl.when(pl.program_id(2) == 0)
    def _():
        s_scr[...] = jnp.zeros_like(s_scr)

    hr = range(heads)
    ksl = [slice(hh * dk, (hh + 1) * dk) for hh in hr]
    vsl = [slice(hh * dv, (hh + 1) * dv) for hh in hr]

    def sub_chunk(it, carry):
        rs = pl.ds(pl.multiple_of(it * sub, sub), sub)
        q = [q_ref[rs, ksl[hh]] for hh in hr]
        k = [k_ref[rs, ksl[hh]] for hh in hr]
        kz = [(k[hh].astype(F32) * zeta_ref[hh]).astype(BF16) for hh in hr]
        states = [s_scr[hh] for hh in hr]
        scores = [(_dot_nt(q[hh], k[hh]) * dmat_ref[hh]).astype(BF16) for hh in hr]
        lhs = [jnp.concatenate([scores[hh], (q[hh].astype(F32) * xi_ref[hh]).astype(BF16)], axis=1) for hh in hr]
        rhs = [jnp.concatenate([v_ref[rs, vsl[hh]], states[hh].astype(BF16)], axis=0) for hh in hr]
        upd = [_dot_tn(kz[hh], v_ref[rs, vsl[hh]]) for hh in hr]
        for hh in hr:
            s_scr[hh] = states[hh] * gpow_ref[hh] + upd[hh]
        o = [_dot(lhs[hh], rhs[hh]) for hh in hr]
        for hh in hr:
            mu = jnp.mean(o[hh], axis=-1, keepdims=True)
            dev = o[hh] - mu
            var = jnp.mean(dev * dev, axis=-1, keepdims=True)
            ob = dev * lax.rsqrt(var + EPS) * nw_ref[:, vsl[hh]]
            mixed = ob * f_ref[rs, vsl[hh]].astype(F32) + ma_ref[rs, vsl[hh]].astype(F32)
            o_ref[rs, vsl[hh]] = mixed.astype(o_ref.dtype)
        return carry

    lax.fori_loop(0, q_ref.shape[0] // sub, sub_chunk, 0)


def _retention(qk, proj, gates, ma, ret_nw, tables, dims, rows, sub, heads, cols):
    b, s, h, dk, dv = dims
    dmat, xi, zeta, gpow = tables
    rqcol, rkcol, rvcol, fcol = cols
    steps = s // rows
    kw, vw = heads * dk, heads * dv
    assert rows % sub == 0

    def rb(bi, i):
        return bi * steps + i

    in_specs = [
        pl.BlockSpec((rows, kw), lambda bi, gi, i: (rb(bi, i), rqcol // kw + gi)),
        pl.BlockSpec((rows, kw), lambda bi, gi, i: (rb(bi, i), rkcol // kw + gi)),
        pl.BlockSpec((rows, vw), lambda bi, gi, i: (rb(bi, i), rvcol // vw + gi)),
        pl.BlockSpec((rows, vw), lambda bi, gi, i: (rb(bi, i), fcol // vw + gi)),
        pl.BlockSpec((rows, vw), lambda bi, gi, i: (rb(bi, i), gi)),
        pl.BlockSpec((heads, sub, sub), lambda bi, gi, i: (gi, 0, 0)),
        pl.BlockSpec((heads, sub, 1), lambda bi, gi, i: (gi, 0, 0)),
        pl.BlockSpec((heads, sub, 1), lambda bi, gi, i: (gi, 0, 0)),
        pl.BlockSpec((heads, 1, dv), lambda bi, gi, i: (gi, 0, 0)),
        pl.BlockSpec((1, vw), lambda bi, gi, i: (0, gi)),
    ]
    return pl.pallas_call(
        functools.partial(_retention_kernel, dk=dk, dv=dv, heads=heads, sub=sub),
        grid=(b, h // heads, steps),
        in_specs=in_specs,
        out_specs=pl.BlockSpec((rows, vw), lambda bi, gi, i: (rb(bi, i), gi)),
        out_shape=jax.ShapeDtypeStruct((b * s, h * dv), BF16),
        scratch_shapes=[pltpu.VMEM((heads, dk, dv), F32)],
        compiler_params=_cparams(("parallel", "parallel", "arbitrary")),
        name="retention",
    )(qk, qk, proj, gates, ma, dmat, xi, zeta, gpow, ret_nw)


def _rotary_tables(s, dk, rows):
    inv = ROPE_BASE ** (-jnp.arange(0, dk, 2, dtype=F32) / dk)
    lo = jnp.arange(rows, dtype=F32)[:, None] * inv[None, :]
    hi = (jnp.arange(s // rows, dtype=F32) * rows)[:, None] * inv[None, :]
    cl, sl, ch, sh = jnp.cos(lo), jnp.sin(lo), jnp.cos(hi)[:, None], jnp.sin(hi)[:, None]
    cos = (ch * cl - sh * sl).reshape(s, dk // 2)
    sin = (sh * cl + ch * sl).reshape(s, dk // 2)
    return jnp.concatenate([cos, cos], axis=-1), jnp.concatenate([-sin, sin], axis=-1)


def _retention_tables(h, dk, dv, rows):
    log_gamma = jnp.log1p(-jnp.exp2(-5.0 - jnp.arange(h, dtype=F32)))
    pos = jnp.arange(rows, dtype=F32)
    dist = pos[:, None] - pos[None, :]
    kscale = dk ** -0.5
    dmat = jnp.exp(jnp.where(dist >= 0, dist * log_gamma[:, None, None], -jnp.inf)) * kscale
    xi = jnp.exp((pos + 1.0) * log_gamma[:, None])[:, :, None]
    zeta = jnp.exp((rows - 1.0 - pos) * log_gamma[:, None])[:, :, None] * kscale
    gpow = jnp.broadcast_to(jnp.exp(rows * log_gamma)[:, None, None], (h, 1, dv))
    return dmat, xi, zeta, gpow


def _outproj_kernel(m_ref, w_ref, x_ref, nw_ref, h_ref, hn_ref):
    hres = x_ref[...] + _dot(m_ref[...], w_ref[...])
    h_ref[...] = hres
    ms = jnp.mean(hres * hres, axis=-1, keepdims=True)
    hn_ref[...] = ((hres * lax.rsqrt(ms + EPS)) * nw_ref[...]).astype(hn_ref.dtype)


def _outproj(mixed, w, x2, nw, tm):
    m, d = x2.shape
    kdim = mixed.shape[1]
    return pl.pallas_call(
        _outproj_kernel,
        grid=(m // tm,),
        in_specs=[
            pl.BlockSpec((tm, kdim), lambda i: (i, 0)),
            pl.BlockSpec((kdim, d), lambda i: (0, 0)),
            pl.BlockSpec((tm, d), lambda i: (i, 0)),
            pl.BlockSpec((1, d), lambda i: (0, 0)),
        ],
        out_specs=[pl.BlockSpec((tm, d), lambda i: (i, 0))] * 2,
        out_shape=[jax.ShapeDtypeStruct((m, d), F32), jax.ShapeDtypeStruct((m, d), BF16)],
        compiler_params=_cparams(("parallel",)),
        name="outproj",
    )(mixed, w, x2, nw)


def _ffn_kernel(hn_ref, h_ref, wg_ref, wu_ref, wd_ref, nw_ref, o_ref):
    f = pl.program_id(1)

    @pl.when(f == 0)
    def _():
        o_ref[...] = h_ref[...]

    hn = hn_ref[...]
    g = _dot(hn, wg_ref[...])
    up = _dot(hn, wu_ref[...])
    act = (_silu(g) * up).astype(BF16)
    o_ref[...] += _dot(act, wd_ref[...])

    @pl.when(f == pl.num_programs(1) - 1)
    def _():
        hres = o_ref[...]
        ms = jnp.mean(hres * hres, axis=-1, keepdims=True)
        o_ref[...] = (hres * lax.rsqrt(ms + EPS)) * nw_ref[...]


def _ffn(hn, hres, wg, wu, wd, nw, tm, tf):
    m, d = hres.shape
    ff = wg.shape[1]
    return pl.pallas_call(
        _ffn_kernel,
        grid=(m // tm, ff // tf),
        in_specs=[
            pl.BlockSpec((tm, d), lambda i, j: (i, 0)),
            pl.BlockSpec((tm, d), lambda i, j: (i, 0)),
            pl.BlockSpec((d, tf), lambda i, j: (0, j)),
            pl.BlockSpec((d, tf), lambda i, j: (0, j)),
            pl.BlockSpec((tf, d), lambda i, j: (j, 0)),
            pl.BlockSpec((1, d), lambda i, j: (0, 0)),
        ],
        out_specs=pl.BlockSpec((tm, d), lambda i, j: (i, 0)),
        out_shape=jax.ShapeDtypeStruct((m, d), F32),
        compiler_params=_cparams(("parallel", "arbitrary")),
        name="ffn",
    )(hn, hres, wg, wu, wd, nw)


def _pick(total, want):
    t = min(total, want)
    while total % t:
        t //= 2
    return t


def _layer(h2, dims, norm1_w, w_in, conv_w, a_log, dt_bias, gdn_norm_w, ret_norm_w,
           w_out, norm2_w, w_gate, w_up, w_down, out_norm_w):
    b, s, h, dk, dv = dims
    m, d = h2.shape
    qk, vd = h * dk, h * dv
    chunk = GDN_CHUNK
    n = s // chunk

    assert 2 * qk == vd
    tn = vd
    sc0 = 3 * vd
    sc = 2 * h
    w_in_t = w_in.T
    w_ba = w_in_t[sc0:sc0 + sc].astype(BF16)
    tr = _pick(vd, 1024)
    per = vd // tr
    w_proj = _cast_rows(w_in_t, [g * vd + k * tr for g in (0, 1) for k in range(per)]
                        + [sc0 + sc + g * vd + k * tr for g in (0, 1) for k in range(per)], tr, sc)

    proj, u = _inproj(h2, norm1_w.reshape(1, d), w_proj, _pick(m, 1024), tn, (0, 1, 3))
    ret_sub = _pick(s, 256)
    cosf, sins = _rotary_tables(s, dk, ret_sub)
    rot_tm = _pick(s, 1024)
    gtr = _pick(vd, (4 * vd) // (m // rot_tm))
    gper = vd // gtr
    gate_srcs = ([2 * vd + k * gtr for k in range(gper)]
                 + [sc0 + sc + g * vd + k * gtr for g in (2, 3, 4) for k in range(gper)])
    qk_rot, beta, gcum, w_gate_rows = _rotproj(
        u, w_proj, cosf, sins, w_ba, a_log.reshape(1, h), dt_bias.reshape(1, h), w_in_t, gate_srcs, gtr, sc,
        rot_tm, tn, 2, s, dk, chunk, _pick(s, 256))
    gates, (wg_b, wu_b, wd_b, wo_b) = _gates(u, w_gate_rows, _pick(m, 1024), _pick(vd, 1024), vd,
                                             ((0, 2 * vd), (vd, 3 * vd)), [w_gate, w_up, w_down, w_out])

    gc_r = gcum.reshape(b, n // 2, 2 * chunk, h).transpose(0, 3, 1, 2)[:, :, :, None, :]
    wq, kpt, uu, attn, egl = _gdn_prep(proj, conv_w, beta, gcum, gc_r, dims, _pick(s, 512), _pick(h, 8))

    ma = _gdn_scan(wq, kpt, uu, attn, egl, gates.reshape(b, s, 2 * vd), gdn_norm_w.reshape(1, dv), dims,
                   _pick(n, 8), _pick(n, 2), _pick(h, 8), 0).reshape(m, vd)

    tables = _retention_tables(h, dk, dv, ret_sub)
    mixed = _retention(qk_rot, proj, gates, ma, ret_norm_w.reshape(1, vd), tables, dims, _pick(s, 512),
                       ret_sub, _pick(h, 8), (0, qk, 2 * tn, vd))

    hres, hn = _outproj(mixed, wo_b, h2, norm2_w.reshape(1, d), _pick(m, 512))

    return _ffn(hn, hres, wg_b, wu_b, wd_b, out_norm_w.reshape(1, d), _pick(m, 512), _pick(w_gate.shape[1], 512))


def kernel(x, norm1_w, w_in, conv_w, a_log, dt_bias, gdn_norm_w, ret_norm_w, w_out, norm2_w,
           w_gate, w_up, w_down, norm_f_w):
    b, s, d = x.shape
    depth, h = a_log.shape
    assert depth == 1, "the final norm is fused into the single layer's FFN kernel"
    dk = d // 16
    dv = d // h
    dims = (b, s, h, dk, dv)
    out = _layer(x.reshape(b * s, d), dims, norm1_w[0], w_in[0], conv_w[0], a_log[0], dt_bias[0],
                 gdn_norm_w[0], ret_norm_w[0], w_out[0], norm2_w[0], w_gate[0], w_up[0], w_down[0],
                 norm_f_w)
    return out.reshape(b, s, d)
```

```python
import functools

import jax
import jax.numpy as jnp
import numpy as np
from jax import lax
from jax.experimental import pallas as pl
from jax.experimental.pallas import tpu as pltpu

F32 = jnp.float32
BF16 = jnp.bfloat16

EPS = 1e-6
CONV_WIDTH = 4
GDN_CHUNK = 64
CONV_BLOCK_ROWS = 128
ROPE_BASE = 10000.0
BF16_TILE_ROWS = 16
HALO_ROWS = BF16_TILE_ROWS
VMEM_LIMIT_BYTES = 56 * 1024 * 1024


def _cparams(sem):
    return pltpu.CompilerParams(dimension_semantics=sem, vmem_limit_bytes=VMEM_LIMIT_BYTES)


def _sigmoid(x):
    return 1.0 / (1.0 + jnp.exp(-x))


def _silu(x):
    h = 0.5 * x
    return h + h * jnp.tanh(h)


def _dot(a, b):
    return jnp.dot(a, b, preferred_element_type=F32)


def _dot_nt(a, b):
    return lax.dot_general(a, b, (((1,), (1,)), ((), ())), preferred_element_type=F32)


def _dot_tn(a, b):
    return lax.dot_general(a, b, (((0,), (0,)), ((), ())), preferred_element_type=F32)


def _nth(j, values):
    idx = values[0]
    for p in range(1, len(values)):
        idx = jnp.where(j >= p, values[p], idx)
    return idx


def _row_tile_specs(srcs, tr, d, count, index):
    assert all(src % tr in (0, count) for src in srcs) and tr % count == 0 and count % BF16_TILE_ROWS == 0
    bases = tuple(src // tr for src in srcs)
    nexts = tuple((src // tr + 1) * (tr // count) for src in srcs)
    return [pl.BlockSpec((tr, d), lambda *g: (_nth(index(*g), bases), 0)),
            pl.BlockSpec((count, d), lambda *g: (_nth(index(*g), nexts), 0))]


def _cast_row_tile(cur_ref, nxt_ref, shifted, count):
    cur = cur_ref[...]
    moved = jnp.concatenate([cur[count:], nxt_ref[...]], axis=0)
    return jnp.where(shifted, moved, cur).astype(BF16)


def _cast_rows_kernel(cur_ref, nxt_ref, o_ref, *, flags, count):
    o_ref[...] = _cast_row_tile(cur_ref, nxt_ref, _nth(pl.program_id(0), flags) > 0, count)


def _cast_rows(wt, srcs, tr, count):
    d = wt.shape[1]
    flags = tuple(int(src % tr != 0) for src in srcs)
    return pl.pallas_call(
        functools.partial(_cast_rows_kernel, flags=flags, count=count),
        grid=(len(srcs),),
        in_specs=_row_tile_specs(srcs, tr, d, count, lambda j: j),
        out_specs=pl.BlockSpec((tr, d), lambda j: (j, 0)),
        out_shape=jax.ShapeDtypeStruct((len(srcs) * tr, d), BF16),
        compiler_params=_cparams(("parallel",)),
        name="cast_rows",
    )(wt, wt)


def _inproj_kernel(x_ref, nw_ref, w_ref, o_ref, u_ref):
    @pl.when(pl.program_id(1) == 0)
    def _():
        xf = x_ref[...]
        ms = jnp.mean(xf * xf, axis=-1, keepdims=True)
        u_ref[...] = ((xf * lax.rsqrt(ms + EPS)) * nw_ref[...]).astype(u_ref.dtype)

    o_ref[...] = _dot_nt(u_ref[...], w_ref[...]).astype(o_ref.dtype)


def _inproj(x2, nw, w, tm, tn, tiles):
    m, d = x2.shape
    return pl.pallas_call(
        _inproj_kernel,
        grid=(m // tm, len(tiles)),
        in_specs=[
            pl.BlockSpec((tm, d), lambda i, j: (i, 0)),
            pl.BlockSpec((1, d), lambda i, j: (0, 0)),
            pl.BlockSpec((tn, d), lambda i, j: (_nth(j, tiles), 0)),
        ],
        out_specs=[
            pl.BlockSpec((tm, tn), lambda i, j: (i, j)),
            pl.BlockSpec((tm, d), lambda i, j: (i, 0)),
        ],
        out_shape=[jax.ShapeDtypeStruct((m, len(tiles) * tn), BF16), jax.ShapeDtypeStruct((m, d), BF16)],
        compiler_params=_cparams(("parallel", "arbitrary")),
        name="inproj",
    )(x2, nw, w)


def _gate_scalars(u_ref, wba_ref, alog_ref, dtb_ref, beta_ref, gc_ref, chunk, sub):
    heads = alog_ref.shape[1]
    parts = [slice(p * sub, (p + 1) * sub) for p in range(u_ref.shape[0] // sub)]
    row = lax.broadcasted_iota(jnp.int32, (sub, sub), 0)
    col = lax.broadcasted_iota(jnp.int32, (sub, sub), 1)
    incl = jnp.where(((row // chunk) == (col // chunk)) & (row >= col), 1.0, 0.0).astype(BF16)
    bas = [_dot_nt(u_ref[sl, :], wba_ref[...]) for sl in parts]
    gs = []
    for sl, ba in zip(parts, bas):
        beta_ref[sl, :] = _sigmoid(ba[:, 0:heads])
        a = ba[:, heads:2 * heads] + dtb_ref[...]
        softplus = jnp.maximum(a, 0.0) + jnp.log1p(jnp.exp(-jnp.abs(a)))
        gs.append(-jnp.exp(alog_ref[...]) * softplus)
    g1 = [g.astype(BF16) for g in gs]
    r1 = [g - p1.astype(F32) for g, p1 in zip(gs, g1)]
    g2 = [r.astype(BF16) for r in r1]
    g3 = [(r - p2.astype(F32)).astype(BF16) for r, p2 in zip(r1, g2)]
    gcs = [_dot(incl, p1) + _dot(incl, p2) + _dot(incl, p3) for p1, p2, p3 in zip(g1, g2, g3)]
    for sl, gc in zip(parts, gcs):
        gc_ref[sl, :] = gc


def _rotproj_kernel(u_ref, w_ref, cos_ref, sin_ref, wba_ref, alog_ref, dtb_ref, *rest,
                    dk, chunk, sub, cast_flags, cast_count, per_step):
    cast_in, (o_ref, beta_ref, gc_ref, cast_ref) = rest[:2 * per_step], rest[2 * per_step:]
    cast_tr = cast_ref.shape[0] // per_step
    res = _dot_nt(u_ref[...], w_ref[...])
    cos = cos_ref[...]
    sin = sin_ref[...]
    for hh in range(res.shape[1] // dk):
        sl = slice(hh * dk, (hh + 1) * dk)
        x = res[:, sl]
        o_ref[:, sl] = (x * cos + pltpu.roll(x, dk // 2, 1) * sin).astype(o_ref.dtype)
    _gate_scalars(u_ref, wba_ref, alog_ref, dtb_ref, beta_ref, gc_ref, chunk, sub)
    for kk in range(per_step):
        shifted = _nth(pl.program_id(0) * per_step + kk, cast_flags) > 0
        cast_ref[kk * cast_tr:(kk + 1) * cast_tr, :] = _cast_row_tile(cast_in[2 * kk], cast_in[2 * kk + 1],
                                                                     shifted, cast_count)


def _rotproj(u, w, cosf, sins, wba, alog, dtb, wt, cast_srcs, cast_tr, cast_count,
             tm, tn, tile, seq, dk, chunk, sub):
    m, d = u.shape
    h = alog.shape[1]
    steps = m // tm
    assert seq % tm == 0 and tm % sub == 0 and sub % chunk == 0 and len(cast_srcs) % steps == 0
    per_seq = seq // tm
    per_step = len(cast_srcs) // steps
    scalar_out = jax.ShapeDtypeStruct((m, h), F32)
    cast_flags = tuple(int(src % cast_tr != 0) for src in cast_srcs)
    cast_in_specs = []
    for kk in range(per_step):
        cast_in_specs += _row_tile_specs(cast_srcs, cast_tr, wt.shape[1], cast_count,
                                         lambda i, kk=kk: i * per_step + kk)
    return pl.pallas_call(
        functools.partial(_rotproj_kernel, dk=dk, chunk=chunk, sub=sub, cast_flags=cast_flags,
                          cast_count=cast_count, per_step=per_step),
        grid=(steps,),
        in_specs=[
            pl.BlockSpec((tm, d), lambda i: (i, 0)),
            pl.BlockSpec((tn, d), lambda i: (tile, 0)),
            pl.BlockSpec((tm, dk), lambda i: (i % per_seq, 0)),
            pl.BlockSpec((tm, dk), lambda i: (i % per_seq, 0)),
            pl.BlockSpec((2 * h, d), lambda i: (0, 0)),
            pl.BlockSpec((1, h), lambda i: (0, 0)),
            pl.BlockSpec((1, h), lambda i: (0, 0)),
        ] + cast_in_specs,
        out_specs=[pl.BlockSpec((tm, tn), lambda i: (i, 0))] + [pl.BlockSpec((tm, h), lambda i: (i, 0))] * 2
        + [pl.BlockSpec((per_step * cast_tr, wt.shape[1]), lambda i: (i, 0))],
        out_shape=[jax.ShapeDtypeStruct((m, tn), BF16), scalar_out, scalar_out,
                   jax.ShapeDtypeStruct((len(cast_srcs) * cast_tr, wt.shape[1]), BF16)],
        compiler_params=_cparams(("arbitrary",)),
        name="rotproj",
    )(u, w, cosf, sins, wba, alog, dtb, *([wt, wt] * per_step))


def _gates_kernel(u_ref, ws_ref, wg_ref, *rest):
    ncast = (len(rest) - 1) // 2
    cast_in, o_ref, cast_out = rest[:ncast], rest[ncast], rest[ncast + 1:]
    u = u_ref[...]
    a = _dot_nt(u, ws_ref[...])
    g = _dot_nt(u, wg_ref[...])
    o_ref[...] = (_silu(a) * (0.5 + 0.5 * jnp.tanh(0.5 * g))).astype(o_ref.dtype)
    for src, dst in zip(cast_in, cast_out):
        dst[...] = src[...].astype(dst.dtype)


def _cast_slabs(rows, nsteps):
    nb = nsteps
    while nb > 1 and (rows % nb or (rows // nb) % BF16_TILE_ROWS):
        nb //= 2
    return nb


def _gates(u, w, tm, tg, vd, pairs, to_cast):
    m, d = u.shape
    per = vd // tg
    nj = len(pairs) * per
    nsteps = (m // tm) * nj

    def blk(j, which):
        idx = pairs[0][which] // tg + j
        for p in range(1, len(pairs)):
            idx = jnp.where(j >= p * per, pairs[p][which] // tg + j - p * per, idx)
        return idx

    cast_specs = []
    for mat in to_cast:
        nb = _cast_slabs(mat.shape[0], nsteps)
        rep = nsteps // nb
        cast_specs.append(pl.BlockSpec((mat.shape[0] // nb, mat.shape[1]),
                                       lambda i, j, rep=rep: ((i * nj + j) // rep, 0)))

    outs = pl.pallas_call(
        _gates_kernel,
        grid=(m // tm, nj),
        in_specs=[
            pl.BlockSpec((tm, d), lambda i, j: (i, 0)),
            pl.BlockSpec((tg, d), lambda i, j: (blk(j, 0), 0)),
            pl.BlockSpec((tg, d), lambda i, j: (blk(j, 1), 0)),
        ] + cast_specs,
        out_specs=[pl.BlockSpec((tm, tg), lambda i, j: (i, j))] + cast_specs,
        out_shape=[jax.ShapeDtypeStruct((m, len(pairs) * vd), BF16)]
        + [jax.ShapeDtypeStruct(mat.shape, BF16) for mat in to_cast],
        compiler_params=_cparams(("arbitrary", "arbitrary")),
        name="gates",
    )(u, w, w, *to_cast)
    return outs[0], outs[1:]


def _gdn_prep_kernel(q_ref, qh_ref, k_ref, kh_ref, v_ref, vh_ref, cwq_ref, cwk_ref, cwv_ref,
                     beta_ref, gcc_ref, gcr_ref, pick_ref,
                     wq_ref, kpt_ref, u_ref, attn_ref, egl_ref, *, dk, dv, chunk, nchunks, heads):
    t = pl.program_id(1)
    rows = chunk * nchunks
    lane_head = lax.broadcasted_iota(jnp.int32, (1, beta_ref.shape[1]), 1)

    def head_col(ref, hh):
        hsel = lane_head == pl.program_id(2) * heads + hh
        return jnp.sum(jnp.where(hsel, ref[...], 0.0), axis=1, keepdims=True)

    pick = pick_ref[...]
    blk = pick.shape[0]
    win = HALO_ROWS + blk

    def conv_silu(x_ref, h_ref, cw_ref):
        halo = jnp.where(t == 0, jnp.zeros_like(h_ref[...]), h_ref[...])
        ext = jnp.concatenate([halo, x_ref[...]], axis=0)
        taps = [ext * cw_ref[tap:tap + 1, :].astype(BF16) for tap in range(CONV_WIDTH)]
        out = []
        for r in range(rows // blk):
            stacked = jnp.concatenate([tp[r * blk:r * blk + win] for tp in taps], axis=0)
            out.append(_silu(_dot(pick, stacked)))
        return jnp.concatenate(out, axis=0)

    def l2norm(y):
        return y * lax.rsqrt(jnp.sum(y * y, axis=-1, keepdims=True) + EPS)

    q_all = conv_silu(q_ref, qh_ref, cwq_ref)
    k_all = conv_silu(k_ref, kh_ref, cwk_ref)
    v_all = conv_silu(v_ref, vh_ref, cwv_ref)

    ri = lax.broadcasted_iota(jnp.int32, (chunk, 2 * chunk), 0)
    li = lax.broadcasted_iota(jnp.int32, (chunk, 2 * chunk), 1)
    first = li < chunk
    ci = jnp.where(first, li, li - chunk)
    incl = ri >= ci
    strict = ri > ci
    eye = jnp.where(ri == ci, 1.0, 0.0).astype(F32)
    ident_k = jnp.where(lax.broadcasted_iota(jnp.int32, (dk, dk), 0)
                        == lax.broadcasted_iota(jnp.int32, (dk, dk), 1), 1.0, 0.0).astype(BF16)

    def blockdiag(x):
        return jnp.concatenate([jnp.where(first, x, 0.0), jnp.where(first, 0.0, x)], axis=0).astype(BF16)

    def pair_rows(xa, xb):
        z = jnp.zeros_like(xa)
        return jnp.concatenate([jnp.concatenate([xa, z], axis=1), jnp.concatenate([z, xb], axis=1)], axis=0)

    npairs = nchunks // 2
    units = [(hh, p) for hh in range(heads) for p in range(npairs)]
    ur = range(len(units))
    qs, ks, vs, bs, gcs = [], [], [], [], []
    for hh in range(heads):
        q = l2norm(q_all[:, hh * dk:(hh + 1) * dk]) * (dk ** -0.5)
        k = l2norm(k_all[:, hh * dk:(hh + 1) * dk])
        v = v_all[:, hh * dv:(hh + 1) * dv]
        beta, gcum = head_col(beta_ref, hh), head_col(gcc_ref, hh)
        for c in range(nchunks):
            sl = slice(c * chunk, (c + 1) * chunk)
            qs.append(q[sl])
            ks.append(k[sl])
            vs.append(v[sl])
            bs.append(beta[sl])
            gcs.append(gcum[sl])
    nc = range(len(qs))
    gls = [gc[chunk - 1:chunk, :] for gc in gcs]
    egs = [jnp.exp(gc) for gc in gcs]
    kbs = [ks[c] * bs[c] for c in nc]
    decays = [jnp.exp(jnp.where(incl, jnp.where(first, gcs[2 * u], gcs[2 * u + 1]) - gcr_ref[0, hh, p], -1e30))
              for u, (hh, p) in enumerate(units)]
    lhs = [jnp.concatenate([jnp.concatenate([kbs[2 * u], kbs[2 * u + 1]], axis=1),
                            jnp.concatenate([qs[2 * u], qs[2 * u + 1]], axis=1)], axis=0).astype(BF16) for u in ur]
    kq = [_dot_nt(lhs[u], pair_rows(ks[2 * u], ks[2 * u + 1]).astype(BF16)) for u in ur]
    amats = [jnp.where(strict, kq[u][0:chunk] * decays[u], 0.0) for u in ur]
    for u, (hh, p) in enumerate(units):
        attn_ref[0, hh, p] = (kq[u][chunk:2 * chunk] * decays[u]).astype(attn_ref.dtype)
    tinvs = [eye - a for a in amats]
    apows = amats
    span = 2
    while span < chunk:
        apows = [_dot(apows[u].astype(BF16), blockdiag(apows[u])) for u in ur]
        tinvs = [tinvs[u] + _dot(tinvs[u].astype(BF16), blockdiag(apows[u])) for u in ur]
        span *= 2
    rhs = [jnp.concatenate([jnp.concatenate([kbs[c] * egs[c], vs[c] * bs[c]], axis=1)
                            for c in (2 * u, 2 * u + 1)], axis=0).astype(BF16) for u in ur]
    wu = [_dot(blockdiag(tinvs[u]), rhs[u]) for u in ur]
    kps = [jnp.concatenate([ks[c] * jnp.exp(gls[c] - gcs[c]) for c in (2 * u, 2 * u + 1)], axis=0).astype(BF16)
           for u in ur]
    kpt = [_dot_nt(ident_k, kps[u]) for u in ur]
    for u, (hh, p) in enumerate(units):
        kpt_ref[0, hh, p] = kpt[u].astype(kpt_ref.dtype)
        for half in range(2):
            c = 2 * u + half
            rs = slice(half * chunk, (half + 1) * chunk)
            wq_ref[0, hh, 2 * p + half, 0:chunk, :] = wu[u][rs, 0:dk].astype(wq_ref.dtype)
            wq_ref[0, hh, 2 * p + half, chunk:2 * chunk, :] = (qs[c] * egs[c]).astype(wq_ref.dtype)
            u_ref[0, hh, 2 * p + half] = wu[u][rs, dk:dk + dv].astype(u_ref.dtype)
            egl_ref[0, hh, 2 * p + half] = jnp.broadcast_to(jnp.exp(gls[c]), (1, dv))


def _gdn_prep(proj, conv_w, beta_c, gc_c, gc_r, dims, rows, heads):
    b, s, h, dk, dv = dims
    chunk = GDN_CHUNK
    nchunks = rows // chunk
    n = s // chunk
    qk = h * dk
    tiles = s // rows
    kw, vw = heads * dk, heads * dv
    kcol = qk // kw
    vcol = 2 * qk // vw

    blk = _pick(rows, CONV_BLOCK_ROWS)
    win = HALO_ROWS + blk
    out_row = np.arange(blk)[:, None]
    col = np.arange(CONV_WIDTH * win)[None, :]
    pick = jnp.asarray(col % win == HALO_ROWS + out_row - (CONV_WIDTH - 1 - col // win), dtype=BF16)

    def rowblk(bi, ti):
        return bi * tiles + ti

    def halo(bi, ti):
        return jnp.maximum((bi * s + ti * rows) // HALO_ROWS - 1, 0)

    col_spec = pl.BlockSpec((rows, h), lambda bi, ti, hi: (rowblk(bi, ti), 0))
    in_specs = [
        pl.BlockSpec((rows, kw), lambda bi, ti, hi: (rowblk(bi, ti), hi)),
        pl.BlockSpec((HALO_ROWS, kw), lambda bi, ti, hi: (halo(bi, ti), hi)),
        pl.BlockSpec((rows, kw), lambda bi, ti, hi: (rowblk(bi, ti), kcol + hi)),
        pl.BlockSpec((HALO_ROWS, kw), lambda bi, ti, hi: (halo(bi, ti), kcol + hi)),
        pl.BlockSpec((rows, vw), lambda bi, ti, hi: (rowblk(bi, ti), vcol + hi)),
        pl.BlockSpec((HALO_ROWS, vw), lambda bi, ti, hi: (halo(bi, ti), vcol + hi)),
        pl.BlockSpec((CONV_WIDTH, kw), lambda bi, ti, hi: (0, hi)),
        pl.BlockSpec((CONV_WIDTH, kw), lambda bi, ti, hi: (0, kcol + hi)),
        pl.BlockSpec((CONV_WIDTH, vw), lambda bi, ti, hi: (0, vcol + hi)),
        col_spec, col_spec,
        pl.BlockSpec((1, heads, nchunks // 2, 1, 2 * chunk), lambda bi, ti, hi: (bi, hi, ti, 0, 0)),
        pl.BlockSpec(pick.shape, lambda bi, ti, hi: (0, 0)),
    ]

    def out5(r, c):
        return pl.BlockSpec((1, heads, nchunks, r, c), lambda bi, ti, hi: (bi, hi, ti, 0, 0))

    def pair5(r, c):
        return pl.BlockSpec((1, heads, nchunks // 2, r, c), lambda bi, ti, hi: (bi, hi, ti, 0, 0))

    out_specs = [out5(2 * chunk, dk), pair5(dk, 2 * chunk), out5(chunk, dv), pair5(chunk, 2 * chunk), out5(1, dv)]
    out_shape = [
        jax.ShapeDtypeStruct((b, h, n, 2 * chunk, dk), BF16),
        jax.ShapeDtypeStruct((b, h, n // 2, dk, 2 * chunk), BF16),
        jax.ShapeDtypeStruct((b, h, n, chunk, dv), BF16),
        jax.ShapeDtypeStruct((b, h, n // 2, chunk, 2 * chunk), BF16),
        jax.ShapeDtypeStruct((b, h, n, 1, dv), F32),
    ]
    return pl.pallas_call(
        functools.partial(_gdn_prep_kernel, dk=dk, dv=dv, chunk=chunk, nchunks=nchunks, heads=heads),
        grid=(b, tiles, h // heads),
        in_specs=in_specs,
        out_specs=out_specs,
        out_shape=out_shape,
        compiler_params=_cparams(("parallel", "parallel", "parallel")),
        name="gdn_prep",
    )(proj, proj, proj, proj, proj, proj, conv_w, conv_w, conv_w, beta_c, gc_c, gc_r, pick)


def _pair_pad(x, half):
    z = jnp.zeros_like(x)
    return jnp.concatenate([x, z] if half == 0 else [z, x], axis=0)


def _gdn_scan_kernel(wq_ref, kpt_ref, u_ref, attn_ref, egl_ref, f_ref, nw_ref, o_ref, s_scr,
                     *, dv, chunk, nchunks, unroll, heads, batch):
    @pl.when(pl.program_id(1) == 0)
    def _():
        s_scr[...] = jnp.zeros_like(s_scr)

    nw = nw_ref[...]
    seqs = [(bb, hh) for bb in range(batch) for hh in range(heads)]
    sr = range(len(seqs))

    def chunk_group(it, carry):
        for cc in range(unroll):
            c = it * unroll + cc
            rs = pl.ds(pl.multiple_of(c * chunk, chunk), chunk)
            states = [s_scr[i] for i in sr]
            r = [_dot(wq_ref[bb, hh, c], states[i].astype(BF16)) for i, (bb, hh) in enumerate(seqs)]
            v_new = [_pair_pad((u_ref[bb, hh, c] - r[i][0:chunk]).astype(BF16), cc)
                     for i, (bb, hh) in enumerate(seqs)]
            upd = [_dot(kpt_ref[bb, hh, it], v_new[i]) for i, (bb, hh) in enumerate(seqs)]
            for i, (bb, hh) in enumerate(seqs):
                s_scr[i] = states[i] * egl_ref[bb, hh, c] + upd[i]
            o = [r[i][chunk:2 * chunk] + _dot(attn_ref[bb, hh, it], v_new[i]) for i, (bb, hh) in enumerate(seqs)]
            for i, (bb, hh) in enumerate(seqs):
                cs = slice(hh * dv, (hh + 1) * dv)
                on = o[i] * lax.rsqrt(jnp.mean(o[i] * o[i], axis=-1, keepdims=True) + EPS) * nw
                o_ref[bb, rs, cs] = (on * f_ref[bb, rs, cs].astype(F32)).astype(o_ref.dtype)
        return carry

    lax.fori_loop(0, nchunks // unroll, chunk_group, 0)


def _gdn_scan(wq, kpt, u, attn, egl, gates, gdn_nw, dims, nchunks, unroll, heads, fcol):
    b, s, h, dk, dv = dims
    chunk = GDN_CHUNK
    n = s // chunk
    rows = nchunks * chunk
    steps = n // nchunks
    width = heads * dv
    assert nchunks % unroll == 0 and unroll == 2

    def in5(r, c):
        return pl.BlockSpec((b, heads, nchunks, r, c), lambda gi, i: (0, gi, i, 0, 0))

    def pair5(r, c):
        return pl.BlockSpec((b, heads, nchunks // 2, r, c), lambda gi, i: (0, gi, i, 0, 0))

    in_specs = [
        in5(2 * chunk, dk), pair5(dk, 2 * chunk), in5(chunk, dv), pair5(chunk, 2 * chunk), in5(1, dv),
        pl.BlockSpec((b, rows, width), lambda gi, i: (0, i, fcol // width + gi)),
        pl.BlockSpec((1, dv), lambda gi, i: (0, 0)),
    ]
    return pl.pallas_call(
        functools.partial(_gdn_scan_kernel, dv=dv, chunk=chunk, nchunks=nchunks, unroll=unroll, heads=heads,
                          batch=b),
        grid=(h // heads, steps),
        in_specs=in_specs,
        out_specs=pl.BlockSpec((b, rows, width), lambda gi, i: (0, i, gi)),
        out_shape=jax.ShapeDtypeStruct((b, s, h * dv), BF16),
        scratch_shapes=[pltpu.VMEM((b * heads, dk, dv), F32)],
        compiler_params=_cparams(("parallel", "arbitrary")),
        name="gdn_scan",
    )(wq, kpt, u, attn, egl, gates, gdn_nw)


def _retention_kernel(q_ref, k_ref, v_ref, f_ref, ma_ref, dmat_ref,
                      xi_ref, zeta_ref, gpow_ref, nw_ref, o_ref, s_scr, *, dk, dv, heads, sub):
    @pl.when(pl.program_id(2) == 0)
    def _():
        s_scr[...] = jnp.zeros_like(s_scr)

    hr = range(heads)
    ksl = [slice(hh * dk, (hh + 1) * dk) for hh in hr]
    vsl = [slice(hh * dv, (hh + 1) * dv) for hh in hr]

    def sub_chunk(it, carry):
        rs = pl.ds(pl.multiple_of(it * sub, sub), sub)
        q = [q_ref[rs, ksl[hh]] for hh in hr]
        k = [k_ref[rs, ksl[hh]] for hh in hr]
        kz = [(k[hh].astype(F32) * zeta_ref[hh]).astype(BF16) for hh in hr]
        states = [s_scr[hh] for hh in hr]
        scores = [(_dot_nt(q[hh], k[hh]) * dmat_ref[hh]).astype(BF16) for hh in hr]
        lhs = [jnp.concatenate([scores[hh], (q[hh].astype(F32) * xi_ref[hh]).astype(BF16)], axis=1) for hh in hr]
        rhs = [jnp.concatenate([v_ref[rs, vsl[hh]], states[hh].astype(BF16)], axis=0) for hh in hr]
        upd = [_dot_tn(kz[hh], v_ref[rs, vsl[hh]]) for hh in hr]
        for hh in hr:
            s_scr[hh] = states[hh] * gpow_ref[hh] + upd[hh]
        o = [_dot(lhs[hh], rhs[hh]) for hh in hr]
        for hh in hr:
            mu = jnp.mean(o[hh], axis=-1, keepdims=True)
            dev = o[hh] - mu
            var = jnp.mean(dev * dev, axis=-1, keepdims=True)
            ob = dev * lax.rsqrt(var + EPS) * nw_ref[:, vsl[hh]]
            mixed = ob * f_ref[rs, vsl[hh]].astype(F32) + ma_ref[rs, vsl[hh]].astype(F32)
            o_ref[rs, vsl[hh]] = mixed.astype(o_ref.dtype)
        return carry

    lax.fori_loop(0, q_ref.shape[0] // sub, sub_chunk, 0)


def _retention(qk, proj, gates, ma, ret_nw, tables, dims, rows, sub, heads, cols):
    b, s, h, dk, dv = dims
    dmat, xi, zeta, gpow = tables
    rqcol, rkcol, rvcol, fcol = cols
    steps = s // rows
    kw, vw = heads * dk, heads * dv
    assert rows % sub == 0

    def rb(bi, i):
        return bi * steps + i

    in_specs = [
        pl.BlockSpec((rows, kw), lambda bi, gi, i: (rb(bi, i), rqcol // kw + gi)),
        pl.BlockSpec((rows, kw), lambda bi, gi, i: (rb(bi, i), rkcol // kw + gi)),
        pl.BlockSpec((rows, vw), lambda bi, gi, i: (rb(bi, i), rvcol // vw + gi)),
        pl.BlockSpec((rows, vw), lambda bi, gi, i: (rb(bi, i), fcol // vw + gi)),
        pl.BlockSpec((rows, vw), lambda bi, gi, i: (rb(bi, i), gi)),
        pl.BlockSpec((heads, sub, sub), lambda bi, gi, i: (gi, 0, 0)),
        pl.BlockSpec((heads, sub, 1), lambda bi, gi, i: (gi, 0, 0)),
        pl.BlockSpec((heads, sub, 1), lambda bi, gi, i: (gi, 0, 0)),
        pl.BlockSpec((heads, 1, dv), lambda bi, gi, i: (gi, 0, 0)),
        pl.BlockSpec((1, vw), lambda bi, gi, i: (0, gi)),
    ]
    return pl.pallas_call(
        functools.partial(_retention_kernel, dk=dk, dv=dv, heads=heads, sub=sub),
        grid=(b, h // heads, steps),
        in_specs=in_specs,
        out_specs=pl.BlockSpec((rows, vw), lambda bi, gi, i: (rb(bi, i), gi)),
        out_shape=jax.ShapeDtypeStruct((b * s, h * dv), BF16),
        scratch_shapes=[pltpu.VMEM((heads, dk, dv), F32)],
        compiler_params=_cparams(("parallel", "parallel", "arbitrary")),
        name="retention",
    )(qk, qk, proj, gates, ma, dmat, xi, zeta, gpow, ret_nw)


def _rotary_tables(s, dk, rows):
    inv = ROPE_BASE ** (-jnp.arange(0, dk, 2, dtype=F32) / dk)
    inv = jnp.concatenate([inv, inv])
    sign = jnp.concatenate([-jnp.ones(dk // 2, F32), jnp.ones(dk // 2, F32)])
    lo = jnp.arange(rows, dtype=F32)[:, None] * inv[None, :]
    hi = (jnp.arange(s // rows, dtype=F32) * rows)[:, None] * inv[None, :]
    cl, sl, ch, sh = jnp.cos(lo), jnp.sin(lo), jnp.cos(hi)[:, None], jnp.sin(hi)[:, None]
    cos = (ch * cl - sh * sl).reshape(s, dk)
    sin = ((sh * cl + ch * sl) * sign).reshape(s, dk)
    return cos, sin


def _retention_tables(h, dk, dv, rows):
    log_gamma = jnp.log1p(-jnp.exp2(-5.0 - jnp.arange(h, dtype=F32)))
    pos = jnp.arange(rows, dtype=F32)
    dist = pos[:, None] - pos[None, :]
    kscale = dk ** -0.5
    dmat = jnp.exp(jnp.where(dist >= 0, dist * log_gamma[:, None, None], -jnp.inf)) * kscale
    xi = jnp.exp((pos + 1.0) * log_gamma[:, None])[:, :, None]
    zeta = jnp.exp((rows - 1.0 - pos) * log_gamma[:, None])[:, :, None] * kscale
    gpow = jnp.broadcast_to(jnp.exp(rows * log_gamma)[:, None, None], (h, 1, dv))
    return dmat, xi, zeta, gpow


def _outproj_kernel(m_ref, w_ref, x_ref, nw_ref, h_ref, hn_ref):
    hres = x_ref[...] + _dot(m_ref[...], w_ref[...])
    h_ref[...] = hres
    ms = jnp.mean(hres * hres, axis=-1, keepdims=True)
    hn_ref[...] = ((hres * lax.rsqrt(ms + EPS)) * nw_ref[...]).astype(hn_ref.dtype)


def _outproj(mixed, w, x2, nw, tm):
    m, d = x2.shape
    kdim = mixed.shape[1]
    return pl.pallas_call(
        _outproj_kernel,
        grid=(m // tm,),
        in_specs=[
            pl.BlockSpec((tm, kdim), lambda i: (i, 0)),
            pl.BlockSpec((kdim, d), lambda i: (0, 0)),
            pl.BlockSpec((tm, d), lambda i: (i, 0)),
            pl.BlockSpec((1, d), lambda i: (0, 0)),
        ],
        out_specs=[pl.BlockSpec((tm, d), lambda i: (i, 0))] * 2,
        out_shape=[jax.ShapeDtypeStruct((m, d), F32), jax.ShapeDtypeStruct((m, d), BF16)],
        compiler_params=_cparams(("parallel",)),
        name="outproj",
    )(mixed, w, x2, nw)


def _ffn_kernel(hn_ref, h_ref, wg_ref, wu_ref, wd_ref, nw_ref, o_ref):
    f = pl.program_id(1)

    @pl.when(f == 0)
    def _():
        o_ref[...] = h_ref[...]

    hn = hn_ref[...]
    g = _dot(hn, wg_ref[...])
    up = _dot(hn, wu_ref[...])
    act = (_silu(g) * up).astype(BF16)
    o_ref[...] += _dot(act, wd_ref[...])

    @pl.when(f == pl.num_programs(1) - 1)
    def _():
        hres = o_ref[...]
        ms = jnp.mean(hres * hres, axis=-1, keepdims=True)
        o_ref[...] = (hres * lax.rsqrt(ms + EPS)) * nw_ref[...]


def _ffn(hn, hres, wg, wu, wd, nw, tm, tf):
    m, d = hres.shape
    ff = wg.shape[1]
    return pl.pallas_call(
        _ffn_kernel,
        grid=(m // tm, ff // tf),
        in_specs=[
            pl.BlockSpec((tm, d), lambda i, j: (i, 0)),
            pl.BlockSpec((tm, d), lambda i, j: (i, 0)),
            pl.BlockSpec((d, tf), lambda i, j: (0, j)),
            pl.BlockSpec((d, tf), lambda i, j: (0, j)),
            pl.BlockSpec((tf, d), lambda i, j: (j, 0)),
            pl.BlockSpec((1, d), lambda i, j: (0, 0)),
        ],
        out_specs=pl.BlockSpec((tm, d), lambda i, j: (i, 0)),
        out_shape=jax.ShapeDtypeStruct((m, d), F32),
        compiler_params=_cparams(("parallel", "arbitrary")),
        name="ffn",
    )(hn, hres, wg, wu, wd, nw)


def _pick(total, want):
    t = min(total, want)
    while total % t:
        t //= 2
    return t


def _layer(h2, dims, norm1_w, w_in, conv_w, a_log, dt_bias, gdn_norm_w, ret_norm_w,
           w_out, norm2_w, w_gate, w_up, w_down, out_norm_w):
    b, s, h, dk, dv = dims
    m, d = h2.shape
    qk, vd = h * dk, h * dv
    chunk = GDN_CHUNK
    n = s // chunk

    assert 2 * qk == vd
    tn = vd
    sc0 = 3 * vd
    sc = 2 * h
    w_in_t = w_in.T
    w_ba = w_in_t[sc0:sc0 + sc].astype(BF16)
    tr = _pick(vd, 1024)
    per = vd // tr
    w_proj = _cast_rows(w_in_t, [g * vd + k * tr for g in (0, 1) for k in range(per)]
                        + [sc0 + sc + g * vd + k * tr for g in (0, 1) for k in range(per)], tr, sc)

    proj, u = _inproj(h2, norm1_w.reshape(1, d), w_proj, _pick(m, 1024), tn, (0, 1, 3))
    ret_sub = _pick(s, 256)
    cosf, sins = _rotary_tables(s, dk, ret_sub)
    rot_tm = _pick(s, 1024)
    gtr = _pick(vd, (4 * vd) // (m // rot_tm))
    gper = vd // gtr
    gate_srcs = ([2 * vd + k * gtr for k in range(gper)]
                 + [sc0 + sc + g * vd + k * gtr for g in (2, 3, 4) for k in range(gper)])
    qk_rot, beta, gcum, w_gate_rows = _rotproj(
        u, w_proj, cosf, sins, w_ba, a_log.reshape(1, h), dt_bias.reshape(1, h), w_in_t, gate_srcs, gtr, sc,
        rot_tm, tn, 2, s, dk, chunk, _pick(s, 256))
    gates, (wg_b, wu_b, wd_b, wo_b) = _gates(u, w_gate_rows, _pick(m, 1024), _pick(vd, 1024), vd,
                                             ((0, 2 * vd), (vd, 3 * vd)), [w_gate, w_up, w_down, w_out])

    gc_r = gcum.reshape(b, n // 2, 2 * chunk, h).transpose(0, 3, 1, 2)[:, :, :, None, :]
    wq, kpt, uu, attn, egl = _gdn_prep(proj, conv_w, beta, gcum, gc_r, dims, _pick(s, 512), _pick(h, 8))

    ma = _gdn_scan(wq, kpt, uu, attn, egl, gates.reshape(b, s, 2 * vd), gdn_norm_w.reshape(1, dv), dims,
                   _pick(n, 8), _pick(n, 2), _pick(h, 8), 0).reshape(m, vd)

    tables = _retention_tables(h, dk, dv, ret_sub)
    mixed = _retention(qk_rot, proj, gates, ma, ret_norm_w.reshape(1, vd), tables, dims, _pick(s, 512),
                       ret_sub, _pick(h, 8), (0, qk, 2 * tn, vd))

    hres, hn = _outproj(mixed, wo_b, h2, norm2_w.reshape(1, d), _pick(m, 512))

    return _ffn(hn, hres, wg_b, wu_b, wd_b, out_norm_w.reshape(1, d), _pick(m, 512), _pick(w_gate.shape[1], 512))


def kernel(x, norm1_w, w_in, conv_w, a_log, dt_bias, gdn_norm_w, ret_norm_w, w_out, norm2_w,
           w_gate, w_up, w_down, norm_f_w):
    b, s, d = x.shape
    depth, h = a_log.shape
    assert depth == 1, "the final norm is fused into the single layer's FFN kernel"
    dk = d // 16
    dv = d // h
    dims = (b, s, h, dk, dv)
    out = _layer(x.reshape(b * s, d), dims, norm1_w[0], w_in[0], conv_w[0], a_log[0], dt_bias[0],
                 gdn_norm_w[0], ret_norm_w[0], w_out[0], norm2_w[0], w_gate[0], w_up[0], w_down[0],
                 norm_f_w)
    return out.reshape(b, s, d)
```

```python
import functools

import jax
import jax.numpy as jnp
import numpy as np
from jax import lax
from jax.experimental import pallas as pl
from jax.experimental.pallas import tpu as pltpu

F32 = jnp.float32
BF16 = jnp.bfloat16

EPS = 1e-6
CONV_WIDTH = 4
GDN_CHUNK = 64
CONV_BLOCK_ROWS = 128
ROPE_BASE = 10000.0
BF16_TILE_ROWS = 16
HALO_ROWS = BF16_TILE_ROWS
VMEM_LIMIT_BYTES = 56 * 1024 * 1024


def _cparams(sem):
    return pltpu.CompilerParams(dimension_semantics=sem, vmem_limit_bytes=VMEM_LIMIT_BYTES)


def _sigmoid(x):
    return 1.0 / (1.0 + jnp.exp(-x))


def _silu(x):
    h = 0.5 * x
    return h + h * jnp.tanh(h)


def _dot(a, b):
    return jnp.dot(a, b, preferred_element_type=F32)


def _dot_nt(a, b):
    return lax.dot_general(a, b, (((1,), (1,)), ((), ())), preferred_element_type=F32)


def _dot_tn(a, b):
    return lax.dot_general(a, b, (((0,), (0,)), ((), ())), preferred_element_type=F32)


def _nth(j, values):
    idx = values[0]
    for p in range(1, len(values)):
        idx = jnp.where(j >= p, values[p], idx)
    return idx


def _row_tile_specs(srcs, tr, d, count, index):
    assert all(src % tr in (0, count) for src in srcs) and tr % count == 0 and count % BF16_TILE_ROWS == 0
    bases = tuple(src // tr for src in srcs)
    nexts = tuple((src // tr + 1) * (tr // count) for src in srcs)
    return [pl.BlockSpec((tr, d), lambda *g: (_nth(index(*g), bases), 0)),
            pl.BlockSpec((count, d), lambda *g: (_nth(index(*g), nexts), 0))]


def _cast_row_tile(cur_ref, nxt_ref, shifted, count):
    cur = cur_ref[...]
    moved = jnp.concatenate([cur[count:], nxt_ref[...]], axis=0)
    return jnp.where(shifted, moved, cur).astype(BF16)


def _cast_rows_kernel(cur_ref, nxt_ref, o_ref, *, flags, count):
    o_ref[...] = _cast_row_tile(cur_ref, nxt_ref, _nth(pl.program_id(0), flags) > 0, count)


def _cast_rows(wt, srcs, tr, count):
    d = wt.shape[1]
    flags = tuple(int(src % tr != 0) for src in srcs)
    return pl.pallas_call(
        functools.partial(_cast_rows_kernel, flags=flags, count=count),
        grid=(len(srcs),),
        in_specs=_row_tile_specs(srcs, tr, d, count, lambda j: j),
        out_specs=pl.BlockSpec((tr, d), lambda j: (j, 0)),
        out_shape=jax.ShapeDtypeStruct((len(srcs) * tr, d), BF16),
        compiler_params=_cparams(("parallel",)),
        name="cast_rows",
    )(wt, wt)


def _inproj_kernel(x_ref, nw_ref, w_ref, o_ref, u_ref):
    @pl.when(pl.program_id(1) == 0)
    def _():
        xf = x_ref[...]
        ms = jnp.mean(xf * xf, axis=-1, keepdims=True)
        u_ref[...] = ((xf * lax.rsqrt(ms + EPS)) * nw_ref[...]).astype(u_ref.dtype)

    o_ref[...] = _dot_nt(u_ref[...], w_ref[...]).astype(o_ref.dtype)


def _inproj(x2, nw, w, tm, tn, tiles):
    m, d = x2.shape
    return pl.pallas_call(
        _inproj_kernel,
        grid=(m // tm, len(tiles)),
        in_specs=[
            pl.BlockSpec((tm, d), lambda i, j: (i, 0)),
            pl.BlockSpec((1, d), lambda i, j: (0, 0)),
            pl.BlockSpec((tn, d), lambda i, j: (_nth(j, tiles), 0)),
        ],
        out_specs=[
            pl.BlockSpec((tm, tn), lambda i, j: (i, j)),
            pl.BlockSpec((tm, d), lambda i, j: (i, 0)),
        ],
        out_shape=[jax.ShapeDtypeStruct((m, len(tiles) * tn), BF16), jax.ShapeDtypeStruct((m, d), BF16)],
        compiler_params=_cparams(("parallel", "arbitrary")),
        name="inproj",
    )(x2, nw, w)


def _gate_scalars(u_ref, wba_ref, alog_ref, dtb_ref, beta_ref, gc_ref, chunk, sub):
    heads = alog_ref.shape[1]
    parts = [slice(p * sub, (p + 1) * sub) for p in range(u_ref.shape[0] // sub)]
    row = lax.broadcasted_iota(jnp.int32, (sub, sub), 0)
    col = lax.broadcasted_iota(jnp.int32, (sub, sub), 1)
    incl = jnp.where(((row // chunk) == (col // chunk)) & (row >= col), 1.0, 0.0).astype(BF16)
    bas = [_dot_nt(u_ref[sl, :], wba_ref[...]) for sl in parts]
    gs = []
    for sl, ba in zip(parts, bas):
        beta_ref[sl, :] = _sigmoid(ba[:, 0:heads])
        a = ba[:, heads:2 * heads] + dtb_ref[...]
        softplus = jnp.maximum(a, 0.0) + jnp.log1p(jnp.exp(-jnp.abs(a)))
        gs.append(-jnp.exp(alog_ref[...]) * softplus)
    g1 = [g.astype(BF16) for g in gs]
    r1 = [g - p1.astype(F32) for g, p1 in zip(gs, g1)]
    g2 = [r.astype(BF16) for r in r1]
    g3 = [(r - p2.astype(F32)).astype(BF16) for r, p2 in zip(r1, g2)]
    gcs = [_dot(incl, p1) + _dot(incl, p2) + _dot(incl, p3) for p1, p2, p3 in zip(g1, g2, g3)]
    for sl, gc in zip(parts, gcs):
        gc_ref[sl, :] = gc


def _rotproj_kernel(u_ref, w_ref, cos_ref, sin_ref, wba_ref, alog_ref, dtb_ref, *rest,
                    dk, chunk, sub, cast_flags, cast_count, per_step):
    cast_in, (o_ref, beta_ref, gc_ref, cast_ref) = rest[:2 * per_step], rest[2 * per_step:]
    cast_tr = cast_ref.shape[0] // per_step
    res = _dot_nt(u_ref[...], w_ref[...])
    cos = cos_ref[...]
    sin = sin_ref[...]
    for hh in range(res.shape[1] // dk):
        sl = slice(hh * dk, (hh + 1) * dk)
        x = res[:, sl]
        o_ref[:, sl] = (x * cos + pltpu.roll(x, dk // 2, 1) * sin).astype(o_ref.dtype)
    _gate_scalars(u_ref, wba_ref, alog_ref, dtb_ref, beta_ref, gc_ref, chunk, sub)
    for kk in range(per_step):
        shifted = _nth(pl.program_id(0) * per_step + kk, cast_flags) > 0
        cast_ref[kk * cast_tr:(kk + 1) * cast_tr, :] = _cast_row_tile(cast_in[2 * kk], cast_in[2 * kk + 1],
                                                                     shifted, cast_count)


def _rotproj(u, w, cosf, sins, wba, alog, dtb, wt, cast_srcs, cast_tr, cast_count,
             tm, tn, tile, seq, dk, chunk, sub):
    m, d = u.shape
    h = alog.shape[1]
    steps = m // tm
    assert seq % tm == 0 and tm % sub == 0 and sub % chunk == 0 and len(cast_srcs) % steps == 0
    per_seq = seq // tm
    per_step = len(cast_srcs) // steps
    scalar_out = jax.ShapeDtypeStruct((m, h), F32)
    cast_flags = tuple(int(src % cast_tr != 0) for src in cast_srcs)
    cast_in_specs = []
    for kk in range(per_step):
        cast_in_specs += _row_tile_specs(cast_srcs, cast_tr, wt.shape[1], cast_count,
                                         lambda i, kk=kk: i * per_step + kk)
    return pl.pallas_call(
        functools.partial(_rotproj_kernel, dk=dk, chunk=chunk, sub=sub, cast_flags=cast_flags,
                          cast_count=cast_count, per_step=per_step),
        grid=(steps,),
        in_specs=[
            pl.BlockSpec((tm, d), lambda i: (i, 0)),
            pl.BlockSpec((tn, d), lambda i: (tile, 0)),
            pl.BlockSpec((tm, dk), lambda i: (i % per_seq, 0)),
            pl.BlockSpec((tm, dk), lambda i: (i % per_seq, 0)),
            pl.BlockSpec((2 * h, d), lambda i: (0, 0)),
            pl.BlockSpec((1, h), lambda i: (0, 0)),
            pl.BlockSpec((1, h), lambda i: (0, 0)),
        ] + cast_in_specs,
        out_specs=[pl.BlockSpec((tm, tn), lambda i: (i, 0))] + [pl.BlockSpec((tm, h), lambda i: (i, 0))] * 2
        + [pl.BlockSpec((per_step * cast_tr, wt.shape[1]), lambda i: (i, 0))],
        out_shape=[jax.ShapeDtypeStruct((m, tn), BF16), scalar_out, scalar_out,
                   jax.ShapeDtypeStruct((len(cast_srcs) * cast_tr, wt.shape[1]), BF16)],
        compiler_params=_cparams(("arbitrary",)),
        name="rotproj",
    )(u, w, cosf, sins, wba, alog, dtb, *([wt, wt] * per_step))


def _gates_kernel(u_ref, ws_ref, wg_ref, *rest):
    ncast = (len(rest) - 1) // 2
    cast_in, o_ref, cast_out = rest[:ncast], rest[ncast], rest[ncast + 1:]
    u = u_ref[...]
    a = _dot_nt(u, ws_ref[...])
    g = _dot_nt(u, wg_ref[...])
    o_ref[...] = (_silu(a) * (0.5 + 0.5 * jnp.tanh(0.5 * g))).astype(o_ref.dtype)
    for src, dst in zip(cast_in, cast_out):
        dst[...] = src[...].astype(dst.dtype)


def _cast_slabs(rows, nsteps):
    nb = nsteps
    while nb > 1 and (rows % nb or (rows // nb) % BF16_TILE_ROWS):
        nb //= 2
    return nb


def _gates(u, w, tm, tg, vd, pairs, to_cast):
    m, d = u.shape
    per = vd // tg
    nj = len(pairs) * per
    nsteps = (m // tm) * nj

    def blk(j, which):
        idx = pairs[0][which] // tg + j
        for p in range(1, len(pairs)):
            idx = jnp.where(j >= p * per, pairs[p][which] // tg + j - p * per, idx)
        return idx

    cast_specs = []
    for mat in to_cast:
        nb = _cast_slabs(mat.shape[0], nsteps)
        rep = nsteps // nb
        cast_specs.append(pl.BlockSpec((mat.shape[0] // nb, mat.shape[1]),
                                       lambda i, j, rep=rep: ((i * nj + j) // rep, 0)))

    outs = pl.pallas_call(
        _gates_kernel,
        grid=(m // tm, nj),
        in_specs=[
            pl.BlockSpec((tm, d), lambda i, j: (i, 0)),
            pl.BlockSpec((tg, d), lambda i, j: (blk(j, 0), 0)),
            pl.BlockSpec((tg, d), lambda i, j: (blk(j, 1), 0)),
        ] + cast_specs,
        out_specs=[pl.BlockSpec((tm, tg), lambda i, j: (i, j))] + cast_specs,
        out_shape=[jax.ShapeDtypeStruct((m, len(pairs) * vd), BF16)]
        + [jax.ShapeDtypeStruct(mat.shape, BF16) for mat in to_cast],
        compiler_params=_cparams(("arbitrary", "arbitrary")),
        name="gates",
    )(u, w, w, *to_cast)
    return outs[0], outs[1:]


def _gdn_prep_kernel(q_ref, qh_ref, k_ref, kh_ref, v_ref, vh_ref, cwq_ref, cwk_ref, cwv_ref,
                     beta_ref, gcc_ref, gcr_ref, pick_ref,
                     wq_ref, kpt_ref, u_ref, attn_ref, egl_ref, *, dk, dv, chunk, nchunks, heads):
    t = pl.program_id(1)
    rows = chunk * nchunks
    lane_head = lax.broadcasted_iota(jnp.int32, (1, beta_ref.shape[1]), 1)

    def head_col(ref, hh):
        hsel = lane_head == pl.program_id(2) * heads + hh
        return jnp.sum(jnp.where(hsel, ref[...], 0.0), axis=1, keepdims=True)

    pick = pick_ref[...]
    blk = pick.shape[0]
    win = HALO_ROWS + blk

    def conv_silu(x_ref, h_ref, cw_ref):
        halo = jnp.where(t == 0, jnp.zeros_like(h_ref[...]), h_ref[...])
        ext = jnp.concatenate([halo, x_ref[...]], axis=0)
        taps = [ext * cw_ref[tap:tap + 1, :].astype(BF16) for tap in range(CONV_WIDTH)]
        out = []
        for r in range(rows // blk):
            stacked = jnp.concatenate([tp[r * blk:r * blk + win] for tp in taps], axis=0)
            out.append(_silu(_dot(pick, stacked)))
        return jnp.concatenate(out, axis=0)

    def l2norm(y):
        return y * lax.rsqrt(jnp.sum(y * y, axis=-1, keepdims=True) + EPS)

    q_all = conv_silu(q_ref, qh_ref, cwq_ref)
    k_all = conv_silu(k_ref, kh_ref, cwk_ref)
    v_all = conv_silu(v_ref, vh_ref, cwv_ref)

    ri = lax.broadcasted_iota(jnp.int32, (chunk, 2 * chunk), 0)
    li = lax.broadcasted_iota(jnp.int32, (chunk, 2 * chunk), 1)
    first = li < chunk
    ci = jnp.where(first, li, li - chunk)
    incl = ri >= ci
    strict = ri > ci
    eye = jnp.where(ri == ci, 1.0, 0.0).astype(F32)
    ident_k = jnp.where(lax.broadcasted_iota(jnp.int32, (dk, dk), 0)
                        == lax.broadcasted_iota(jnp.int32, (dk, dk), 1), 1.0, 0.0).astype(BF16)

    def blockdiag(x):
        return jnp.concatenate([jnp.where(first, x, 0.0), jnp.where(first, 0.0, x)], axis=0).astype(BF16)

    def pair_rows(xa, xb):
        z = jnp.zeros_like(xa)
        return jnp.concatenate([jnp.concatenate([xa, z], axis=1), jnp.concatenate([z, xb], axis=1)], axis=0)

    npairs = nchunks // 2
    units = [(hh, p) for hh in range(heads) for p in range(npairs)]
    ur = range(len(units))
    qs, ks, vs, bs, gcs = [], [], [], [], []
    for hh in range(heads):
        q = l2norm(q_all[:, hh * dk:(hh + 1) * dk]) * (dk ** -0.5)
        k = l2norm(k_all[:, hh * dk:(hh + 1) * dk])
        v = v_all[:, hh * dv:(hh + 1) * dv]
        beta, gcum = head_col(beta_ref, hh), head_col(gcc_ref, hh)
        for c in range(nchunks):
            sl = slice(c * chunk, (c + 1) * chunk)
            qs.append(q[sl])
            ks.append(k[sl])
            vs.append(v[sl])
            bs.append(beta[sl])
            gcs.append(gcum[sl])
    nc = range(len(qs))
    gls = [gc[chunk - 1:chunk, :] for gc in gcs]
    egs = [jnp.exp(gc) for gc in gcs]
    kbs = [ks[c] * bs[c] for c in nc]
    decays = [jnp.exp(jnp.where(incl, jnp.where(first, gcs[2 * u], gcs[2 * u + 1]) - gcr_ref[0, hh, p], -1e30))
              for u, (hh, p) in enumerate(units)]
    lhs = [jnp.concatenate([jnp.concatenate([kbs[2 * u], kbs[2 * u + 1]], axis=1),
                            jnp.concatenate([qs[2 * u], qs[2 * u + 1]], axis=1)], axis=0).astype(BF16) for u in ur]
    kq = [_dot_nt(lhs[u], pair_rows(ks[2 * u], ks[2 * u + 1]).astype(BF16)) for u in ur]
    amats = [jnp.where(strict, kq[u][0:chunk] * decays[u], 0.0) for u in ur]
    for u, (hh, p) in enumerate(units):
        attn_ref[0, hh, p] = (kq[u][chunk:2 * chunk] * decays[u]).astype(attn_ref.dtype)
    tinvs = [eye - a for a in amats]
    apows = amats
    span = 2
    while span < chunk:
        apows = [_dot(apows[u].astype(BF16), blockdiag(apows[u])) for u in ur]
        tinvs = [tinvs[u] + _dot(tinvs[u].astype(BF16), blockdiag(apows[u])) for u in ur]
        span *= 2
    rhs = [jnp.concatenate([jnp.concatenate([kbs[c] * egs[c], vs[c] * bs[c]], axis=1)
                            for c in (2 * u, 2 * u + 1)], axis=0).astype(BF16) for u in ur]
    wu = [_dot(blockdiag(tinvs[u]), rhs[u]) for u in ur]
    kps = [jnp.concatenate([ks[c] * jnp.exp(gls[c] - gcs[c]) for c in (2 * u, 2 * u + 1)], axis=0).astype(BF16)
           for u in ur]
    kpt = [_dot_nt(ident_k, kps[u]) for u in ur]
    for u, (hh, p) in enumerate(units):
        kpt_ref[0, hh, p] = kpt[u].astype(kpt_ref.dtype)
        for half in range(2):
            c = 2 * u + half
            rs = slice(half * chunk, (half + 1) * chunk)
            wq_ref[0, hh, 2 * p + half, 0:chunk, :] = wu[u][rs, 0:dk].astype(wq_ref.dtype)
            wq_ref[0, hh, 2 * p + half, chunk:2 * chunk, :] = (qs[c] * egs[c]).astype(wq_ref.dtype)
            u_ref[0, hh, 2 * p + half] = wu[u][rs, dk:dk + dv].astype(u_ref.dtype)
            egl_ref[0, hh, 2 * p + half] = jnp.broadcast_to(jnp.exp(gls[c]), (1, dv))


def _gdn_prep(proj, conv_w, beta_c, gc_c, gc_r, dims, rows, heads):
    b, s, h, dk, dv = dims
    chunk = GDN_CHUNK
    nchunks = rows // chunk
    n = s // chunk
    qk = h * dk
    tiles = s // rows
    kw, vw = heads * dk, heads * dv
    kcol = qk // kw
    vcol = 2 * qk // vw

    blk = _pick(rows, CONV_BLOCK_ROWS)
    win = HALO_ROWS + blk
    out_row = np.arange(blk)[:, None]
    col = np.arange(CONV_WIDTH * win)[None, :]
    pick = jnp.asarray(col % win == HALO_ROWS + out_row - (CONV_WIDTH - 1 - col // win), dtype=BF16)

    def rowblk(bi, ti):
        return bi * tiles + ti

    def halo(bi, ti):
        return jnp.maximum((bi * s + ti * rows) // HALO_ROWS - 1, 0)

    col_spec = pl.BlockSpec((rows, h), lambda bi, ti, hi: (rowblk(bi, ti), 0))
    in_specs = [
        pl.BlockSpec((rows, kw), lambda bi, ti, hi: (rowblk(bi, ti), hi)),
        pl.BlockSpec((HALO_ROWS, kw), lambda bi, ti, hi: (halo(bi, ti), hi)),
        pl.BlockSpec((rows, kw), lambda bi, ti, hi: (rowblk(bi, ti), kcol + hi)),
        pl.BlockSpec((HALO_ROWS, kw), lambda bi, ti, hi: (halo(bi, ti), kcol + hi)),
        pl.BlockSpec((rows, vw), lambda bi, ti, hi: (rowblk(bi, ti), vcol + hi)),
        pl.BlockSpec((HALO_ROWS, vw), lambda bi, ti, hi: (halo(bi, ti), vcol + hi)),
        pl.BlockSpec((CONV_WIDTH, kw), lambda bi, ti, hi: (0, hi)),
        pl.BlockSpec((CONV_WIDTH, kw), lambda bi, ti, hi: (0, kcol + hi)),
        pl.BlockSpec((CONV_WIDTH, vw), lambda bi, ti, hi: (0, vcol + hi)),
        col_spec, col_spec,
        pl.BlockSpec((1, heads, nchunks // 2, 1, 2 * chunk), lambda bi, ti, hi: (bi, hi, ti, 0, 0)),
        pl.BlockSpec(pick.shape, lambda bi, ti, hi: (0, 0)),
    ]

    def out5(r, c):
        return pl.BlockSpec((1, heads, nchunks, r, c), lambda bi, ti, hi: (bi, hi, ti, 0, 0))

    def pair5(r, c):
        return pl.BlockSpec((1, heads, nchunks // 2, r, c), lambda bi, ti, hi: (bi, hi, ti, 0, 0))

    out_specs = [out5(2 * chunk, dk), pair5(dk, 2 * chunk), out5(chunk, dv), pair5(chunk, 2 * chunk), out5(1, dv)]
    out_shape = [
        jax.ShapeDtypeStruct((b, h, n, 2 * chunk, dk), BF16),
        jax.ShapeDtypeStruct((b, h, n // 2, dk, 2 * chunk), BF16),
        jax.ShapeDtypeStruct((b, h, n, chunk, dv), BF16),
        jax.ShapeDtypeStruct((b, h, n // 2, chunk, 2 * chunk), BF16),
        jax.ShapeDtypeStruct((b, h, n, 1, dv), F32),
    ]
    return pl.pallas_call(
        functools.partial(_gdn_prep_kernel, dk=dk, dv=dv, chunk=chunk, nchunks=nchunks, heads=heads),
        grid=(b, tiles, h // heads),
        in_specs=in_specs,
        out_specs=out_specs,
        out_shape=out_shape,
        compiler_params=_cparams(("parallel", "parallel", "parallel")),
        name="gdn_prep",
    )(proj, proj, proj, proj, proj, proj, conv_w, conv_w, conv_w, beta_c, gc_c, gc_r, pick)


def _pair_pad(x, half):
    z = jnp.zeros_like(x)
    return jnp.concatenate([x, z] if half == 0 else [z, x], axis=0)


def _gdn_scan_kernel(wq_ref, kpt_ref, u_ref, attn_ref, egl_ref, f_ref, nw_ref, o_ref, s_scr,
                     *, dv, chunk, nchunks, unroll, heads, batch):
    @pl.when(pl.program_id(1) == 0)
    def _():
        s_scr[...] = jnp.zeros_like(s_scr)

    nw = nw_ref[...]
    seqs = [(bb, hh) for bb in range(batch) for hh in range(heads)]
    sr = range(len(seqs))

    def chunk_group(it, carry):
        for cc in range(unroll):
            c = it * unroll + cc
            rs = pl.ds(pl.multiple_of(c * chunk, chunk), chunk)
            states = [s_scr[i] for i in sr]
            r = [_dot(wq_ref[bb, hh, c], states[i].astype(BF16)) for i, (bb, hh) in enumerate(seqs)]
            v_new = [_pair_pad((u_ref[bb, hh, c] - r[i][0:chunk]).astype(BF16), cc)
                     for i, (bb, hh) in enumerate(seqs)]
            upd = [_dot(kpt_ref[bb, hh, it], v_new[i]) for i, (bb, hh) in enumerate(seqs)]
            for i, (bb, hh) in enumerate(seqs):
                s_scr[i] = states[i] * egl_ref[bb, hh, c] + upd[i]
            o = [r[i][chunk:2 * chunk] + _dot(attn_ref[bb, hh, it], v_new[i]) for i, (bb, hh) in enumerate(seqs)]
            for i, (bb, hh) in enumerate(seqs):
                cs = slice(hh * dv, (hh + 1) * dv)
                on = o[i] * lax.rsqrt(jnp.mean(o[i] * o[i], axis=-1, keepdims=True) + EPS) * nw
                o_ref[bb, rs, cs] = (on * f_ref[bb, rs, cs].astype(F32)).astype(o_ref.dtype)
        return carry

    lax.fori_loop(0, nchunks // unroll, chunk_group, 0)


def _gdn_scan(wq, kpt, u, attn, egl, gates, gdn_nw, dims, nchunks, unroll, heads, fcol):
    b, s, h, dk, dv = dims
    chunk = GDN_CHUNK
    n = s // chunk
    rows = nchunks * chunk
    steps = n // nchunks
    width = heads * dv
    assert nchunks % unroll == 0 and unroll == 2

    def in5(r, c):
        return pl.BlockSpec((b, heads, nchunks, r, c), lambda gi, i: (0, gi, i, 0, 0))

    def pair5(r, c):
        return pl.BlockSpec((b, heads, nchunks // 2, r, c), lambda gi, i: (0, gi, i, 0, 0))

    in_specs = [
        in5(2 * chunk, dk), pair5(dk, 2 * chunk), in5(chunk, dv), pair5(chunk, 2 * chunk), in5(1, dv),
        pl.BlockSpec((b, rows, width), lambda gi, i: (0, i, fcol // width + gi)),
        pl.BlockSpec((1, dv), lambda gi, i: (0, 0)),
    ]
    return pl.pallas_call(
        functools.partial(_gdn_scan_kernel, dv=dv, chunk=chunk, nchunks=nchunks, unroll=unroll, heads=heads,
                          batch=b),
        grid=(h // heads, steps),
        in_specs=in_specs,
        out_specs=pl.BlockSpec((b, rows, width), lambda gi, i: (0, i, gi)),
        out_shape=jax.ShapeDtypeStruct((b, s, h * dv), BF16),
        scratch_shapes=[pltpu.VMEM((b * heads, dk, dv), F32)],
        compiler_params=_cparams(("parallel", "arbitrary")),
        name="gdn_scan",
    )(wq, kpt, u, attn, egl, gates, gdn_nw)


def _retention_kernel(q_ref, k_ref, v_ref, f_ref, ma_ref, dmat_ref,
                      xi_ref, zeta_ref, gpow_ref, nw_ref, o_ref, s_scr, *, dk, dv, heads, sub):
    @pl.when(pl.program_id(2) == 0)
    def _():
        s_scr[...] = jnp.zeros_like(s_scr)

    hr = range(heads)
    ksl = [slice(hh * dk, (hh + 1) * dk) for hh in hr]
    vsl = [slice(hh * dv, (hh + 1) * dv) for hh in hr]

    def sub_chunk(it, carry):
        rs = pl.ds(pl.multiple_of(it * sub, sub), sub)
        q = [q_ref[rs, ksl[hh]] for hh in hr]
        k = [k_ref[rs, ksl[hh]] for hh in hr]
        kz = [(k[hh].astype(F32) * zeta_ref[hh]).astype(BF16) for hh in hr]
        states = [s_scr[hh] for hh in hr]
        scores = [(_dot_nt(q[hh], k[hh]) * dmat_ref[hh]).astype(BF16) for hh in hr]
        lhs = [jnp.concatenate([scores[hh], (q[hh].astype(F32) * xi_ref[hh]).astype(BF16)], axis=1) for hh in hr]
        rhs = [jnp.concatenate([v_ref[rs, vsl[hh]], states[hh].astype(BF16)], axis=0) for hh in hr]
        upd = [_dot_tn(kz[hh], v_ref[rs, vsl[hh]]) for hh in hr]
        for hh in hr:
            s_scr[hh] = states[hh] * gpow_ref[hh] + upd[hh]
        o = [_dot(lhs[hh], rhs[hh]) for hh in hr]
        for hh in hr:
            mu = jnp.mean(o[hh], axis=-1, keepdims=True)
            dev = o[hh] - mu
            var = jnp.mean(dev * dev, axis=-1, keepdims=True)
            ob = dev * lax.rsqrt(var + EPS) * nw_ref[:, vsl[hh]]
            mixed = ob * f_ref[rs, vsl[hh]].astype(F32) + ma_ref[rs, vsl[hh]].astype(F32)
            o_ref[rs, vsl[hh]] = mixed.astype(o_ref.dtype)
        return carry

    lax.fori_loop(0, q_ref.shape[0] // sub, sub_chunk, 0)


def _retention(qk, proj, gates, ma, ret_nw, tables, dims, rows, sub, heads, cols):
    b, s, h, dk, dv = dims
    dmat, xi, zeta, gpow = tables
    rqcol, rkcol, rvcol, fcol = cols
    steps = s // rows
    kw, vw = heads * dk, heads * dv
    assert rows % sub == 0

    def rb(bi, i):
        return bi * steps + i

    in_specs = [
        pl.BlockSpec((rows, kw), lambda bi, gi, i: (rb(bi, i), rqcol // kw + gi)),
        pl.BlockSpec((rows, kw), lambda bi, gi, i: (rb(bi, i), rkcol // kw + gi)),
        pl.BlockSpec((rows, vw), lambda bi, gi, i: (rb(bi, i), rvcol // vw + gi)),
        pl.BlockSpec((rows, vw), lambda bi, gi, i: (rb(bi, i), fcol // vw + gi)),
        pl.BlockSpec((rows, vw), lambda bi, gi, i: (rb(bi, i), gi)),
        pl.BlockSpec((heads, sub, sub), lambda bi, gi, i: (gi, 0, 0)),
        pl.BlockSpec((heads, sub, dk), lambda bi, gi, i: (gi, 0, 0)),
        pl.BlockSpec((heads, sub, dk), lambda bi, gi, i: (gi, 0, 0)),
        pl.BlockSpec((heads, 1, dv), lambda bi, gi, i: (gi, 0, 0)),
        pl.BlockSpec((1, vw), lambda bi, gi, i: (0, gi)),
    ]
    return pl.pallas_call(
        functools.partial(_retention_kernel, dk=dk, dv=dv, heads=heads, sub=sub),
        grid=(b, h // heads, steps),
        in_specs=in_specs,
        out_specs=pl.BlockSpec((rows, vw), lambda bi, gi, i: (rb(bi, i), gi)),
        out_shape=jax.ShapeDtypeStruct((b * s, h * dv), BF16),
        scratch_shapes=[pltpu.VMEM((heads, dk, dv), F32)],
        compiler_params=_cparams(("parallel", "parallel", "arbitrary")),
        name="retention",
    )(qk, qk, proj, gates, ma, dmat, xi, zeta, gpow, ret_nw)


def _rotary_tables(s, dk, rows):
    inv = ROPE_BASE ** (-jnp.arange(0, dk, 2, dtype=F32) / dk)
    inv = jnp.concatenate([inv, inv])
    sign = jnp.concatenate([-jnp.ones(dk // 2, F32), jnp.ones(dk // 2, F32)])
    lo = jnp.arange(rows, dtype=F32)[:, None] * inv[None, :]
    hi = (jnp.arange(s // rows, dtype=F32) * rows)[:, None] * inv[None, :]
    cl, sl, ch, sh = jnp.cos(lo), jnp.sin(lo), jnp.cos(hi)[:, None], jnp.sin(hi)[:, None]
    cos = (ch * cl - sh * sl).reshape(s, dk)
    sin = ((sh * cl + ch * sl) * sign).reshape(s, dk)
    return cos, sin


def _retention_tables(h, dk, dv, rows):
    log_gamma = jnp.log1p(-jnp.exp2(-5.0 - jnp.arange(h, dtype=F32)))
    pos = jnp.arange(rows, dtype=F32)
    dist = pos[:, None] - pos[None, :]
    kscale = dk ** -0.5
    dmat = jnp.exp(jnp.where(dist >= 0, dist * log_gamma[:, None, None], -jnp.inf)) * kscale
    xi = jnp.broadcast_to(jnp.exp((pos + 1.0) * log_gamma[:, None])[:, :, None], (h, rows, dk))
    zeta = jnp.broadcast_to(jnp.exp((rows - 1.0 - pos) * log_gamma[:, None])[:, :, None] * kscale, (h, rows, dk))
    gpow = jnp.broadcast_to(jnp.exp(rows * log_gamma)[:, None, None], (h, 1, dv))
    return dmat, xi, zeta, gpow


def _outproj_kernel(m_ref, w_ref, x_ref, nw_ref, h_ref, hn_ref):
    hres = x_ref[...] + _dot(m_ref[...], w_ref[...])
    h_ref[...] = hres
    ms = jnp.mean(hres * hres, axis=-1, keepdims=True)
    hn_ref[...] = ((hres * lax.rsqrt(ms + EPS)) * nw_ref[...]).astype(hn_ref.dtype)


def _outproj(mixed, w, x2, nw, tm):
    m, d = x2.shape
    kdim = mixed.shape[1]
    return pl.pallas_call(
        _outproj_kernel,
        grid=(m // tm,),
        in_specs=[
            pl.BlockSpec((tm, kdim), lambda i: (i, 0)),
            pl.BlockSpec((kdim, d), lambda i: (0, 0)),
            pl.BlockSpec((tm, d), lambda i: (i, 0)),
            pl.BlockSpec((1, d), lambda i: (0, 0)),
        ],
        out_specs=[pl.BlockSpec((tm, d), lambda i: (i, 0))] * 2,
        out_shape=[jax.ShapeDtypeStruct((m, d), F32), jax.ShapeDtypeStruct((m, d), BF16)],
        compiler_params=_cparams(("parallel",)),
        name="outproj",
    )(mixed, w, x2, nw)


def _ffn_kernel(hn_ref, h_ref, wg_ref, wu_ref, wd_ref, nw_ref, o_ref):
    f = pl.program_id(1)

    @pl.when(f == 0)
    def _():
        o_ref[...] = h_ref[...]

    hn = hn_ref[...]
    g = _dot(hn, wg_ref[...])
    up = _dot(hn, wu_ref[...])
    act = (_silu(g) * up).astype(BF16)
    o_ref[...] += _dot(act, wd_ref[...])

    @pl.when(f == pl.num_programs(1) - 1)
    def _():
        hres = o_ref[...]
        ms = jnp.mean(hres * hres, axis=-1, keepdims=True)
        o_ref[...] = (hres * lax.rsqrt(ms + EPS)) * nw_ref[...]


def _ffn(hn, hres, wg, wu, wd, nw, tm, tf):
    m, d = hres.shape
    ff = wg.shape[1]
    return pl.pallas_call(
        _ffn_kernel,
        grid=(m // tm, ff // tf),
        in_specs=[
            pl.BlockSpec((tm, d), lambda i, j: (i, 0)),
            pl.BlockSpec((tm, d), lambda i, j: (i, 0)),
            pl.BlockSpec((d, tf), lambda i, j: (0, j)),
            pl.BlockSpec((d, tf), lambda i, j: (0, j)),
            pl.BlockSpec((tf, d), lambda i, j: (j, 0)),
            pl.BlockSpec((1, d), lambda i, j: (0, 0)),
        ],
        out_specs=pl.BlockSpec((tm, d), lambda i, j: (i, 0)),
        out_shape=jax.ShapeDtypeStruct((m, d), F32),
        compiler_params=_cparams(("parallel", "arbitrary")),
        name="ffn",
    )(hn, hres, wg, wu, wd, nw)


def _pick(total, want):
    t = min(total, want)
    while total % t:
        t //= 2
    return t


def _layer(h2, dims, norm1_w, w_in, conv_w, a_log, dt_bias, gdn_norm_w, ret_norm_w,
           w_out, norm2_w, w_gate, w_up, w_down, out_norm_w):
    b, s, h, dk, dv = dims
    m, d = h2.shape
    qk, vd = h * dk, h * dv
    chunk = GDN_CHUNK
    n = s // chunk

    assert 2 * qk == vd
    tn = vd
    sc0 = 3 * vd
    sc = 2 * h
    w_in_t = w_in.T
    w_ba = w_in_t[sc0:sc0 + sc].astype(BF16)
    tr = _pick(vd, 1024)
    per = vd // tr
    w_proj = _cast_rows(w_in_t, [g * vd + k * tr for g in (0, 1) for k in range(per)]
                        + [sc0 + sc + g * vd + k * tr for g in (0, 1) for k in range(per)], tr, sc)

    proj, u = _inproj(h2, norm1_w.reshape(1, d), w_proj, _pick(m, 1024), tn, (0, 1, 3))
    ret_sub = _pick(s, 256)
    cosf, sins = _rotary_tables(s, dk, ret_sub)
    rot_tm = _pick(s, 1024)
    gtr = _pick(vd, (4 * vd) // (m // rot_tm))
    gper = vd // gtr
    gate_srcs = ([2 * vd + k * gtr for k in range(gper)]
                 + [sc0 + sc + g * vd + k * gtr for g in (2, 3, 4) for k in range(gper)])
    qk_rot, beta, gcum, w_gate_rows = _rotproj(
        u, w_proj, cosf, sins, w_ba, a_log.reshape(1, h), dt_bias.reshape(1, h), w_in_t, gate_srcs, gtr, sc,
        rot_tm, tn, 2, s, dk, chunk, _pick(s, 256))
    gates, (wg_b, wu_b, wd_b, wo_b) = _gates(u, w_gate_rows, _pick(m, 1024), _pick(vd, 1024), vd,
                                             ((0, 2 * vd), (vd, 3 * vd)), [w_gate, w_up, w_down, w_out])

    gc_r = gcum.reshape(b, n // 2, 2 * chunk, h).transpose(0, 3, 1, 2)[:, :, :, None, :]
    wq, kpt, uu, attn, egl = _gdn_prep(proj, conv_w, beta, gcum, gc_r, dims, _pick(s, 512), _pick(h, 8))

    ma = _gdn_scan(wq, kpt, uu, attn, egl, gates.reshape(b, s, 2 * vd), gdn_norm_w.reshape(1, dv), dims,
                   _pick(n, 8), _pick(n, 2), _pick(h, 8), 0).reshape(m, vd)

    tables = _retention_tables(h, dk, dv, ret_sub)
    mixed = _retention(qk_rot, proj, gates, ma, ret_norm_w.reshape(1, vd), tables, dims, _pick(s, 512),
                       ret_sub, _pick(h, 8), (0, qk, 2 * tn, vd))

    hres, hn = _outproj(mixed, wo_b, h2, norm2_w.reshape(1, d), _pick(m, 512))

    return _ffn(hn, hres, wg_b, wu_b, wd_b, out_norm_w.reshape(1, d), _pick(m, 512), _pick(w_gate.shape[1], 512))


def kernel(x, norm1_w, w_in, conv_w, a_log, dt_bias, gdn_norm_w, ret_norm_w, w_out, norm2_w,
           w_gate, w_up, w_down, norm_f_w):
    b, s, d = x.shape
    depth, h = a_log.shape
    assert depth == 1, "the final norm is fused into the single layer's FFN kernel"
    dk = d // 16
    dv = d // h
    dims = (b, s, h, dk, dv)
    out = _layer(x.reshape(b * s, d), dims, norm1_w[0], w_in[0], conv_w[0], a_log[0], dt_bias[0],
                 gdn_norm_w[0], ret_norm_w[0], w_out[0], norm2_w[0], w_gate[0], w_up[0], w_down[0],
                 norm_f_w)
    return out.reshape(b, s, d)
```

```python
import functools

import jax
import jax.numpy as jnp
import numpy as np
from jax import lax
from jax.experimental import pallas as pl
from jax.experimental.pallas import tpu as pltpu

F32 = jnp.float32
BF16 = jnp.bfloat16

EPS = 1e-6
CONV_WIDTH = 4
GDN_CHUNK = 64
CONV_BLOCK_ROWS = 128
ROPE_BASE = 10000.0
BF16_TILE_ROWS = 16
HALO_ROWS = BF16_TILE_ROWS
VMEM_LIMIT_BYTES = 56 * 1024 * 1024
FFN_VMEM_LIMIT_BYTES = 60 * 1024 * 1024


def _cparams(sem, vmem_limit=VMEM_LIMIT_BYTES):
    return pltpu.CompilerParams(dimension_semantics=sem, vmem_limit_bytes=vmem_limit)


def _sigmoid(x):
    return 1.0 / (1.0 + jnp.exp(-x))


def _silu(x):
    h = 0.5 * x
    return h + h * jnp.tanh(h)


def _dot(a, b):
    return jnp.dot(a, b, preferred_element_type=F32)


def _dot_nt(a, b):
    return lax.dot_general(a, b, (((1,), (1,)), ((), ())), preferred_element_type=F32)


def _dot_tn(a, b):
    return lax.dot_general(a, b, (((0,), (0,)), ((), ())), preferred_element_type=F32)


def _nth(j, values):
    idx = values[0]
    for p in range(1, len(values)):
        idx = jnp.where(j >= p, values[p], idx)
    return idx


def _row_tile_specs(srcs, tr, d, count, index):
    assert all(src % tr in (0, count) for src in srcs) and tr % count == 0 and count % BF16_TILE_ROWS == 0
    bases = tuple(src // tr for src in srcs)
    nexts = tuple((src // tr + 1) * (tr // count) for src in srcs)
    return [pl.BlockSpec((tr, d), lambda *g: (_nth(index(*g), bases), 0)),
            pl.BlockSpec((count, d), lambda *g: (_nth(index(*g), nexts), 0))]


def _cast_row_tile(cur_ref, nxt_ref, shifted, count):
    cur = cur_ref[...]
    moved = jnp.concatenate([cur[count:], nxt_ref[...]], axis=0)
    return jnp.where(shifted, moved, cur).astype(BF16)


def _cast_rows_kernel(cur_ref, nxt_ref, o_ref, *, flags, count):
    o_ref[...] = _cast_row_tile(cur_ref, nxt_ref, _nth(pl.program_id(0), flags) > 0, count)


def _cast_rows(wt, srcs, tr, count):
    d = wt.shape[1]
    flags = tuple(int(src % tr != 0) for src in srcs)
    return pl.pallas_call(
        functools.partial(_cast_rows_kernel, flags=flags, count=count),
        grid=(len(srcs),),
        in_specs=_row_tile_specs(srcs, tr, d, count, lambda j: j),
        out_specs=pl.BlockSpec((tr, d), lambda j: (j, 0)),
        out_shape=jax.ShapeDtypeStruct((len(srcs) * tr, d), BF16),
        compiler_params=_cparams(("parallel",)),
        name="cast_rows",
    )(wt, wt)


def _inproj_kernel(x_ref, nw_ref, w_ref, o_ref, u_ref):
    @pl.when(pl.program_id(1) == 0)
    def _():
        xf = x_ref[...]
        ms = jnp.mean(xf * xf, axis=-1, keepdims=True)
        u_ref[...] = ((xf * lax.rsqrt(ms + EPS)) * nw_ref[...]).astype(u_ref.dtype)

    o_ref[...] = _dot_nt(u_ref[...], w_ref[...]).astype(o_ref.dtype)


def _inproj(x2, nw, w, tm, tn, tiles):
    m, d = x2.shape
    return pl.pallas_call(
        _inproj_kernel,
        grid=(m // tm, len(tiles)),
        in_specs=[
            pl.BlockSpec((tm, d), lambda i, j: (i, 0)),
            pl.BlockSpec((1, d), lambda i, j: (0, 0)),
            pl.BlockSpec((tn, d), lambda i, j: (_nth(j, tiles), 0)),
        ],
        out_specs=[
            pl.BlockSpec((tm, tn), lambda i, j: (i, j)),
            pl.BlockSpec((tm, d), lambda i, j: (i, 0)),
        ],
        out_shape=[jax.ShapeDtypeStruct((m, len(tiles) * tn), BF16), jax.ShapeDtypeStruct((m, d), BF16)],
        compiler_params=_cparams(("parallel", "arbitrary")),
        name="inproj",
    )(x2, nw, w)


def _gate_scalars(u_ref, wba_ref, alog_ref, dtb_ref, beta_ref, gc_ref, chunk, sub):
    heads = alog_ref.shape[1]
    parts = [slice(p * sub, (p + 1) * sub) for p in range(u_ref.shape[0] // sub)]
    row = lax.broadcasted_iota(jnp.int32, (sub, sub), 0)
    col = lax.broadcasted_iota(jnp.int32, (sub, sub), 1)
    incl = jnp.where(((row // chunk) == (col // chunk)) & (row >= col), 1.0, 0.0).astype(BF16)
    bas = [_dot_nt(u_ref[sl, :], wba_ref[...]) for sl in parts]
    gs = []
    for sl, ba in zip(parts, bas):
        beta_ref[sl, :] = _sigmoid(ba[:, 0:heads])
        a = ba[:, heads:2 * heads] + dtb_ref[...]
        softplus = jnp.maximum(a, 0.0) + jnp.log1p(jnp.exp(-jnp.abs(a)))
        gs.append(-jnp.exp(alog_ref[...]) * softplus)
    g1 = [g.astype(BF16) for g in gs]
    r1 = [g - p1.astype(F32) for g, p1 in zip(gs, g1)]
    g2 = [r.astype(BF16) for r in r1]
    g3 = [(r - p2.astype(F32)).astype(BF16) for r, p2 in zip(r1, g2)]
    gcs = [_dot(incl, p1) + _dot(incl, p2) + _dot(incl, p3) for p1, p2, p3 in zip(g1, g2, g3)]
    for sl, gc in zip(parts, gcs):
        gc_ref[sl, :] = gc


def _rotproj_kernel(u_ref, w_ref, cos_ref, sin_ref, wba_ref, alog_ref, dtb_ref, *rest,
                    dk, chunk, sub, cast_flags, cast_count, per_step):
    cast_in, (o_ref, beta_ref, gc_ref, cast_ref) = rest[:2 * per_step], rest[2 * per_step:]
    cast_tr = cast_ref.shape[0] // per_step
    res = _dot_nt(u_ref[...], w_ref[...])
    cos = cos_ref[...]
    sin = sin_ref[...]
    for hh in range(res.shape[1] // dk):
        sl = slice(hh * dk, (hh + 1) * dk)
        x = res[:, sl]
        o_ref[:, sl] = (x * cos + pltpu.roll(x, dk // 2, 1) * sin).astype(o_ref.dtype)
    _gate_scalars(u_ref, wba_ref, alog_ref, dtb_ref, beta_ref, gc_ref, chunk, sub)
    for kk in range(per_step):
        shifted = _nth(pl.program_id(0) * per_step + kk, cast_flags) > 0
        cast_ref[kk * cast_tr:(kk + 1) * cast_tr, :] = _cast_row_tile(cast_in[2 * kk], cast_in[2 * kk + 1],
                                                                     shifted, cast_count)


def _rotproj(u, w, cosf, sins, wba, alog, dtb, wt, cast_srcs, cast_tr, cast_count,
             tm, tn, tile, seq, dk, chunk, sub):
    m, d = u.shape
    h = alog.shape[1]
    steps = m // tm
    assert seq % tm == 0 and tm % sub == 0 and sub % chunk == 0 and len(cast_srcs) % steps == 0
    per_seq = seq // tm
    per_step = len(cast_srcs) // steps
    scalar_out = jax.ShapeDtypeStruct((m, h), F32)
    cast_flags = tuple(int(src % cast_tr != 0) for src in cast_srcs)
    cast_in_specs = []
    for kk in range(per_step):
        cast_in_specs += _row_tile_specs(cast_srcs, cast_tr, wt.shape[1], cast_count,
                                         lambda i, kk=kk: i * per_step + kk)
    return pl.pallas_call(
        functools.partial(_rotproj_kernel, dk=dk, chunk=chunk, sub=sub, cast_flags=cast_flags,
                          cast_count=cast_count, per_step=per_step),
        grid=(steps,),
        in_specs=[
            pl.BlockSpec((tm, d), lambda i: (i, 0)),
            pl.BlockSpec((tn, d), lambda i: (tile, 0)),
            pl.BlockSpec((tm, dk), lambda i: (i % per_seq, 0)),
            pl.BlockSpec((tm, dk), lambda i: (i % per_seq, 0)),
            pl.BlockSpec((2 * h, d), lambda i: (0, 0)),
            pl.BlockSpec((1, h), lambda i: (0, 0)),
            pl.BlockSpec((1, h), lambda i: (0, 0)),
        ] + cast_in_specs,
        out_specs=[pl.BlockSpec((tm, tn), lambda i: (i, 0))] + [pl.BlockSpec((tm, h), lambda i: (i, 0))] * 2
        + [pl.BlockSpec((per_step * cast_tr, wt.shape[1]), lambda i: (i, 0))],
        out_shape=[jax.ShapeDtypeStruct((m, tn), BF16), scalar_out, scalar_out,
                   jax.ShapeDtypeStruct((len(cast_srcs) * cast_tr, wt.shape[1]), BF16)],
        compiler_params=_cparams(("arbitrary",)),
        name="rotproj",
    )(u, w, cosf, sins, wba, alog, dtb, *([wt, wt] * per_step))


def _gates_kernel(u_ref, ws_ref, wg_ref, *rest):
    ncast = (len(rest) - 1) // 2
    cast_in, o_ref, cast_out = rest[:ncast], rest[ncast], rest[ncast + 1:]
    u = u_ref[...]
    a = _dot_nt(u, ws_ref[...])
    g = _dot_nt(u, wg_ref[...])
    o_ref[...] = (_silu(a) * (0.5 + 0.5 * jnp.tanh(0.5 * g))).astype(o_ref.dtype)
    for src, dst in zip(cast_in, cast_out):
        dst[...] = src[...].astype(dst.dtype)


def _cast_slabs(rows, nsteps):
    nb = nsteps
    while nb > 1 and (rows % nb or (rows // nb) % BF16_TILE_ROWS):
        nb //= 2
    return nb


def _gates(u, w, tm, tg, vd, pairs, to_cast):
    m, d = u.shape
    per = vd // tg
    nj = len(pairs) * per
    nsteps = (m // tm) * nj

    def blk(j, which):
        idx = pairs[0][which] // tg + j
        for p in range(1, len(pairs)):
            idx = jnp.where(j >= p * per, pairs[p][which] // tg + j - p * per, idx)
        return idx

    cast_specs = []
    for mat in to_cast:
        nb = _cast_slabs(mat.shape[0], nsteps)
        rep = nsteps // nb
        cast_specs.append(pl.BlockSpec((mat.shape[0] // nb, mat.shape[1]),
                                       lambda i, j, rep=rep: ((i * nj + j) // rep, 0)))

    outs = pl.pallas_call(
        _gates_kernel,
        grid=(m // tm, nj),
        in_specs=[
            pl.BlockSpec((tm, d), lambda i, j: (i, 0)),
            pl.BlockSpec((tg, d), lambda i, j: (blk(j, 0), 0)),
            pl.BlockSpec((tg, d), lambda i, j: (blk(j, 1), 0)),
        ] + cast_specs,
        out_specs=[pl.BlockSpec((tm, tg), lambda i, j: (i, j))] + cast_specs,
        out_shape=[jax.ShapeDtypeStruct((m, len(pairs) * vd), BF16)]
        + [jax.ShapeDtypeStruct(mat.shape, BF16) for mat in to_cast],
        compiler_params=_cparams(("arbitrary", "arbitrary")),
        name="gates",
    )(u, w, w, *to_cast)
    return outs[0], outs[1:]


def _gdn_prep_kernel(q_ref, qh_ref, k_ref, kh_ref, v_ref, vh_ref, cwq_ref, cwk_ref, cwv_ref,
                     beta_ref, gcc_ref, gcr_ref, pick_ref,
                     wq_ref, kpt_ref, u_ref, attn_ref, egl_ref, *, dk, dv, chunk, nchunks, heads):
    t = pl.program_id(1)
    rows = chunk * nchunks
    lane_head = lax.broadcasted_iota(jnp.int32, (1, beta_ref.shape[1]), 1)

    def head_col(ref, hh):
        hsel = lane_head == pl.program_id(2) * heads + hh
        return jnp.sum(jnp.where(hsel, ref[...], 0.0), axis=1, keepdims=True)

    pick = pick_ref[...]
    blk = pick.shape[0]
    win = HALO_ROWS + blk

    def conv_silu(x_ref, h_ref, cw_ref):
        halo = jnp.where(t == 0, jnp.zeros_like(h_ref[...]), h_ref[...])
        ext = jnp.concatenate([halo, x_ref[...]], axis=0)
        taps = [ext * cw_ref[tap:tap + 1, :].astype(BF16) for tap in range(CONV_WIDTH)]
        out = []
        for r in range(rows // blk):
            stacked = jnp.concatenate([tp[r * blk:r * blk + win] for tp in taps], axis=0)
            out.append(_silu(_dot(pick, stacked)))
        return jnp.concatenate(out, axis=0)

    def l2norm(y):
        return y * lax.rsqrt(jnp.sum(y * y, axis=-1, keepdims=True) + EPS)

    q_all = conv_silu(q_ref, qh_ref, cwq_ref)
    k_all = conv_silu(k_ref, kh_ref, cwk_ref)
    v_all = conv_silu(v_ref, vh_ref, cwv_ref)

    ri = lax.broadcasted_iota(jnp.int32, (chunk, 2 * chunk), 0)
    li = lax.broadcasted_iota(jnp.int32, (chunk, 2 * chunk), 1)
    first = li < chunk
    ci = jnp.where(first, li, li - chunk)
    incl = ri >= ci
    strict = ri > ci
    eye = jnp.where(ri == ci, 1.0, 0.0).astype(F32)
    ident_k = jnp.where(lax.broadcasted_iota(jnp.int32, (dk, dk), 0)
                        == lax.broadcasted_iota(jnp.int32, (dk, dk), 1), 1.0, 0.0).astype(BF16)

    def blockdiag(x):
        return jnp.concatenate([jnp.where(first, x, 0.0), jnp.where(first, 0.0, x)], axis=0).astype(BF16)

    def pair_rows(xa, xb):
        z = jnp.zeros_like(xa)
        return jnp.concatenate([jnp.concatenate([xa, z], axis=1), jnp.concatenate([z, xb], axis=1)], axis=0)

    npairs = nchunks // 2
    units = [(hh, p) for hh in range(heads) for p in range(npairs)]
    ur = range(len(units))
    qs, ks, vs, bs, gcs = [], [], [], [], []
    for hh in range(heads):
        q = l2norm(q_all[:, hh * dk:(hh + 1) * dk]) * (dk ** -0.5)
        k = l2norm(k_all[:, hh * dk:(hh + 1) * dk])
        v = v_all[:, hh * dv:(hh + 1) * dv]
        beta, gcum = head_col(beta_ref, hh), head_col(gcc_ref, hh)
        for c in range(nchunks):
            sl = slice(c * chunk, (c + 1) * chunk)
            qs.append(q[sl])
            ks.append(k[sl])
            vs.append(v[sl])
            bs.append(beta[sl])
            gcs.append(gcum[sl])
    nc = range(len(qs))
    gls = [gc[chunk - 1:chunk, :] for gc in gcs]
    egs = [jnp.exp(gc) for gc in gcs]
    kbs = [ks[c] * bs[c] for c in nc]
    decays = [jnp.exp(jnp.where(incl, jnp.where(first, gcs[2 * u], gcs[2 * u + 1]) - gcr_ref[0, hh, p], -1e30))
              for u, (hh, p) in enumerate(units)]
    lhs = [jnp.concatenate([jnp.concatenate([kbs[2 * u], kbs[2 * u + 1]], axis=1),
                            jnp.concatenate([qs[2 * u], qs[2 * u + 1]], axis=1)], axis=0).astype(BF16) for u in ur]
    kq = [_dot_nt(lhs[u], pair_rows(ks[2 * u], ks[2 * u + 1]).astype(BF16)) for u in ur]
    amats = [jnp.where(strict, kq[u][0:chunk] * decays[u], 0.0) for u in ur]
    for u, (hh, p) in enumerate(units):
        attn_ref[0, hh, p] = (kq[u][chunk:2 * chunk] * decays[u]).astype(attn_ref.dtype)
    tinvs = [eye - a for a in amats]
    apows = amats
    span = 2
    while span < chunk:
        apows = [_dot(apows[u].astype(BF16), blockdiag(apows[u])) for u in ur]
        tinvs = [tinvs[u] + _dot(tinvs[u].astype(BF16), blockdiag(apows[u])) for u in ur]
        span *= 2
    rhs = [jnp.concatenate([jnp.concatenate([kbs[c] * egs[c], vs[c] * bs[c]], axis=1)
                            for c in (2 * u, 2 * u + 1)], axis=0).astype(BF16) for u in ur]
    wu = [_dot(blockdiag(tinvs[u]), rhs[u]) for u in ur]
    kps = [jnp.concatenate([ks[c] * jnp.exp(gls[c] - gcs[c]) for c in (2 * u, 2 * u + 1)], axis=0).astype(BF16)
           for u in ur]
    kpt = [_dot_nt(ident_k, kps[u]) for u in ur]
    for u, (hh, p) in enumerate(units):
        kpt_ref[0, hh, p] = kpt[u].astype(kpt_ref.dtype)
        for half in range(2):
            c = 2 * u + half
            rs = slice(half * chunk, (half + 1) * chunk)
            wq_ref[0, hh, 2 * p + half, 0:chunk, :] = wu[u][rs, 0:dk].astype(wq_ref.dtype)
            wq_ref[0, hh, 2 * p + half, chunk:2 * chunk, :] = (qs[c] * egs[c]).astype(wq_ref.dtype)
            u_ref[0, hh, 2 * p + half] = wu[u][rs, dk:dk + dv].astype(u_ref.dtype)
            egl_ref[0, hh, 2 * p + half] = jnp.broadcast_to(jnp.exp(gls[c]), (1, dv))


def _gdn_prep(proj, conv_w, beta_c, gc_c, gc_r, dims, rows, heads):
    b, s, h, dk, dv = dims
    chunk = GDN_CHUNK
    nchunks = rows // chunk
    n = s // chunk
    qk = h * dk
    tiles = s // rows
    kw, vw = heads * dk, heads * dv
    kcol = qk // kw
    vcol = 2 * qk // vw

    blk = _pick(rows, CONV_BLOCK_ROWS)
    win = HALO_ROWS + blk
    out_row = np.arange(blk)[:, None]
    col = np.arange(CONV_WIDTH * win)[None, :]
    pick = jnp.asarray(col % win == HALO_ROWS + out_row - (CONV_WIDTH - 1 - col // win), dtype=BF16)

    def rowblk(bi, ti):
        return bi * tiles + ti

    def halo(bi, ti):
        return jnp.maximum((bi * s + ti * rows) // HALO_ROWS - 1, 0)

    col_spec = pl.BlockSpec((rows, h), lambda bi, ti, hi: (rowblk(bi, ti), 0))
    in_specs = [
        pl.BlockSpec((rows, kw), lambda bi, ti, hi: (rowblk(bi, ti), hi)),
        pl.BlockSpec((HALO_ROWS, kw), lambda bi, ti, hi: (halo(bi, ti), hi)),
        pl.BlockSpec((rows, kw), lambda bi, ti, hi: (rowblk(bi, ti), kcol + hi)),
        pl.BlockSpec((HALO_ROWS, kw), lambda bi, ti, hi: (halo(bi, ti), kcol + hi)),
        pl.BlockSpec((rows, vw), lambda bi, ti, hi: (rowblk(bi, ti), vcol + hi)),
        pl.BlockSpec((HALO_ROWS, vw), lambda bi, ti, hi: (halo(bi, ti), vcol + hi)),
        pl.BlockSpec((CONV_WIDTH, kw), lambda bi, ti, hi: (0, hi)),
        pl.BlockSpec((CONV_WIDTH, kw), lambda bi, ti, hi: (0, kcol + hi)),
        pl.BlockSpec((CONV_WIDTH, vw), lambda bi, ti, hi: (0, vcol + hi)),
        col_spec, col_spec,
        pl.BlockSpec((1, heads, nchunks // 2, 1, 2 * chunk), lambda bi, ti, hi: (bi, hi, ti, 0, 0)),
        pl.BlockSpec(pick.shape, lambda bi, ti, hi: (0, 0)),
    ]

    def out5(r, c):
        return pl.BlockSpec((1, heads, nchunks, r, c), lambda bi, ti, hi: (bi, hi, ti, 0, 0))

    def pair5(r, c):
        return pl.BlockSpec((1, heads, nchunks // 2, r, c), lambda bi, ti, hi: (bi, hi, ti, 0, 0))

    out_specs = [out5(2 * chunk, dk), pair5(dk, 2 * chunk), out5(chunk, dv), pair5(chunk, 2 * chunk), out5(1, dv)]
    out_shape = [
        jax.ShapeDtypeStruct((b, h, n, 2 * chunk, dk), BF16),
        jax.ShapeDtypeStruct((b, h, n // 2, dk, 2 * chunk), BF16),
        jax.ShapeDtypeStruct((b, h, n, chunk, dv), BF16),
        jax.ShapeDtypeStruct((b, h, n // 2, chunk, 2 * chunk), BF16),
        jax.ShapeDtypeStruct((b, h, n, 1, dv), F32),
    ]
    return pl.pallas_call(
        functools.partial(_gdn_prep_kernel, dk=dk, dv=dv, chunk=chunk, nchunks=nchunks, heads=heads),
        grid=(b, tiles, h // heads),
        in_specs=in_specs,
        out_specs=out_specs,
        out_shape=out_shape,
        compiler_params=_cparams(("parallel", "parallel", "parallel")),
        name="gdn_prep",
    )(proj, proj, proj, proj, proj, proj, conv_w, conv_w, conv_w, beta_c, gc_c, gc_r, pick)


def _pair_pad(x, half):
    z = jnp.zeros_like(x)
    return jnp.concatenate([x, z] if half == 0 else [z, x], axis=0)


def _gdn_scan_kernel(wq_ref, kpt_ref, u_ref, attn_ref, egl_ref, f_ref, nw_ref, o_ref, s_scr,
                     *, dv, chunk, nchunks, unroll, heads, batch):
    @pl.when(pl.program_id(1) == 0)
    def _():
        s_scr[...] = jnp.zeros_like(s_scr)

    nw = nw_ref[...]
    seqs = [(bb, hh) for bb in range(batch) for hh in range(heads)]
    sr = range(len(seqs))

    def chunk_group(it, carry):
        for cc in range(unroll):
            c = it * unroll + cc
            rs = pl.ds(pl.multiple_of(c * chunk, chunk), chunk)
            states = [s_scr[i] for i in sr]
            r = [_dot(wq_ref[bb, hh, c], states[i].astype(BF16)) for i, (bb, hh) in enumerate(seqs)]
            v_new = [_pair_pad((u_ref[bb, hh, c] - r[i][0:chunk]).astype(BF16), cc)
                     for i, (bb, hh) in enumerate(seqs)]
            upd = [_dot(kpt_ref[bb, hh, it], v_new[i]) for i, (bb, hh) in enumerate(seqs)]
            for i, (bb, hh) in enumerate(seqs):
                s_scr[i] = states[i] * egl_ref[bb, hh, c] + upd[i]
            o = [r[i][chunk:2 * chunk] + _dot(attn_ref[bb, hh, it], v_new[i]) for i, (bb, hh) in enumerate(seqs)]
            for i, (bb, hh) in enumerate(seqs):
                cs = slice(hh * dv, (hh + 1) * dv)
                on = o[i] * lax.rsqrt(jnp.mean(o[i] * o[i], axis=-1, keepdims=True) + EPS) * nw
                o_ref[bb, rs, cs] = (on * f_ref[bb, rs, cs].astype(F32)).astype(o_ref.dtype)
        return carry

    lax.fori_loop(0, nchunks // unroll, chunk_group, 0)


def _gdn_scan(wq, kpt, u, attn, egl, gates, gdn_nw, dims, nchunks, unroll, heads, fcol):
    b, s, h, dk, dv = dims
    chunk = GDN_CHUNK
    n = s // chunk
    rows = nchunks * chunk
    steps = n // nchunks
    width = heads * dv
    assert nchunks % unroll == 0 and unroll == 2

    def in5(r, c):
        return pl.BlockSpec((b, heads, nchunks, r, c), lambda gi, i: (0, gi, i, 0, 0))

    def pair5(r, c):
        return pl.BlockSpec((b, heads, nchunks // 2, r, c), lambda gi, i: (0, gi, i, 0, 0))

    in_specs = [
        in5(2 * chunk, dk), pair5(dk, 2 * chunk), in5(chunk, dv), pair5(chunk, 2 * chunk), in5(1, dv),
        pl.BlockSpec((b, rows, width), lambda gi, i: (0, i, fcol // width + gi)),
        pl.BlockSpec((1, dv), lambda gi, i: (0, 0)),
    ]
    return pl.pallas_call(
        functools.partial(_gdn_scan_kernel, dv=dv, chunk=chunk, nchunks=nchunks, unroll=unroll, heads=heads,
                          batch=b),
        grid=(h // heads, steps),
        in_specs=in_specs,
        out_specs=pl.BlockSpec((b, rows, width), lambda gi, i: (0, i, gi)),
        out_shape=jax.ShapeDtypeStruct((b, s, h * dv), BF16),
        scratch_shapes=[pltpu.VMEM((b * heads, dk, dv), F32)],
        compiler_params=_cparams(("parallel", "arbitrary")),
        name="gdn_scan",
    )(wq, kpt, u, attn, egl, gates, gdn_nw)


def _retention_kernel(q_ref, k_ref, v_ref, f_ref, ma_ref, dmat_ref,
                      xi_ref, zeta_ref, gpow_ref, nw_ref, o_ref, s_scr, *, dk, dv, heads, sub):
    @pl.when(pl.program_id(2) == 0)
    def _():
        s_scr[...] = jnp.zeros_like(s_scr)

    hr = range(heads)
    ksl = [slice(hh * dk, (hh + 1) * dk) for hh in hr]
    vsl = [slice(hh * dv, (hh + 1) * dv) for hh in hr]

    def sub_chunk(it, carry):
        rs = pl.ds(pl.multiple_of(it * sub, sub), sub)
        q = [q_ref[rs, ksl[hh]] for hh in hr]
        k = [k_ref[rs, ksl[hh]] for hh in hr]
        kz = [(k[hh].astype(F32) * zeta_ref[hh]).astype(BF16) for hh in hr]
        states = [s_scr[hh] for hh in hr]
        scores = [(_dot_nt(q[hh], k[hh]) * dmat_ref[hh]).astype(BF16) for hh in hr]
        lhs = [jnp.concatenate([scores[hh], (q[hh].astype(F32) * xi_ref[hh]).astype(BF16)], axis=1) for hh in hr]
        rhs = [jnp.concatenate([v_ref[rs, vsl[hh]], states[hh].astype(BF16)], axis=0) for hh in hr]
        upd = [_dot_tn(kz[hh], v_ref[rs, vsl[hh]]) for hh in hr]
        for hh in hr:
            s_scr[hh] = states[hh] * gpow_ref[hh] + upd[hh]
        o = [_dot(lhs[hh], rhs[hh]) for hh in hr]
        for hh in hr:
            mu = jnp.mean(o[hh], axis=-1, keepdims=True)
            dev = o[hh] - mu
            var = jnp.mean(dev * dev, axis=-1, keepdims=True)
            ob = dev * lax.rsqrt(var + EPS) * nw_ref[:, vsl[hh]]
            mixed = ob * f_ref[rs, vsl[hh]].astype(F32) + ma_ref[rs, vsl[hh]].astype(F32)
            o_ref[rs, vsl[hh]] = mixed.astype(o_ref.dtype)
        return carry

    lax.fori_loop(0, q_ref.shape[0] // sub, sub_chunk, 0)


def _retention(qk, proj, gates, ma, ret_nw, tables, dims, rows, sub, heads, cols):
    b, s, h, dk, dv = dims
    dmat, xi, zeta, gpow = tables
    rqcol, rkcol, rvcol, fcol = cols
    steps = s // rows
    kw, vw = heads * dk, heads * dv
    assert rows % sub == 0

    def rb(bi, i):
        return bi * steps + i

    in_specs = [
        pl.BlockSpec((rows, kw), lambda bi, gi, i: (rb(bi, i), rqcol // kw + gi)),
        pl.BlockSpec((rows, kw), lambda bi, gi, i: (rb(bi, i), rkcol // kw + gi)),
        pl.BlockSpec((rows, vw), lambda bi, gi, i: (rb(bi, i), rvcol // vw + gi)),
        pl.BlockSpec((rows, vw), lambda bi, gi, i: (rb(bi, i), fcol // vw + gi)),
        pl.BlockSpec((rows, vw), lambda bi, gi, i: (rb(bi, i), gi)),
        pl.BlockSpec((heads, sub, sub), lambda bi, gi, i: (gi, 0, 0)),
        pl.BlockSpec((heads, sub, dk), lambda bi, gi, i: (gi, 0, 0)),
        pl.BlockSpec((heads, sub, dk), lambda bi, gi, i: (gi, 0, 0)),
        pl.BlockSpec((heads, 1, dv), lambda bi, gi, i: (gi, 0, 0)),
        pl.BlockSpec((1, vw), lambda bi, gi, i: (0, gi)),
    ]
    return pl.pallas_call(
        functools.partial(_retention_kernel, dk=dk, dv=dv, heads=heads, sub=sub),
        grid=(b, h // heads, steps),
        in_specs=in_specs,
        out_specs=pl.BlockSpec((rows, vw), lambda bi, gi, i: (rb(bi, i), gi)),
        out_shape=jax.ShapeDtypeStruct((b * s, h * dv), BF16),
        scratch_shapes=[pltpu.VMEM((heads, dk, dv), F32)],
        compiler_params=_cparams(("parallel", "parallel", "arbitrary")),
        name="retention",
    )(qk, qk, proj, gates, ma, dmat, xi, zeta, gpow, ret_nw)


def _rotary_tables(s, dk, rows):
    inv = ROPE_BASE ** (-jnp.arange(0, dk, 2, dtype=F32) / dk)
    inv = jnp.concatenate([inv, inv])
    sign = jnp.concatenate([-jnp.ones(dk // 2, F32), jnp.ones(dk // 2, F32)])
    lo = jnp.arange(rows, dtype=F32)[:, None] * inv[None, :]
    hi = (jnp.arange(s // rows, dtype=F32) * rows)[:, None] * inv[None, :]
    cl, sl, ch, sh = jnp.cos(lo), jnp.sin(lo), jnp.cos(hi)[:, None], jnp.sin(hi)[:, None]
    cos = (ch * cl - sh * sl).reshape(s, dk)
    sin = ((sh * cl + ch * sl) * sign).reshape(s, dk)
    return cos, sin


def _retention_tables(h, dk, dv, rows):
    log_gamma = jnp.log1p(-jnp.exp2(-5.0 - jnp.arange(h, dtype=F32)))
    pos = jnp.arange(rows, dtype=F32)
    dist = pos[:, None] - pos[None, :]
    kscale = dk ** -0.5
    dmat = jnp.exp(jnp.where(dist >= 0, dist * log_gamma[:, None, None], -jnp.inf)) * kscale
    xi = jnp.broadcast_to(jnp.exp((pos + 1.0) * log_gamma[:, None])[:, :, None], (h, rows, dk))
    zeta = jnp.broadcast_to(jnp.exp((rows - 1.0 - pos) * log_gamma[:, None])[:, :, None] * kscale, (h, rows, dk))
    gpow = jnp.broadcast_to(jnp.exp(rows * log_gamma)[:, None, None], (h, 1, dv))
    return dmat, xi, zeta, gpow


def _outproj_kernel(m_ref, w_ref, x_ref, nw_ref, h_ref, hn_ref):
    hres = x_ref[...] + _dot(m_ref[...], w_ref[...])
    h_ref[...] = hres
    ms = jnp.mean(hres * hres, axis=-1, keepdims=True)
    hn_ref[...] = ((hres * lax.rsqrt(ms + EPS)) * nw_ref[...]).astype(hn_ref.dtype)


def _outproj(mixed, w, x2, nw, tm):
    m, d = x2.shape
    kdim = mixed.shape[1]
    return pl.pallas_call(
        _outproj_kernel,
        grid=(m // tm,),
        in_specs=[
            pl.BlockSpec((tm, kdim), lambda i: (i, 0)),
            pl.BlockSpec((kdim, d), lambda i: (0, 0)),
            pl.BlockSpec((tm, d), lambda i: (i, 0)),
            pl.BlockSpec((1, d), lambda i: (0, 0)),
        ],
        out_specs=[pl.BlockSpec((tm, d), lambda i: (i, 0))] * 2,
        out_shape=[jax.ShapeDtypeStruct((m, d), F32), jax.ShapeDtypeStruct((m, d), BF16)],
        compiler_params=_cparams(("parallel",)),
        name="outproj",
    )(mixed, w, x2, nw)


def _ffn_kernel(hn_ref, h_ref, wg_ref, wu_ref, wd_ref, nw_ref, o_ref):
    f = pl.program_id(1)

    @pl.when(f == 0)
    def _():
        o_ref[...] = h_ref[...]

    hn = hn_ref[...]
    g = _dot(hn, wg_ref[...])
    up = _dot(hn, wu_ref[...])
    act = (_silu(g) * up).astype(BF16)
    o_ref[...] += _dot(act, wd_ref[...])

    @pl.when(f == pl.num_programs(1) - 1)
    def _():
        hres = o_ref[...]
        ms = jnp.mean(hres * hres, axis=-1, keepdims=True)
        o_ref[...] = (hres * lax.rsqrt(ms + EPS)) * nw_ref[...]


def _ffn(hn, hres, wg, wu, wd, nw, tm, tf):
    m, d = hres.shape
    ff = wg.shape[1]
    return pl.pallas_call(
        _ffn_kernel,
        grid=(m // tm, ff // tf),
        in_specs=[
            pl.BlockSpec((tm, d), lambda i, j: (i, 0)),
            pl.BlockSpec((tm, d), lambda i, j: (i, 0)),
            pl.BlockSpec((d, tf), lambda i, j: (0, j)),
            pl.BlockSpec((d, tf), lambda i, j: (0, j)),
            pl.BlockSpec((tf, d), lambda i, j: (j, 0)),
            pl.BlockSpec((1, d), lambda i, j: (0, 0)),
        ],
        out_specs=pl.BlockSpec((tm, d), lambda i, j: (i, 0)),
        out_shape=jax.ShapeDtypeStruct((m, d), F32),
        compiler_params=_cparams(("parallel", "arbitrary"), FFN_VMEM_LIMIT_BYTES),
        name="ffn",
    )(hn, hres, wg, wu, wd, nw)


def _pick(total, want):
    t = min(total, want)
    while total % t:
        t //= 2
    return t


def _layer(h2, dims, norm1_w, w_in, conv_w, a_log, dt_bias, gdn_norm_w, ret_norm_w,
           w_out, norm2_w, w_gate, w_up, w_down, out_norm_w):
    b, s, h, dk, dv = dims
    m, d = h2.shape
    qk, vd = h * dk, h * dv
    chunk = GDN_CHUNK
    n = s // chunk

    assert 2 * qk == vd
    tn = vd
    sc0 = 3 * vd
    sc = 2 * h
    w_in_t = w_in.T
    w_ba = w_in_t[sc0:sc0 + sc].astype(BF16)
    tr = _pick(vd, 1024)
    per = vd // tr
    w_proj = _cast_rows(w_in_t, [g * vd + k * tr for g in (0, 1) for k in range(per)]
                        + [sc0 + sc + g * vd + k * tr for g in (0, 1) for k in range(per)], tr, sc)

    proj, u = _inproj(h2, norm1_w.reshape(1, d), w_proj, _pick(m, 1024), tn, (0, 1, 3))
    ret_sub = _pick(s, 256)
    cosf, sins = _rotary_tables(s, dk, ret_sub)
    rot_tm = _pick(s, 1024)
    gtr = _pick(vd, (4 * vd) // (m // rot_tm))
    gper = vd // gtr
    gate_srcs = ([2 * vd + k * gtr for k in range(gper)]
                 + [sc0 + sc + g * vd + k * gtr for g in (2, 3, 4) for k in range(gper)])
    qk_rot, beta, gcum, w_gate_rows = _rotproj(
        u, w_proj, cosf, sins, w_ba, a_log.reshape(1, h), dt_bias.reshape(1, h), w_in_t, gate_srcs, gtr, sc,
        rot_tm, tn, 2, s, dk, chunk, _pick(s, 256))
    gates, (wg_b, wu_b, wd_b, wo_b) = _gates(u, w_gate_rows, _pick(m, 1024), _pick(vd, 1024), vd,
                                             ((0, 2 * vd), (vd, 3 * vd)), [w_gate, w_up, w_down, w_out])

    gc_r = gcum.reshape(b, n // 2, 2 * chunk, h).transpose(0, 3, 1, 2)[:, :, :, None, :]
    wq, kpt, uu, attn, egl = _gdn_prep(proj, conv_w, beta, gcum, gc_r, dims, _pick(s, 512), _pick(h, 8))

    ma = _gdn_scan(wq, kpt, uu, attn, egl, gates.reshape(b, s, 2 * vd), gdn_norm_w.reshape(1, dv), dims,
                   _pick(n, 8), _pick(n, 2), _pick(h, 8), 0).reshape(m, vd)

    tables = _retention_tables(h, dk, dv, ret_sub)
    mixed = _retention(qk_rot, proj, gates, ma, ret_norm_w.reshape(1, vd), tables, dims, _pick(s, 512),
                       ret_sub, _pick(h, 8), (0, qk, 2 * tn, vd))

    hres, hn = _outproj(mixed, wo_b, h2, norm2_w.reshape(1, d), _pick(m, 512))

    return _ffn(hn, hres, wg_b, wu_b, wd_b, out_norm_w.reshape(1, d), _pick(m, 1024), _pick(w_gate.shape[1], 512))


def kernel(x, norm1_w, w_in, conv_w, a_log, dt_bias, gdn_norm_w, ret_norm_w, w_out, norm2_w,
           w_gate, w_up, w_down, norm_f_w):
    b, s, d = x.shape
    depth, h = a_log.shape
    assert depth == 1, "the final norm is fused into the single layer's FFN kernel"
    dk = d // 16
    dv = d // h
    dims = (b, s, h, dk, dv)
    out = _layer(x.reshape(b * s, d), dims, norm1_w[0], w_in[0], conv_w[0], a_log[0], dt_bias[0],
                 gdn_norm_w[0], ret_norm_w[0], w_out[0], norm2_w[0], w_gate[0], w_up[0], w_down[0],
                 norm_f_w)
    return out.reshape(b, s, d)
```

```python
import functools

import jax
import jax.numpy as jnp
import numpy as np
from jax import lax
from jax.experimental import pallas as pl
from jax.experimental.pallas import tpu as pltpu

F32 = jnp.float32
BF16 = jnp.bfloat16

EPS = 1e-6
CONV_WIDTH = 4
GDN_CHUNK = 64
ROPE_BASE = 10000.0
LANE = 128
BF16_TILE_ROWS = 16
HALO_ROWS = BF16_TILE_ROWS
CONV_BLOCK_ROWS = LANE - HALO_ROWS
VMEM_LIMIT_BYTES = 56 * 1024 * 1024
FFN_VMEM_LIMIT_BYTES = 60 * 1024 * 1024


def _cparams(sem, vmem_limit=VMEM_LIMIT_BYTES):
    return pltpu.CompilerParams(dimension_semantics=sem, vmem_limit_bytes=vmem_limit)


def _sigmoid(x):
    return 1.0 / (1.0 + jnp.exp(-x))


def _silu(x):
    h = 0.5 * x
    return h + h * jnp.tanh(h)


def _dot(a, b):
    return jnp.dot(a, b, preferred_element_type=F32)


def _dot_nt(a, b):
    return lax.dot_general(a, b, (((1,), (1,)), ((), ())), preferred_element_type=F32)


def _dot_tn(a, b):
    return lax.dot_general(a, b, (((0,), (0,)), ((), ())), preferred_element_type=F32)


def _nth(j, values):
    idx = values[0]
    for p in range(1, len(values)):
        idx = jnp.where(j >= p, values[p], idx)
    return idx


def _row_tile_specs(srcs, tr, d, count, index):
    assert all(src % tr in (0, count) for src in srcs) and tr % count == 0 and count % BF16_TILE_ROWS == 0
    bases = tuple(src // tr for src in srcs)
    nexts = tuple((src // tr + 1) * (tr // count) for src in srcs)
    return [pl.BlockSpec((tr, d), lambda *g: (_nth(index(*g), bases), 0)),
            pl.BlockSpec((count, d), lambda *g: (_nth(index(*g), nexts), 0))]


def _cast_row_tile(cur_ref, nxt_ref, shifted, count):
    cur = cur_ref[...]
    moved = jnp.concatenate([cur[count:], nxt_ref[...]], axis=0)
    return jnp.where(shifted, moved, cur).astype(BF16)


def _cast_rows_kernel(cur_ref, nxt_ref, o_ref, *, flags, count):
    o_ref[...] = _cast_row_tile(cur_ref, nxt_ref, _nth(pl.program_id(0), flags) > 0, count)


def _cast_rows(wt, srcs, tr, count):
    d = wt.shape[1]
    flags = tuple(int(src % tr != 0) for src in srcs)
    return pl.pallas_call(
        functools.partial(_cast_rows_kernel, flags=flags, count=count),
        grid=(len(srcs),),
        in_specs=_row_tile_specs(srcs, tr, d, count, lambda j: j),
        out_specs=pl.BlockSpec((tr, d), lambda j: (j, 0)),
        out_shape=jax.ShapeDtypeStruct((len(srcs) * tr, d), BF16),
        compiler_params=_cparams(("parallel",)),
        name="cast_rows",
    )(wt, wt)


def _inproj_kernel(x_ref, nw_ref, w_ref, o_ref, u_ref):
    @pl.when(pl.program_id(1) == 0)
    def _():
        xf = x_ref[...]
        ms = jnp.mean(xf * xf, axis=-1, keepdims=True)
        u_ref[...] = ((xf * lax.rsqrt(ms + EPS)) * nw_ref[...]).astype(u_ref.dtype)

    o_ref[...] = _dot_nt(u_ref[...], w_ref[...]).astype(o_ref.dtype)


def _inproj(x2, nw, w, tm, tn, tiles):
    m, d = x2.shape
    return pl.pallas_call(
        _inproj_kernel,
        grid=(m // tm, len(tiles)),
        in_specs=[
            pl.BlockSpec((tm, d), lambda i, j: (i, 0)),
            pl.BlockSpec((1, d), lambda i, j: (0, 0)),
            pl.BlockSpec((tn, d), lambda i, j: (_nth(j, tiles), 0)),
        ],
        out_specs=[
            pl.BlockSpec((tm, tn), lambda i, j: (i, j)),
            pl.BlockSpec((tm, d), lambda i, j: (i, 0)),
        ],
        out_shape=[jax.ShapeDtypeStruct((m, len(tiles) * tn), BF16), jax.ShapeDtypeStruct((m, d), BF16)],
        compiler_params=_cparams(("parallel", "arbitrary")),
        name="inproj",
    )(x2, nw, w)


def _gate_scalars(u_ref, wba_ref, alog_ref, dtb_ref, beta_ref, gc_ref, chunk, sub):
    heads = alog_ref.shape[1]
    parts = [slice(p * sub, (p + 1) * sub) for p in range(u_ref.shape[0] // sub)]
    row = lax.broadcasted_iota(jnp.int32, (sub, sub), 0)
    col = lax.broadcasted_iota(jnp.int32, (sub, sub), 1)
    incl = jnp.where(((row // chunk) == (col // chunk)) & (row >= col), 1.0, 0.0).astype(BF16)
    bas = [_dot_nt(u_ref[sl, :], wba_ref[...]) for sl in parts]
    gs = []
    for sl, ba in zip(parts, bas):
        beta_ref[sl, :] = _sigmoid(ba[:, 0:heads])
        a = ba[:, heads:2 * heads] + dtb_ref[...]
        softplus = jnp.maximum(a, 0.0) + jnp.log1p(jnp.exp(-jnp.abs(a)))
        gs.append(-jnp.exp(alog_ref[...]) * softplus)
    g1 = [g.astype(BF16) for g in gs]
    r1 = [g - p1.astype(F32) for g, p1 in zip(gs, g1)]
    g2 = [r.astype(BF16) for r in r1]
    g3 = [(r - p2.astype(F32)).astype(BF16) for r, p2 in zip(r1, g2)]
    gcs = [_dot(incl, p1) + _dot(incl, p2) + _dot(incl, p3) for p1, p2, p3 in zip(g1, g2, g3)]
    for sl, gc in zip(parts, gcs):
        gc_ref[sl, :] = gc


def _rotproj_kernel(u_ref, w_ref, cos_ref, sin_ref, wba_ref, alog_ref, dtb_ref, *rest,
                    dk, chunk, sub, cast_flags, cast_count, per_step):
    cast_in, (o_ref, beta_ref, gc_ref, cast_ref) = rest[:2 * per_step], rest[2 * per_step:]
    cast_tr = cast_ref.shape[0] // per_step
    res = _dot_nt(u_ref[...], w_ref[...])
    cos = cos_ref[...]
    sin = sin_ref[...]
    for hh in range(res.shape[1] // dk):
        sl = slice(hh * dk, (hh + 1) * dk)
        x = res[:, sl]
        o_ref[:, sl] = (x * cos + pltpu.roll(x, dk // 2, 1) * sin).astype(o_ref.dtype)
    _gate_scalars(u_ref, wba_ref, alog_ref, dtb_ref, beta_ref, gc_ref, chunk, sub)
    for kk in range(per_step):
        shifted = _nth(pl.program_id(0) * per_step + kk, cast_flags) > 0
        cast_ref[kk * cast_tr:(kk + 1) * cast_tr, :] = _cast_row_tile(cast_in[2 * kk], cast_in[2 * kk + 1],
                                                                     shifted, cast_count)


def _rotproj(u, w, cosf, sins, wba, alog, dtb, wt, cast_srcs, cast_tr, cast_count,
             tm, tn, tile, seq, dk, chunk, sub):
    m, d = u.shape
    h = alog.shape[1]
    steps = m // tm
    assert seq % tm == 0 and tm % sub == 0 and sub % chunk == 0 and len(cast_srcs) % steps == 0
    per_seq = seq // tm
    per_step = len(cast_srcs) // steps
    scalar_out = jax.ShapeDtypeStruct((m, h), F32)
    cast_flags = tuple(int(src % cast_tr != 0) for src in cast_srcs)
    cast_in_specs = []
    for kk in range(per_step):
        cast_in_specs += _row_tile_specs(cast_srcs, cast_tr, wt.shape[1], cast_count,
                                         lambda i, kk=kk: i * per_step + kk)
    return pl.pallas_call(
        functools.partial(_rotproj_kernel, dk=dk, chunk=chunk, sub=sub, cast_flags=cast_flags,
                          cast_count=cast_count, per_step=per_step),
        grid=(steps,),
        in_specs=[
            pl.BlockSpec((tm, d), lambda i: (i, 0)),
            pl.BlockSpec((tn, d), lambda i: (tile, 0)),
            pl.BlockSpec((tm, dk), lambda i: (i % per_seq, 0)),
            pl.BlockSpec((tm, dk), lambda i: (i % per_seq, 0)),
            pl.BlockSpec((2 * h, d), lambda i: (0, 0)),
            pl.BlockSpec((1, h), lambda i: (0, 0)),
            pl.BlockSpec((1, h), lambda i: (0, 0)),
        ] + cast_in_specs,
        out_specs=[pl.BlockSpec((tm, tn), lambda i: (i, 0))] + [pl.BlockSpec((tm, h), lambda i: (i, 0))] * 2
        + [pl.BlockSpec((per_step * cast_tr, wt.shape[1]), lambda i: (i, 0))],
        out_shape=[jax.ShapeDtypeStruct((m, tn), BF16), scalar_out, scalar_out,
                   jax.ShapeDtypeStruct((len(cast_srcs) * cast_tr, wt.shape[1]), BF16)],
        compiler_params=_cparams(("arbitrary",)),
        name="rotproj",
    )(u, w, cosf, sins, wba, alog, dtb, *([wt, wt] * per_step))


def _gates_kernel(u_ref, ws_ref, wg_ref, *rest):
    ncast = (len(rest) - 1) // 2
    cast_in, o_ref, cast_out = rest[:ncast], rest[ncast], rest[ncast + 1:]
    u = u_ref[...]
    a = _dot_nt(u, ws_ref[...])
    g = _dot_nt(u, wg_ref[...])
    o_ref[...] = (_silu(a) * (0.5 + 0.5 * jnp.tanh(0.5 * g))).astype(o_ref.dtype)
    for src, dst in zip(cast_in, cast_out):
        dst[...] = src[...].astype(dst.dtype)


def _cast_slabs(rows, nsteps):
    nb = nsteps
    while nb > 1 and (rows % nb or (rows // nb) % BF16_TILE_ROWS):
        nb //= 2
    return nb


def _gates(u, w, tm, tg, vd, pairs, to_cast):
    m, d = u.shape
    per = vd // tg
    nj = len(pairs) * per
    nsteps = (m // tm) * nj

    def blk(j, which):
        idx = pairs[0][which] // tg + j
        for p in range(1, len(pairs)):
            idx = jnp.where(j >= p * per, pairs[p][which] // tg + j - p * per, idx)
        return idx

    cast_specs = []
    for mat in to_cast:
        nb = _cast_slabs(mat.shape[0], nsteps)
        rep = nsteps // nb
        cast_specs.append(pl.BlockSpec((mat.shape[0] // nb, mat.shape[1]),
                                       lambda i, j, rep=rep: ((i * nj + j) // rep, 0)))

    outs = pl.pallas_call(
        _gates_kernel,
        grid=(m // tm, nj),
        in_specs=[
            pl.BlockSpec((tm, d), lambda i, j: (i, 0)),
            pl.BlockSpec((tg, d), lambda i, j: (blk(j, 0), 0)),
            pl.BlockSpec((tg, d), lambda i, j: (blk(j, 1), 0)),
        ] + cast_specs,
        out_specs=[pl.BlockSpec((tm, tg), lambda i, j: (i, j))] + cast_specs,
        out_shape=[jax.ShapeDtypeStruct((m, len(pairs) * vd), BF16)]
        + [jax.ShapeDtypeStruct(mat.shape, BF16) for mat in to_cast],
        compiler_params=_cparams(("arbitrary", "arbitrary")),
        name="gates",
    )(u, w, w, *to_cast)
    return outs[0], outs[1:]


def _gdn_prep_kernel(q_ref, qh_ref, k_ref, kh_ref, v_ref, vh_ref, cwq_ref, cwk_ref, cwv_ref,
                     beta_ref, gcc_ref, gcr_ref, pick_ref, pick_tail_ref,
                     wq_ref, kpt_ref, u_ref, attn_ref, egl_ref, *, dk, dv, chunk, nchunks, heads):
    t = pl.program_id(1)
    rows = chunk * nchunks
    lane_head = lax.broadcasted_iota(jnp.int32, (1, beta_ref.shape[1]), 1)

    def head_col(ref, hh):
        hsel = lane_head == pl.program_id(2) * heads + hh
        return jnp.sum(jnp.where(hsel, ref[...], 0.0), axis=1, keepdims=True)

    blk = pick_ref.shape[0]
    blocks = [(r0, pick_ref) for r0 in range(0, rows - blk + 1, blk)]
    if rows % blk:
        blocks.append((rows - rows % blk, pick_tail_ref))

    def conv_silu(x_ref, h_ref, cw_ref):
        halo = jnp.where(t == 0, jnp.zeros_like(h_ref[...]), h_ref[...])
        ext = jnp.concatenate([halo, x_ref[...]], axis=0)
        taps = [ext * cw_ref[tap:tap + 1, :].astype(BF16) for tap in range(CONV_WIDTH)]
        out = []
        for r0, pk in blocks:
            win = HALO_ROWS + pk.shape[0]
            stacked = jnp.concatenate([tp[r0:r0 + win] for tp in taps], axis=0)
            out.append(_silu(_dot(pk[...], stacked)))
        return jnp.concatenate(out, axis=0)

    def l2norm(y):
        return y * lax.rsqrt(jnp.sum(y * y, axis=-1, keepdims=True) + EPS)

    q_all = conv_silu(q_ref, qh_ref, cwq_ref)
    k_all = conv_silu(k_ref, kh_ref, cwk_ref)
    v_all = conv_silu(v_ref, vh_ref, cwv_ref)

    ri = lax.broadcasted_iota(jnp.int32, (chunk, 2 * chunk), 0)
    li = lax.broadcasted_iota(jnp.int32, (chunk, 2 * chunk), 1)
    first = li < chunk
    ci = jnp.where(first, li, li - chunk)
    incl = ri >= ci
    strict = ri > ci
    eye = jnp.where(ri == ci, 1.0, 0.0).astype(F32)
    ident_k = jnp.where(lax.broadcasted_iota(jnp.int32, (dk, dk), 0)
                        == lax.broadcasted_iota(jnp.int32, (dk, dk), 1), 1.0, 0.0).astype(BF16)

    def blockdiag(x):
        return jnp.concatenate([jnp.where(first, x, 0.0), jnp.where(first, 0.0, x)], axis=0).astype(BF16)

    def pair_rows(xa, xb):
        z = jnp.zeros_like(xa)
        return jnp.concatenate([jnp.concatenate([xa, z], axis=1), jnp.concatenate([z, xb], axis=1)], axis=0)

    npairs = nchunks // 2
    units = [(hh, p) for hh in range(heads) for p in range(npairs)]
    ur = range(len(units))
    qs, ks, vs, bs, gcs = [], [], [], [], []
    for hh in range(heads):
        q = l2norm(q_all[:, hh * dk:(hh + 1) * dk]) * (dk ** -0.5)
        k = l2norm(k_all[:, hh * dk:(hh + 1) * dk])
        v = v_all[:, hh * dv:(hh + 1) * dv]
        beta, gcum = head_col(beta_ref, hh), head_col(gcc_ref, hh)
        for c in range(nchunks):
            sl = slice(c * chunk, (c + 1) * chunk)
            qs.append(q[sl])
            ks.append(k[sl])
            vs.append(v[sl])
            bs.append(beta[sl])
            gcs.append(gcum[sl])
    nc = range(len(qs))
    gls = [gc[chunk - 1:chunk, :] for gc in gcs]
    egs = [jnp.exp(gc) for gc in gcs]
    kbs = [ks[c] * bs[c] for c in nc]
    decays = [jnp.exp(jnp.where(incl, jnp.where(first, gcs[2 * u], gcs[2 * u + 1]) - gcr_ref[0, hh, p], -1e30))
              for u, (hh, p) in enumerate(units)]
    lhs = [jnp.concatenate([jnp.concatenate([kbs[2 * u], kbs[2 * u + 1]], axis=1),
                            jnp.concatenate([qs[2 * u], qs[2 * u + 1]], axis=1)], axis=0).astype(BF16) for u in ur]
    kq = [_dot_nt(lhs[u], pair_rows(ks[2 * u], ks[2 * u + 1]).astype(BF16)) for u in ur]
    amats = [jnp.where(strict, kq[u][0:chunk] * decays[u], 0.0) for u in ur]
    for u, (hh, p) in enumerate(units):
        attn_ref[0, hh, p] = (kq[u][chunk:2 * chunk] * decays[u]).astype(attn_ref.dtype)
    tinvs = [eye - a for a in amats]
    apows = amats
    span = 2
    while span < chunk:
        apows = [_dot(apows[u].astype(BF16), blockdiag(apows[u])) for u in ur]
        tinvs = [tinvs[u] + _dot(tinvs[u].astype(BF16), blockdiag(apows[u])) for u in ur]
        span *= 2
    rhs = [jnp.concatenate([jnp.concatenate([kbs[c] * egs[c], vs[c] * bs[c]], axis=1)
                            for c in (2 * u, 2 * u + 1)], axis=0).astype(BF16) for u in ur]
    wu = [_dot(blockdiag(tinvs[u]), rhs[u]) for u in ur]
    kps = [jnp.concatenate([ks[c] * jnp.exp(gls[c] - gcs[c]) for c in (2 * u, 2 * u + 1)], axis=0).astype(BF16)
           for u in ur]
    kpt = [_dot_nt(ident_k, kps[u]) for u in ur]
    for u, (hh, p) in enumerate(units):
        kpt_ref[0, hh, p] = kpt[u].astype(kpt_ref.dtype)
        for half in range(2):
            c = 2 * u + half
            rs = slice(half * chunk, (half + 1) * chunk)
            wq_ref[0, hh, 2 * p + half, 0:chunk, :] = wu[u][rs, 0:dk].astype(wq_ref.dtype)
            wq_ref[0, hh, 2 * p + half, chunk:2 * chunk, :] = (qs[c] * egs[c]).astype(wq_ref.dtype)
            u_ref[0, hh, 2 * p + half] = wu[u][rs, dk:dk + dv].astype(u_ref.dtype)
            egl_ref[0, hh, 2 * p + half] = jnp.broadcast_to(jnp.exp(gls[c]), (1, dv))


def _gdn_prep(proj, conv_w, beta_c, gc_c, gc_r, dims, rows, heads):
    b, s, h, dk, dv = dims
    chunk = GDN_CHUNK
    nchunks = rows // chunk
    n = s // chunk
    qk = h * dk
    tiles = s // rows
    kw, vw = heads * dk, heads * dv
    kcol = qk // kw
    vcol = 2 * qk // vw

    def pick_table(blk):
        win = HALO_ROWS + blk
        out_row = np.arange(blk)[:, None]
        col = np.arange(CONV_WIDTH * win)[None, :]
        return jnp.asarray(col % win == HALO_ROWS + out_row - (CONV_WIDTH - 1 - col // win), dtype=BF16)

    blk = min(rows, CONV_BLOCK_ROWS)
    pick = pick_table(blk)
    pick_tail = pick_table(rows % blk if rows % blk else blk)

    def rowblk(bi, ti):
        return bi * tiles + ti

    def halo(bi, ti):
        return jnp.maximum((bi * s + ti * rows) // HALO_ROWS - 1, 0)

    col_spec = pl.BlockSpec((rows, h), lambda bi, ti, hi: (rowblk(bi, ti), 0))
    in_specs = [
        pl.BlockSpec((rows, kw), lambda bi, ti, hi: (rowblk(bi, ti), hi)),
        pl.BlockSpec((HALO_ROWS, kw), lambda bi, ti, hi: (halo(bi, ti), hi)),
        pl.BlockSpec((rows, kw), lambda bi, ti, hi: (rowblk(bi, ti), kcol + hi)),
        pl.BlockSpec((HALO_ROWS, kw), lambda bi, ti, hi: (halo(bi, ti), kcol + hi)),
        pl.BlockSpec((rows, vw), lambda bi, ti, hi: (rowblk(bi, ti), vcol + hi)),
        pl.BlockSpec((HALO_ROWS, vw), lambda bi, ti, hi: (halo(bi, ti), vcol + hi)),
        pl.BlockSpec((CONV_WIDTH, kw), lambda bi, ti, hi: (0, hi)),
        pl.BlockSpec((CONV_WIDTH, kw), lambda bi, ti, hi: (0, kcol + hi)),
        pl.BlockSpec((CONV_WIDTH, vw), lambda bi, ti, hi: (0, vcol + hi)),
        col_spec, col_spec,
        pl.BlockSpec((1, heads, nchunks // 2, 1, 2 * chunk), lambda bi, ti, hi: (bi, hi, ti, 0, 0)),
        pl.BlockSpec(pick.shape, lambda bi, ti, hi: (0, 0)),
        pl.BlockSpec(pick_tail.shape, lambda bi, ti, hi: (0, 0)),
    ]

    def out5(r, c):
        return pl.BlockSpec((1, heads, nchunks, r, c), lambda bi, ti, hi: (bi, hi, ti, 0, 0))

    def pair5(r, c):
        return pl.BlockSpec((1, heads, nchunks // 2, r, c), lambda bi, ti, hi: (bi, hi, ti, 0, 0))

    out_specs = [out5(2 * chunk, dk), pair5(dk, 2 * chunk), out5(chunk, dv), pair5(chunk, 2 * chunk), out5(1, dv)]
    out_shape = [
        jax.ShapeDtypeStruct((b, h, n, 2 * chunk, dk), BF16),
        jax.ShapeDtypeStruct((b, h, n // 2, dk, 2 * chunk), BF16),
        jax.ShapeDtypeStruct((b, h, n, chunk, dv), BF16),
        jax.ShapeDtypeStruct((b, h, n // 2, chunk, 2 * chunk), BF16),
        jax.ShapeDtypeStruct((b, h, n, 1, dv), F32),
    ]
    return pl.pallas_call(
        functools.partial(_gdn_prep_kernel, dk=dk, dv=dv, chunk=chunk, nchunks=nchunks, heads=heads),
        grid=(b, tiles, h // heads),
        in_specs=in_specs,
        out_specs=out_specs,
        out_shape=out_shape,
        compiler_params=_cparams(("parallel", "parallel", "parallel")),
        name="gdn_prep",
    )(proj, proj, proj, proj, proj, proj, conv_w, conv_w, conv_w, beta_c, gc_c, gc_r, pick, pick_tail)


def _pair_pad(x, half):
    z = jnp.zeros_like(x)
    return jnp.concatenate([x, z] if half == 0 else [z, x], axis=0)


def _gdn_scan_kernel(wq_ref, kpt_ref, u_ref, attn_ref, egl_ref, f_ref, nw_ref, o_ref, s_scr,
                     *, dv, chunk, nchunks, unroll, heads, batch):
    @pl.when(pl.program_id(1) == 0)
    def _():
        s_scr[...] = jnp.zeros_like(s_scr)

    nw = nw_ref[...]
    seqs = [(bb, hh) for bb in range(batch) for hh in range(heads)]
    sr = range(len(seqs))

    def chunk_group(it, carry):
        for cc in range(unroll):
            c = it * unroll + cc
            rs = pl.ds(pl.multiple_of(c * chunk, chunk), chunk)
            states = [s_scr[i] for i in sr]
            r = [_dot(wq_ref[bb, hh, c], states[i].astype(BF16)) for i, (bb, hh) in enumerate(seqs)]
            v_new = [_pair_pad((u_ref[bb, hh, c] - r[i][0:chunk]).astype(BF16), cc)
                     for i, (bb, hh) in enumerate(seqs)]
            upd = [_dot(kpt_ref[bb, hh, it], v_new[i]) for i, (bb, hh) in enumerate(seqs)]
            for i, (bb, hh) in enumerate(seqs):
                s_scr[i] = states[i] * egl_ref[bb, hh, c] + upd[i]
            o = [r[i][chunk:2 * chunk] + _dot(attn_ref[bb, hh, it], v_new[i]) for i, (bb, hh) in enumerate(seqs)]
            for i, (bb, hh) in enumerate(seqs):
                cs = slice(hh * dv, (hh + 1) * dv)
                on = o[i] * lax.rsqrt(jnp.mean(o[i] * o[i], axis=-1, keepdims=True) + EPS) * nw
                o_ref[bb, rs, cs] = (on * f_ref[bb, rs, cs].astype(F32)).astype(o_ref.dtype)
        return carry

    lax.fori_loop(0, nchunks // unroll, chunk_group, 0)


def _gdn_scan(wq, kpt, u, attn, egl, gates, gdn_nw, dims, nchunks, unroll, heads, fcol):
    b, s, h, dk, dv = dims
    chunk = GDN_CHUNK
    n = s // chunk
    rows = nchunks * chunk
    steps = n // nchunks
    width = heads * dv
    assert nchunks % unroll == 0 and unroll == 2

    def in5(r, c):
        return pl.BlockSpec((b, heads, nchunks, r, c), lambda gi, i: (0, gi, i, 0, 0))

    def pair5(r, c):
        return pl.BlockSpec((b, heads, nchunks // 2, r, c), lambda gi, i: (0, gi, i, 0, 0))

    in_specs = [
        in5(2 * chunk, dk), pair5(dk, 2 * chunk), in5(chunk, dv), pair5(chunk, 2 * chunk), in5(1, dv),
        pl.BlockSpec((b, rows, width), lambda gi, i: (0, i, fcol // width + gi)),
        pl.BlockSpec((1, dv), lambda gi, i: (0, 0)),
    ]
    return pl.pallas_call(
        functools.partial(_gdn_scan_kernel, dv=dv, chunk=chunk, nchunks=nchunks, unroll=unroll, heads=heads,
                          batch=b),
        grid=(h // heads, steps),
        in_specs=in_specs,
        out_specs=pl.BlockSpec((b, rows, width), lambda gi, i: (0, i, gi)),
        out_shape=jax.ShapeDtypeStruct((b, s, h * dv), BF16),
        scratch_shapes=[pltpu.VMEM((b * heads, dk, dv), F32)],
        compiler_params=_cparams(("parallel", "arbitrary")),
        name="gdn_scan",
    )(wq, kpt, u, attn, egl, gates, gdn_nw)


def _retention_kernel(q_ref, k_ref, v_ref, f_ref, ma_ref, dmat_ref,
                      xi_ref, zeta_ref, gpow_ref, nw_ref, o_ref, s_scr, *, dk, dv, heads, sub):
    @pl.when(pl.program_id(2) == 0)
    def _():
        s_scr[...] = jnp.zeros_like(s_scr)

    hr = range(heads)
    ksl = [slice(hh * dk, (hh + 1) * dk) for hh in hr]
    vsl = [slice(hh * dv, (hh + 1) * dv) for hh in hr]

    def sub_chunk(it, carry):
        rs = pl.ds(pl.multiple_of(it * sub, sub), sub)
        q = [q_ref[rs, ksl[hh]] for hh in hr]
        k = [k_ref[rs, ksl[hh]] for hh in hr]
        kz = [(k[hh].astype(F32) * zeta_ref[hh]).astype(BF16) for hh in hr]
        states = [s_scr[hh] for hh in hr]
        scores = [(_dot_nt(q[hh], k[hh]) * dmat_ref[hh]).astype(BF16) for hh in hr]
        lhs = [jnp.concatenate([scores[hh], (q[hh].astype(F32) * xi_ref[hh]).astype(BF16)], axis=1) for hh in hr]
        rhs = [jnp.concatenate([v_ref[rs, vsl[hh]], states[hh].astype(BF16)], axis=0) for hh in hr]
        upd = [_dot_tn(kz[hh], v_ref[rs, vsl[hh]]) for hh in hr]
        for hh in hr:
            s_scr[hh] = states[hh] * gpow_ref[hh] + upd[hh]
        o = [_dot(lhs[hh], rhs[hh]) for hh in hr]
        for hh in hr:
            mu = jnp.mean(o[hh], axis=-1, keepdims=True)
            dev = o[hh] - mu
            var = jnp.mean(dev * dev, axis=-1, keepdims=True)
            ob = dev * lax.rsqrt(var + EPS) * nw_ref[:, vsl[hh]]
            mixed = ob * f_ref[rs, vsl[hh]].astype(F32) + ma_ref[rs, vsl[hh]].astype(F32)
            o_ref[rs, vsl[hh]] = mixed.astype(o_ref.dtype)
        return carry

    lax.fori_loop(0, q_ref.shape[0] // sub, sub_chunk, 0)


def _retention(qk, proj, gates, ma, ret_nw, tables, dims, rows, sub, heads, cols):
    b, s, h, dk, dv = dims
    dmat, xi, zeta, gpow = tables
    rqcol, rkcol, rvcol, fcol = cols
    steps = s // rows
    kw, vw = heads * dk, heads * dv
    assert rows % sub == 0

    def rb(bi, i):
        return bi * steps + i

    in_specs = [
        pl.BlockSpec((rows, kw), lambda bi, gi, i: (rb(bi, i), rqcol // kw + gi)),
        pl.BlockSpec((rows, kw), lambda bi, gi, i: (rb(bi, i), rkcol // kw + gi)),
        pl.BlockSpec((rows, vw), lambda bi, gi, i: (rb(bi, i), rvcol // vw + gi)),
        pl.BlockSpec((rows, vw), lambda bi, gi, i: (rb(bi, i), fcol // vw + gi)),
        pl.BlockSpec((rows, vw), lambda bi, gi, i: (rb(bi, i), gi)),
        pl.BlockSpec((heads, sub, sub), lambda bi, gi, i: (gi, 0, 0)),
        pl.BlockSpec((heads, sub, dk), lambda bi, gi, i: (gi, 0, 0)),
        pl.BlockSpec((heads, sub, dk), lambda bi, gi, i: (gi, 0, 0)),
        pl.BlockSpec((heads, 1, dv), lambda bi, gi, i: (gi, 0, 0)),
        pl.BlockSpec((1, vw), lambda bi, gi, i: (0, gi)),
    ]
    return pl.pallas_call(
        functools.partial(_retention_kernel, dk=dk, dv=dv, heads=heads, sub=sub),
        grid=(b, h // heads, steps),
        in_specs=in_specs,
        out_specs=pl.BlockSpec((rows, vw), lambda bi, gi, i: (rb(bi, i), gi)),
        out_shape=jax.ShapeDtypeStruct((b * s, h * dv), BF16),
        scratch_shapes=[pltpu.VMEM((heads, dk, dv), F32)],
        compiler_params=_cparams(("parallel", "parallel", "arbitrary")),
        name="retention",
    )(qk, qk, proj, gates, ma, dmat, xi, zeta, gpow, ret_nw)


def _rotary_tables(s, dk, rows):
    inv = ROPE_BASE ** (-jnp.arange(0, dk, 2, dtype=F32) / dk)
    inv = jnp.concatenate([inv, inv])
    sign = jnp.concatenate([-jnp.ones(dk // 2, F32), jnp.ones(dk // 2, F32)])
    lo = jnp.arange(rows, dtype=F32)[:, None] * inv[None, :]
    hi = (jnp.arange(s // rows, dtype=F32) * rows)[:, None] * inv[None, :]
    cl, sl, ch, sh = jnp.cos(lo), jnp.sin(lo), jnp.cos(hi)[:, None], jnp.sin(hi)[:, None]
    cos = (ch * cl - sh * sl).reshape(s, dk)
    sin = ((sh * cl + ch * sl) * sign).reshape(s, dk)
    return cos, sin


def _retention_tables(h, dk, dv, rows):
    log_gamma = jnp.log1p(-jnp.exp2(-5.0 - jnp.arange(h, dtype=F32)))
    pos = jnp.arange(rows, dtype=F32)
    dist = pos[:, None] - pos[None, :]
    kscale = dk ** -0.5
    dmat = jnp.exp(jnp.where(dist >= 0, dist * log_gamma[:, None, None], -jnp.inf)) * kscale
    xi = jnp.broadcast_to(jnp.exp((pos + 1.0) * log_gamma[:, None])[:, :, None], (h, rows, dk))
    zeta = jnp.broadcast_to(jnp.exp((rows - 1.0 - pos) * log_gamma[:, None])[:, :, None] * kscale, (h, rows, dk))
    gpow = jnp.broadcast_to(jnp.exp(rows * log_gamma)[:, None, None], (h, 1, dv))
    return dmat, xi, zeta, gpow


def _outproj_kernel(m_ref, w_ref, x_ref, nw_ref, h_ref, hn_ref):
    hres = x_ref[...] + _dot(m_ref[...], w_ref[...])
    h_ref[...] = hres
    ms = jnp.mean(hres * hres, axis=-1, keepdims=True)
    hn_ref[...] = ((hres * lax.rsqrt(ms + EPS)) * nw_ref[...]).astype(hn_ref.dtype)


def _outproj(mixed, w, x2, nw, tm):
    m, d = x2.shape
    kdim = mixed.shape[1]
    return pl.pallas_call(
        _outproj_kernel,
        grid=(m // tm,),
        in_specs=[
            pl.BlockSpec((tm, kdim), lambda i: (i, 0)),
            pl.BlockSpec((kdim, d), lambda i: (0, 0)),
            pl.BlockSpec((tm, d), lambda i: (i, 0)),
            pl.BlockSpec((1, d), lambda i: (0, 0)),
        ],
        out_specs=[pl.BlockSpec((tm, d), lambda i: (i, 0))] * 2,
        out_shape=[jax.ShapeDtypeStruct((m, d), F32), jax.ShapeDtypeStruct((m, d), BF16)],
        compiler_params=_cparams(("parallel",)),
        name="outproj",
    )(mixed, w, x2, nw)


def _ffn_kernel(hn_ref, h_ref, wg_ref, wu_ref, wd_ref, nw_ref, o_ref):
    f = pl.program_id(1)

    @pl.when(f == 0)
    def _():
        o_ref[...] = h_ref[...]

    hn = hn_ref[...]
    g = _dot(hn, wg_ref[...])
    up = _dot(hn, wu_ref[...])
    act = (_silu(g) * up).astype(BF16)
    o_ref[...] += _dot(act, wd_ref[...])

    @pl.when(f == pl.num_programs(1) - 1)
    def _():
        hres = o_ref[...]
        ms = jnp.mean(hres * hres, axis=-1, keepdims=True)
        o_ref[...] = (hres * lax.rsqrt(ms + EPS)) * nw_ref[...]


def _ffn(hn, hres, wg, wu, wd, nw, tm, tf):
    m, d = hres.shape
    ff = wg.shape[1]
    return pl.pallas_call(
        _ffn_kernel,
        grid=(m // tm, ff // tf),
        in_specs=[
            pl.BlockSpec((tm, d), lambda i, j: (i, 0)),
            pl.BlockSpec((tm, d), lambda i, j: (i, 0)),
            pl.BlockSpec((d, tf), lambda i, j: (0, j)),
            pl.BlockSpec((d, tf), lambda i, j: (0, j)),
            pl.BlockSpec((tf, d), lambda i, j: (j, 0)),
            pl.BlockSpec((1, d), lambda i, j: (0, 0)),
        ],
        out_specs=pl.BlockSpec((tm, d), lambda i, j: (i, 0)),
        out_shape=jax.ShapeDtypeStruct((m, d), F32),
        compiler_params=_cparams(("parallel", "arbitrary"), FFN_VMEM_LIMIT_BYTES),
        name="ffn",
    )(hn, hres, wg, wu, wd, nw)


def _pick(total, want):
    t = min(total, want)
    while total % t:
        t //= 2
    return t


def _layer(h2, dims, norm1_w, w_in, conv_w, a_log, dt_bias, gdn_norm_w, ret_norm_w,
           w_out, norm2_w, w_gate, w_up, w_down, out_norm_w):
    b, s, h, dk, dv = dims
    m, d = h2.shape
    qk, vd = h * dk, h * dv
    chunk = GDN_CHUNK
    n = s // chunk

    assert 2 * qk == vd
    tn = vd
    sc0 = 3 * vd
    sc = 2 * h
    w_in_t = w_in.T
    w_ba = w_in_t[sc0:sc0 + sc].astype(BF16)
    tr = _pick(vd, 1024)
    per = vd // tr
    w_proj = _cast_rows(w_in_t, [g * vd + k * tr for g in (0, 1) for k in range(per)]
                        + [sc0 + sc + g * vd + k * tr for g in (0, 1) for k in range(per)], tr, sc)

    proj, u = _inproj(h2, norm1_w.reshape(1, d), w_proj, _pick(m, 1024), tn, (0, 1, 3))
    ret_sub = _pick(s, 256)
    cosf, sins = _rotary_tables(s, dk, ret_sub)
    rot_tm = _pick(s, 1024)
    gtr = _pick(vd, (4 * vd) // (m // rot_tm))
    gper = vd // gtr
    gate_srcs = ([2 * vd + k * gtr for k in range(gper)]
                 + [sc0 + sc + g * vd + k * gtr for g in (2, 3, 4) for k in range(gper)])
    qk_rot, beta, gcum, w_gate_rows = _rotproj(
        u, w_proj, cosf, sins, w_ba, a_log.reshape(1, h), dt_bias.reshape(1, h), w_in_t, gate_srcs, gtr, sc,
        rot_tm, tn, 2, s, dk, chunk, _pick(s, 256))
    gates, (wg_b, wu_b, wd_b, wo_b) = _gates(u, w_gate_rows, _pick(m, 1024), _pick(vd, 1024), vd,
                                             ((0, 2 * vd), (vd, 3 * vd)), [w_gate, w_up, w_down, w_out])

    gc_r = gcum.reshape(b, n // 2, 2 * chunk, h).transpose(0, 3, 1, 2)[:, :, :, None, :]
    wq, kpt, uu, attn, egl = _gdn_prep(proj, conv_w, beta, gcum, gc_r, dims, _pick(s, 512), _pick(h, 8))

    ma = _gdn_scan(wq, kpt, uu, attn, egl, gates.reshape(b, s, 2 * vd), gdn_norm_w.reshape(1, dv), dims,
                   _pick(n, 8), _pick(n, 2), _pick(h, 8), 0).reshape(m, vd)

    tables = _retention_tables(h, dk, dv, ret_sub)
    mixed = _retention(qk_rot, proj, gates, ma, ret_norm_w.reshape(1, vd), tables, dims, _pick(s, 512),
                       ret_sub, _pick(h, 8), (0, qk, 2 * tn, vd))

    hres, hn = _outproj(mixed, wo_b, h2, norm2_w.reshape(1, d), _pick(m, 512))

    return _ffn(hn, hres, wg_b, wu_b, wd_b, out_norm_w.reshape(1, d), _pick(m, 1024), _pick(w_gate.shape[1], 512))


def kernel(x, norm1_w, w_in, conv_w, a_log, dt_bias, gdn_norm_w, ret_norm_w, w_out, norm2_w,
           w_gate, w_up, w_down, norm_f_w):
    b, s, d = x.shape
    depth, h = a_log.shape
    assert depth == 1, "the final norm is fused into the single layer's FFN kernel"
    dk = d // 16
    dv = d // h
    dims = (b, s, h, dk, dv)
    out = _layer(x.reshape(b * s, d), dims, norm1_w[0], w_in[0], conv_w[0], a_log[0], dt_bias[0],
                 gdn_norm_w[0], ret_norm_w[0], w_out[0], norm2_w[0], w_gate[0], w_up[0], w_down[0],
                 norm_f_w)
    return out.reshape(b, s, d)
```

```python
import functools

import jax
import jax.numpy as jnp
import numpy as np
from jax import lax
from jax.experimental import pallas as pl
from jax.experimental.pallas import tpu as pltpu

F32 = jnp.float32
BF16 = jnp.bfloat16

EPS = 1e-6
CONV_WIDTH = 4
GDN_CHUNK = 64
ROPE_BASE = 10000.0
BF16_TILE_ROWS = 16
HALO_ROWS = BF16_TILE_ROWS
CONV_BLOCK_ROWS = 128
VMEM_LIMIT_BYTES = 56 * 1024 * 1024
FFN_VMEM_LIMIT_BYTES = 60 * 1024 * 1024


def _cparams(sem, vmem_limit=VMEM_LIMIT_BYTES):
    return pltpu.CompilerParams(dimension_semantics=sem, vmem_limit_bytes=vmem_limit)


def _sigmoid(x):
    return 1.0 / (1.0 + jnp.exp(-x))


def _silu(x):
    h = 0.5 * x
    return h + h * jnp.tanh(h)


def _dot(a, b):
    return jnp.dot(a, b, preferred_element_type=F32)


def _dot_nt(a, b):
    return lax.dot_general(a, b, (((1,), (1,)), ((), ())), preferred_element_type=F32)


def _dot_tn(a, b):
    return lax.dot_general(a, b, (((0,), (0,)), ((), ())), preferred_element_type=F32)


def _nth(j, values):
    idx = values[0]
    for p in range(1, len(values)):
        idx = jnp.where(j >= p, values[p], idx)
    return idx


def _row_tile_specs(srcs, tr, d, count, index):
    assert all(src % tr in (0, count) for src in srcs) and tr % count == 0 and count % BF16_TILE_ROWS == 0
    bases = tuple(src // tr for src in srcs)
    nexts = tuple((src // tr + 1) * (tr // count) for src in srcs)
    return [pl.BlockSpec((tr, d), lambda *g: (_nth(index(*g), bases), 0)),
            pl.BlockSpec((count, d), lambda *g: (_nth(index(*g), nexts), 0))]


def _cast_row_tile(cur_ref, nxt_ref, shifted, count):
    cur = cur_ref[...]
    moved = jnp.concatenate([cur[count:], nxt_ref[...]], axis=0)
    return jnp.where(shifted, moved, cur).astype(BF16)


def _cast_rows_kernel(cur_ref, nxt_ref, o_ref, *, flags, count):
    o_ref[...] = _cast_row_tile(cur_ref, nxt_ref, _nth(pl.program_id(0), flags) > 0, count)


def _cast_rows(wt, srcs, tr, count):
    d = wt.shape[1]
    flags = tuple(int(src % tr != 0) for src in srcs)
    return pl.pallas_call(
        functools.partial(_cast_rows_kernel, flags=flags, count=count),
        grid=(len(srcs),),
        in_specs=_row_tile_specs(srcs, tr, d, count, lambda j: j),
        out_specs=pl.BlockSpec((tr, d), lambda j: (j, 0)),
        out_shape=jax.ShapeDtypeStruct((len(srcs) * tr, d), BF16),
        compiler_params=_cparams(("parallel",)),
        name="cast_rows",
    )(wt, wt)


def _inproj_kernel(x_ref, nw_ref, w_ref, o_ref, u_ref):
    @pl.when(pl.program_id(1) == 0)
    def _():
        xf = x_ref[...]
        ms = jnp.mean(xf * xf, axis=-1, keepdims=True)
        u_ref[...] = ((xf * lax.rsqrt(ms + EPS)) * nw_ref[...]).astype(u_ref.dtype)

    o_ref[...] = _dot_nt(u_ref[...], w_ref[...]).astype(o_ref.dtype)


def _inproj(x2, nw, w, tm, tn, tiles):
    m, d = x2.shape
    return pl.pallas_call(
        _inproj_kernel,
        grid=(m // tm, len(tiles)),
        in_specs=[
            pl.BlockSpec((tm, d), lambda i, j: (i, 0)),
            pl.BlockSpec((1, d), lambda i, j: (0, 0)),
            pl.BlockSpec((tn, d), lambda i, j: (_nth(j, tiles), 0)),
        ],
        out_specs=[
            pl.BlockSpec((tm, tn), lambda i, j: (i, j)),
            pl.BlockSpec((tm, d), lambda i, j: (i, 0)),
        ],
        out_shape=[jax.ShapeDtypeStruct((m, len(tiles) * tn), BF16), jax.ShapeDtypeStruct((m, d), BF16)],
        compiler_params=_cparams(("parallel", "arbitrary")),
        name="inproj",
    )(x2, nw, w)


def _gate_scalars(u_ref, wba_ref, alog_ref, dtb_ref, beta_ref, gc_ref, chunk, sub):
    heads = alog_ref.shape[1]
    parts = [slice(p * sub, (p + 1) * sub) for p in range(u_ref.shape[0] // sub)]
    row = lax.broadcasted_iota(jnp.int32, (sub, sub), 0)
    col = lax.broadcasted_iota(jnp.int32, (sub, sub), 1)
    incl = jnp.where(((row // chunk) == (col // chunk)) & (row >= col), 1.0, 0.0).astype(BF16)
    bas = [_dot_nt(u_ref[sl, :], wba_ref[...]) for sl in parts]
    gs = []
    for sl, ba in zip(parts, bas):
        beta_ref[sl, :] = _sigmoid(ba[:, 0:heads])
        a = ba[:, heads:2 * heads] + dtb_ref[...]
        softplus = jnp.maximum(a, 0.0) + jnp.log1p(jnp.exp(-jnp.abs(a)))
        gs.append(-jnp.exp(alog_ref[...]) * softplus)
    g1 = [g.astype(BF16) for g in gs]
    r1 = [g - p1.astype(F32) for g, p1 in zip(gs, g1)]
    g2 = [r.astype(BF16) for r in r1]
    g3 = [(r - p2.astype(F32)).astype(BF16) for r, p2 in zip(r1, g2)]
    gcs = [_dot(incl, p1) + _dot(incl, p2) + _dot(incl, p3) for p1, p2, p3 in zip(g1, g2, g3)]
    for sl, gc in zip(parts, gcs):
        gc_ref[sl, :] = gc


def _rotproj_kernel(u_ref, w_ref, cos_ref, sin_ref, wba_ref, alog_ref, dtb_ref, *rest,
                    dk, chunk, sub, cast_flags, cast_count, per_step):
    cast_in, (o_ref, beta_ref, gc_ref, cast_ref) = rest[:2 * per_step], rest[2 * per_step:]
    cast_tr = cast_ref.shape[0] // per_step
    res = _dot_nt(u_ref[...], w_ref[...])
    cos = cos_ref[...]
    sin = sin_ref[...]
    for hh in range(res.shape[1] // dk):
        sl = slice(hh * dk, (hh + 1) * dk)
        x = res[:, sl]
        o_ref[:, sl] = (x * cos + pltpu.roll(x, dk // 2, 1) * sin).astype(o_ref.dtype)
    _gate_scalars(u_ref, wba_ref, alog_ref, dtb_ref, beta_ref, gc_ref, chunk, sub)
    for kk in range(per_step):
        shifted = _nth(pl.program_id(0) * per_step + kk, cast_flags) > 0
        cast_ref[kk * cast_tr:(kk + 1) * cast_tr, :] = _cast_row_tile(cast_in[2 * kk], cast_in[2 * kk + 1],
                                                                     shifted, cast_count)


def _rotproj(u, w, cosf, sins, wba, alog, dtb, wt, cast_srcs, cast_tr, cast_count,
             tm, tn, tile, seq, dk, chunk, sub):
    m, d = u.shape
    h = alog.shape[1]
    steps = m // tm
    assert seq % tm == 0 and tm % sub == 0 and sub % chunk == 0 and len(cast_srcs) % steps == 0
    per_seq = seq // tm
    per_step = len(cast_srcs) // steps
    scalar_out = jax.ShapeDtypeStruct((m, h), F32)
    cast_flags = tuple(int(src % cast_tr != 0) for src in cast_srcs)
    cast_in_specs = []
    for kk in range(per_step):
        cast_in_specs += _row_tile_specs(cast_srcs, cast_tr, wt.shape[1], cast_count,
                                         lambda i, kk=kk: i * per_step + kk)
    return pl.pallas_call(
        functools.partial(_rotproj_kernel, dk=dk, chunk=chunk, sub=sub, cast_flags=cast_flags,
                          cast_count=cast_count, per_step=per_step),
        grid=(steps,),
        in_specs=[
            pl.BlockSpec((tm, d), lambda i: (i, 0)),
            pl.BlockSpec((tn, d), lambda i: (tile, 0)),
            pl.BlockSpec((tm, dk), lambda i: (i % per_seq, 0)),
            pl.BlockSpec((tm, dk), lambda i: (i % per_seq, 0)),
            pl.BlockSpec((2 * h, d), lambda i: (0, 0)),
            pl.BlockSpec((1, h), lambda i: (0, 0)),
            pl.BlockSpec((1, h), lambda i: (0, 0)),
        ] + cast_in_specs,
        out_specs=[pl.BlockSpec((tm, tn), lambda i: (i, 0))] + [pl.BlockSpec((tm, h), lambda i: (i, 0))] * 2
        + [pl.BlockSpec((per_step * cast_tr, wt.shape[1]), lambda i: (i, 0))],
        out_shape=[jax.ShapeDtypeStruct((m, tn), BF16), scalar_out, scalar_out,
                   jax.ShapeDtypeStruct((len(cast_srcs) * cast_tr, wt.shape[1]), BF16)],
        compiler_params=_cparams(("arbitrary",)),
        name="rotproj",
    )(u, w, cosf, sins, wba, alog, dtb, *([wt, wt] * per_step))


def _gates_kernel(u_ref, ws_ref, wg_ref, *rest):
    ncast = (len(rest) - 1) // 2
    cast_in, o_ref, cast_out = rest[:ncast], rest[ncast], rest[ncast + 1:]
    u = u_ref[...]
    a = _dot_nt(u, ws_ref[...])
    g = _dot_nt(u, wg_ref[...])
    o_ref[...] = (_silu(a) * (0.5 + 0.5 * jnp.tanh(0.5 * g))).astype(o_ref.dtype)
    for src, dst in zip(cast_in, cast_out):
        dst[...] = src[...].astype(dst.dtype)


def _cast_slabs(rows, nsteps):
    nb = nsteps
    while nb > 1 and (rows % nb or (rows // nb) % BF16_TILE_ROWS):
        nb //= 2
    return nb


def _gates(u, w, tm, tg, vd, pairs, to_cast):
    m, d = u.shape
    per = vd // tg
    nj = len(pairs) * per
    nsteps = (m // tm) * nj

    def blk(j, which):
        idx = pairs[0][which] // tg + j
        for p in range(1, len(pairs)):
            idx = jnp.where(j >= p * per, pairs[p][which] // tg + j - p * per, idx)
        return idx

    cast_specs = []
    for mat in to_cast:
        nb = _cast_slabs(mat.shape[0], nsteps)
        rep = nsteps // nb
        cast_specs.append(pl.BlockSpec((mat.shape[0] // nb, mat.shape[1]),
                                       lambda i, j, rep=rep: ((i * nj + j) // rep, 0)))

    outs = pl.pallas_call(
        _gates_kernel,
        grid=(m // tm, nj),
        in_specs=[
            pl.BlockSpec((tm, d), lambda i, j: (i, 0)),
            pl.BlockSpec((tg, d), lambda i, j: (blk(j, 0), 0)),
            pl.BlockSpec((tg, d), lambda i, j: (blk(j, 1), 0)),
        ] + cast_specs,
        out_specs=[pl.BlockSpec((tm, tg), lambda i, j: (i, j))] + cast_specs,
        out_shape=[jax.ShapeDtypeStruct((m, len(pairs) * vd), BF16)]
        + [jax.ShapeDtypeStruct(mat.shape, BF16) for mat in to_cast],
        compiler_params=_cparams(("arbitrary", "arbitrary")),
        name="gates",
    )(u, w, w, *to_cast)
    return outs[0], outs[1:]


def _gdn_prep_kernel(q_ref, qh_ref, k_ref, kh_ref, v_ref, vh_ref, cwq_ref, cwk_ref, cwv_ref,
                     beta_ref, gcc_ref, gcr_ref, pick_ref,
                     wq_ref, kpt_ref, u_ref, attn_ref, egl_ref, *, dk, dv, chunk, nchunks, heads):
    t = pl.program_id(1)
    rows = chunk * nchunks
    lane_head = lax.broadcasted_iota(jnp.int32, (1, beta_ref.shape[1]), 1)

    def head_col(ref, hh):
        hsel = lane_head == pl.program_id(2) * heads + hh
        return jnp.sum(jnp.where(hsel, ref[...], 0.0), axis=1, keepdims=True)

    pick = pick_ref[...]
    blk = pick.shape[0]
    win = HALO_ROWS + blk

    def conv_silu(x_ref, h_ref, cw_ref):
        halo = jnp.where(t == 0, jnp.zeros_like(h_ref[...]), h_ref[...])
        ext = jnp.concatenate([halo, x_ref[...]], axis=0)
        taps = [ext * cw_ref[tap:tap + 1, :].astype(BF16) for tap in range(CONV_WIDTH)]
        out = []
        for r in range(rows // blk):
            stacked = jnp.concatenate([tp[r * blk:r * blk + win] for tp in taps], axis=0)
            out.append(_silu(_dot(pick, stacked)))
        return jnp.concatenate(out, axis=0)

    def l2norm(y):
        return y * lax.rsqrt(jnp.sum(y * y, axis=-1, keepdims=True) + EPS)

    q_all = conv_silu(q_ref, qh_ref, cwq_ref)
    k_all = conv_silu(k_ref, kh_ref, cwk_ref)
    v_all = conv_silu(v_ref, vh_ref, cwv_ref)

    ri = lax.broadcasted_iota(jnp.int32, (chunk, 2 * chunk), 0)
    li = lax.broadcasted_iota(jnp.int32, (chunk, 2 * chunk), 1)
    first = li < chunk
    ci = jnp.where(first, li, li - chunk)
    incl = ri >= ci
    strict = ri > ci
    eye = jnp.where(ri == ci, 1.0, 0.0).astype(F32)
    ident_k = jnp.where(lax.broadcasted_iota(jnp.int32, (dk, dk), 0)
                        == lax.broadcasted_iota(jnp.int32, (dk, dk), 1), 1.0, 0.0).astype(BF16)

    def blockdiag(x):
        return jnp.concatenate([jnp.where(first, x, 0.0), jnp.where(first, 0.0, x)], axis=0).astype(BF16)

    def pair_rows(xa, xb):
        z = jnp.zeros_like(xa)
        return jnp.concatenate([jnp.concatenate([xa, z], axis=1), jnp.concatenate([z, xb], axis=1)], axis=0)

    npairs = nchunks // 2
    units = [(hh, p) for hh in range(heads) for p in range(npairs)]
    ur = range(len(units))
    qs, ks, vs, bs, gcs = [], [], [], [], []
    for hh in range(heads):
        q = l2norm(q_all[:, hh * dk:(hh + 1) * dk]) * (dk ** -0.5)
        k = l2norm(k_all[:, hh * dk:(hh + 1) * dk])
        v = v_all[:, hh * dv:(hh + 1) * dv]
        beta, gcum = head_col(beta_ref, hh), head_col(gcc_ref, hh)
        for c in range(nchunks):
            sl = slice(c * chunk, (c + 1) * chunk)
            qs.append(q[sl])
            ks.append(k[sl])
            vs.append(v[sl])
            bs.append(beta[sl])
            gcs.append(gcum[sl])
    nc = range(len(qs))
    gls = [gc[chunk - 1:chunk, :] for gc in gcs]
    egs = [jnp.exp(gc) for gc in gcs]
    kbs = [ks[c] * bs[c] for c in nc]
    decays = [jnp.exp(jnp.where(incl, jnp.where(first, gcs[2 * u], gcs[2 * u + 1]) - gcr_ref[0, hh, p], -1e30))
              for u, (hh, p) in enumerate(units)]
    lhs = [jnp.concatenate([jnp.concatenate([kbs[2 * u], kbs[2 * u + 1]], axis=1),
                            jnp.concatenate([qs[2 * u], qs[2 * u + 1]], axis=1)], axis=0).astype(BF16) for u in ur]
    kq = [_dot_nt(lhs[u], pair_rows(ks[2 * u], ks[2 * u + 1]).astype(BF16)) for u in ur]
    amats = [jnp.where(strict, kq[u][0:chunk] * decays[u], 0.0) for u in ur]
    for u, (hh, p) in enumerate(units):
        attn_ref[0, hh, p] = (kq[u][chunk:2 * chunk] * decays[u]).astype(attn_ref.dtype)
    tinvs = [eye - a for a in amats]
    apows = amats
    span = 2
    while span < chunk:
        apows = [_dot(apows[u].astype(BF16), blockdiag(apows[u])) for u in ur]
        tinvs = [tinvs[u] + _dot(tinvs[u].astype(BF16), blockdiag(apows[u])) for u in ur]
        span *= 2
    rhs = [jnp.concatenate([jnp.concatenate([kbs[c] * egs[c], vs[c] * bs[c]], axis=1)
                            for c in (2 * u, 2 * u + 1)], axis=0).astype(BF16) for u in ur]
    wu = [_dot(blockdiag(tinvs[u]), rhs[u]) for u in ur]
    kps = [jnp.concatenate([ks[c] * jnp.exp(gls[c] - gcs[c]) for c in (2 * u, 2 * u + 1)], axis=0).astype(BF16)
           for u in ur]
    kpt = [_dot_nt(ident_k, kps[u]) for u in ur]
    for u, (hh, p) in enumerate(units):
        kpt_ref[0, hh, p] = kpt[u].astype(kpt_ref.dtype)
        for half in range(2):
            c = 2 * u + half
            rs = slice(half * chunk, (half + 1) * chunk)
            wq_ref[0, hh, 2 * p + half, 0:chunk, :] = wu[u][rs, 0:dk].astype(wq_ref.dtype)
            wq_ref[0, hh, 2 * p + half, chunk:2 * chunk, :] = (qs[c] * egs[c]).astype(wq_ref.dtype)
            u_ref[0, hh, 2 * p + half] = wu[u][rs, dk:dk + dv].astype(u_ref.dtype)
            egl_ref[0, hh, 2 * p + half] = jnp.broadcast_to(jnp.exp(gls[c]), (1, dv))


def _gdn_prep(proj, conv_w, beta_c, gc_c, gc_r, dims, rows, heads):
    b, s, h, dk, dv = dims
    chunk = GDN_CHUNK
    nchunks = rows // chunk
    n = s // chunk
    qk = h * dk
    tiles = s // rows
    kw, vw = heads * dk, heads * dv
    kcol = qk // kw
    vcol = 2 * qk // vw

    blk = _pick(rows, CONV_BLOCK_ROWS)
    win = HALO_ROWS + blk
    out_row = np.arange(blk)[:, None]
    col = np.arange(CONV_WIDTH * win)[None, :]
    pick = jnp.asarray(col % win == HALO_ROWS + out_row - (CONV_WIDTH - 1 - col // win), dtype=BF16)

    def rowblk(bi, ti):
        return bi * tiles + ti

    def halo(bi, ti):
        return jnp.maximum((bi * s + ti * rows) // HALO_ROWS - 1, 0)

    col_spec = pl.BlockSpec((rows, h), lambda bi, ti, hi: (rowblk(bi, ti), 0))
    in_specs = [
        pl.BlockSpec((rows, kw), lambda bi, ti, hi: (rowblk(bi, ti), hi)),
        pl.BlockSpec((HALO_ROWS, kw), lambda bi, ti, hi: (halo(bi, ti), hi)),
        pl.BlockSpec((rows, kw), lambda bi, ti, hi: (rowblk(bi, ti), kcol + hi)),
        pl.BlockSpec((HALO_ROWS, kw), lambda bi, ti, hi: (halo(bi, ti), kcol + hi)),
        pl.BlockSpec((rows, vw), lambda bi, ti, hi: (rowblk(bi, ti), vcol + hi)),
        pl.BlockSpec((HALO_ROWS, vw), lambda bi, ti, hi: (halo(bi, ti), vcol + hi)),
        pl.BlockSpec((CONV_WIDTH, kw), lambda bi, ti, hi: (0, hi)),
        pl.BlockSpec((CONV_WIDTH, kw), lambda bi, ti, hi: (0, kcol + hi)),
        pl.BlockSpec((CONV_WIDTH, vw), lambda bi, ti, hi: (0, vcol + hi)),
        col_spec, col_spec,
        pl.BlockSpec((1, heads, nchunks // 2, 1, 2 * chunk), lambda bi, ti, hi: (bi, hi, ti, 0, 0)),
        pl.BlockSpec(pick.shape, lambda bi, ti, hi: (0, 0)),
    ]

    def out5(r, c):
        return pl.BlockSpec((1, heads, nchunks, r, c), lambda bi, ti, hi: (bi, hi, ti, 0, 0))

    def pair5(r, c):
        return pl.BlockSpec((1, heads, nchunks // 2, r, c), lambda bi, ti, hi: (bi, hi, ti, 0, 0))

    out_specs = [out5(2 * chunk, dk), pair5(dk, 2 * chunk), out5(chunk, dv), pair5(chunk, 2 * chunk), out5(1, dv)]
    out_shape = [
        jax.ShapeDtypeStruct((b, h, n, 2 * chunk, dk), BF16),
        jax.ShapeDtypeStruct((b, h, n // 2, dk, 2 * chunk), BF16),
        jax.ShapeDtypeStruct((b, h, n, chunk, dv), BF16),
        jax.ShapeDtypeStruct((b, h, n // 2, chunk, 2 * chunk), BF16),
        jax.ShapeDtypeStruct((b, h, n, 1, dv), F32),
    ]
    return pl.pallas_call(
        functools.partial(_gdn_prep_kernel, dk=dk, dv=dv, chunk=chunk, nchunks=nchunks, heads=heads),
        grid=(b, tiles, h // heads),
        in_specs=in_specs,
        out_specs=out_specs,
        out_shape=out_shape,
        compiler_params=_cparams(("parallel", "parallel", "parallel")),
        name="gdn_prep",
    )(proj, proj, proj, proj, proj, proj, conv_w, conv_w, conv_w, beta_c, gc_c, gc_r, pick)


def _pair_pad(x, half):
    z = jnp.zeros_like(x)
    return jnp.concatenate([x, z] if half == 0 else [z, x], axis=0)


def _gdn_scan_kernel(wq_ref, kpt_ref, u_ref, attn_ref, egl_ref, f_ref, nw_ref, o_ref, s_scr,
                     *, dv, chunk, nchunks, unroll, heads, batch):
    @pl.when(pl.program_id(1) == 0)
    def _():
        s_scr[...] = jnp.zeros_like(s_scr)

    nw = nw_ref[...]
    seqs = [(bb, hh) for bb in range(batch) for hh in range(heads)]
    sr = range(len(seqs))

    def chunk_group(it, carry):
        for cc in range(unroll):
            c = it * unroll + cc
            rs = pl.ds(pl.multiple_of(c * chunk, chunk), chunk)
            states = [s_scr[i] for i in sr]
            r = [_dot(wq_ref[bb, hh, c], states[i].astype(BF16)) for i, (bb, hh) in enumerate(seqs)]
            v_new = [_pair_pad((u_ref[bb, hh, c] - r[i][0:chunk]).astype(BF16), cc)
                     for i, (bb, hh) in enumerate(seqs)]
            upd = [_dot(kpt_ref[bb, hh, it], v_new[i]) for i, (bb, hh) in enumerate(seqs)]
            for i, (bb, hh) in enumerate(seqs):
                s_scr[i] = states[i] * egl_ref[bb, hh, c] + upd[i]
            o = [r[i][chunk:2 * chunk] + _dot(attn_ref[bb, hh, it], v_new[i]) for i, (bb, hh) in enumerate(seqs)]
            for i, (bb, hh) in enumerate(seqs):
                cs = slice(hh * dv, (hh + 1) * dv)
                on = o[i] * lax.rsqrt(jnp.mean(o[i] * o[i], axis=-1, keepdims=True) + EPS) * nw
                o_ref[bb, rs, cs] = (on * f_ref[bb, rs, cs].astype(F32)).astype(o_ref.dtype)
        return carry

    lax.fori_loop(0, nchunks // unroll, chunk_group, 0)


def _gdn_scan(wq, kpt, u, attn, egl, gates, gdn_nw, dims, nchunks, unroll, heads, fcol):
    b, s, h, dk, dv = dims
    chunk = GDN_CHUNK
    n = s // chunk
    rows = nchunks * chunk
    steps = n // nchunks
    width = heads * dv
    assert nchunks % unroll == 0 and unroll == 2

    def in5(r, c):
        return pl.BlockSpec((b, heads, nchunks, r, c), lambda gi, i: (0, gi, i, 0, 0))

    def pair5(r, c):
        return pl.BlockSpec((b, heads, nchunks // 2, r, c), lambda gi, i: (0, gi, i, 0, 0))

    in_specs = [
        in5(2 * chunk, dk), pair5(dk, 2 * chunk), in5(chunk, dv), pair5(chunk, 2 * chunk), in5(1, dv),
        pl.BlockSpec((b, rows, width), lambda gi, i: (0, i, fcol // width + gi)),
        pl.BlockSpec((1, dv), lambda gi, i: (0, 0)),
    ]
    return pl.pallas_call(
        functools.partial(_gdn_scan_kernel, dv=dv, chunk=chunk, nchunks=nchunks, unroll=unroll, heads=heads,
                          batch=b),
        grid=(h // heads, steps),
        in_specs=in_specs,
        out_specs=pl.BlockSpec((b, rows, width), lambda gi, i: (0, i, gi)),
        out_shape=jax.ShapeDtypeStruct((b, s, h * dv), BF16),
        scratch_shapes=[pltpu.VMEM((b * heads, dk, dv), F32)],
        compiler_params=_cparams(("parallel", "arbitrary")),
        name="gdn_scan",
    )(wq, kpt, u, attn, egl, gates, gdn_nw)


def _retention_kernel(q_ref, k_ref, v_ref, f_ref, ma_ref, dmat_ref,
                      xi_ref, zeta_ref, gpow_ref, nw_ref, o_ref, s_scr, *, dk, dv, heads, sub):
    @pl.when(pl.program_id(2) == 0)
    def _():
        s_scr[...] = jnp.zeros_like(s_scr)

    hr = range(heads)
    ksl = [slice(hh * dk, (hh + 1) * dk) for hh in hr]
    vsl = [slice(hh * dv, (hh + 1) * dv) for hh in hr]

    def sub_chunk(it, carry):
        rs = pl.ds(pl.multiple_of(it * sub, sub), sub)
        q = [q_ref[rs, ksl[hh]] for hh in hr]
        k = [k_ref[rs, ksl[hh]] for hh in hr]
        kz = [(k[hh].astype(F32) * zeta_ref[hh]).astype(BF16) for hh in hr]
        states = [s_scr[hh] for hh in hr]
        scores = [(_dot_nt(q[hh], k[hh]) * dmat_ref[hh]).astype(BF16) for hh in hr]
        lhs = [jnp.concatenate([scores[hh], (q[hh].astype(F32) * xi_ref[hh]).astype(BF16)], axis=1) for hh in hr]
        rhs = [jnp.concatenate([v_ref[rs, vsl[hh]], states[hh].astype(BF16)], axis=0) for hh in hr]
        upd = [_dot_tn(kz[hh], v_ref[rs, vsl[hh]]) for hh in hr]
        for hh in hr:
            s_scr[hh] = states[hh] * gpow_ref[hh] + upd[hh]
        o = [_dot(lhs[hh], rhs[hh]) for hh in hr]
        for hh in hr:
            mu = jnp.mean(o[hh], axis=-1, keepdims=True)
            dev = o[hh] - mu
            var = jnp.mean(dev * dev, axis=-1, keepdims=True)
            ob = dev * lax.rsqrt(var + EPS) * nw_ref[:, vsl[hh]]
            mixed = ob * f_ref[rs, vsl[hh]].astype(F32) + ma_ref[rs, vsl[hh]].astype(F32)
            o_ref[rs, vsl[hh]] = mixed.astype(o_ref.dtype)
        return carry

    lax.fori_loop(0, q_ref.shape[0] // sub, sub_chunk, 0)


def _retention(qk, proj, gates, ma, ret_nw, tables, dims, rows, sub, heads, cols):
    b, s, h, dk, dv = dims
    dmat, xi, zeta, gpow = tables
    rqcol, rkcol, rvcol, fcol = cols
    steps = s // rows
    kw, vw = heads * dk, heads * dv
    assert rows % sub == 0

    def rb(bi, i):
        return bi * steps + i

    in_specs = [
        pl.BlockSpec((rows, kw), lambda bi, gi, i: (rb(bi, i), rqcol // kw + gi)),
        pl.BlockSpec((rows, kw), lambda bi, gi, i: (rb(bi, i), rkcol // kw + gi)),
        pl.BlockSpec((rows, vw), lambda bi, gi, i: (rb(bi, i), rvcol // vw + gi)),
        pl.BlockSpec((rows, vw), lambda bi, gi, i: (rb(bi, i), fcol // vw + gi)),
        pl.BlockSpec((rows, vw), lambda bi, gi, i: (rb(bi, i), gi)),
        pl.BlockSpec((heads, sub, sub), lambda bi, gi, i: (gi, 0, 0)),
        pl.BlockSpec((heads, sub, dk), lambda bi, gi, i: (gi, 0, 0)),
        pl.BlockSpec((heads, sub, dk), lambda bi, gi, i: (gi, 0, 0)),
        pl.BlockSpec((heads, 1, dv), lambda bi, gi, i: (gi, 0, 0)),
        pl.BlockSpec((1, vw), lambda bi, gi, i: (0, gi)),
    ]
    return pl.pallas_call(
        functools.partial(_retention_kernel, dk=dk, dv=dv, heads=heads, sub=sub),
        grid=(b, h // heads, steps),
        in_specs=in_specs,
        out_specs=pl.BlockSpec((rows, vw), lambda bi, gi, i: (rb(bi, i), gi)),
        out_shape=jax.ShapeDtypeStruct((b * s, h * dv), BF16),
        scratch_shapes=[pltpu.VMEM((heads, dk, dv), F32)],
        compiler_params=_cparams(("parallel", "parallel", "arbitrary")),
        name="retention",
    )(qk, qk, proj, gates, ma, dmat, xi, zeta, gpow, ret_nw)


def _rotary_tables(s, dk, rows):
    inv = ROPE_BASE ** (-jnp.arange(0, dk, 2, dtype=F32) / dk)
    inv = jnp.concatenate([inv, inv])
    sign = jnp.concatenate([-jnp.ones(dk // 2, F32), jnp.ones(dk // 2, F32)])
    lo = jnp.arange(rows, dtype=F32)[:, None] * inv[None, :]
    hi = (jnp.arange(s // rows, dtype=F32) * rows)[:, None] * inv[None, :]
    cl, sl, ch, sh = jnp.cos(lo), jnp.sin(lo), jnp.cos(hi)[:, None], jnp.sin(hi)[:, None]
    cos = (ch * cl - sh * sl).reshape(s, dk)
    sin = ((sh * cl + ch * sl) * sign).reshape(s, dk)
    return cos, sin


def _retention_tables(h, dk, dv, rows):
    log_gamma = jnp.log1p(-jnp.exp2(-5.0 - jnp.arange(h, dtype=F32)))
    pos = jnp.arange(rows, dtype=F32)
    dist = pos[:, None] - pos[None, :]
    kscale = dk ** -0.5
    dmat = jnp.exp(jnp.where(dist >= 0, dist * log_gamma[:, None, None], -jnp.inf)) * kscale
    xi = jnp.broadcast_to(jnp.exp((pos + 1.0) * log_gamma[:, None])[:, :, None], (h, rows, dk))
    zeta = jnp.broadcast_to(jnp.exp((rows - 1.0 - pos) * log_gamma[:, None])[:, :, None] * kscale, (h, rows, dk))
    gpow = jnp.broadcast_to(jnp.exp(rows * log_gamma)[:, None, None], (h, 1, dv))
    return dmat, xi, zeta, gpow


def _outproj_kernel(m_ref, w_ref, x_ref, nw_ref, h_ref, hn_ref):
    hres = x_ref[...] + _dot(m_ref[...], w_ref[...])
    h_ref[...] = hres
    ms = jnp.mean(hres * hres, axis=-1, keepdims=True)
    hn_ref[...] = ((hres * lax.rsqrt(ms + EPS)) * nw_ref[...]).astype(hn_ref.dtype)


def _outproj(mixed, w, x2, nw, tm):
    m, d = x2.shape
    kdim = mixed.shape[1]
    return pl.pallas_call(
        _outproj_kernel,
        grid=(m // tm,),
        in_specs=[
            pl.BlockSpec((tm, kdim), lambda i: (i, 0)),
            pl.BlockSpec((kdim, d), lambda i: (0, 0)),
            pl.BlockSpec((tm, d), lambda i: (i, 0)),
            pl.BlockSpec((1, d), lambda i: (0, 0)),
        ],
        out_specs=[pl.BlockSpec((tm, d), lambda i: (i, 0))] * 2,
        out_shape=[jax.ShapeDtypeStruct((m, d), F32), jax.ShapeDtypeStruct((m, d), BF16)],
        compiler_params=_cparams(("parallel",)),
        name="outproj",
    )(mixed, w, x2, nw)


def _ffn_kernel(hn_ref, h_ref, wg_ref, wu_ref, wd_ref, nw_ref, o_ref):
    f = pl.program_id(1)

    @pl.when(f == 0)
    def _():
        o_ref[...] = h_ref[...]

    hn = hn_ref[...]
    g = _dot(hn, wg_ref[...])
    up = _dot(hn, wu_ref[...])
    act = (_silu(g) * up).astype(BF16)
    o_ref[...] += _dot(act, wd_ref[...])

    @pl.when(f == pl.num_programs(1) - 1)
    def _():
        hres = o_ref[...]
        ms = jnp.mean(hres * hres, axis=-1, keepdims=True)
        o_ref[...] = (hres * lax.rsqrt(ms + EPS)) * nw_ref[...]


def _ffn(hn, hres, wg, wu, wd, nw, tm, tf):
    m, d = hres.shape
    ff = wg.shape[1]
    return pl.pallas_call(
        _ffn_kernel,
        grid=(m // tm, ff // tf),
        in_specs=[
            pl.BlockSpec((tm, d), lambda i, j: (i, 0)),
            pl.BlockSpec((tm, d), lambda i, j: (i, 0)),
            pl.BlockSpec((d, tf), lambda i, j: (0, j)),
            pl.BlockSpec((d, tf), lambda i, j: (0, j)),
            pl.BlockSpec((tf, d), lambda i, j: (j, 0)),
            pl.BlockSpec((1, d), lambda i, j: (0, 0)),
        ],
        out_specs=pl.BlockSpec((tm, d), lambda i, j: (i, 0)),
        out_shape=jax.ShapeDtypeStruct((m, d), F32),
        compiler_params=_cparams(("parallel", "arbitrary"), FFN_VMEM_LIMIT_BYTES),
        name="ffn",
    )(hn, hres, wg, wu, wd, nw)


def _pick(total, want):
    t = min(total, want)
    while total % t:
        t //= 2
    return t


def _layer(h2, dims, norm1_w, w_in, conv_w, a_log, dt_bias, gdn_norm_w, ret_norm_w,
           w_out, norm2_w, w_gate, w_up, w_down, out_norm_w):
    b, s, h, dk, dv = dims
    m, d = h2.shape
    qk, vd = h * dk, h * dv
    chunk = GDN_CHUNK
    n = s // chunk

    assert 2 * qk == vd
    tn = vd
    sc0 = 3 * vd
    sc = 2 * h
    w_in_t = w_in.T
    w_ba = w_in_t[sc0:sc0 + sc].astype(BF16)
    tr = _pick(vd, 1024)
    per = vd // tr
    w_proj = _cast_rows(w_in_t, [g * vd + k * tr for g in (0, 1) for k in range(per)]
                        + [sc0 + sc + g * vd + k * tr for g in (0, 1) for k in range(per)], tr, sc)

    proj, u = _inproj(h2, norm1_w.reshape(1, d), w_proj, _pick(m, 1024), tn, (0, 1, 3))
    ret_sub = _pick(s, 256)
    cosf, sins = _rotary_tables(s, dk, ret_sub)
    rot_tm = _pick(s, 1024)
    gtr = _pick(vd, (4 * vd) // (m // rot_tm))
    gper = vd // gtr
    gate_srcs = ([2 * vd + k * gtr for k in range(gper)]
                 + [sc0 + sc + g * vd + k * gtr for g in (2, 3, 4) for k in range(gper)])
    qk_rot, beta, gcum, w_gate_rows = _rotproj(
        u, w_proj, cosf, sins, w_ba, a_log.reshape(1, h), dt_bias.reshape(1, h), w_in_t, gate_srcs, gtr, sc,
        rot_tm, tn, 2, s, dk, chunk, _pick(s, 256))
    gates, (wg_b, wu_b, wd_b, wo_b) = _gates(u, w_gate_rows, _pick(m, 1024), _pick(vd, 1024), vd,
                                             ((0, 2 * vd), (vd, 3 * vd)), [w_gate, w_up, w_down, w_out])

    gc_r = gcum.reshape(b, n // 2, 2 * chunk, h).transpose(0, 3, 1, 2)[:, :, :, None, :]
    wq, kpt, uu, attn, egl = _gdn_prep(proj, conv_w, beta, gcum, gc_r, dims, _pick(s, 512), _pick(h, 8))

    ma = _gdn_scan(wq, kpt, uu, attn, egl, gates.reshape(b, s, 2 * vd), gdn_norm_w.reshape(1, dv), dims,
                   _pick(n, 8), _pick(n, 2), _pick(h, 8), 0).reshape(m, vd)

    tables = _retention_tables(h, dk, dv, ret_sub)
    mixed = _retention(qk_rot, proj, gates, ma, ret_norm_w.reshape(1, vd), tables, dims, _pick(s, 512),
                       ret_sub, _pick(h, 8), (0, qk, 2 * tn, vd))

    hres, hn = _outproj(mixed, wo_b, h2, norm2_w.reshape(1, d), _pick(m, 512))

    return _ffn(hn, hres, wg_b, wu_b, wd_b, out_norm_w.reshape(1, d), _pick(m, 1024), _pick(w_gate.shape[1], 512))


def kernel(x, norm1_w, w_in, conv_w, a_log, dt_bias, gdn_norm_w, ret_norm_w, w_out, norm2_w,
           w_gate, w_up, w_down, norm_f_w):
    b, s, d = x.shape
    depth, h = a_log.shape
    assert depth == 1, "the final norm is fused into the single layer's FFN kernel"
    dk = d // 16
    dv = d // h
    dims = (b, s, h, dk, dv)
    out = _layer(x.reshape(b * s, d), dims, norm1_w[0], w_in[0], conv_w[0], a_log[0], dt_bias[0],
                 gdn_norm_w[0], ret_norm_w[0], w_out[0], norm2_w[0], w_gate[0], w_up[0], w_down[0],
                 norm_f_w)
    return out.reshape(b, s, d)
```
